```python
import jax
import jax.numpy as jnp
from jax import lax
import numpy as np

D_MODEL = 1024
BATCH = 4
SEQ = 4096
DEPTH = 1
DEC_BATCH = 32
DEC_SEQ = 8
PAST_LEN = 16384
PAGE_SIZE = 128

PLE_DIM = 256
RET_HEADS = 4
RET_DK = 128
RET_DV = 128
RET_WIDTH = RET_HEADS * RET_DV
RET_CHUNK = 128
NSA_HEADS = 8
NSA_KV_HEADS = 2
NSA_GROUP = NSA_HEADS // NSA_KV_HEADS
NSA_HD = 64
NSA_WIDTH = NSA_HEADS * NSA_HD
KV_WIDTH = NSA_KV_HEADS * NSA_HD
CMP_BLOCK = 32
SLC_BLOCK = 64
TOP_N = 16
WINDOW = 512
Q_BLOCK = 128
MIX_WIDTH = RET_WIDTH + NSA_WIDTH
SPLIT_SIZES = (RET_HEADS * RET_DK, RET_HEADS * RET_DK, RET_WIDTH, RET_WIDTH, NSA_WIDTH, KV_WIDTH, KV_WIDTH, KV_WIDTH, KV_WIDTH, KV_WIDTH, KV_WIDTH, 3 * NSA_HEADS, NSA_WIDTH)
PROJ_WIDTH = sum(SPLIT_SIZES)
ROPE_THETA = 10000.0
RMS_EPS = 1e-6
GN_EPS = 1e-5
NEG_INF = -1e9
FORCED_SCORE = 1e4
ATTN_SCALE = NSA_HD ** -0.5

kernel_name = 'hybrid_retention_nsa_decode_step'

F32 = jnp.float32


def rmsnorm(x, g):
    xf = x.astype(F32)
    y = xf * lax.rsqrt(jnp.mean(xf * xf, axis=-1, keepdims=True) + RMS_EPS)
    return (y * g.astype(F32)).astype(x.dtype)


def rope(x, pos):
    half = x.shape[-1] // 2
    inv = ROPE_THETA ** (-jnp.arange(half, dtype=F32) / half)
    ang = pos.astype(F32)[:, None] * inv[None, :]
    cos = jnp.cos(ang)[:, None, :]
    sin = jnp.sin(ang)[:, None, :]
    xf = x.astype(F32)
    x1, x2 = xf[..., :half], xf[..., half:]
    return jnp.concatenate([x1 * cos - x2 * sin, x2 * cos + x1 * sin], axis=-1).astype(x.dtype)


def in_proj(x, norm_g, w_in):
    B, T = x.shape[:2]
    h = rmsnorm(x, norm_g)
    points = [int(v) for v in np.cumsum(SPLIT_SIZES)[:-1]]
    rq, rk, rv, rg, nq, ck, cv, sk, sv, wk, wv, ngl, ng = jnp.split(h @ w_in, points, axis=-1)
    r = lambda a, H, D: a.reshape(B, T, H, D)
    return (r(rq, RET_HEADS, RET_DK), r(rk, RET_HEADS, RET_DK), r(rv, RET_HEADS, RET_DV), rg,
            r(nq, NSA_HEADS, NSA_HD), r(ck, NSA_KV_HEADS, NSA_HD), r(cv, NSA_KV_HEADS, NSA_HD),
            r(sk, NSA_KV_HEADS, NSA_HD), r(sv, NSA_KV_HEADS, NSA_HD),
            r(wk, NSA_KV_HEADS, NSA_HD), r(wv, NSA_KV_HEADS, NSA_HD), ngl, ng)


def retention_chunk(state, q, k, v):
    C = q.shape[1]
    log_g = jnp.log(1.0 - 2.0 ** (-5.0 - jnp.arange(RET_HEADS, dtype=F32)))
    i = jnp.arange(C, dtype=F32)
    diff = i[:, None] - i[None, :]
    causal = diff >= 0
    dmask = jnp.where(causal[None], jnp.exp(jnp.where(causal, diff, 0.0)[None] * log_g[:, None, None]), 0.0)
    s = jnp.einsum('bihd,bjhd->bhij', q, k) * dmask[None]
    intra = jnp.einsum('bhij,bjhe->bihe', s, v)
    q_decay = jnp.exp((i[:, None] + 1.0) * log_g[None, :])
    cross = jnp.einsum('bihd,bhde->bihe', q * q_decay[None, :, :, None], state)
    k_decay = jnp.exp((C - 1.0 - i)[:, None] * log_g[None, :])
    new_state = jnp.exp(C * log_g)[None, :, None, None] * state + jnp.einsum('bjhd,bjhe->bhde', k * k_decay[None, :, :, None], v)
    return new_state, intra + cross


def retention_mix(rq, rk, rv, pos, state):
    B, T = rq.shape[:2]
    q = rope(rq, pos).astype(F32)
    k = rope(rk, pos).astype(F32) * (RET_DK ** -0.5)
    v = rv.astype(F32)
    C = RET_CHUNK if T % RET_CHUNK == 0 else T
    nC = T // C
    to_chunks = lambda a: a.reshape(B, nC, C, *a.shape[2:]).swapaxes(0, 1)
    final, o = lax.scan(lambda st, xs: retention_chunk(st, *xs), state.astype(F32), (to_chunks(q), to_chunks(k), to_chunks(v)))
    return final, o.swapaxes(0, 1).reshape(B, T, RET_HEADS, RET_DV)


def retention_out(o, gate, gn_g, gn_b):
    B, T = o.shape[:2]
    mu = jnp.mean(o, axis=-1, keepdims=True)
    var = jnp.mean(jnp.square(o - mu), axis=-1, keepdims=True)
    on = ((o - mu) * lax.rsqrt(var + GN_EPS)).reshape(B, T, RET_WIDTH)
    return (on * gn_g.astype(F32) + gn_b.astype(F32)) * jax.nn.silu(gate.astype(F32))


def compress(rows, pe, w1, w2):
    B, L = rows.shape[:2]
    nc = L // CMP_BLOCK
    blk = rows[:, :nc * CMP_BLOCK].reshape(B, nc, CMP_BLOCK, NSA_KV_HEADS, NSA_HD).astype(F32) + pe.astype(F32)
    hid = jax.nn.gelu(jnp.einsum('bcjkd,kjde->bcke', blk, w1))
    return jnp.einsum('bcke,kef->bckf', hid, w2)


def cmp_attend(q, qpos, kc, vc):
    nc = kc.shape[1]
    cend = (jnp.arange(nc) + 1) * CMP_BLOCK - 1
    mask = (cend[None, :] <= qpos[:, None])[None, :, None, None, :]
    s = jnp.einsum('btkgd,bckd->btkgc', q, kc).astype(F32) * ATTN_SCALE
    p = jax.nn.softmax(jnp.where(mask, s, NEG_INF), axis=-1) * mask.astype(F32)
    return jnp.einsum('btkgc,bckd->btkgd', p, vc.astype(F32)), p


def select_blocks(p_cmp, qpos, L):
    nslc = -(-L // SLC_BLOCK)
    ratio = SLC_BLOCK // CMP_BLOCK
    imp = p_cmp.sum(axis=3)
    imp = jnp.pad(imp, ((0, 0), (0, 0), (0, 0), (0, nslc * ratio - imp.shape[-1])))
    imp = imp.reshape(*imp.shape[:-1], nslc, ratio).sum(axis=-1)
    b = jnp.arange(nslc)
    valid = (b[None, :] * SLC_BLOCK <= qpos[:, None])[None, :, None, :]
    forced = ((b[None, :] == 0) | (b[None, :] == (qpos // SLC_BLOCK)[:, None]))[None, :, None, :]
    score = jnp.where(forced, FORCED_SCORE, jnp.where(valid, imp, NEG_INF))
    _, idx = lax.top_k(score, min(TOP_N, nslc))
    return idx


def slc_attend(q, qpos, k_full, v_full, idx):
    B, L = k_full.shape[:2]
    T = q.shape[1]
    nslc = -(-L // SLC_BLOCK)
    padw = ((0, 0), (0, nslc * SLC_BLOCK - L), (0, 0), (0, 0))
    to_blocks = lambda a: jnp.pad(a, padw).reshape(B, nslc, SLC_BLOCK, NSA_KV_HEADS, NSA_HD).transpose(0, 3, 1, 2, 4)
    kb, vb = to_blocks(k_full), to_blocks(v_full)
    qb = Q_BLOCK if T % Q_BLOCK == 0 else T
    nqb = T // qb
    bi = jnp.arange(B)[:, None, None, None]
    hi = jnp.arange(NSA_KV_HEADS)[None, None, :, None]

    def block(args):
        qi, pi, ii = args
        n = ii.shape[-1]
        ks = kb[bi, hi, ii].reshape(B, qb, NSA_KV_HEADS, n * SLC_BLOCK, NSA_HD)
        vs = vb[bi, hi, ii].reshape(B, qb, NSA_KV_HEADS, n * SLC_BLOCK, NSA_HD)
        kpos = (ii[..., None] * SLC_BLOCK + jnp.arange(SLC_BLOCK)).reshape(B, qb, NSA_KV_HEADS, n * SLC_BLOCK)
        mask = (kpos <= pi[None, :, None, None])[:, :, :, None, :]
        s = jnp.einsum('bqkgd,bqknd->bqkgn', qi, ks).astype(F32) * ATTN_SCALE
        p = jax.nn.softmax(jnp.where(mask, s, NEG_INF), axis=-1)
        return jnp.einsum('bqkgn,bqknd->bqkgd', p, vs.astype(F32))

    qs = q.reshape(B, nqb, qb, NSA_KV_HEADS, NSA_GROUP, NSA_HD).swapaxes(0, 1)
    ps = qpos.reshape(nqb, qb)
    iss = idx.reshape(B, nqb, qb, NSA_KV_HEADS, idx.shape[-1]).swapaxes(0, 1)
    out = lax.map(block, (qs, ps, iss))
    return out.swapaxes(0, 1).reshape(B, T, NSA_KV_HEADS, NSA_GROUP, NSA_HD)


def win_attend(q, qpos, k, v, kpos):
    dist = qpos[:, :, None] - kpos[:, None, :]
    mask = ((dist >= 0) & (dist <= WINDOW) & (kpos[:, None, :] >= 0))[None, :, :, None, None, :]
    s = jnp.einsum('bnqkgd,bnmkd->bnqkgm', q, k).astype(F32) * ATTN_SCALE
    p = jax.nn.softmax(jnp.where(mask, s, NEG_INF), axis=-1)
    return jnp.einsum('bnqkgm,bnmkd->bnqkgd', p, v.astype(F32))


def window_prompt(q_rot, wk_r, wv, pos):
    B, S = q_rot.shape[:2]
    qb = Q_BLOCK if S % Q_BLOCK == 0 else S
    nb = S // qb
    idx = (jnp.arange(nb) * qb)[:, None] + jnp.arange(WINDOW + qb)[None, :]
    padw = ((0, 0), (WINDOW, 0), (0, 0), (0, 0))
    kblk = jnp.pad(wk_r, padw)[:, idx]
    vblk = jnp.pad(wv, padw)[:, idx]
    qg = q_rot.reshape(B, nb, qb, NSA_KV_HEADS, NSA_GROUP, NSA_HD)
    o = win_attend(qg, pos.reshape(nb, qb), kblk, vblk, idx - WINDOW)
    return o.reshape(B, S, NSA_KV_HEADS, NSA_GROUP, NSA_HD)


def nsa_combine(q, q_rot, qpos, ck_full, cv_full, sk_full, sv_full, o_win, gate_logits, gate,
                pe_k, w1_k, w2_k, pe_v, w1_v, w2_v):
    B, T = q.shape[:2]
    L = ck_full.shape[1]
    kc = compress(ck_full, pe_k, w1_k, w2_k)
    vc = compress(cv_full, pe_v, w1_v, w2_v)
    o_cmp, p_cmp = cmp_attend(q.reshape(B, T, NSA_KV_HEADS, NSA_GROUP, NSA_HD), qpos, kc, vc)
    idx = select_blocks(p_cmp, qpos, L)
    o_slc = slc_attend(q_rot.reshape(B, T, NSA_KV_HEADS, NSA_GROUP, NSA_HD), qpos, sk_full, sv_full, idx)
    g = jax.nn.sigmoid(gate_logits.astype(F32)).reshape(B, T, 3, NSA_KV_HEADS, NSA_GROUP, 1)
    o = g[:, :, 0] * o_cmp + g[:, :, 1] * o_slc + g[:, :, 2] * o_win
    return o.reshape(B, T, NSA_WIDTH) * jax.nn.silu(gate.astype(F32))


def mix_prompt(x, norm_g, w_in, gn_g, gn_b, pe_k, w1_k, w2_k, pe_v, w1_v, w2_v):
    B, S = x.shape[:2]
    pos = jnp.arange(S)
    rq, rk, rv, rg, nq, ck, cv, sk, sv, wk, wv, ngl, ng = in_proj(x, norm_g, w_in)
    ret_state, o_ret = retention_mix(rq, rk, rv, pos, jnp.zeros((B, RET_HEADS, RET_DK, RET_DV), F32))
    y_ret = retention_out(o_ret, rg, gn_g, gn_b)
    q_rot, sk_r, wk_r = rope(nq, pos), rope(sk, pos), rope(wk, pos)
    o_win = window_prompt(q_rot, wk_r, wv, pos)
    y_nsa = nsa_combine(nq, q_rot, pos, ck, cv, sk_r, sv, o_win, ngl, ng, pe_k, w1_k, w2_k, pe_v, w1_v, w2_v)
    y = jnp.concatenate([y_ret, y_nsa], axis=-1).astype(x.dtype)
    wb = min(WINDOW, S)
    return y, (ret_state, ck, cv, sk_r, sv, wk_r[:, S - wb:], wv[:, S - wb:])


def mix_sample(x, c_ck, c_cv, c_sk, c_sv, win_k, win_v, ret_state, page_table,
               norm_g, w_in, gn_g, gn_b, pe_k, w1_k, w2_k, pe_v, w1_v, w2_v):
    DB, T = x.shape[:2]
    P = page_table.shape[1] * PAGE_SIZE
    pos = P + jnp.arange(T)
    rq, rk, rv, rg, nq, ck, cv, sk, sv, wk, wv, ngl, ng = in_proj(x, norm_g, w_in)
    ret_new, o_ret = retention_mix(rq, rk, rv, pos, ret_state)
    y_ret = retention_out(o_ret, rg, gn_g, gn_b)
    q_rot, sk_r, wk_r = rope(nq, pos), rope(sk, pos), rope(wk, pos)
    past = lambda c: c[page_table].reshape(DB, P, NSA_KV_HEADS, NSA_HD)
    ck_full = jnp.concatenate([past(c_ck), ck], axis=1)
    cv_full = jnp.concatenate([past(c_cv), cv], axis=1)
    sk_full = jnp.concatenate([past(c_sk), sk_r], axis=1)
    sv_full = jnp.concatenate([past(c_sv), sv], axis=1)
    wb = win_k.shape[1]
    kw = jnp.concatenate([win_k, wk_r], axis=1)
    vw = jnp.concatenate([win_v, wv], axis=1)
    kpos = (P - wb + jnp.arange(wb + T))[None, :]
    o_win = win_attend(q_rot.reshape(DB, 1, T, NSA_KV_HEADS, NSA_GROUP, NSA_HD), pos[None, :], kw[:, None], vw[:, None], kpos)
    o_win = o_win.reshape(DB, T, NSA_KV_HEADS, NSA_GROUP, NSA_HD)
    y_nsa = nsa_combine(nq, q_rot, pos, ck_full, cv_full, sk_full, sv_full, o_win, ngl, ng, pe_k, w1_k, w2_k, pe_v, w1_v, w2_v)
    y = jnp.concatenate([y_ret, y_nsa], axis=-1).astype(x.dtype)
    return y, (ret_new, ck, cv, sk_r, sv, kw[:, T:], vw[:, T:])


def finish(x, y_mix, w_out, p_l, norm_ple, w_ple_gate, w_ple):
    x = x + y_mix @ w_out
    gate = jax.nn.sigmoid((rmsnorm(x, norm_ple) @ w_ple_gate).astype(F32))
    return (x.astype(F32) + gate * (p_l @ w_ple).astype(F32)).astype(x.dtype)


def setup_inputs(seed: int = 0) -> dict:
    key = jax.random.key(seed)
    ks = jax.random.split(key, 32)
    nrm = lambda k, shape, s: s * jax.random.normal(k, shape, F32)
    n_pages = PAST_LEN // PAGE_SIZE
    used = DEC_BATCH * n_pages
    n_pool = used + max(1, used // 4)
    kvsh = (DEPTH, n_pool, PAGE_SIZE, NSA_KV_HEADS, NSA_HD)
    wb = min(WINDOW, PAST_LEN)
    page_table = jax.random.permutation(ks[9], n_pool)[:used].reshape(DEC_BATCH, n_pages).astype(jnp.int32)
    return {
        'x_prompt': nrm(ks[0], (BATCH, SEQ, D_MODEL), 1.0),
        'x_sample': nrm(ks[1], (DEC_BATCH, DEC_SEQ, D_MODEL), 1.0),
        'cache_cmp_k': nrm(ks[2], kvsh, 1.0),
        'cache_cmp_v': nrm(ks[3], kvsh, 1.0),
        'cache_slc_k': nrm(ks[4], kvsh, 1.0),
        'cache_slc_v': nrm(ks[5], kvsh, 1.0),
        'state_win_k': nrm(ks[6], (DEPTH, DEC_BATCH, wb, NSA_KV_HEADS, NSA_HD), 1.0),
        'state_win_v': nrm(ks[7], (DEPTH, DEC_BATCH, wb, NSA_KV_HEADS, NSA_HD), 1.0),
        'state_ret': nrm(ks[8], (DEPTH, DEC_BATCH, RET_HEADS, RET_DK, RET_DV), 0.5),
        'page_table': page_table,
        'p_prompt': nrm(ks[10], (DEPTH, BATCH, SEQ, PLE_DIM), 1.0),
        'p_sample': nrm(ks[11], (DEPTH, DEC_BATCH, DEC_SEQ, PLE_DIM), 1.0),
        'norm_mix': 1.0 + nrm(ks[12], (DEPTH, D_MODEL), 0.02),
        'w_in': nrm(ks[13], (DEPTH, D_MODEL, PROJ_WIDTH), D_MODEL ** -0.5),
        'ret_gn_g': 1.0 + nrm(ks[14], (DEPTH, RET_WIDTH), 0.02),
        'ret_gn_b': nrm(ks[15], (DEPTH, RET_WIDTH), 0.02),
        'cmp_pe_k': nrm(ks[16], (DEPTH, CMP_BLOCK, NSA_KV_HEADS, NSA_HD), 0.5),
        'cmp_w1_k': nrm(ks[17], (DEPTH, NSA_KV_HEADS, CMP_BLOCK, NSA_HD, NSA_HD), (CMP_BLOCK * NSA_HD) ** -0.5),
        'cmp_w2_k': nrm(ks[18], (DEPTH, NSA_KV_HEADS, NSA_HD, NSA_HD), NSA_HD ** -0.5),
        'cmp_pe_v': nrm(ks[19], (DEPTH, CMP_BLOCK, NSA_KV_HEADS, NSA_HD), 0.5),
        'cmp_w1_v': nrm(ks[20], (DEPTH, NSA_KV_HEADS, CMP_BLOCK, NSA_HD, NSA_HD), (CMP_BLOCK * NSA_HD) ** -0.5),
        'cmp_w2_v': nrm(ks[21], (DEPTH, NSA_KV_HEADS, NSA_HD, NSA_HD), NSA_HD ** -0.5),
        'w_out': nrm(ks[22], (DEPTH, MIX_WIDTH, D_MODEL), MIX_WIDTH ** -0.5),
        'norm_ple': 1.0 + nrm(ks[23], (DEPTH, D_MODEL), 0.02),
        'w_ple_gate': nrm(ks[24], (DEPTH, D_MODEL, D_MODEL), D_MODEL ** -0.5),
        'w_ple': nrm(ks[25], (DEPTH, PLE_DIM, D_MODEL), PLE_DIM ** -0.5),
        'norm_f': 1.0 + nrm(ks[26], (D_MODEL,), 0.02),
    }


def reference(x_prompt, x_sample, cache_cmp_k, cache_cmp_v, cache_slc_k, cache_slc_v, state_win_k, state_win_v,
              state_ret, page_table, p_prompt, p_sample, norm_mix, w_in, ret_gn_g, ret_gn_b,
              cmp_pe_k, cmp_w1_k, cmp_w2_k, cmp_pe_v, cmp_w1_v, cmp_w2_v, w_out, norm_ple, w_ple_gate, w_ple, norm_f):
    xp, xs = x_prompt, x_sample
    st_p, st_s = [], []
    for l in range(DEPTH):
        cw = (cmp_pe_k[l], cmp_w1_k[l], cmp_w2_k[l], cmp_pe_v[l], cmp_w1_v[l], cmp_w2_v[l])
        yp, sp = mix_prompt(xp, norm_mix[l], w_in[l], ret_gn_g[l], ret_gn_b[l], *cw)
        ys, ss = mix_sample(xs, cache_cmp_k[l], cache_cmp_v[l], cache_slc_k[l], cache_slc_v[l],
                            state_win_k[l], state_win_v[l], state_ret[l], page_table,
                            norm_mix[l], w_in[l], ret_gn_g[l], ret_gn_b[l], *cw)
        xp = finish(xp, yp, w_out[l], p_prompt[l], norm_ple[l], w_ple_gate[l], w_ple[l])
        xs = finish(xs, ys, w_out[l], p_sample[l], norm_ple[l], w_ple_gate[l], w_ple[l])
        st_p.append(sp)
        st_s.append(ss)
    ret_p, cmp_k_p, cmp_v_p, slc_k_p, slc_v_p, win_k_p, win_v_p = [jnp.stack(a) for a in zip(*st_p)]
    ret_s, cmp_k_s, cmp_v_s, slc_k_s, slc_v_s, win_k_s, win_v_s = [jnp.stack(a) for a in zip(*st_s)]
    y_prompt = rmsnorm(xp, norm_f)
    y_sample = rmsnorm(xs, norm_f)
    return (y_prompt, y_sample, ret_p, cmp_k_p, cmp_v_p, slc_k_p, slc_v_p, win_k_p, win_v_p,
            ret_s, cmp_k_s, cmp_v_s, slc_k_s, slc_v_s, win_k_s, win_v_s)
```

```python
import functools

import numpy as np
import jax
import jax.numpy as jnp
from jax import lax
from jax.experimental import pallas as pl
from jax.experimental.pallas import tpu as pltpu

F32 = jnp.float32
BF16 = jnp.bfloat16

D_MODEL = 1024
PLE_DIM = 256
PAGE_SIZE = 128
RET_HEADS = 4
RET_DK = 128
RET_WIDTH = 512
RET_CHUNK = 128
NSA_HEADS = 8
KV_HEADS = 2
NSA_GROUP = 4
NSA_HD = 64
NSA_WIDTH = 512
KV_WIDTH = 128
CMP_BLOCK = 32
SLC_BLOCK = 64
TOP_N = 16
WINDOW = 512
Q_BLOCK = 128
ROPE_THETA = 10000.0
RMS_EPS = 1e-6
GN_EPS = 1e-5
NEG_INF = -1e9
FORCED_SCORE = 1e4
M_INIT = -1e30
ATTN_SCALE = NSA_HD ** -0.5
SPLIT_SIZES = (512, 512, 512, 512, 512, 128, 128, 128, 128, 128, 128, 24, 512)

LANES = 128
SLC_KEY_TILE = 512
SAMPLE_PAGES_PER_STEP = 8
VMEM_LIMIT = 56 * 1024 * 1024


def _dot(a, b):
    return jnp.dot(a, b, preferred_element_type=F32)


def _dot_nt(a, b):
    return lax.dot_general(a, b, (((1,), (1,)), ((), ())), preferred_element_type=F32)


def _sigmoid(x):
    return 1.0 / (1.0 + jnp.exp(-x))


def _params(sem):
    return pltpu.CompilerParams(dimension_semantics=sem, vmem_limit_bytes=VMEM_LIMIT)


def _proj_kernel(x_ref, g_ref, wrm_ref, wkv_ref, cosr_ref, sinr_ref, cosn_ref, sinn_ref, cost_ref, sint_ref,
                 qret_ref, kret_ref, vret_ref, rg_ref, nq_ref, nqr_ref, ng_ref, ngl_ref,
                 ck_ref, cv_ref, sv_ref, wv_ref, sk_ref, wk_ref):
    x = x_ref[...]
    ms = jnp.mean(x * x, axis=-1, keepdims=True)
    h = (x * lax.rsqrt(ms + RMS_EPS) * g_ref[...]).astype(BF16)

    def proj(lo, hi):
        return _dot(h, wrm_ref[:, lo:hi])

    lane = lax.broadcasted_iota(jnp.int32, (1, LANES), 1)
    cr, sr = cosr_ref[...], sinr_ref[...]
    yq = proj(0, 512)
    yk = proj(512, 1024)
    for hh in range(RET_HEADS):
        sl = slice(hh * LANES, (hh + 1) * LANES)
        a = yq[:, sl]
        qret_ref[:, sl] = a * cr + pltpu.roll(a, 64, 1) * sr
        a = yk[:, sl]
        kret_ref[:, sl] = (a * cr + pltpu.roll(a, 64, 1) * sr) * (RET_DK ** -0.5)

    yn = proj(1024, 1536)
    cn, sn = cosn_ref[...], sinn_ref[...]
    first_half = (lane % NSA_HD) < (NSA_HD // 2)
    for gg in range(NSA_GROUP):
        sl = slice(gg * LANES, (gg + 1) * LANES)
        a = yn[:, sl]
        partner = jnp.where(first_half, pltpu.roll(a, 96, 1), pltpu.roll(a, 32, 1))
        nq_ref[:, sl] = a.astype(BF16)
        nqr_ref[:, sl] = (a * cn + partner * sn).astype(BF16)

    vret_ref[...] = proj(1536, 2048)
    rg_ref[...] = proj(2048, 2560)
    ng_ref[...] = proj(2560, 3072)
    ngl_ref[...] = proj(3072, 3200)

    kvt = _dot_nt(wkv_ref[...], h)
    ck_ref[0] = kvt[0:128]
    cv_ref[0] = kvt[128:256]
    sv_ref[0] = kvt[256:384]
    wv_ref[0] = kvt[384:512]
    ct, st = cost_ref[...], sint_ref[...]

    def rope_t(blk):
        outs = []
        for k in range(KV_HEADS):
            x1 = blk[k * 64:k * 64 + 32]
            x2 = blk[k * 64 + 32:k * 64 + 64]
            outs += [x1 * ct - x2 * st, x2 * ct + x1 * st]
        return jnp.concatenate(outs, axis=0)

    sk_ref[0] = rope_t(kvt[512:640])
    wk_ref[0] = rope_t(kvt[640:768])


def _rope_tables(pos, half):
    inv = ROPE_THETA ** (-jnp.arange(half, dtype=F32) / half)
    ang = pos.astype(F32)[:, None] * inv[None, :]
    return jnp.cos(ang), jnp.sin(ang)


def _proj(x2d, pos_rows, norm_g, w_rm, w_kvt, kv_batch, tm):
    n = x2d.shape[0]
    nt = pos_rows.shape[0] // tm
    skv = n // kv_batch
    nkt = skv // tm
    c64, s64 = _rope_tables(pos_rows, 64)
    c32, s32 = _rope_tables(pos_rows, 32)
    cosr = jnp.concatenate([c64, c64], axis=1)
    sinr = jnp.concatenate([-s64, s64], axis=1)
    cosn = jnp.concatenate([c32, c32, c32, c32], axis=1)
    sinn = jnp.concatenate([-s32, s32, -s32, s32], axis=1)
    cost, sint = c32.T, s32.T
    row = lambda w: pl.BlockSpec((tm, w), lambda i: (i, 0))
    tab = pl.BlockSpec((tm, LANES), lambda i: (i % nt, 0))
    tabt = pl.BlockSpec((32, tm), lambda i: (0, i % nt))
    kvo = pl.BlockSpec((1, KV_WIDTH, tm), lambda i: (i // nkt, 0, i % nkt))
    f = lambda w, dt: jax.ShapeDtypeStruct((n, w), dt)
    kvs = jax.ShapeDtypeStruct((kv_batch, KV_WIDTH, skv), F32)
    return pl.pallas_call(
        _proj_kernel,
        grid=(n // tm,),
        in_specs=[row(D_MODEL), pl.BlockSpec((1, D_MODEL), lambda i: (0, 0)),
                  pl.BlockSpec(w_rm.shape, lambda i: (0, 0)), pl.BlockSpec(w_kvt.shape, lambda i: (0, 0)),
                  tab, tab, tab, tab, tabt, tabt],
        out_specs=[row(512), row(512), row(512), row(512), row(512), row(512), row(512), row(LANES),
                   kvo, kvo, kvo, kvo, kvo, kvo],
        out_shape=[f(512, F32), f(512, F32), f(512, F32), f(512, F32), f(512, BF16), f(512, BF16), f(512, F32),
                   f(LANES, F32), kvs, kvs, kvs, kvs, kvs, kvs],
        compiler_params=_params(("arbitrary",)),
        name="proj",
    )(x2d, norm_g, w_rm, w_kvt, cosr, sinr, cosn, sinn, cost, sint)


def _ret_kernel(q_ref, k_ref, v_ref, rg_ref, st_ref, dm_ref, qd_ref, kd_ref, sd_ref, gg_ref, gb_ref,
                y_ref, so_ref, st_scr, *, n_chunks):
    c = pl.program_id(1)

    @pl.when(c == 0)
    def _():
        st_scr[...] = st_ref[0]

    for h in range(RET_HEADS):
        sl = slice(h * LANES, (h + 1) * LANES)
        q = q_ref[:, sl]
        k = k_ref[:, sl]
        v = v_ref[:, sl].astype(BF16)
        state = st_scr[h]
        s = _dot_nt(q.astype(BF16), k.astype(BF16)) * dm_ref[h]
        intra = _dot(s.astype(BF16), v)
        cross = _dot((q * qd_ref[:, sl]).astype(BF16), state.astype(BF16))
        o = intra + cross
        kd_t = (k * kd_ref[:, sl]).T.astype(BF16)
        st_scr[h] = sd_ref[h] * state + _dot(kd_t, v)
        mu = jnp.mean(o, axis=-1, keepdims=True)
        var = jnp.mean(jnp.square(o - mu), axis=-1, keepdims=True)
        on = (o - mu) * lax.rsqrt(var + GN_EPS)
        gate = rg_ref[:, sl]
        y_ref[:, sl] = (on * gg_ref[:, sl] + gb_ref[:, sl]) * (gate * _sigmoid(gate))

    @pl.when(c == n_chunks - 1)
    def _():
        so_ref[0] = st_scr[...]


def _retention(q, k, v, rg, state, gn_g, gn_b, n_batch, n_chunks, chunk_len):
    c = RET_CHUNK
    log_g = jnp.log(1.0 - 2.0 ** (-5.0 - jnp.arange(RET_HEADS, dtype=F32)))
    i = jnp.arange(c, dtype=F32)
    diff = i[:, None] - i[None, :]
    causal = diff >= 0
    dmask = jnp.where(causal[None], jnp.exp(jnp.where(causal, diff, 0.0)[None] * log_g[:, None, None]), 0.0)
    expand = lambda t: jnp.repeat(t, LANES, axis=1)
    qdec = expand(jnp.exp((i[:, None] + 1.0) * log_g[None, :]))
    kdec = expand(jnp.exp((chunk_len - 1.0 - i)[:, None] * log_g[None, :]))
    sdec = jnp.broadcast_to(jnp.exp(chunk_len * log_g)[:, None, None], (RET_HEADS, 1, LANES))
    row = pl.BlockSpec((c, 512), lambda b, j: (b * n_chunks + j, 0))
    stspec = pl.BlockSpec((1, RET_HEADS, 128, 128), lambda b, j: (b, 0, 0, 0))
    const = lambda shape: pl.BlockSpec(shape, lambda b, j: (0,) * len(shape))
    return pl.pallas_call(
        functools.partial(_ret_kernel, n_chunks=n_chunks),
        grid=(n_batch, n_chunks),
        in_specs=[row, row, row, row, stspec, const((RET_HEADS, c, c)), const((c, 512)), const((c, 512)),
                  const((RET_HEADS, 1, LANES)), const((1, 512)), const((1, 512))],
        out_specs=[row, stspec],
        out_shape=[jax.ShapeDtypeStruct(q.shape, F32), jax.ShapeDtypeStruct(state.shape, F32)],
        scratch_shapes=[pltpu.VMEM((RET_HEADS, 128, 128), F32)],
        compiler_params=_params(("arbitrary", "arbitrary")),
        name="retention",
    )(q, k, v, rg, state, dmask, qdec, kdec, sdec, gn_g, gn_b)


def _gelu_tanh(x):
    return 0.5 * x * (1.0 + jnp.tanh(np.sqrt(2.0 / np.pi).astype(np.float32) * (x + 0.044715 * (x * x * x))))


def _compress_one(tiles, pe_ref, w1_ref, w2_ref, out_ref, scr):
    for t, tile in enumerate(tiles):
        scr[t * LANES:(t + 1) * LANES, :] = tile.T
    n_blk = len(tiles) * LANES // CMP_BLOCK
    acc = jnp.zeros((n_blk, LANES), F32)
    for j in range(CMP_BLOCK):
        rows = scr[pl.ds(j, n_blk, stride=CMP_BLOCK), :] + pe_ref[j:j + 1, :]
        acc = acc + _dot(rows.astype(BF16), w1_ref[j])
    hid = _gelu_tanh(acc)
    out_ref[0] = _dot(hid.astype(BF16), w2_ref[...])


def _compress_prompt_kernel(k_ref, v_ref, pek_ref, w1k_ref, w2k_ref, pev_ref, w1v_ref, w2v_ref,
                            kc_ref, vc_ref, scr, *, n_tiles):
    for src, pe, w1, w2, dst in ((k_ref, pek_ref, w1k_ref, w2k_ref, kc_ref), (v_ref, pev_ref, w1v_ref, w2v_ref, vc_ref)):
        tiles = [src[0, :, t * LANES:(t + 1) * LANES] for t in range(n_tiles)]
        _compress_one(tiles, pe, w1, w2, dst, scr)


def _compress_sample_kernel(pt_ref, *refs, n_pages):
    k_pages = refs[:n_pages]
    v_pages = refs[n_pages:2 * n_pages]
    pek_ref, w1k_ref, w2k_ref, pev_ref, w1v_ref, w2v_ref, kc_ref, vc_ref, scr = refs[2 * n_pages:]
    _compress_one([r[0] for r in k_pages], pek_ref, w1k_ref, w2k_ref, kc_ref, scr)
    _compress_one([r[0] for r in v_pages], pev_ref, w1v_ref, w2v_ref, vc_ref, scr)


def _compress_weights(pe, w1, w2):
    pe_rows = pe.reshape(CMP_BLOCK, KV_WIDTH)
    z = jnp.zeros((CMP_BLOCK, NSA_HD, NSA_HD), F32)
    w1bd = jnp.concatenate([jnp.concatenate([w1[0], z], axis=2), jnp.concatenate([z, w1[1]], axis=2)], axis=1)
    z2 = jnp.zeros((NSA_HD, NSA_HD), F32)
    w2bd = jnp.concatenate([jnp.concatenate([w2[0], z2], axis=1), jnp.concatenate([z2, w2[1]], axis=1)], axis=0)
    return pe_rows, w1bd.astype(BF16), w2bd.astype(BF16)


def _cw_specs(nidx):
    zero = lambda n: (lambda *a: (0,) * n)
    return [pl.BlockSpec((CMP_BLOCK, KV_WIDTH), zero(2)), pl.BlockSpec((CMP_BLOCK, KV_WIDTH, KV_WIDTH), zero(3)),
            pl.BlockSpec((KV_WIDTH, KV_WIDTH), zero(2))]


def _compress_prompt(ck_t, cv_t, cwk, cwv):
    b, _, s = ck_t.shape
    n_blk = s // CMP_BLOCK
    src = pl.BlockSpec((1, KV_WIDTH, s), lambda i: (i, 0, 0))
    dst = pl.BlockSpec((1, n_blk, KV_WIDTH), lambda i: (i, 0, 0))
    shp = jax.ShapeDtypeStruct((b, n_blk, KV_WIDTH), F32)
    return pl.pallas_call(
        functools.partial(_compress_prompt_kernel, n_tiles=s // LANES),
        grid=(b,),
        in_specs=[src, src] + _cw_specs(1) + _cw_specs(1),
        out_specs=[dst, dst],
        out_shape=[shp, shp],
        scratch_shapes=[pltpu.VMEM((s, KV_WIDTH), F32)],
        compiler_params=_params(("arbitrary",)),
        name="compress_prompt",
    )(ck_t, cv_t, *cwk, *cwv)


def _compress_sample(page_table, cache_k_t, cache_v_t, cwk, cwv):
    db, n_pages = page_table.shape
    g = SAMPLE_PAGES_PER_STEP
    steps = n_pages // g
    blk_per_step = g * PAGE_SIZE // CMP_BLOCK
    n_blk = n_pages * PAGE_SIZE // CMP_BLOCK

    def page_spec(i):
        return pl.BlockSpec((1, KV_WIDTH, PAGE_SIZE), lambda b, s, pt: (pt[b, s * g + i], 0, 0))

    zero = lambda n: (lambda b, s, pt: (0,) * n)
    wspecs = [pl.BlockSpec((CMP_BLOCK, KV_WIDTH), zero(2)), pl.BlockSpec((CMP_BLOCK, KV_WIDTH, KV_WIDTH), zero(3)),
              pl.BlockSpec((KV_WIDTH, KV_WIDTH), zero(2))]
    dst = pl.BlockSpec((1, blk_per_step, KV_WIDTH), lambda b, s, pt: (b, s, 0))
    shp = jax.ShapeDtypeStruct((db, n_blk, KV_WIDTH), F32)
    grid_spec = pltpu.PrefetchScalarGridSpec(
        num_scalar_prefetch=1,
        grid=(db, steps),
        in_specs=[page_spec(i % g) for i in range(2 * g)] + wspecs + wspecs,
        out_specs=[dst, dst],
        scratch_shapes=[pltpu.VMEM((g * PAGE_SIZE, KV_WIDTH), F32)],
    )
    return pl.pallas_call(
        functools.partial(_compress_sample_kernel, n_pages=g),
        grid_spec=grid_spec,
        out_shape=[shp, shp],
        compiler_params=_params(("arbitrary", "arbitrary")),
        name="compress_sample",
    )(page_table, *([cache_k_t] * g), *([cache_v_t] * g), *cwk, *cwv)


def _stack_heads(q_ref):
    lane = lax.broadcasted_iota(jnp.int32, (1, LANES), 1)
    lo = lane < NSA_HD
    slabs = [q_ref[:, g * LANES:(g + 1) * LANES] for g in range(NSA_GROUP)]
    zero = jnp.zeros_like(slabs[0])
    stacked = jnp.concatenate([jnp.where(lo, s, zero) for s in slabs] + [jnp.where(lo, zero, s) for s in slabs], axis=0)
    return stacked.astype(BF16)


def _mask_rows(sc, masks, rows):
    n = sc.shape[-1]
    s4 = sc.reshape(2 * NSA_GROUP, rows, n)
    out = [jnp.where(masks[k][None], s4[k * NSA_GROUP:(k + 1) * NSA_GROUP], NEG_INF) for k in range(KV_HEADS)]
    return jnp.concatenate(out, axis=0).reshape(2 * NSA_GROUP * rows, n)


def _pair_scores(imp, qpos, n_lanes):
    lane = lax.broadcasted_iota(jnp.int32, (1, n_lanes), 1)
    even = (lane % 2) == 0
    ps = imp + jnp.where(even, pltpu.roll(imp, n_lanes - 1, 1), pltpu.roll(imp, 1, 1))
    blk = lane // 2
    valid = blk * SLC_BLOCK <= qpos
    forced = (blk == 0) | (blk == qpos // SLC_BLOCK)
    return jnp.where(forced, FORCED_SCORE, jnp.where(valid, ps, NEG_INF))


def _rank_select(score, n_lanes, extra_forced):
    lane = lax.broadcasted_iota(jnp.int32, (1, n_lanes), 1)
    blk = lane // 2
    rank = jnp.zeros(score.shape, jnp.int32)
    for j in range(0, n_lanes, 2):
        col = score[:, j:j + 1]
        beats = (col > score) | ((col == score) & ((j // 2) < blk))
        rank = rank + beats.astype(jnp.int32)
    if extra_forced:
        rank = rank + (score < FORCED_SCORE).astype(jnp.int32)
    return rank < TOP_N


def _merge_kv_heads(acc, rows):
    lane = lax.broadcasted_iota(jnp.int32, (1, LANES), 1)
    lo = lane < NSA_HD
    half = NSA_GROUP * rows
    return [jnp.where(lo, acc[g * rows:(g + 1) * rows], acc[half + g * rows:half + (g + 1) * rows])
            for g in range(NSA_GROUP)]


def _gate_and_store(o_cmp, o_slc, o_win, ngl_ref, ng_ref, y_ref):
    lane = lax.broadcasted_iota(jnp.int32, (1, LANES), 1)
    lo = lane < NSA_HD
    sig = _sigmoid(ngl_ref[...])
    for g in range(NSA_GROUP):
        gates = [jnp.where(lo, sig[:, r * 8 + g:r * 8 + g + 1], sig[:, r * 8 + 4 + g:r * 8 + 4 + g + 1]) for r in range(3)]
        o = gates[0] * o_cmp[g] + gates[1] * o_slc[g] + gates[2] * o_win[g]
        gate = ng_ref[:, g * LANES:(g + 1) * LANES]
        y_ref[:, g * LANES:(g + 1) * LANES] = (o * (gate * _sigmoid(gate))).astype(y_ref.dtype)


def _attn_prompt_kernel(nq_ref, nqr_ref, ngl_ref, ng_ref, kc_ref, vc_ref, sk_ref, sv_ref, wk_ref, wv_ref, e_ref,
                        y_ref):
    i = pl.program_id(1)
    qb = Q_BLOCK
    q0 = i * qb
    rows8 = 2 * NSA_GROUP * qb
    half = NSA_GROUP * qb
    qpos = q0 + lax.broadcasted_iota(jnp.int32, (qb, 1), 0)

    lhs = _stack_heads(nq_ref)
    n_cmp = kc_ref.shape[1]
    s = _dot_nt(lhs, kc_ref[0].astype(BF16)) * ATTN_SCALE
    cend = (lax.broadcasted_iota(jnp.int32, (1, n_cmp), 1) + 1) * CMP_BLOCK - 1
    cmask = cend <= qpos
    s4 = jnp.where(cmask[None], s.reshape(8, qb, n_cmp), NEG_INF)
    m = jnp.max(s4, axis=-1, keepdims=True)
    p = jnp.exp(s4 - m)
    p = p / jnp.sum(p, axis=-1, keepdims=True) * cmask[None].astype(F32)
    vc = vc_ref[0].astype(BF16)
    o_cmp = _merge_kv_heads(_dot(p.reshape(rows8, n_cmp).astype(BF16), vc), qb)
    sels = []
    for k in range(KV_HEADS):
        imp = p[k * NSA_GROUP]
        for g in range(1, NSA_GROUP):
            imp = imp + p[k * NSA_GROUP + g]
        sel = _rank_select(_pair_scores(imp, qpos, n_cmp), n_cmp, False)
        sels.append(sel.astype(BF16))

    lhs_r = _stack_heads(nqr_ref)
    tk = SLC_KEY_TILE
    n_tiles = (q0 + qb - 1) // tk + 1

    def body(t, carry):
        m_i, l_i, acc = carry
        k0 = pl.multiple_of(t * tk, tk)
        kt = sk_ref[0, :, pl.ds(k0, tk)].astype(BF16)
        vt = sv_ref[0, :, pl.ds(k0, tk)].astype(BF16)
        et = e_ref[:, pl.ds(k0, tk)]
        sc = _dot(lhs_r, kt) * ATTN_SCALE
        kpos = k0 + lax.broadcasted_iota(jnp.int32, (1, tk), 1)
        vis = kpos <= qpos
        masks = [(_dot(sels[k], et) > 0.5) & vis for k in range(KV_HEADS)]
        sc = _mask_rows(sc, masks, qb)
        m_new = jnp.maximum(m_i, jnp.max(sc, axis=-1, keepdims=True))
        alpha = jnp.exp(m_i - m_new)
        pr = jnp.exp(sc - m_new)
        l_new = alpha * l_i + jnp.sum(pr, axis=-1, keepdims=True)
        pb = pr.astype(BF16)
        pv = jnp.concatenate([_dot_nt(pb[:half], vt), _dot_nt(pb[half:], vt)], axis=0)
        return m_new, l_new, alpha * acc + pv

    init = (jnp.full((rows8, 1), M_INIT, F32), jnp.zeros((rows8, 1), F32), jnp.zeros((rows8, LANES), F32))
    _, l_f, acc_f = lax.fori_loop(0, n_tiles, body, init)
    o_slc = _merge_kv_heads(acc_f / l_f, qb)

    wlen = WINDOW + qb
    w0 = pl.multiple_of(jnp.maximum(i - WINDOW // qb, 0) * qb, qb)
    kt = wk_ref[0, :, pl.ds(w0, wlen)].astype(BF16)
    vt = wv_ref[0, :, pl.ds(w0, wlen)].astype(BF16)
    sc = _dot(lhs_r, kt) * ATTN_SCALE
    dist = qpos - (w0 + lax.broadcasted_iota(jnp.int32, (1, wlen), 1))
    wmask = (dist >= 0) & (dist <= WINDOW)
    sc = _mask_rows(sc, [wmask, wmask], qb)
    m = jnp.max(sc, axis=-1, keepdims=True)
    pr = jnp.exp(sc - m)
    pr = pr / jnp.sum(pr, axis=-1, keepdims=True)
    pb = pr.astype(BF16)
    o_win = _merge_kv_heads(jnp.concatenate([_dot_nt(pb[:half], vt), _dot_nt(pb[half:], vt)], axis=0), qb)

    _gate_and_store(o_cmp, o_slc, o_win, ngl_ref, ng_ref, y_ref)


def _attn_prompt(nq, nqr, ngl, ng, kc, vc, sk_t, sv_t, wk_t, wv_t):
    b, _, s = sk_t.shape
    nqb = s // Q_BLOCK
    expand = (jnp.arange(s)[None, :] // CMP_BLOCK == jnp.arange(s // CMP_BLOCK)[:, None]).astype(BF16)
    row = lambda w: pl.BlockSpec((Q_BLOCK, w), lambda bb, i: (bb * nqb + i, 0))
    per_b = lambda shape: pl.BlockSpec((1,) + shape, lambda bb, i: (bb, 0, 0))
    return pl.pallas_call(
        _attn_prompt_kernel,
        grid=(b, nqb),
        in_specs=[row(512), row(512), row(LANES), row(512), per_b(kc.shape[1:]), per_b(vc.shape[1:]),
                  per_b((KV_WIDTH, s)), per_b((KV_WIDTH, s)), per_b((KV_WIDTH, s)), per_b((KV_WIDTH, s)),
                  pl.BlockSpec(expand.shape, lambda bb, i: (0, 0))],
        out_specs=row(512),
        out_shape=jax.ShapeDtypeStruct((b * s, 512), BF16),
        compiler_params=_params(("arbitrary", "arbitrary")),
        name="attn_prompt",
    )(nq, nqr, ngl, ng, kc, vc, sk_t, sv_t, wk_t, wv_t, expand)


def _attn_sample_kernel(pt_ref, *refs, n_pages, n_steps, t_len, past_len):
    k_pages = refs[:n_pages]
    v_pages = refs[n_pages:2 * n_pages]
    (nq_ref, nqr_ref, ngl_ref, ng_ref, kc_ref, vc_ref, skn_ref, svn_ref, wko_ref, wvo_ref, wkn_ref, wvn_ref, e_ref,
     y_ref, m_scr, l_scr, acc_scr, ocmp_scr, owin_scr, mask_scr) = refs[2 * n_pages:]
    step = pl.program_id(1)
    t = t_len
    nq_ref, nqr_ref, ngl_ref, ng_ref, y_ref = (r.at[0] for r in (nq_ref, nqr_ref, ngl_ref, ng_ref, y_ref))
    rows8 = 2 * NSA_GROUP * t
    half = NSA_GROUP * t
    tpos = lax.broadcasted_iota(jnp.int32, (t, 1), 0)
    qpos = past_len + tpos
    n_cmp = kc_ref.shape[1]
    wb = wko_ref.shape[2]
    step_keys = n_pages * PAGE_SIZE

    @pl.when(step == 0)
    def _():
        lhs = _stack_heads(nq_ref)
        s = _dot_nt(lhs, kc_ref[0].astype(BF16)) * ATTN_SCALE
        m = jnp.max(s, axis=-1, keepdims=True)
        p = jnp.exp(s - m)
        p = p / jnp.sum(p, axis=-1, keepdims=True)
        oc = _dot(p.astype(BF16), vc_ref[0].astype(BF16))
        for g, slab in enumerate(_merge_kv_heads(oc, t)):
            ocmp_scr[g] = slab
        p4 = p.reshape(8, t, n_cmp)
        for k in range(KV_HEADS):
            imp = p4[k * NSA_GROUP]
            for g in range(1, NSA_GROUP):
                imp = imp + p4[k * NSA_GROUP + g]
            sel = _rank_select(_pair_scores(imp, qpos, n_cmp), n_cmp, True).astype(BF16)
            for c in range(n_cmp // LANES):
                km = _dot(sel[:, c * LANES:(c + 1) * LANES], e_ref[...])
                per = LANES * CMP_BLOCK // step_keys
                for u in range(per):
                    mask_scr[c * per + u, k] = km[:, u * step_keys:(u + 1) * step_keys]

        lhs_r = _stack_heads(nqr_ref)
        so = _dot(lhs_r, wko_ref[0].astype(BF16)) * ATTN_SCALE
        sn = _dot(lhs_r, wkn_ref[0].astype(BF16)) * ATTN_SCALE
        jo = lax.broadcasted_iota(jnp.int32, (1, wb), 1)
        jn = lax.broadcasted_iota(jnp.int32, (1, LANES), 1)
        old_vis = jo >= tpos
        so = _mask_rows(so, [old_vis, old_vis], t)
        new_vis = jn <= tpos
        sn_w = _mask_rows(sn, [new_vis, new_vis], t)
        m = jnp.maximum(jnp.max(so, axis=-1, keepdims=True), jnp.max(sn_w, axis=-1, keepdims=True))
        po = jnp.exp(so - m)
        pn = jnp.exp(sn_w - m)
        den = jnp.sum(po, axis=-1, keepdims=True) + jnp.sum(pn, axis=-1, keepdims=True)
        pob, pnb = (po / den).astype(BF16), (pn / den).astype(BF16)
        wvo, wvn = wvo_ref[0].astype(BF16), wvn_ref[0].astype(BF16)
        ow = jnp.concatenate([_dot_nt(pob[:half], wvo) + _dot_nt(pnb[:half], wvn),
                              _dot_nt(pob[half:], wvo) + _dot_nt(pnb[half:], wvn)], axis=0)
        for g, slab in enumerate(_merge_kv_heads(ow, t)):
            owin_scr[g] = slab

        s2 = _dot(lhs_r, skn_ref[0].astype(BF16)) * ATTN_SCALE
        s2 = _mask_rows(s2, [new_vis, new_vis], t)
        m2 = jnp.max(s2, axis=-1, keepdims=True)
        p2 = jnp.exp(s2 - m2)
        m_scr[...] = m2
        l_scr[...] = jnp.sum(p2, axis=-1, keepdims=True)
        p2b = p2.astype(BF16)
        svn = svn_ref[0].astype(BF16)
        acc_scr[...] = jnp.concatenate([_dot_nt(p2b[:half], svn), _dot_nt(p2b[half:], svn)], axis=0)

    lhs_r = _stack_heads(nqr_ref)
    sc = jnp.concatenate([_dot(lhs_r, r[0].astype(BF16)) for r in k_pages], axis=1) * ATTN_SCALE
    msk = mask_scr[step]
    sc = _mask_rows(sc, [msk[k] > 0.5 for k in range(KV_HEADS)], t)
    m_i = m_scr[...]
    m_new = jnp.maximum(m_i, jnp.max(sc, axis=-1, keepdims=True))
    alpha = jnp.exp(m_i - m_new)
    pr = jnp.exp(sc - m_new)
    l_scr[...] = alpha * l_scr[...] + jnp.sum(pr, axis=-1, keepdims=True)
    m_scr[...] = m_new
    pb = pr.astype(BF16)
    pv = jnp.zeros((rows8, LANES), F32)
    for j, r in enumerate(v_pages):
        vt = r[0].astype(BF16)
        pj = pb[:, j * PAGE_SIZE:(j + 1) * PAGE_SIZE]
        pv = pv + jnp.concatenate([_dot_nt(pj[:half], vt), _dot_nt(pj[half:], vt)], axis=0)
    acc_scr[...] = alpha * acc_scr[...] + pv

    @pl.when(step == n_steps - 1)
    def _():
        o_slc = _merge_kv_heads(acc_scr[...] / l_scr[...], t)
        o_cmp = [ocmp_scr[g] for g in range(NSA_GROUP)]
        o_win = [owin_scr[g] for g in range(NSA_GROUP)]
        _gate_and_store(o_cmp, o_slc, o_win, ngl_ref, ng_ref, y_ref)


def _attn_sample(page_table, cache_sk_t, cache_sv_t, nq, nqr, ngl, ng, kc, vc, skn, svn, wko, wvo, wkn, wvn, t_len):
    db, n_pages = page_table.shape
    g = SAMPLE_PAGES_PER_STEP
    steps = n_pages // g
    past_len = n_pages * PAGE_SIZE
    n_cmp = kc.shape[1]
    step_keys = g * PAGE_SIZE
    span = LANES * CMP_BLOCK
    expand = (jnp.arange(span)[None, :] // CMP_BLOCK == jnp.arange(LANES)[:, None]).astype(BF16)

    def page_spec(i):
        return pl.BlockSpec((1, KV_WIDTH, PAGE_SIZE), lambda b, s, pt: (pt[b, s * g + i], 0, 0))

    row = lambda w: pl.BlockSpec((1, t_len, w), lambda b, s, pt: (b, 0, 0))
    per_b = lambda shape: pl.BlockSpec((1,) + shape, lambda b, s, pt: (b, 0, 0))
    rows8 = 2 * NSA_GROUP * t_len
    grid_spec = pltpu.PrefetchScalarGridSpec(
        num_scalar_prefetch=1,
        grid=(db, steps),
        in_specs=[page_spec(i % g) for i in range(2 * g)]
        + [row(512), row(512), row(LANES), row(512), per_b(kc.shape[1:]), per_b(vc.shape[1:]),
           per_b(skn.shape[1:]), per_b(svn.shape[1:]), per_b(wko.shape[1:]), per_b(wvo.shape[1:]),
           per_b(wkn.shape[1:]), per_b(wvn.shape[1:]), pl.BlockSpec(expand.shape, lambda b, s, pt: (0, 0))],
        out_specs=row(512),
        scratch_shapes=[pltpu.VMEM((rows8, 1), F32), pltpu.VMEM((rows8, 1), F32), pltpu.VMEM((rows8, LANES), F32),
                        pltpu.VMEM((NSA_GROUP, t_len, LANES), F32), pltpu.VMEM((NSA_GROUP, t_len, LANES), F32),
                        pltpu.VMEM((steps, KV_HEADS, t_len, step_keys), F32)],
    )
    return pl.pallas_call(
        functools.partial(_attn_sample_kernel, n_pages=g, n_steps=steps, t_len=t_len, past_len=past_len),
        grid_spec=grid_spec,
        out_shape=jax.ShapeDtypeStruct((db, t_len, 512), F32),
        compiler_params=_params(("arbitrary", "arbitrary")),
        name="attn_sample",
    )(page_table, *([cache_sk_t] * g), *([cache_sv_t] * g), nq, nqr, ngl, ng, kc, vc, skn, svn, wko, wvo, wkn, wvn, expand)


def _finish_kernel(x_ref, yr_ref, yn_ref, p_ref, wo_ref, gple_ref, wg_ref, wp_ref, gf_ref, o_ref):
    x = x_ref[...]
    x = x + _dot(yr_ref[...].astype(BF16), wo_ref[0:RET_WIDTH, :]) + _dot(yn_ref[...].astype(BF16), wo_ref[RET_WIDTH:, :])
    ms = jnp.mean(x * x, axis=-1, keepdims=True)
    hn = (x * lax.rsqrt(ms + RMS_EPS) * gple_ref[...]).astype(BF16)
    gate = _sigmoid(_dot(hn, wg_ref[...]))
    x = x + gate * _dot(p_ref[...].astype(BF16), wp_ref[...])
    ms = jnp.mean(x * x, axis=-1, keepdims=True)
    o_ref[...] = x * lax.rsqrt(ms + RMS_EPS) * gf_ref[...]


def _finish(x2d, y_ret, y_nsa, p2d, w_out, norm_ple, w_gate, w_ple, norm_f, tm):
    n = x2d.shape[0]
    row = lambda w: pl.BlockSpec((tm, w), lambda i: (i, 0))
    const = lambda shape: pl.BlockSpec(shape, lambda i: (0, 0))
    return pl.pallas_call(
        _finish_kernel,
        grid=(n // tm,),
        in_specs=[row(D_MODEL), row(RET_WIDTH), row(NSA_WIDTH), row(PLE_DIM), const(w_out.shape), const((1, D_MODEL)),
                  const(w_gate.shape), const(w_ple.shape), const((1, D_MODEL))],
        out_specs=row(D_MODEL),
        out_shape=jax.ShapeDtypeStruct((n, D_MODEL), F32),
        compiler_params=_params(("arbitrary",)),
        name="finish",
    )(x2d, y_ret, y_nsa, p2d, w_out, norm_ple, w_gate, w_ple, norm_f)


def _slab_perm():
    return np.array([(k * NSA_GROUP + g) * NSA_HD + d for g in range(NSA_GROUP) for k in range(KV_HEADS)
                     for d in range(NSA_HD)], np.int32)


def _to_positions_major(x_t, kv_batch_shape):
    lead = x_t.shape[:-2]
    n = len(lead)
    x4 = x_t.reshape(lead + (KV_HEADS, NSA_HD, x_t.shape[-1]))
    return jnp.transpose(x4, tuple(range(n)) + (n + 2, n, n + 1))


def _cache_t(cache):
    n_pool, page = cache.shape[:2]
    return jnp.transpose(cache, (0, 2, 3, 1)).reshape(n_pool, KV_WIDTH, page)


def _layer(xp, xs, c_ck, c_cv, c_sk, c_sv, win_k, win_v, ret_state, page_table, p_p, p_s, norm_mix, w_in, gn_g, gn_b,
           pe_k, w1_k, w2_k, pe_v, w1_v, w2_v, w_out, norm_ple, w_gate, w_ple, norm_f):
    b, s, d = xp.shape
    db, t, _ = xs.shape
    n_pages = page_table.shape[1]
    past = n_pages * PAGE_SIZE

    off = np.cumsum((0,) + SPLIT_SIZES)
    col = lambda i: w_in[:, off[i]:off[i + 1]]
    perm = _slab_perm()
    ngl_w = jnp.pad(col(11), ((0, 0), (0, LANES - SPLIT_SIZES[11])))
    w_rm = jnp.concatenate([col(0), col(1), col(4)[:, perm], col(2), col(3), col(12)[:, perm], ngl_w], axis=1).astype(BF16)
    w_kvt = jnp.concatenate([col(5), col(6), col(8), col(10), col(7), col(9)], axis=1).T.astype(BF16)
    w_out_p = jnp.concatenate([w_out[:RET_WIDTH], w_out[RET_WIDTH:][perm]], axis=0).astype(BF16)
    g_mix = norm_mix.reshape(1, d)
    cwk = _compress_weights(pe_k, w1_k, w2_k)
    cwv = _compress_weights(pe_v, w1_v, w2_v)
    gn_g2, gn_b2 = gn_g.reshape(1, RET_WIDTH), gn_b.reshape(1, RET_WIDTH)

    tm = 512
    (q_ret, k_ret, v_ret, rg, nq, nqr, ng, ngl, ck_t, cv_t, sv_t, wv_t, sk_t, wk_t) = _proj(
        xp.reshape(b * s, d), jnp.arange(s), g_mix, w_rm, w_kvt, b, tm)
    y_ret, ret_p = _retention(q_ret, k_ret, v_ret, rg, jnp.zeros((b, RET_HEADS, RET_DK, RET_DK), F32), gn_g2, gn_b2,
                              b, s // RET_CHUNK, RET_CHUNK)
    kc, vc = _compress_prompt(ck_t, cv_t, cwk, cwv)
    y_nsa = _attn_prompt(nq, nqr, ngl, ng, kc, vc, sk_t, sv_t, wk_t, wv_t)
    y_prompt = _finish(xp.reshape(b * s, d), y_ret, y_nsa, p_p.reshape(b * s, PLE_DIM), w_out_p,
                       norm_ple.reshape(1, d), w_gate.astype(BF16), w_ple.astype(BF16), norm_f.reshape(1, d), tm)
    wb_p = min(WINDOW, s)
    prompt_states = (ret_p, _to_positions_major(ck_t, b), _to_positions_major(cv_t, b), _to_positions_major(sk_t, b),
                     _to_positions_major(sv_t, b), _to_positions_major(wk_t[:, :, s - wb_p:], b),
                     _to_positions_major(wv_t[:, :, s - wb_p:], b))

    n = db * t
    pos_s = jnp.tile(past + jnp.arange(t), db)
    (q_ret, k_ret, v_ret, rg, nq, nqr, ng, ngl, ck_n, cv_n, sv_n, wv_n, sk_n, wk_n) = _proj(
        xs.reshape(n, d), pos_s, g_mix, w_rm, w_kvt, 1, n)
    pad = lambda a: jnp.pad(a.reshape(db, t, RET_WIDTH), ((0, 0), (0, RET_CHUNK - t), (0, 0))).reshape(db * RET_CHUNK, RET_WIDTH)
    y_ret_pad, ret_s = _retention(pad(q_ret), pad(k_ret), pad(v_ret), pad(rg), ret_state, gn_g2, gn_b2, db, 1, t)
    y_ret = y_ret_pad.reshape(db, RET_CHUNK, RET_WIDTH)[:, :t].reshape(n, RET_WIDTH)
    kc, vc = _compress_sample(page_table, _cache_t(c_ck), _cache_t(c_cv), cwk, cwv)
    per_batch = lambda a: jnp.transpose(a[0].reshape(KV_WIDTH, db, t), (1, 0, 2))
    lane_pad = lambda a: jnp.pad(per_batch(a), ((0, 0), (0, 0), (0, LANES - t)))
    win_t = lambda w: jnp.transpose(w, (0, 2, 3, 1)).reshape(db, KV_WIDTH, w.shape[1])
    rows3 = lambda a: a.astype(F32).reshape(db, t, a.shape[-1])
    y_nsa = _attn_sample(page_table, _cache_t(c_sk), _cache_t(c_sv), rows3(nq), rows3(nqr), rows3(ngl), rows3(ng), kc, vc,
                         lane_pad(sk_n), lane_pad(sv_n), win_t(win_k), win_t(win_v), lane_pad(wk_n), lane_pad(wv_n), t)
    y_nsa = y_nsa.reshape(n, NSA_WIDTH)
    y_sample = _finish(xs.reshape(n, d), y_ret, y_nsa, p_s.reshape(n, PLE_DIM), w_out_p,
                       norm_ple.reshape(1, d), w_gate.astype(BF16), w_ple.astype(BF16), norm_f.reshape(1, d), n)
    new_rows = lambda a: _to_positions_major(per_batch(a), db)
    sample_states = (ret_s, new_rows(ck_n), new_rows(cv_n), new_rows(sk_n), new_rows(sv_n),
                     jnp.concatenate([win_k[:, t:], new_rows(wk_n)], axis=1),
                     jnp.concatenate([win_v[:, t:], new_rows(wv_n)], axis=1))
    return y_prompt.reshape(b, s, d), y_sample.reshape(db, t, d), prompt_states, sample_states


def kernel(x_prompt, x_sample, cache_cmp_k, cache_cmp_v, cache_slc_k, cache_slc_v, state_win_k, state_win_v, state_ret, page_table, p_prompt, p_sample, norm_mix, w_in, ret_gn_g, ret_gn_b, cmp_pe_k, cmp_w1_k, cmp_w2_k, cmp_pe_v, cmp_w1_v, cmp_w2_v, w_out, norm_ple, w_ple_gate, w_ple, norm_f):
    depth = w_in.shape[0]
    assert depth == 1, "single trunk layer"
    l = 0
    yp, ys, sp, ss = _layer(x_prompt, x_sample, cache_cmp_k[l], cache_cmp_v[l], cache_slc_k[l], cache_slc_v[l],
                            state_win_k[l], state_win_v[l], state_ret[l], page_table, p_prompt[l], p_sample[l],
                            norm_mix[l], w_in[l], ret_gn_g[l], ret_gn_b[l], cmp_pe_k[l], cmp_w1_k[l], cmp_w2_k[l],
                            cmp_pe_v[l], cmp_w1_v[l], cmp_w2_v[l], w_out[l], norm_ple[l], w_ple_gate[l], w_ple[l], norm_f)
    return (yp, ys) + tuple(a[None] for a in sp) + tuple(a[None] for a in ss)
```

```python
import functools

import numpy as np
import jax
import jax.numpy as jnp
from jax import lax
from jax.experimental import pallas as pl
from jax.experimental.pallas import tpu as pltpu

F32 = jnp.float32
BF16 = jnp.bfloat16

D_MODEL = 1024
PLE_DIM = 256
PAGE_SIZE = 128
RET_HEADS = 4
RET_DK = 128
RET_WIDTH = 512
RET_CHUNK = 128
NSA_HEADS = 8
KV_HEADS = 2
NSA_GROUP = 4
NSA_HD = 64
NSA_WIDTH = 512
KV_WIDTH = 128
CMP_BLOCK = 32
SLC_BLOCK = 64
TOP_N = 16
WINDOW = 512
Q_BLOCK = 128
ROPE_THETA = 10000.0
RMS_EPS = 1e-6
GN_EPS = 1e-5
NEG_INF = -1e9
FORCED_SCORE = 1e4
M_INIT = -1e30
CAP_OPEN = 3e38
ATTN_SCALE = NSA_HD ** -0.5
SPLIT_SIZES = (512, 512, 512, 512, 512, 128, 128, 128, 128, 128, 128, 24, 512)
N_GATES = 3 * NSA_HEADS

LANES = 128
SUBLANES = 8
SLC_KEY_TILE = 512
SAMPLE_PAGES_PER_STEP = 8
COMPRESS_PAGES_PER_STEP = 16
BLOCK_PITCH = 40
VMEM_LIMIT = 56 * 1024 * 1024


def _dot(a, b):
    return jnp.dot(a, b, preferred_element_type=F32)


def _dot_nt(a, b):
    return lax.dot_general(a, b, (((1,), (1,)), ((), ())), preferred_element_type=F32)


def _sigmoid(x):
    return 1.0 / (1.0 + jnp.exp(-x))


def _params(sem):
    return pltpu.CompilerParams(dimension_semantics=sem, vmem_limit_bytes=VMEM_LIMIT)


def _rope_lanes(a, cos, sin_signed, half):
    if 2 * half == LANES:
        partner = pltpu.roll(a, half, 1)
    else:
        lane = lax.broadcasted_iota(jnp.int32, (1, LANES), 1)
        partner = jnp.where((lane % (2 * half)) < half, pltpu.roll(a, LANES - half, 1), pltpu.roll(a, half, 1))
    return a * cos + partner * sin_signed


def _rope_sublanes(blk, cos_t, sin_t):
    outs = []
    for h in range(blk.shape[0] // NSA_HD):
        x1 = blk[h * NSA_HD:h * NSA_HD + NSA_HD // 2]
        x2 = blk[h * NSA_HD + NSA_HD // 2:(h + 1) * NSA_HD]
        outs += [x1 * cos_t - x2 * sin_t, x2 * cos_t + x1 * sin_t]
    return jnp.concatenate(outs, axis=0)


def _proj_kernel(*refs, prompt):
    (x_ref, g_ref, wrm_ref, wt_ref, cosr_ref, sinr_ref, cosn_ref, sinn_ref, cost_ref, sint_ref,
     qret_ref, kret_ref, vret_ref, rg_ref) = refs[:14]
    x = x_ref[...]
    ms = jnp.mean(x * x, axis=-1, keepdims=True)
    h = (x * lax.rsqrt(ms + RMS_EPS) * g_ref[...]).astype(BF16)

    def proj(lo, hi):
        return _dot(h, wrm_ref[:, lo:hi])

    cr, sr = cosr_ref[...], sinr_ref[...]
    cn, sn = cosn_ref[...], sinn_ref[...]
    ct, st = cost_ref[...], sint_ref[...]
    yq = proj(0, 512)
    yk = proj(512, 1024)
    for hh in range(RET_HEADS):
        sl = slice(hh * LANES, (hh + 1) * LANES)
        qret_ref[:, sl] = _rope_lanes(yq[:, sl], cr, sr, RET_DK // 2)
        kret_ref[:, sl] = _rope_lanes(yk[:, sl], cr, sr, RET_DK // 2) * (RET_DK ** -0.5)
    vret_ref[...] = proj(1024, 1536)
    rg_ref[...] = proj(1536, 2048)

    yt = _dot_nt(wt_ref[...], h)
    if prompt:
        (skr_ref, wkr_ref, ck_ref, cv_ref, sv_ref, wv_ref, sk_ref, wk_ref, svb_ref, wvb_ref,
         nqt_ref, nqrt_ref, ngt_ref, nglt_ref) = refs[14:]
    else:
        nq_ref, nqr_ref, ng_ref, ngl_ref, ck_ref, cv_ref, sv_ref, wv_ref, sk_ref, wk_ref = refs[14:]
    ck_ref[0] = yt[0:128]
    cv_ref[0] = yt[128:256]
    sv_ref[0] = yt[256:384]
    wv_ref[0] = yt[384:512]
    sk_ref[0] = _rope_sublanes(yt[512:640], ct, st)
    wk_ref[0] = _rope_sublanes(yt[640:768], ct, st)
    if prompt:
        svb_ref[0] = yt[256:384].astype(BF16)
        wvb_ref[0] = yt[384:512].astype(BF16)
        nqt = yt[768:1280]
        nqt_ref[0] = nqt.astype(BF16)
        nqrt_ref[0] = _rope_sublanes(nqt, ct, st).astype(BF16)
        ngt_ref[0] = yt[1280:1792]
        nglt_ref[0] = yt[1792:1920]
        skr_ref[...] = _rope_lanes(proj(2048, 2176), cn, sn, NSA_HD // 2).astype(BF16)
        wkr_ref[...] = _rope_lanes(proj(2176, 2304), cn, sn, NSA_HD // 2).astype(BF16)
    else:
        yn = proj(2048, 2560)
        for gg in range(NSA_GROUP):
            sl = slice(gg * LANES, (gg + 1) * LANES)
            nq_ref[:, sl] = yn[:, sl].astype(BF16)
            nqr_ref[:, sl] = _rope_lanes(yn[:, sl], cn, sn, NSA_HD // 2).astype(BF16)
        ng_ref[...] = proj(2560, 3072)
        ngl_ref[...] = proj(3072, 3200)


def _rope_tables(pos, half):
    inv = ROPE_THETA ** (-jnp.arange(half, dtype=F32) / half)
    ang = pos.astype(F32)[:, None] * inv[None, :]
    return jnp.cos(ang), jnp.sin(ang)


def _proj(x2d, pos_rows, norm_g, w_rm, w_t, kv_batch, tm, prompt):
    n = x2d.shape[0]
    nt = pos_rows.shape[0] // tm
    skv = n // kv_batch
    nkt = skv // tm
    c64, s64 = _rope_tables(pos_rows, 64)
    c32, s32 = _rope_tables(pos_rows, 32)
    cosr = jnp.concatenate([c64, c64], axis=1)
    sinr = jnp.concatenate([-s64, s64], axis=1)
    cosn = jnp.concatenate([c32, c32, c32, c32], axis=1)
    sinn = jnp.concatenate([-s32, s32, -s32, s32], axis=1)
    cost, sint = c32.T, s32.T
    row = lambda w: pl.BlockSpec((tm, w), lambda i: (i, 0))
    tab = pl.BlockSpec((tm, LANES), lambda i: (i % nt, 0))
    tabt = pl.BlockSpec((32, tm), lambda i: (0, i % nt))
    tspec = lambda r: pl.BlockSpec((1, r, tm), lambda i: (i // nkt, 0, i % nkt))
    f = lambda w, dt: jax.ShapeDtypeStruct((n, w), dt)
    ts = lambda r, dt: jax.ShapeDtypeStruct((kv_batch, r, skv), dt)
    kvo, kvs = tspec(KV_WIDTH), ts(KV_WIDTH, F32)
    out_specs = [row(512)] * 4
    out_shape = [f(512, F32)] * 4
    if prompt:
        out_specs += [row(KV_WIDTH)] * 2 + [kvo] * 8 + [tspec(512)] * 3 + [tspec(LANES)]
        out_shape += [f(KV_WIDTH, BF16)] * 2 + [kvs] * 6 + [ts(KV_WIDTH, BF16)] * 2 + [ts(512, BF16)] * 2 + [
            ts(512, F32), ts(LANES, F32)]
    else:
        out_specs += [row(512)] * 3 + [row(LANES)] + [kvo] * 6
        out_shape += [f(512, BF16)] * 2 + [f(512, F32), f(LANES, F32)] + [kvs] * 6
    return pl.pallas_call(
        functools.partial(_proj_kernel, prompt=prompt),
        grid=(n // tm,),
        in_specs=[row(D_MODEL), pl.BlockSpec((1, D_MODEL), lambda i: (0, 0)),
                  pl.BlockSpec(w_rm.shape, lambda i: (0, 0)), pl.BlockSpec(w_t.shape, lambda i: (0, 0)),
                  tab, tab, tab, tab, tabt, tabt],
        out_specs=out_specs,
        out_shape=out_shape,
        compiler_params=_params(("arbitrary",)),
        name="proj_prompt" if prompt else "proj_sample",
    )(x2d, norm_g, w_rm, w_t, cosr, sinr, cosn, sinn, cost, sint)


def _ret_kernel(q_ref, k_ref, v_ref, rg_ref, st_ref, dm_ref, qd_ref, kd_ref, sd_ref, gg_ref, gb_ref,
                y_ref, so_ref, st_scr, *, n_chunks):
    c = pl.program_id(1)

    @pl.when(c == 0)
    def _():
        st_scr[...] = st_ref[0]

    for h in range(RET_HEADS):
        sl = slice(h * LANES, (h + 1) * LANES)
        q = q_ref[:, sl]
        k = k_ref[:, sl]
        v = v_ref[:, sl].astype(BF16)
        state = st_scr[h]
        s = _dot_nt(q.astype(BF16), k.astype(BF16)) * dm_ref[h]
        intra = _dot(s.astype(BF16), v)
        cross = _dot((q * qd_ref[:, sl]).astype(BF16), state.astype(BF16))
        o = intra + cross
        kd_t = (k * kd_ref[:, sl]).T.astype(BF16)
        st_scr[h] = sd_ref[h] * state + _dot(kd_t, v)
        mu = jnp.mean(o, axis=-1, keepdims=True)
        var = jnp.mean(jnp.square(o - mu), axis=-1, keepdims=True)
        on = (o - mu) * lax.rsqrt(var + GN_EPS)
        gate = rg_ref[:, sl]
        y_ref[:, sl] = (on * gg_ref[:, sl] + gb_ref[:, sl]) * (gate * _sigmoid(gate))

    @pl.when(c == n_chunks - 1)
    def _():
        so_ref[0] = st_scr[...]


def _retention(q, k, v, rg, state, gn_g, gn_b, n_batch, n_chunks, chunk_len):
    c = RET_CHUNK
    log_g = jnp.log(1.0 - 2.0 ** (-5.0 - jnp.arange(RET_HEADS, dtype=F32)))
    i = jnp.arange(c, dtype=F32)
    diff = i[:, None] - i[None, :]
    causal = diff >= 0
    dmask = jnp.where(causal[None], jnp.exp(jnp.where(causal, diff, 0.0)[None] * log_g[:, None, None]), 0.0)
    expand = lambda t: jnp.repeat(t, LANES, axis=1)
    qdec = expand(jnp.exp((i[:, None] + 1.0) * log_g[None, :]))
    kdec = expand(jnp.exp((chunk_len - 1.0 - i)[:, None] * log_g[None, :]))
    sdec = jnp.broadcast_to(jnp.exp(chunk_len * log_g)[:, None, None], (RET_HEADS, 1, LANES))
    row = pl.BlockSpec((c, 512), lambda b, j: (b * n_chunks + j, 0))
    stspec = pl.BlockSpec((1, RET_HEADS, 128, 128), lambda b, j: (b, 0, 0, 0))
    const = lambda shape: pl.BlockSpec(shape, lambda b, j: (0,) * len(shape))
    return pl.pallas_call(
        functools.partial(_ret_kernel, n_chunks=n_chunks),
        grid=(n_batch, n_chunks),
        in_specs=[row, row, row, row, stspec, const((RET_HEADS, c, c)), const((c, 512)), const((c, 512)),
                  const((RET_HEADS, 1, LANES)), const((1, 512)), const((1, 512))],
        out_specs=[row, stspec],
        out_shape=[jax.ShapeDtypeStruct(q.shape, F32), jax.ShapeDtypeStruct(state.shape, F32)],
        scratch_shapes=[pltpu.VMEM((RET_HEADS, 128, 128), F32)],
        compiler_params=_params(("arbitrary", "arbitrary")),
        name="retention",
    )(q, k, v, rg, state, dmask, qdec, kdec, sdec, gn_g, gn_b)


def _gelu_tanh(x):
    return 0.5 * x * (1.0 + jnp.tanh(np.sqrt(2.0 / np.pi).astype(np.float32) * (x + 0.044715 * (x * x * x))))


def _compress_one(tiles, pe_ref, w1_ref, w2_ref, out_ref, scr, transpose_out=False):
    per_tile = LANES // CMP_BLOCK
    for t, tile in enumerate(tiles):
        rows_pm = tile.T
        for c in range(per_tile):
            r0 = (t * per_tile + c) * BLOCK_PITCH
            scr[r0:r0 + CMP_BLOCK, :] = rows_pm[c * CMP_BLOCK:(c + 1) * CMP_BLOCK]
    n_blk = len(tiles) * per_tile
    flat = [(scr[pl.ds(j, n_blk, stride=BLOCK_PITCH), :] + pe_ref[j:j + 1, :]).astype(BF16) for j in range(CMP_BLOCK)]
    hid = _gelu_tanh(_dot(jnp.concatenate(flat, axis=1), w1_ref[...]))
    if transpose_out:
        out_ref[0] = _dot_nt(w2_ref[...], hid.astype(BF16))
    else:
        out_ref[0] = _dot(hid.astype(BF16), w2_ref[...])


def _compress_prompt_kernel(k_ref, v_ref, pek_ref, w1k_ref, w2k_ref, pev_ref, w1v_ref, w2v_ref,
                            kc_ref, vct_ref, scr_k, scr_v, *, n_tiles):
    for src, pe, w1, w2, dst, scr, tr in ((k_ref, pek_ref, w1k_ref, w2k_ref, kc_ref, scr_k, False),
                                          (v_ref, pev_ref, w1v_ref, w2v_ref, vct_ref, scr_v, True)):
        tiles = [src[0, :, t * LANES:(t + 1) * LANES] for t in range(n_tiles)]
        _compress_one(tiles, pe, w1, w2, dst, scr, tr)


def _compress_sample_kernel(pt_ref, *refs, n_pages):
    k_pages = refs[:n_pages]
    v_pages = refs[n_pages:2 * n_pages]
    pek_ref, w1k_ref, w2k_ref, pev_ref, w1v_ref, w2v_ref, kc_ref, vc_ref, scr_k, scr_v = refs[2 * n_pages:]
    _compress_one([r[0] for r in k_pages], pek_ref, w1k_ref, w2k_ref, kc_ref, scr_k)
    _compress_one([r[0] for r in v_pages], pev_ref, w1v_ref, w2v_ref, vc_ref, scr_v)


def _compress_weights(pe, w1, w2, transpose_w2=False):
    pe_rows = pe.reshape(CMP_BLOCK, KV_WIDTH)
    z = jnp.zeros((CMP_BLOCK, NSA_HD, NSA_HD), F32)
    w1bd = jnp.concatenate([jnp.concatenate([w1[0], z], axis=2), jnp.concatenate([z, w1[1]], axis=2)], axis=1)
    z2 = jnp.zeros((NSA_HD, NSA_HD), F32)
    w2bd = jnp.concatenate([jnp.concatenate([w2[0], z2], axis=1), jnp.concatenate([z2, w2[1]], axis=1)], axis=0)
    if transpose_w2:
        w2bd = w2bd.T
    return pe_rows, w1bd.reshape(CMP_BLOCK * KV_WIDTH, KV_WIDTH).astype(BF16), w2bd.astype(BF16)


def _cw_specs():
    zero = lambda *a: (0, 0)
    return [pl.BlockSpec((CMP_BLOCK, KV_WIDTH), zero), pl.BlockSpec((CMP_BLOCK * KV_WIDTH, KV_WIDTH), zero),
            pl.BlockSpec((KV_WIDTH, KV_WIDTH), zero)]


def _compress_prompt(ck_t, cv_t, cwk, cwv_t):
    b, _, s = ck_t.shape
    n_blk = s // CMP_BLOCK
    src = pl.BlockSpec((1, KV_WIDTH, s), lambda i: (i, 0, 0))
    return pl.pallas_call(
        functools.partial(_compress_prompt_kernel, n_tiles=s // LANES),
        grid=(b,),
        in_specs=[src, src] + _cw_specs() + _cw_specs(),
        out_specs=[pl.BlockSpec((1, n_blk, KV_WIDTH), lambda i: (i, 0, 0)),
                   pl.BlockSpec((1, KV_WIDTH, n_blk), lambda i: (i, 0, 0))],
        out_shape=[jax.ShapeDtypeStruct((b, n_blk, KV_WIDTH), F32), jax.ShapeDtypeStruct((b, KV_WIDTH, n_blk), F32)],
        scratch_shapes=[pltpu.VMEM((n_blk * BLOCK_PITCH, KV_WIDTH), F32)] * 2,
        compiler_params=_params(("arbitrary",)),
        name="compress_prompt",
    )(ck_t, cv_t, *cwk, *cwv_t)


def _compress_sample(page_table, cache_k_t, cache_v_t, cwk, cwv):
    db, n_pages = page_table.shape
    g = COMPRESS_PAGES_PER_STEP
    steps = n_pages // g
    blk_per_step = g * PAGE_SIZE // CMP_BLOCK
    n_blk = n_pages * PAGE_SIZE // CMP_BLOCK

    def page_spec(i):
        return pl.BlockSpec((1, KV_WIDTH, PAGE_SIZE), lambda b, s, pt: (pt[b, s * g + i], 0, 0))

    dst = pl.BlockSpec((1, blk_per_step, KV_WIDTH), lambda b, s, pt: (b, s, 0))
    shp = jax.ShapeDtypeStruct((db, n_blk, KV_WIDTH), F32)
    grid_spec = pltpu.PrefetchScalarGridSpec(
        num_scalar_prefetch=1,
        grid=(db, steps),
        in_specs=[page_spec(i % g) for i in range(2 * g)] + _cw_specs() + _cw_specs(),
        out_specs=[dst, dst],
        scratch_shapes=[pltpu.VMEM((blk_per_step * BLOCK_PITCH, KV_WIDTH), F32)] * 2,
    )
    return pl.pallas_call(
        functools.partial(_compress_sample_kernel, n_pages=g),
        grid_spec=grid_spec,
        out_shape=[shp, shp],
        compiler_params=_params(("arbitrary", "arbitrary")),
        name="compress_sample",
    )(page_table, *([cache_k_t] * g), *([cache_v_t] * g), *cwk, *cwv)


def _stack_heads_t(qt_ref):
    nq = qt_ref.shape[2]
    zero = jnp.zeros((NSA_HD, nq), BF16)
    cols = []
    for k in range(KV_HEADS):
        for g in range(NSA_GROUP):
            h = k * NSA_GROUP + g
            tile = qt_ref[0, h * NSA_HD:(h + 1) * NSA_HD, :] * ATTN_SCALE
            cols.append(jnp.concatenate([tile, zero] if k == 0 else [zero, tile], axis=0))
    return jnp.concatenate(cols, axis=1)


def _rank_select_rows(score, n_blk):
    parts = [score[v * SUBLANES:(v + 1) * SUBLANES] for v in range(n_blk // SUBLANES)]
    ranks = [jnp.zeros(p.shape, jnp.int32) for p in parts]
    for j in range(n_blk):
        col = score[j:j + 1, :]
        for v, part in enumerate(parts):
            ge, gt = (col >= part).astype(jnp.int32), (col > part).astype(jnp.int32)
            if v * SUBLANES > j:
                beats = ge
            elif (v + 1) * SUBLANES - 1 <= j:
                beats = gt
            else:
                row = v * SUBLANES + lax.broadcasted_iota(jnp.int32, (SUBLANES, 1), 0)
                beats = jnp.where(row > j, ge, gt)
            ranks[v] = ranks[v] + beats
    return jnp.concatenate([(r < TOP_N).astype(F32) for r in ranks], axis=0)


def _softmax_tile_t(s, v_t, m_ref, l_ref, acc_ref):
    half = s.shape[1] // KV_HEADS
    m_old = m_ref[...]
    m_new = jnp.maximum(m_old, jnp.max(s, axis=0, keepdims=True))
    alpha = jnp.exp(m_old - m_new)
    p = jnp.exp(s - m_new)
    l_ref[...] = alpha * l_ref[...] + jnp.sum(p, axis=0, keepdims=True)
    m_ref[...] = m_new
    pb = p.astype(BF16)
    pv = jnp.concatenate([_dot(v_t, pb[:, :half]), _dot(v_t, pb[:, half:])], axis=1)
    acc_ref[...] = alpha * acc_ref[...] + pv


def _attn_prompt_kernel(nqt_ref, nqrt_ref, nglt_ref, ngt_ref, kc_ref, vct_ref, sk_ref, svt_ref, wk_ref, wvt_ref,
                        et_ref, y_ref, lhs_scr, lhsr_scr, ocmp_scr, t_scr, sel_scr, m_scr, l_scr, acc_scr):
    i = pl.program_id(1)
    qb = Q_BLOCK
    q0 = i * qb
    cols8 = 2 * NSA_GROUP * qb
    qpos = q0 + lax.broadcasted_iota(jnp.int32, (1, qb), 1)
    lhs_scr[...] = _stack_heads_t(nqt_ref)
    lhsr_scr[...] = _stack_heads_t(nqrt_ref)

    n_cmp = kc_ref.shape[1]
    s_all = _dot(kc_ref[0].astype(BF16), lhs_scr[...])
    cend = (lax.broadcasted_iota(jnp.int32, (n_cmp, 1), 0) + 1) * CMP_BLOCK - 1
    cmask = cend <= qpos
    vct = vct_ref[0].astype(BF16)
    imps = []
    for k in range(KV_HEADS):
        imp = None
        for g in range(NSA_GROUP):
            c0 = (k * NSA_GROUP + g) * qb
            s = jnp.where(cmask, s_all[:, c0:c0 + qb], NEG_INF)
            p = jnp.exp(s - jnp.max(s, axis=0, keepdims=True))
            p = p / jnp.sum(p, axis=0, keepdims=True) * cmask.astype(F32)
            imp = p if imp is None else imp + p
            ocmp_scr[:, c0:c0 + qb] = _dot(vct, p.astype(BF16))
        imps.append(imp)

    @pl.when(q0 + qb <= TOP_N * SLC_BLOCK)
    def _():
        sel_scr[...] = jnp.ones(sel_scr.shape, BF16)

    @pl.when(q0 + qb > TOP_N * SLC_BLOCK)
    def _():
        n_slc = n_cmp // 2
        blk = lax.broadcasted_iota(jnp.int32, (n_slc, 1), 0)
        valid = blk * SLC_BLOCK <= qpos
        forced = (blk == 0) | (blk == qpos // SLC_BLOCK)
        for k in range(KV_HEADS):
            t_scr[...] = imps[k]
            pair = t_scr[pl.ds(0, n_slc, stride=2), :] + t_scr[pl.ds(1, n_slc, stride=2), :]
            score = jnp.where(forced, FORCED_SCORE, jnp.where(valid, pair, NEG_INF))
            sel = _rank_select_rows(score, n_slc)
            sel_scr[k] = jnp.concatenate([sel, jnp.zeros((LANES - n_slc, qb), F32)], axis=0).astype(BF16)

    tk = SLC_KEY_TILE
    n_tiles = (q0 + qb - 1) // tk + 1
    m_scr[...] = jnp.full(m_scr.shape, M_INIT, F32)
    l_scr[...] = jnp.zeros(l_scr.shape, F32)
    acc_scr[...] = jnp.zeros(acc_scr.shape, F32)

    def slc_tile(t, carry):
        k0 = pl.multiple_of(t * tk, tk)
        vis = (k0 + lax.broadcasted_iota(jnp.int32, (tk, 1), 0)) <= qpos
        et = et_ref[pl.ds(k0, tk), :]
        caps = [jnp.where((_dot(et, sel_scr[k]) > 0.5) & vis, CAP_OPEN, NEG_INF) for k in range(KV_HEADS)]
        cap = jnp.concatenate([caps[0]] * NSA_GROUP + [caps[1]] * NSA_GROUP, axis=1)
        s = jnp.minimum(_dot(sk_ref[pl.ds(k0, tk), :], lhsr_scr[...]), cap)
        _softmax_tile_t(s, svt_ref[0, :, pl.ds(k0, tk)], m_scr.at[0], l_scr.at[0], acc_scr.at[0])
        return carry

    lax.fori_loop(0, n_tiles, slc_tile, 0)

    wlen = WINDOW + qb
    w0 = pl.multiple_of(jnp.maximum(i - WINDOW // qb, 0) * qb, qb)
    dist = qpos - (w0 + lax.broadcasted_iota(jnp.int32, (wlen, 1), 0))
    wcap = jnp.where((dist >= 0) & (dist <= WINDOW), CAP_OPEN, NEG_INF)
    s = jnp.minimum(_dot(wk_ref[pl.ds(w0, wlen), :], lhsr_scr[...]), jnp.concatenate([wcap] * (2 * NSA_GROUP), axis=1))
    _softmax_tile_t(s, wvt_ref[0, :, pl.ds(w0, wlen)], m_scr.at[1], l_scr.at[1], acc_scr.at[1])

    o_slc = acc_scr[0] / l_scr[0]
    o_win = acc_scr[1] / l_scr[1]
    sig = _sigmoid(nglt_ref[0])
    for pair in range(NSA_HEADS // 2):
        tiles = []
        for h in (2 * pair, 2 * pair + 1):
            k = h // NSA_GROUP
            rows = slice(k * NSA_HD, (k + 1) * NSA_HD)
            cols = slice(h * qb, (h + 1) * qb)
            gate = lambda r: sig[r * NSA_HEADS + h:r * NSA_HEADS + h + 1, :]
            o = gate(0) * ocmp_scr[rows, cols] + gate(1) * o_slc[rows, cols] + gate(2) * o_win[rows, cols]
            ng = ngt_ref[0, h * NSA_HD:(h + 1) * NSA_HD, :]
            tiles.append(o * (ng * _sigmoid(ng)))
        y_ref[:, pair * LANES:(pair + 1) * LANES] = jnp.concatenate(tiles, axis=0).T.astype(y_ref.dtype)


def _attn_prompt(nq_t, nqr_t, ngl_t, ng_t, kc, vc_t, sk_rm, sv_t, wk_rm, wv_t):
    b, _, s = sv_t.shape
    assert s // CMP_BLOCK == LANES, "one lane per compressed block"
    qb = Q_BLOCK
    nqb = s // qb
    cols8 = 2 * NSA_GROUP * qb
    expand_t = (jnp.arange(s)[:, None] // SLC_BLOCK == jnp.arange(LANES)[None, :]).astype(BF16)
    qcol = lambda r: pl.BlockSpec((1, r, qb), lambda bb, i: (bb, 0, i))
    per_b = lambda shape: pl.BlockSpec((1,) + shape, lambda bb, i: (bb, 0, 0))
    rows_b = pl.BlockSpec((s, KV_WIDTH), lambda bb, i: (bb, 0))
    return pl.pallas_call(
        _attn_prompt_kernel,
        grid=(b, nqb),
        in_specs=[qcol(512), qcol(512), qcol(LANES), qcol(512), per_b(kc.shape[1:]), per_b(vc_t.shape[1:]),
                  rows_b, per_b((KV_WIDTH, s)), rows_b, per_b((KV_WIDTH, s)),
                  pl.BlockSpec(expand_t.shape, lambda bb, i: (0, 0))],
        out_specs=pl.BlockSpec((qb, 512), lambda bb, i: (bb * nqb + i, 0)),
        out_shape=jax.ShapeDtypeStruct((b * s, 512), BF16),
        scratch_shapes=[pltpu.VMEM((KV_WIDTH, cols8), BF16), pltpu.VMEM((KV_WIDTH, cols8), BF16),
                        pltpu.VMEM((KV_WIDTH, cols8), F32), pltpu.VMEM((LANES, qb), F32),
                        pltpu.VMEM((KV_HEADS, LANES, qb), BF16),
                        pltpu.VMEM((2, 1, cols8), F32), pltpu.VMEM((2, 1, cols8), F32),
                        pltpu.VMEM((2, KV_WIDTH, cols8), F32)],
        compiler_params=_params(("arbitrary", "arbitrary")),
        name="attn_prompt",
    )(nq_t, nqr_t, ngl_t, ng_t, kc, vc_t, sk_rm, sv_t, wk_rm, wv_t, expand_t)


def _stack_heads(q_ref):
    lane = lax.broadcasted_iota(jnp.int32, (1, LANES), 1)
    lo = lane < NSA_HD
    slabs = [q_ref[:, g * LANES:(g + 1) * LANES] for g in range(NSA_GROUP)]
    zero = jnp.zeros_like(slabs[0])
    stacked = jnp.concatenate([jnp.where(lo, s, zero) for s in slabs] + [jnp.where(lo, zero, s) for s in slabs], axis=0)
    return stacked.astype(BF16)


def _mask_rows(sc, masks, rows):
    n = sc.shape[-1]
    s4 = sc.reshape(2 * NSA_GROUP, rows, n)
    out = [jnp.where(masks[k][None], s4[k * NSA_GROUP:(k + 1) * NSA_GROUP], NEG_INF) for k in range(KV_HEADS)]
    return jnp.concatenate(out, axis=0).reshape(2 * NSA_GROUP * rows, n)


def _pair_scores(imp, qpos, n_lanes):
    lane = lax.broadcasted_iota(jnp.int32, (1, n_lanes), 1)
    even = (lane % 2) == 0
    ps = imp + jnp.where(even, pltpu.roll(imp, n_lanes - 1, 1), pltpu.roll(imp, 1, 1))
    blk = lane // 2
    valid = blk * SLC_BLOCK <= qpos
    forced = (blk == 0) | (blk == qpos // SLC_BLOCK)
    return jnp.where(forced, FORCED_SCORE, jnp.where(valid, ps, NEG_INF))


def _rank_select(score, n_lanes, extra_forced):
    lane = lax.broadcasted_iota(jnp.int32, (1, n_lanes), 1)
    blk = lane // 2
    rank = jnp.zeros(score.shape, jnp.int32)
    for j in range(0, n_lanes, 2):
        col = score[:, j:j + 1]
        beats = (col > score) | ((col == score) & ((j // 2) < blk))
        rank = rank + beats.astype(jnp.int32)
    if extra_forced:
        rank = rank + (score < FORCED_SCORE).astype(jnp.int32)
    return rank < TOP_N


def _merge_kv_heads(acc, rows):
    lane = lax.broadcasted_iota(jnp.int32, (1, LANES), 1)
    lo = lane < NSA_HD
    half = NSA_GROUP * rows
    return [jnp.where(lo, acc[g * rows:(g + 1) * rows], acc[half + g * rows:half + (g + 1) * rows])
            for g in range(NSA_GROUP)]


def _gate_and_store(o_cmp, o_slc, o_win, ngl_ref, ng_ref, y_ref):
    lane = lax.broadcasted_iota(jnp.int32, (1, LANES), 1)
    lo = lane < NSA_HD
    sig = _sigmoid(ngl_ref[...])
    for g in range(NSA_GROUP):
        gates = [jnp.where(lo, sig[:, r * 8 + g:r * 8 + g + 1], sig[:, r * 8 + 4 + g:r * 8 + 4 + g + 1]) for r in range(3)]
        o = gates[0] * o_cmp[g] + gates[1] * o_slc[g] + gates[2] * o_win[g]
        gate = ng_ref[:, g * LANES:(g + 1) * LANES]
        y_ref[:, g * LANES:(g + 1) * LANES] = (o * (gate * _sigmoid(gate))).astype(y_ref.dtype)


def _attn_sample_kernel(pt_ref, *refs, n_pages, n_steps, t_len, past_len):
    k_pages = refs[:n_pages]
    v_pages = refs[n_pages:2 * n_pages]
    (nq_ref, nqr_ref, ngl_ref, ng_ref, kc_ref, vc_ref, skn_ref, svn_ref, wko_ref, wvo_ref, wkn_ref, wvn_ref, e_ref,
     y_ref, m_scr, l_scr, acc_scr, ocmp_scr, owin_scr, mask_scr) = refs[2 * n_pages:]
    step = pl.program_id(1)
    t = t_len
    nq_ref, nqr_ref, ngl_ref, ng_ref, y_ref = (r.at[0] for r in (nq_ref, nqr_ref, ngl_ref, ng_ref, y_ref))
    rows8 = 2 * NSA_GROUP * t
    half = NSA_GROUP * t
    tpos = lax.broadcasted_iota(jnp.int32, (t, 1), 0)
    qpos = past_len + tpos
    n_cmp = kc_ref.shape[1]
    wb = wko_ref.shape[2]
    step_keys = n_pages * PAGE_SIZE

    @pl.when(step == 0)
    def _():
        lhs = _stack_heads(nq_ref)
        s = _dot_nt(lhs, kc_ref[0].astype(BF16)) * ATTN_SCALE
        m = jnp.max(s, axis=-1, keepdims=True)
        p = jnp.exp(s - m)
        p = p / jnp.sum(p, axis=-1, keepdims=True)
        oc = _dot(p.astype(BF16), vc_ref[0].astype(BF16))
        for g, slab in enumerate(_merge_kv_heads(oc, t)):
            ocmp_scr[g] = slab
        p4 = p.reshape(8, t, n_cmp)
        for k in range(KV_HEADS):
            imp = p4[k * NSA_GROUP]
            for g in range(1, NSA_GROUP):
                imp = imp + p4[k * NSA_GROUP + g]
            sel = _rank_select(_pair_scores(imp, qpos, n_cmp), n_cmp, True).astype(BF16)
            for c in range(n_cmp // LANES):
                km = _dot(sel[:, c * LANES:(c + 1) * LANES], e_ref[...])
                per = LANES * CMP_BLOCK // step_keys
                for u in range(per):
                    mask_scr[c * per + u, k] = km[:, u * step_keys:(u + 1) * step_keys]

        lhs_r = _stack_heads(nqr_ref)
        so = _dot(lhs_r, wko_ref[0].astype(BF16)) * ATTN_SCALE
        sn = _dot(lhs_r, wkn_ref[0].astype(BF16)) * ATTN_SCALE
        jo = lax.broadcasted_iota(jnp.int32, (1, wb), 1)
        jn = lax.broadcasted_iota(jnp.int32, (1, LANES), 1)
        old_vis = jo >= tpos
        so = _mask_rows(so, [old_vis, old_vis], t)
        new_vis = jn <= tpos
        sn_w = _mask_rows(sn, [new_vis, new_vis], t)
        m = jnp.maximum(jnp.max(so, axis=-1, keepdims=True), jnp.max(sn_w, axis=-1, keepdims=True))
        po = jnp.exp(so - m)
        pn = jnp.exp(sn_w - m)
        den = jnp.sum(po, axis=-1, keepdims=True) + jnp.sum(pn, axis=-1, keepdims=True)
        pob, pnb = (po / den).astype(BF16), (pn / den).astype(BF16)
        wvo, wvn = wvo_ref[0].astype(BF16), wvn_ref[0].astype(BF16)
        ow = jnp.concatenate([_dot_nt(pob[:half], wvo) + _dot_nt(pnb[:half], wvn),
                              _dot_nt(pob[half:], wvo) + _dot_nt(pnb[half:], wvn)], axis=0)
        for g, slab in enumerate(_merge_kv_heads(ow, t)):
            owin_scr[g] = slab

        s2 = _dot(lhs_r, skn_ref[0].astype(BF16)) * ATTN_SCALE
        s2 = _mask_rows(s2, [new_vis, new_vis], t)
        m2 = jnp.max(s2, axis=-1, keepdims=True)
        p2 = jnp.exp(s2 - m2)
        m_scr[...] = m2
        l_scr[...] = jnp.sum(p2, axis=-1, keepdims=True)
        p2b = p2.astype(BF16)
        svn = svn_ref[0].astype(BF16)
        acc_scr[...] = jnp.concatenate([_dot_nt(p2b[:half], svn), _dot_nt(p2b[half:], svn)], axis=0)

    lhs_r = _stack_heads(nqr_ref)
    sc = jnp.concatenate([_dot(lhs_r, r[0].astype(BF16)) for r in k_pages], axis=1) * ATTN_SCALE
    msk = mask_scr[step]
    sc = _mask_rows(sc, [msk[k] > 0.5 for k in range(KV_HEADS)], t)
    m_i = m_scr[...]
    m_new = jnp.maximum(m_i, jnp.max(sc, axis=-1, keepdims=True))
    alpha = jnp.exp(m_i - m_new)
    pr = jnp.exp(sc - m_new)
    l_scr[...] = alpha * l_scr[...] + jnp.sum(pr, axis=-1, keepdims=True)
    m_scr[...] = m_new
    pb = pr.astype(BF16)
    pv = jnp.zeros((rows8, LANES), F32)
    for j, r in enumerate(v_pages):
        vt = r[0].astype(BF16)
        pj = pb[:, j * PAGE_SIZE:(j + 1) * PAGE_SIZE]
        pv = pv + jnp.concatenate([_dot_nt(pj[:half], vt), _dot_nt(pj[half:], vt)], axis=0)
    acc_scr[...] = alpha * acc_scr[...] + pv

    @pl.when(step == n_steps - 1)
    def _():
        o_slc = _merge_kv_heads(acc_scr[...] / l_scr[...], t)
        o_cmp = [ocmp_scr[g] for g in range(NSA_GROUP)]
        o_win = [owin_scr[g] for g in range(NSA_GROUP)]
        _gate_and_store(o_cmp, o_slc, o_win, ngl_ref, ng_ref, y_ref)


def _attn_sample(page_table, cache_sk_t, cache_sv_t, nq, nqr, ngl, ng, kc, vc, skn, svn, wko, wvo, wkn, wvn, t_len):
    db, n_pages = page_table.shape
    g = SAMPLE_PAGES_PER_STEP
    steps = n_pages // g
    past_len = n_pages * PAGE_SIZE
    step_keys = g * PAGE_SIZE
    span = LANES * CMP_BLOCK
    expand = (jnp.arange(span)[None, :] // CMP_BLOCK == jnp.arange(LANES)[:, None]).astype(BF16)

    def page_spec(i):
        return pl.BlockSpec((1, KV_WIDTH, PAGE_SIZE), lambda b, s, pt: (pt[b, s * g + i], 0, 0))

    row = lambda w: pl.BlockSpec((1, t_len, w), lambda b, s, pt: (b, 0, 0))
    per_b = lambda shape: pl.BlockSpec((1,) + shape, lambda b, s, pt: (b, 0, 0))
    rows8 = 2 * NSA_GROUP * t_len
    grid_spec = pltpu.PrefetchScalarGridSpec(
        num_scalar_prefetch=1,
        grid=(db, steps),
        in_specs=[page_spec(i % g) for i in range(2 * g)]
        + [row(512), row(512), row(LANES), row(512), per_b(kc.shape[1:]), per_b(vc.shape[1:]),
           per_b(skn.shape[1:]), per_b(svn.shape[1:]), per_b(wko.shape[1:]), per_b(wvo.shape[1:]),
           per_b(wkn.shape[1:]), per_b(wvn.shape[1:]), pl.BlockSpec(expand.shape, lambda b, s, pt: (0, 0))],
        out_specs=row(512),
        scratch_shapes=[pltpu.VMEM((rows8, 1), F32), pltpu.VMEM((rows8, 1), F32), pltpu.VMEM((rows8, LANES), F32),
                        pltpu.VMEM((NSA_GROUP, t_len, LANES), F32), pltpu.VMEM((NSA_GROUP, t_len, LANES), F32),
                        pltpu.VMEM((steps, KV_HEADS, t_len, step_keys), F32)],
    )
    return pl.pallas_call(
        functools.partial(_attn_sample_kernel, n_pages=g, n_steps=steps, t_len=t_len, past_len=past_len),
        grid_spec=grid_spec,
        out_shape=jax.ShapeDtypeStruct((db, t_len, 512), F32),
        compiler_params=_params(("arbitrary", "arbitrary")),
        name="attn_sample",
    )(page_table, *([cache_sk_t] * g), *([cache_sv_t] * g), nq, nqr, ngl, ng, kc, vc, skn, svn, wko, wvo, wkn, wvn, expand)


def _finish_kernel(x_ref, yr_ref, yn_ref, p_ref, wo_ref, gple_ref, wg_ref, wp_ref, gf_ref, o_ref):
    x = x_ref[...]
    x = x + _dot(yr_ref[...].astype(BF16), wo_ref[0:RET_WIDTH, :]) + _dot(yn_ref[...].astype(BF16), wo_ref[RET_WIDTH:, :])
    ms = jnp.mean(x * x, axis=-1, keepdims=True)
    hn = (x * lax.rsqrt(ms + RMS_EPS) * gple_ref[...]).astype(BF16)
    gate = _sigmoid(_dot(hn, wg_ref[...]))
    x = x + gate * _dot(p_ref[...].astype(BF16), wp_ref[...])
    ms = jnp.mean(x * x, axis=-1, keepdims=True)
    o_ref[...] = x * lax.rsqrt(ms + RMS_EPS) * gf_ref[...]


def _finish(x2d, y_ret, y_nsa, p2d, w_out, norm_ple, w_gate, w_ple, norm_f, tm):
    n = x2d.shape[0]
    row = lambda w: pl.BlockSpec((tm, w), lambda i: (i, 0))
    const = lambda shape: pl.BlockSpec(shape, lambda i: (0, 0))
    return pl.pallas_call(
        _finish_kernel,
        grid=(n // tm,),
        in_specs=[row(D_MODEL), row(RET_WIDTH), row(NSA_WIDTH), row(PLE_DIM), const(w_out.shape), const((1, D_MODEL)),
                  const(w_gate.shape), const(w_ple.shape), const((1, D_MODEL))],
        out_specs=row(D_MODEL),
        out_shape=jax.ShapeDtypeStruct((n, D_MODEL), F32),
        compiler_params=_params(("arbitrary",)),
        name="finish",
    )(x2d, y_ret, y_nsa, p2d, w_out, norm_ple, w_gate, w_ple, norm_f)


def _slab_perm():
    return np.array([(k * NSA_GROUP + g) * NSA_HD + d for g in range(NSA_GROUP) for k in range(KV_HEADS)
                     for d in range(NSA_HD)], np.int32)


def _to_positions_major(x_t):
    lead = x_t.shape[:-2]
    n = len(lead)
    x4 = x_t.reshape(lead + (KV_HEADS, NSA_HD, x_t.shape[-1]))
    return jnp.transpose(x4, tuple(range(n)) + (n + 2, n, n + 1))


def _cache_t(cache):
    n_pool, page = cache.shape[:2]
    return jnp.transpose(cache, (0, 2, 3, 1)).reshape(n_pool, KV_WIDTH, page)


def _layer(xp, xs, c_ck, c_cv, c_sk, c_sv, win_k, win_v, ret_state, page_table, p_p, p_s, norm_mix, w_in, gn_g, gn_b,
           pe_k, w1_k, w2_k, pe_v, w1_v, w2_v, w_out, norm_ple, w_gate, w_ple, norm_f):
    b, s, d = xp.shape
    db, t, _ = xs.shape
    n_pages = page_table.shape[1]
    past = n_pages * PAGE_SIZE

    off = np.cumsum((0,) + SPLIT_SIZES)
    col = lambda i: w_in[:, off[i]:off[i + 1]]
    perm = _slab_perm()
    ngl_w = jnp.pad(col(11), ((0, 0), (0, LANES - N_GATES)))
    ret_cols = [col(0), col(1), col(2), col(3)]
    kv_cols = [col(5), col(6), col(8), col(10), col(7), col(9)]
    w_rm_p = jnp.concatenate(ret_cols + [col(7), col(9)], axis=1).astype(BF16)
    w_t_p = jnp.concatenate(kv_cols + [col(4), col(12), ngl_w], axis=1).T.astype(BF16)
    w_rm_s = jnp.concatenate(ret_cols + [col(4)[:, perm], col(12)[:, perm], ngl_w], axis=1).astype(BF16)
    w_t_s = jnp.concatenate(kv_cols, axis=1).T.astype(BF16)
    w_out_b = w_out.astype(BF16)
    w_out_slab = jnp.concatenate([w_out[:RET_WIDTH], w_out[RET_WIDTH:][perm]], axis=0).astype(BF16)
    g_mix = norm_mix.reshape(1, d)
    cwk = _compress_weights(pe_k, w1_k, w2_k)
    cwv = _compress_weights(pe_v, w1_v, w2_v)
    cwv_t = _compress_weights(pe_v, w1_v, w2_v, transpose_w2=True)
    gn_g2, gn_b2 = gn_g.reshape(1, RET_WIDTH), gn_b.reshape(1, RET_WIDTH)
    fin_w = (norm_ple.reshape(1, d), w_gate.astype(BF16), w_ple.astype(BF16), norm_f.reshape(1, d))

    tm = 512
    (q_ret, k_ret, v_ret, rg, sk_rm, wk_rm, ck_t, cv_t, sv_t, wv_t, sk_t, wk_t, sv_b, wv_b, nq_t, nqr_t, ng_t,
     ngl_t) = _proj(xp.reshape(b * s, d), jnp.arange(s), g_mix, w_rm_p, w_t_p, b, tm, True)
    y_ret, ret_p = _retention(q_ret, k_ret, v_ret, rg, jnp.zeros((b, RET_HEADS, RET_DK, RET_DK), F32), gn_g2, gn_b2,
                              b, s // RET_CHUNK, RET_CHUNK)
    kc, vc_t = _compress_prompt(ck_t, cv_t, cwk, cwv_t)
    y_nsa = _attn_prompt(nq_t, nqr_t, ngl_t, ng_t, kc, vc_t, sk_rm, sv_b, wk_rm, wv_b)
    y_prompt = _finish(xp.reshape(b * s, d), y_ret, y_nsa, p_p.reshape(b * s, PLE_DIM), w_out_b, *fin_w, tm)
    wb_p = min(WINDOW, s)
    prompt_states = (ret_p, _to_positions_major(ck_t), _to_positions_major(cv_t), _to_positions_major(sk_t),
                     _to_positions_major(sv_t), _to_positions_major(wk_t[:, :, s - wb_p:]),
                     _to_positions_major(wv_t[:, :, s - wb_p:]))

    n = db * t
    pos_s = jnp.tile(past + jnp.arange(t), db)
    (q_ret, k_ret, v_ret, rg, nq, nqr, ng, ngl, ck_n, cv_n, sv_n, wv_n, sk_n, wk_n) = _proj(
        xs.reshape(n, d), pos_s, g_mix, w_rm_s, w_t_s, 1, n, False)
    pad = lambda a: jnp.pad(a.reshape(db, t, RET_WIDTH), ((0, 0), (0, RET_CHUNK - t), (0, 0))).reshape(db * RET_CHUNK, RET_WIDTH)
    y_ret_pad, ret_s = _retention(pad(q_ret), pad(k_ret), pad(v_ret), pad(rg), ret_state, gn_g2, gn_b2, db, 1, t)
    y_ret = y_ret_pad.reshape(db, RET_CHUNK, RET_WIDTH)[:, :t].reshape(n, RET_WIDTH)
    kc, vc = _compress_sample(page_table, _cache_t(c_ck), _cache_t(c_cv), cwk, cwv)
    per_batch = lambda a: jnp.transpose(a[0].reshape(KV_WIDTH, db, t), (1, 0, 2))
    lane_pad = lambda a: jnp.pad(per_batch(a), ((0, 0), (0, 0), (0, LANES - t)))
    win_t = lambda w: jnp.transpose(w, (0, 2, 3, 1)).reshape(db, KV_WIDTH, w.shape[1])
    rows3 = lambda a: a.astype(F32).reshape(db, t, a.shape[-1])
    y_nsa = _attn_sample(page_table, _cache_t(c_sk), _cache_t(c_sv), rows3(nq), rows3(nqr), rows3(ngl), rows3(ng), kc, vc,
                         lane_pad(sk_n), lane_pad(sv_n), win_t(win_k), win_t(win_v), lane_pad(wk_n), lane_pad(wv_n), t)
    y_nsa = y_nsa.reshape(n, NSA_WIDTH)
    y_sample = _finish(xs.reshape(n, d), y_ret, y_nsa, p_s.reshape(n, PLE_DIM), w_out_slab, *fin_w, n)
    new_rows = lambda a: _to_positions_major(per_batch(a))
    sample_states = (ret_s, new_rows(ck_n), new_rows(cv_n), new_rows(sk_n), new_rows(sv_n),
                     jnp.concatenate([win_k[:, t:], new_rows(wk_n)], axis=1),
                     jnp.concatenate([win_v[:, t:], new_rows(wv_n)], axis=1))
    return y_prompt.reshape(b, s, d), y_sample.reshape(db, t, d), prompt_states, sample_states


def kernel(x_prompt, x_sample, cache_cmp_k, cache_cmp_v, cache_slc_k, cache_slc_v, state_win_k, state_win_v, state_ret, page_table, p_prompt, p_sample, norm_mix, w_in, ret_gn_g, ret_gn_b, cmp_pe_k, cmp_w1_k, cmp_w2_k, cmp_pe_v, cmp_w1_v, cmp_w2_v, w_out, norm_ple, w_ple_gate, w_ple, norm_f):
    depth = w_in.shape[0]
    assert depth == 1, "single trunk layer"
    l = 0
    yp, ys, sp, ss = _layer(x_prompt, x_sample, cache_cmp_k[l], cache_cmp_v[l], cache_slc_k[l], cache_slc_v[l],
                            state_win_k[l], state_win_v[l], state_ret[l], page_table, p_prompt[l], p_sample[l],
                            norm_mix[l], w_in[l], ret_gn_g[l], ret_gn_b[l], cmp_pe_k[l], cmp_w1_k[l], cmp_w2_k[l],
                            cmp_pe_v[l], cmp_w1_v[l], cmp_w2_v[l], w_out[l], norm_ple[l], w_ple_gate[l], w_ple[l], norm_f)
    return (yp, ys) + tuple(a[None] for a in sp) + tuple(a[None] for a in ss)
```

```python
import functools

import numpy as np
import jax
import jax.numpy as jnp
from jax import lax
from jax.experimental import pallas as pl
from jax.experimental.pallas import tpu as pltpu

F32 = jnp.float32
BF16 = jnp.bfloat16

D_MODEL = 1024
PLE_DIM = 256
PAGE_SIZE = 128
RET_HEADS = 4
RET_DK = 128
RET_WIDTH = 512
RET_CHUNK = 128
NSA_HEADS = 8
KV_HEADS = 2
NSA_GROUP = 4
NSA_HD = 64
NSA_WIDTH = 512
KV_WIDTH = 128
CMP_BLOCK = 32
SLC_BLOCK = 64
TOP_N = 16
WINDOW = 512
Q_BLOCK = 128
ROPE_THETA = 10000.0
RMS_EPS = 1e-6
GN_EPS = 1e-5
NEG_INF = -1e9
FORCED_SCORE = 1e4
M_INIT = -1e30
CAP_OPEN = 3e38
ATTN_SCALE = NSA_HD ** -0.5
SPLIT_SIZES = (512, 512, 512, 512, 512, 128, 128, 128, 128, 128, 128, 24, 512)
N_GATES = 3 * NSA_HEADS

LANES = 128
SUBLANES = 8
SLC_KEY_TILE = 512
SAMPLE_PAGES_PER_STEP = 16
COMPRESS_PAGES_PER_STEP = 32
BLOCK_PITCH = 40
VMEM_LIMIT = 56 * 1024 * 1024


def _dot(a, b):
    return jnp.dot(a, b, preferred_element_type=F32)


def _dot_nt(a, b):
    return lax.dot_general(a, b, (((1,), (1,)), ((), ())), preferred_element_type=F32)


def _sigmoid(x):
    return 1.0 / (1.0 + jnp.exp(-x))


def _params(sem):
    return pltpu.CompilerParams(dimension_semantics=sem, vmem_limit_bytes=VMEM_LIMIT)


def _rope_lanes(a, cos, sin_signed, half):
    if 2 * half == LANES:
        partner = pltpu.roll(a, half, 1)
    else:
        lane = lax.broadcasted_iota(jnp.int32, (1, LANES), 1)
        partner = jnp.where((lane % (2 * half)) < half, pltpu.roll(a, LANES - half, 1), pltpu.roll(a, half, 1))
    return a * cos + partner * sin_signed


def _rope_sublanes(blk, cos_t, sin_t):
    outs = []
    for h in range(blk.shape[0] // NSA_HD):
        x1 = blk[h * NSA_HD:h * NSA_HD + NSA_HD // 2]
        x2 = blk[h * NSA_HD + NSA_HD // 2:(h + 1) * NSA_HD]
        outs += [x1 * cos_t - x2 * sin_t, x2 * cos_t + x1 * sin_t]
    return jnp.concatenate(outs, axis=0)


def _proj_kernel(*refs, prompt):
    (x_ref, g_ref, wrm_ref, wt_ref, cosr_ref, sinr_ref, cosn_ref, sinn_ref, cost_ref, sint_ref,
     qret_ref, kret_ref, vret_ref, rg_ref) = refs[:14]
    x = x_ref[...]
    ms = jnp.mean(x * x, axis=-1, keepdims=True)
    h = (x * lax.rsqrt(ms + RMS_EPS) * g_ref[...]).astype(BF16)

    def proj(lo, hi):
        return _dot(h, wrm_ref[:, lo:hi])

    cr, sr = cosr_ref[...], sinr_ref[...]
    cn, sn = cosn_ref[...], sinn_ref[...]
    ct, st = cost_ref[...], sint_ref[...]
    yq = proj(0, 512)
    yk = proj(512, 1024)
    for hh in range(RET_HEADS):
        sl = slice(hh * LANES, (hh + 1) * LANES)
        qret_ref[:, sl] = _rope_lanes(yq[:, sl], cr, sr, RET_DK // 2)
        kret_ref[:, sl] = _rope_lanes(yk[:, sl], cr, sr, RET_DK // 2) * (RET_DK ** -0.5)
    vret_ref[...] = proj(1024, 1536)
    rg_ref[...] = proj(1536, 2048)

    yt = _dot_nt(wt_ref[...], h)
    if prompt:
        (skr_ref, wkr_ref, ck_ref, cv_ref, sv_ref, wv_ref, sk_ref, wk_ref, svb_ref, wvb_ref,
         nqt_ref, nqrt_ref, ngt_ref, nglt_ref) = refs[14:]
    else:
        nq_ref, nqr_ref, ng_ref, ngl_ref, ck_ref, cv_ref, sv_ref, wv_ref, sk_ref, wk_ref = refs[14:]
    ck_ref[0] = yt[0:128]
    cv_ref[0] = yt[128:256]
    sv_ref[0] = yt[256:384]
    wv_ref[0] = yt[384:512]
    sk_ref[0] = _rope_sublanes(yt[512:640], ct, st)
    wk_ref[0] = _rope_sublanes(yt[640:768], ct, st)
    if prompt:
        svb_ref[0] = yt[256:384].astype(BF16)
        wvb_ref[0] = yt[384:512].astype(BF16)
        nqt = yt[768:1280]
        nqt_ref[0] = nqt.astype(BF16)
        nqrt_ref[0] = _rope_sublanes(nqt, ct, st).astype(BF16)
        ngt_ref[0] = yt[1280:1792]
        nglt_ref[0] = yt[1792:1920]
        skr_ref[...] = _rope_lanes(proj(2048, 2176), cn, sn, NSA_HD // 2).astype(BF16)
        wkr_ref[...] = _rope_lanes(proj(2176, 2304), cn, sn, NSA_HD // 2).astype(BF16)
    else:
        yn = proj(2048, 2560)
        for gg in range(NSA_GROUP):
            sl = slice(gg * LANES, (gg + 1) * LANES)
            nq_ref[:, sl] = yn[:, sl].astype(BF16)
            nqr_ref[:, sl] = _rope_lanes(yn[:, sl], cn, sn, NSA_HD // 2).astype(BF16)
        ng_ref[...] = proj(2560, 3072)
        ngl_ref[...] = proj(3072, 3200)


def _rope_tables(pos, half):
    inv = ROPE_THETA ** (-jnp.arange(half, dtype=F32) / half)
    ang = pos.astype(F32)[:, None] * inv[None, :]
    return jnp.cos(ang), jnp.sin(ang)


def _proj(x2d, pos_rows, norm_g, w_rm, w_t, kv_batch, tm, prompt):
    n = x2d.shape[0]
    nt = pos_rows.shape[0] // tm
    skv = n // kv_batch
    nkt = skv // tm
    c64, s64 = _rope_tables(pos_rows, 64)
    c32, s32 = _rope_tables(pos_rows, 32)
    cosr = jnp.concatenate([c64, c64], axis=1)
    sinr = jnp.concatenate([-s64, s64], axis=1)
    cosn = jnp.concatenate([c32, c32, c32, c32], axis=1)
    sinn = jnp.concatenate([-s32, s32, -s32, s32], axis=1)
    cost, sint = c32.T, s32.T
    row = lambda w: pl.BlockSpec((tm, w), lambda i: (i, 0))
    tab = pl.BlockSpec((tm, LANES), lambda i: (i % nt, 0))
    tabt = pl.BlockSpec((32, tm), lambda i: (0, i % nt))
    tspec = lambda r: pl.BlockSpec((1, r, tm), lambda i: (i // nkt, 0, i % nkt))
    f = lambda w, dt: jax.ShapeDtypeStruct((n, w), dt)
    ts = lambda r, dt: jax.ShapeDtypeStruct((kv_batch, r, skv), dt)
    kvo, kvs = tspec(KV_WIDTH), ts(KV_WIDTH, F32)
    out_specs = [row(512)] * 4
    out_shape = [f(512, F32)] * 4
    if prompt:
        out_specs += [row(KV_WIDTH)] * 2 + [kvo] * 8 + [tspec(512)] * 3 + [tspec(LANES)]
        out_shape += [f(KV_WIDTH, BF16)] * 2 + [kvs] * 6 + [ts(KV_WIDTH, BF16)] * 2 + [ts(512, BF16)] * 2 + [
            ts(512, F32), ts(LANES, F32)]
    else:
        out_specs += [row(512)] * 3 + [row(LANES)] + [kvo] * 6
        out_shape += [f(512, BF16)] * 2 + [f(512, F32), f(LANES, F32)] + [kvs] * 6
    return pl.pallas_call(
        functools.partial(_proj_kernel, prompt=prompt),
        grid=(n // tm,),
        in_specs=[row(D_MODEL), pl.BlockSpec((1, D_MODEL), lambda i: (0, 0)),
                  pl.BlockSpec(w_rm.shape, lambda i: (0, 0)), pl.BlockSpec(w_t.shape, lambda i: (0, 0)),
                  tab, tab, tab, tab, tabt, tabt],
        out_specs=out_specs,
        out_shape=out_shape,
        compiler_params=_params(("arbitrary",)),
        name="proj_prompt" if prompt else "proj_sample",
    )(x2d, norm_g, w_rm, w_t, cosr, sinr, cosn, sinn, cost, sint)


def _ret_kernel(q_ref, k_ref, v_ref, rg_ref, st_ref, dm_ref, qd_ref, kd_ref, sd_ref, gg_ref, gb_ref,
                y_ref, so_ref, st_scr, *, n_chunks):
    c = pl.program_id(1)

    @pl.when(c == 0)
    def _():
        st_scr[...] = st_ref[0]

    for h in range(RET_HEADS):
        sl = slice(h * LANES, (h + 1) * LANES)
        q = q_ref[:, sl]
        k = k_ref[:, sl]
        v = v_ref[:, sl].astype(BF16)
        state = st_scr[h]
        s = _dot_nt(q.astype(BF16), k.astype(BF16)) * dm_ref[h]
        intra = _dot(s.astype(BF16), v)
        cross = _dot((q * qd_ref[:, sl]).astype(BF16), state.astype(BF16))
        o = intra + cross
        kd_t = (k * kd_ref[:, sl]).T.astype(BF16)
        st_scr[h] = sd_ref[h] * state + _dot(kd_t, v)
        mu = jnp.mean(o, axis=-1, keepdims=True)
        var = jnp.mean(jnp.square(o - mu), axis=-1, keepdims=True)
        on = (o - mu) * lax.rsqrt(var + GN_EPS)
        gate = rg_ref[:, sl]
        y_ref[:, sl] = (on * gg_ref[:, sl] + gb_ref[:, sl]) * (gate * _sigmoid(gate))

    @pl.when(c == n_chunks - 1)
    def _():
        so_ref[0] = st_scr[...]


def _retention(q, k, v, rg, state, gn_g, gn_b, n_batch, n_chunks, chunk_len):
    c = RET_CHUNK
    log_g = jnp.log(1.0 - 2.0 ** (-5.0 - jnp.arange(RET_HEADS, dtype=F32)))
    i = jnp.arange(c, dtype=F32)
    diff = i[:, None] - i[None, :]
    causal = diff >= 0
    dmask = jnp.where(causal[None], jnp.exp(jnp.where(causal, diff, 0.0)[None] * log_g[:, None, None]), 0.0)
    expand = lambda t: jnp.repeat(t, LANES, axis=1)
    qdec = expand(jnp.exp((i[:, None] + 1.0) * log_g[None, :]))
    kdec = expand(jnp.exp((chunk_len - 1.0 - i)[:, None] * log_g[None, :]))
    sdec = jnp.broadcast_to(jnp.exp(chunk_len * log_g)[:, None, None], (RET_HEADS, 1, LANES))
    row = pl.BlockSpec((c, 512), lambda b, j: (b * n_chunks + j, 0))
    stspec = pl.BlockSpec((1, RET_HEADS, 128, 128), lambda b, j: (b, 0, 0, 0))
    const = lambda shape: pl.BlockSpec(shape, lambda b, j: (0,) * len(shape))
    return pl.pallas_call(
        functools.partial(_ret_kernel, n_chunks=n_chunks),
        grid=(n_batch, n_chunks),
        in_specs=[row, row, row, row, stspec, const((RET_HEADS, c, c)), const((c, 512)), const((c, 512)),
                  const((RET_HEADS, 1, LANES)), const((1, 512)), const((1, 512))],
        out_specs=[row, stspec],
        out_shape=[jax.ShapeDtypeStruct(q.shape, F32), jax.ShapeDtypeStruct(state.shape, F32)],
        scratch_shapes=[pltpu.VMEM((RET_HEADS, 128, 128), F32)],
        compiler_params=_params(("arbitrary", "arbitrary")),
        name="retention",
    )(q, k, v, rg, state, dmask, qdec, kdec, sdec, gn_g, gn_b)


def _gelu_tanh(x):
    return 0.5 * x * (1.0 + jnp.tanh(np.sqrt(2.0 / np.pi).astype(np.float32) * (x + 0.044715 * (x * x * x))))


def _compress_one(tiles, pe_ref, w1_ref, w2_ref, out_ref, scr, transpose_out=False):
    per_tile = LANES // CMP_BLOCK
    for t, tile in enumerate(tiles):
        rows_pm = tile.T
        for c in range(per_tile):
            r0 = (t * per_tile + c) * BLOCK_PITCH
            scr[r0:r0 + CMP_BLOCK, :] = rows_pm[c * CMP_BLOCK:(c + 1) * CMP_BLOCK]
    n_blk = len(tiles) * per_tile
    flat = [(scr[pl.ds(j, n_blk, stride=BLOCK_PITCH), :] + pe_ref[j:j + 1, :]).astype(BF16) for j in range(CMP_BLOCK)]
    hid = _gelu_tanh(_dot(jnp.concatenate(flat, axis=1), w1_ref[...]))
    if transpose_out:
        out_ref[0] = _dot_nt(w2_ref[...], hid.astype(BF16))
    else:
        out_ref[0] = _dot(hid.astype(BF16), w2_ref[...])


def _compress_prompt_kernel(k_ref, v_ref, pek_ref, w1k_ref, w2k_ref, pev_ref, w1v_ref, w2v_ref,
                            kc_ref, vct_ref, scr_k, scr_v, *, n_tiles):
    for src, pe, w1, w2, dst, scr, tr in ((k_ref, pek_ref, w1k_ref, w2k_ref, kc_ref, scr_k, False),
                                          (v_ref, pev_ref, w1v_ref, w2v_ref, vct_ref, scr_v, True)):
        tiles = [src[0, :, t * LANES:(t + 1) * LANES] for t in range(n_tiles)]
        _compress_one(tiles, pe, w1, w2, dst, scr, tr)


def _compress_sample_kernel(pt_ref, *refs, n_pages):
    k_pages = refs[:n_pages]
    v_pages = refs[n_pages:2 * n_pages]
    pek_ref, w1k_ref, w2k_ref, pev_ref, w1v_ref, w2v_ref, kc_ref, vc_ref, scr_k, scr_v = refs[2 * n_pages:]
    _compress_one([r[0] for r in k_pages], pek_ref, w1k_ref, w2k_ref, kc_ref, scr_k)
    _compress_one([r[0] for r in v_pages], pev_ref, w1v_ref, w2v_ref, vc_ref, scr_v)


def _compress_weights(pe, w1, w2, transpose_w2=False):
    pe_rows = pe.reshape(CMP_BLOCK, KV_WIDTH)
    z = jnp.zeros((CMP_BLOCK, NSA_HD, NSA_HD), F32)
    w1bd = jnp.concatenate([jnp.concatenate([w1[0], z], axis=2), jnp.concatenate([z, w1[1]], axis=2)], axis=1)
    z2 = jnp.zeros((NSA_HD, NSA_HD), F32)
    w2bd = jnp.concatenate([jnp.concatenate([w2[0], z2], axis=1), jnp.concatenate([z2, w2[1]], axis=1)], axis=0)
    if transpose_w2:
        w2bd = w2bd.T
    return pe_rows, w1bd.reshape(CMP_BLOCK * KV_WIDTH, KV_WIDTH).astype(BF16), w2bd.astype(BF16)


def _cw_specs():
    zero = lambda *a: (0, 0)
    return [pl.BlockSpec((CMP_BLOCK, KV_WIDTH), zero), pl.BlockSpec((CMP_BLOCK * KV_WIDTH, KV_WIDTH), zero),
            pl.BlockSpec((KV_WIDTH, KV_WIDTH), zero)]


def _compress_prompt(ck_t, cv_t, cwk, cwv_t):
    b, _, s = ck_t.shape
    n_blk = s // CMP_BLOCK
    src = pl.BlockSpec((1, KV_WIDTH, s), lambda i: (i, 0, 0))
    return pl.pallas_call(
        functools.partial(_compress_prompt_kernel, n_tiles=s // LANES),
        grid=(b,),
        in_specs=[src, src] + _cw_specs() + _cw_specs(),
        out_specs=[pl.BlockSpec((1, n_blk, KV_WIDTH), lambda i: (i, 0, 0)),
                   pl.BlockSpec((1, KV_WIDTH, n_blk), lambda i: (i, 0, 0))],
        out_shape=[jax.ShapeDtypeStruct((b, n_blk, KV_WIDTH), F32), jax.ShapeDtypeStruct((b, KV_WIDTH, n_blk), F32)],
        scratch_shapes=[pltpu.VMEM((n_blk * BLOCK_PITCH, KV_WIDTH), F32)] * 2,
        compiler_params=_params(("arbitrary",)),
        name="compress_prompt",
    )(ck_t, cv_t, *cwk, *cwv_t)


def _compress_sample(page_table, cache_k_t, cache_v_t, cwk, cwv):
    db, n_pages = page_table.shape
    g = COMPRESS_PAGES_PER_STEP
    steps = n_pages // g
    blk_per_step = g * PAGE_SIZE // CMP_BLOCK
    n_blk = n_pages * PAGE_SIZE // CMP_BLOCK

    def page_spec(i):
        return pl.BlockSpec((1, KV_WIDTH, PAGE_SIZE), lambda b, s, pt: (pt[b, s * g + i], 0, 0))

    dst = pl.BlockSpec((1, blk_per_step, KV_WIDTH), lambda b, s, pt: (b, s, 0))
    shp = jax.ShapeDtypeStruct((db, n_blk, KV_WIDTH), F32)
    grid_spec = pltpu.PrefetchScalarGridSpec(
        num_scalar_prefetch=1,
        grid=(db, steps),
        in_specs=[page_spec(i % g) for i in range(2 * g)] + _cw_specs() + _cw_specs(),
        out_specs=[dst, dst],
        scratch_shapes=[pltpu.VMEM((blk_per_step * BLOCK_PITCH, KV_WIDTH), F32)] * 2,
    )
    return pl.pallas_call(
        functools.partial(_compress_sample_kernel, n_pages=g),
        grid_spec=grid_spec,
        out_shape=[shp, shp],
        compiler_params=_params(("arbitrary", "arbitrary")),
        name="compress_sample",
    )(page_table, *([cache_k_t] * g), *([cache_v_t] * g), *cwk, *cwv)


def _stack_heads_t(qt_ref):
    nq = qt_ref.shape[2]
    zero = jnp.zeros((NSA_HD, nq), BF16)
    cols = []
    for k in range(KV_HEADS):
        for g in range(NSA_GROUP):
            h = k * NSA_GROUP + g
            tile = qt_ref[0, h * NSA_HD:(h + 1) * NSA_HD, :] * ATTN_SCALE
            cols.append(jnp.concatenate([tile, zero] if k == 0 else [zero, tile], axis=0))
    return jnp.concatenate(cols, axis=1)


def _rank_select_rows(score, n_blk):
    parts = [score[v * SUBLANES:(v + 1) * SUBLANES] for v in range(n_blk // SUBLANES)]
    ranks = [jnp.zeros(p.shape, jnp.int32) for p in parts]
    for j in range(n_blk):
        col = score[j:j + 1, :]
        for v, part in enumerate(parts):
            ge, gt = (col >= part).astype(jnp.int32), (col > part).astype(jnp.int32)
            if v * SUBLANES > j:
                beats = ge
            elif (v + 1) * SUBLANES - 1 <= j:
                beats = gt
            else:
                row = v * SUBLANES + lax.broadcasted_iota(jnp.int32, (SUBLANES, 1), 0)
                beats = jnp.where(row > j, ge, gt)
            ranks[v] = ranks[v] + beats
    return jnp.concatenate([(r < TOP_N).astype(F32) for r in ranks], axis=0)


def _softmax_tile_t(s, v_t, m_ref, l_ref, acc_ref, s_max=None):
    half = s.shape[1] // KV_HEADS
    m_old = m_ref[...]
    m_new = jnp.maximum(m_old, jnp.max(s, axis=0, keepdims=True) if s_max is None else s_max)
    alpha = jnp.exp(m_old - m_new)
    p = jnp.exp(s - m_new)
    l_ref[...] = alpha * l_ref[...] + jnp.sum(p, axis=0, keepdims=True)
    m_ref[...] = m_new
    pb = p.astype(BF16)
    pv = jnp.concatenate([_dot(v_t, pb[:, :half]), _dot(v_t, pb[:, half:])], axis=1)
    acc_ref[...] = alpha * acc_ref[...] + pv


def _attn_prompt_kernel(nqt_ref, nqrt_ref, nglt_ref, ngt_ref, kc_ref, vct_ref, sk_ref, svt_ref, wk_ref, wvt_ref,
                        et_ref, y_ref, lhs_scr, lhsr_scr, ocmp_scr, t_scr, sel_scr, m_scr, l_scr, acc_scr,
                        s_scr, smax_scr):
    i = pl.program_id(1)
    qb = Q_BLOCK
    q0 = i * qb
    cols8 = 2 * NSA_GROUP * qb
    qpos = q0 + lax.broadcasted_iota(jnp.int32, (1, qb), 1)
    lhs_scr[...] = _stack_heads_t(nqt_ref)
    lhsr_scr[...] = _stack_heads_t(nqrt_ref)

    n_cmp = kc_ref.shape[1]
    s_all = _dot(kc_ref[0].astype(BF16), lhs_scr[...])
    cend = (lax.broadcasted_iota(jnp.int32, (n_cmp, 1), 0) + 1) * CMP_BLOCK - 1
    cmask = cend <= qpos
    vct = vct_ref[0].astype(BF16)
    imps = []
    for k in range(KV_HEADS):
        imp = None
        for g in range(NSA_GROUP):
            c0 = (k * NSA_GROUP + g) * qb
            s = jnp.where(cmask, s_all[:, c0:c0 + qb], NEG_INF)
            p = jnp.exp(s - jnp.max(s, axis=0, keepdims=True))
            p = p / jnp.sum(p, axis=0, keepdims=True) * cmask.astype(F32)
            imp = p if imp is None else imp + p
            ocmp_scr[:, c0:c0 + qb] = _dot(vct, p.astype(BF16))
        imps.append(imp)

    @pl.when(q0 + qb <= TOP_N * SLC_BLOCK)
    def _():
        sel_scr[...] = jnp.ones(sel_scr.shape, BF16)

    @pl.when(q0 + qb > TOP_N * SLC_BLOCK)
    def _():
        n_slc = n_cmp // 2
        blk = lax.broadcasted_iota(jnp.int32, (n_slc, 1), 0)
        valid = blk * SLC_BLOCK <= qpos
        forced = (blk == 0) | (blk == qpos // SLC_BLOCK)
        for k in range(KV_HEADS):
            t_scr[...] = imps[k]
            pair = t_scr[pl.ds(0, n_slc, stride=2), :] + t_scr[pl.ds(1, n_slc, stride=2), :]
            score = jnp.where(forced, FORCED_SCORE, jnp.where(valid, pair, NEG_INF))
            sel = _rank_select_rows(score, n_slc)
            sel_scr[k] = jnp.concatenate([sel, jnp.zeros((LANES - n_slc, qb), F32)], axis=0).astype(BF16)

    tk = SLC_KEY_TILE
    n_tiles = (q0 + qb - 1) // tk + 1
    m_scr[...] = jnp.full(m_scr.shape, M_INIT, F32)
    l_scr[...] = jnp.zeros(l_scr.shape, F32)
    acc_scr[...] = jnp.zeros(acc_scr.shape, F32)

    def scores(t):
        k0 = pl.multiple_of(t * tk, tk)
        vis = (k0 + lax.broadcasted_iota(jnp.int32, (tk, 1), 0)) <= qpos
        et = et_ref[pl.ds(k0, tk), :]
        caps = [jnp.where((_dot(et, sel_scr[k]) > 0.5) & vis, CAP_OPEN, NEG_INF) for k in range(KV_HEADS)]
        cap = jnp.concatenate([caps[0]] * NSA_GROUP + [caps[1]] * NSA_GROUP, axis=1)
        s = jnp.minimum(_dot(sk_ref[pl.ds(k0, tk), :], lhsr_scr[...]), cap)
        s_scr[t % 2] = s
        smax_scr[t % 2] = jnp.max(s, axis=0, keepdims=True)

    def update(t):
        k0 = pl.multiple_of(t * tk, tk)
        _softmax_tile_t(s_scr[t % 2], svt_ref[0, :, pl.ds(k0, tk)], m_scr.at[0], l_scr.at[0], acc_scr.at[0],
                        smax_scr[t % 2])

    def slc_step(t, carry):
        scores(t + 1)
        update(t)
        return carry

    scores(0)
    lax.fori_loop(0, n_tiles - 1, slc_step, 0)
    update(n_tiles - 1)

    wlen = WINDOW + qb
    w0 = pl.multiple_of(jnp.maximum(i - WINDOW // qb, 0) * qb, qb)
    dist = qpos - (w0 + lax.broadcasted_iota(jnp.int32, (wlen, 1), 0))
    wcap = jnp.where((dist >= 0) & (dist <= WINDOW), CAP_OPEN, NEG_INF)
    s = jnp.minimum(_dot(wk_ref[pl.ds(w0, wlen), :], lhsr_scr[...]), jnp.concatenate([wcap] * (2 * NSA_GROUP), axis=1))
    _softmax_tile_t(s, wvt_ref[0, :, pl.ds(w0, wlen)], m_scr.at[1], l_scr.at[1], acc_scr.at[1])

    o_slc = acc_scr[0] / l_scr[0]
    o_win = acc_scr[1] / l_scr[1]
    sig = _sigmoid(nglt_ref[0])
    for pair in range(NSA_HEADS // 2):
        tiles = []
        for h in (2 * pair, 2 * pair + 1):
            k = h // NSA_GROUP
            rows = slice(k * NSA_HD, (k + 1) * NSA_HD)
            cols = slice(h * qb, (h + 1) * qb)
            gate = lambda r: sig[r * NSA_HEADS + h:r * NSA_HEADS + h + 1, :]
            o = gate(0) * ocmp_scr[rows, cols] + gate(1) * o_slc[rows, cols] + gate(2) * o_win[rows, cols]
            ng = ngt_ref[0, h * NSA_HD:(h + 1) * NSA_HD, :]
            tiles.append(o * (ng * _sigmoid(ng)))
        y_ref[:, pair * LANES:(pair + 1) * LANES] = jnp.concatenate(tiles, axis=0).T.astype(y_ref.dtype)


def _attn_prompt(nq_t, nqr_t, ngl_t, ng_t, kc, vc_t, sk_rm, sv_t, wk_rm, wv_t):
    b, _, s = sv_t.shape
    assert s // CMP_BLOCK == LANES, "one lane per compressed block"
    qb = Q_BLOCK
    nqb = s // qb
    cols8 = 2 * NSA_GROUP * qb
    expand_t = (jnp.arange(s)[:, None] // SLC_BLOCK == jnp.arange(LANES)[None, :]).astype(BF16)
    qcol = lambda r: pl.BlockSpec((1, r, qb), lambda bb, i: (bb, 0, i))
    per_b = lambda shape: pl.BlockSpec((1,) + shape, lambda bb, i: (bb, 0, 0))
    rows_b = pl.BlockSpec((s, KV_WIDTH), lambda bb, i: (bb, 0))
    return pl.pallas_call(
        _attn_prompt_kernel,
        grid=(b, nqb),
        in_specs=[qcol(512), qcol(512), qcol(LANES), qcol(512), per_b(kc.shape[1:]), per_b(vc_t.shape[1:]),
                  rows_b, per_b((KV_WIDTH, s)), rows_b, per_b((KV_WIDTH, s)),
                  pl.BlockSpec(expand_t.shape, lambda bb, i: (0, 0))],
        out_specs=pl.BlockSpec((qb, 512), lambda bb, i: (bb * nqb + i, 0)),
        out_shape=jax.ShapeDtypeStruct((b * s, 512), BF16),
        scratch_shapes=[pltpu.VMEM((KV_WIDTH, cols8), BF16), pltpu.VMEM((KV_WIDTH, cols8), BF16),
                        pltpu.VMEM((KV_WIDTH, cols8), F32), pltpu.VMEM((LANES, qb), F32),
                        pltpu.VMEM((KV_HEADS, LANES, qb), BF16),
                        pltpu.VMEM((2, 1, cols8), F32), pltpu.VMEM((2, 1, cols8), F32),
                        pltpu.VMEM((2, KV_WIDTH, cols8), F32),
                        pltpu.VMEM((2, SLC_KEY_TILE, cols8), F32), pltpu.VMEM((2, 1, cols8), F32)],
        compiler_params=_params(("arbitrary", "arbitrary")),
        name="attn_prompt",
    )(nq_t, nqr_t, ngl_t, ng_t, kc, vc_t, sk_rm, sv_t, wk_rm, wv_t, expand_t)


def _stack_heads(q_ref):
    lane = lax.broadcasted_iota(jnp.int32, (1, LANES), 1)
    lo = lane < NSA_HD
    slabs = [q_ref[:, g * LANES:(g + 1) * LANES] for g in range(NSA_GROUP)]
    zero = jnp.zeros_like(slabs[0])
    stacked = jnp.concatenate([jnp.where(lo, s, zero) for s in slabs] + [jnp.where(lo, zero, s) for s in slabs], axis=0)
    return stacked.astype(BF16)


def _mask_rows(sc, masks, rows):
    n = sc.shape[-1]
    s4 = sc.reshape(2 * NSA_GROUP, rows, n)
    out = [jnp.where(masks[k][None], s4[k * NSA_GROUP:(k + 1) * NSA_GROUP], NEG_INF) for k in range(KV_HEADS)]
    return jnp.concatenate(out, axis=0).reshape(2 * NSA_GROUP * rows, n)


def _pair_scores(imp, qpos, n_lanes):
    lane = lax.broadcasted_iota(jnp.int32, (1, n_lanes), 1)
    even = (lane % 2) == 0
    ps = imp + jnp.where(even, pltpu.roll(imp, n_lanes - 1, 1), pltpu.roll(imp, 1, 1))
    blk = lane // 2
    valid = blk * SLC_BLOCK <= qpos
    forced = (blk == 0) | (blk == qpos // SLC_BLOCK)
    return jnp.where(forced, FORCED_SCORE, jnp.where(valid, ps, NEG_INF))


def _rank_select(score, n_lanes, extra_forced):
    lane = lax.broadcasted_iota(jnp.int32, (1, n_lanes), 1)
    blk = lane // 2
    rank = jnp.zeros(score.shape, jnp.int32)
    for j in range(0, n_lanes, 2):
        col = score[:, j:j + 1]
        beats = (col > score) | ((col == score) & ((j // 2) < blk))
        rank = rank + beats.astype(jnp.int32)
    if extra_forced:
        rank = rank + (score < FORCED_SCORE).astype(jnp.int32)
    return rank < TOP_N


def _merge_kv_heads(acc, rows):
    lane = lax.broadcasted_iota(jnp.int32, (1, LANES), 1)
    lo = lane < NSA_HD
    half = NSA_GROUP * rows
    return [jnp.where(lo, acc[g * rows:(g + 1) * rows], acc[half + g * rows:half + (g + 1) * rows])
            for g in range(NSA_GROUP)]


def _gate_and_store(o_cmp, o_slc, o_win, ngl_ref, ng_ref, y_ref):
    lane = lax.broadcasted_iota(jnp.int32, (1, LANES), 1)
    lo = lane < NSA_HD
    sig = _sigmoid(ngl_ref[...])
    for g in range(NSA_GROUP):
        gates = [jnp.where(lo, sig[:, r * 8 + g:r * 8 + g + 1], sig[:, r * 8 + 4 + g:r * 8 + 4 + g + 1]) for r in range(3)]
        o = gates[0] * o_cmp[g] + gates[1] * o_slc[g] + gates[2] * o_win[g]
        gate = ng_ref[:, g * LANES:(g + 1) * LANES]
        y_ref[:, g * LANES:(g + 1) * LANES] = (o * (gate * _sigmoid(gate))).astype(y_ref.dtype)


def _attn_sample_kernel(pt_ref, *refs, n_pages, n_steps, t_len, past_len):
    k_pages = refs[:n_pages]
    v_pages = refs[n_pages:2 * n_pages]
    (nq_ref, nqr_ref, ngl_ref, ng_ref, kc_ref, vc_ref, skn_ref, svn_ref, wko_ref, wvo_ref, wkn_ref, wvn_ref, e_ref,
     y_ref, m_scr, l_scr, acc_scr, ocmp_scr, owin_scr, mask_scr) = refs[2 * n_pages:]
    step = pl.program_id(1)
    t = t_len
    nq_ref, nqr_ref, ngl_ref, ng_ref, y_ref = (r.at[0] for r in (nq_ref, nqr_ref, ngl_ref, ng_ref, y_ref))
    rows8 = 2 * NSA_GROUP * t
    half = NSA_GROUP * t
    tpos = lax.broadcasted_iota(jnp.int32, (t, 1), 0)
    qpos = past_len + tpos
    n_cmp = kc_ref.shape[1]
    wb = wko_ref.shape[2]
    step_keys = n_pages * PAGE_SIZE

    @pl.when(step == 0)
    def _():
        lhs = _stack_heads(nq_ref)
        s = _dot_nt(lhs, kc_ref[0].astype(BF16)) * ATTN_SCALE
        m = jnp.max(s, axis=-1, keepdims=True)
        p = jnp.exp(s - m)
        p = p / jnp.sum(p, axis=-1, keepdims=True)
        oc = _dot(p.astype(BF16), vc_ref[0].astype(BF16))
        for g, slab in enumerate(_merge_kv_heads(oc, t)):
            ocmp_scr[g] = slab
        p4 = p.reshape(8, t, n_cmp)
        for k in range(KV_HEADS):
            imp = p4[k * NSA_GROUP]
            for g in range(1, NSA_GROUP):
                imp = imp + p4[k * NSA_GROUP + g]
            sel = _rank_select(_pair_scores(imp, qpos, n_cmp), n_cmp, True).astype(BF16)
            for c in range(n_cmp // LANES):
                km = _dot(sel[:, c * LANES:(c + 1) * LANES], e_ref[...])
                per = LANES * CMP_BLOCK // step_keys
                for u in range(per):
                    mask_scr[c * per + u, k] = km[:, u * step_keys:(u + 1) * step_keys]

        lhs_r = _stack_heads(nqr_ref)
        so = _dot(lhs_r, wko_ref[0].astype(BF16)) * ATTN_SCALE
        sn = _dot(lhs_r, wkn_ref[0].astype(BF16)) * ATTN_SCALE
        jo = lax.broadcasted_iota(jnp.int32, (1, wb), 1)
        jn = lax.broadcasted_iota(jnp.int32, (1, LANES), 1)
        old_vis = jo >= tpos
        so = _mask_rows(so, [old_vis, old_vis], t)
        new_vis = jn <= tpos
        sn_w = _mask_rows(sn, [new_vis, new_vis], t)
        m = jnp.maximum(jnp.max(so, axis=-1, keepdims=True), jnp.max(sn_w, axis=-1, keepdims=True))
        po = jnp.exp(so - m)
        pn = jnp.exp(sn_w - m)
        den = jnp.sum(po, axis=-1, keepdims=True) + jnp.sum(pn, axis=-1, keepdims=True)
        pob, pnb = (po / den).astype(BF16), (pn / den).astype(BF16)
        wvo, wvn = wvo_ref[0].astype(BF16), wvn_ref[0].astype(BF16)
        ow = jnp.concatenate([_dot_nt(pob[:half], wvo) + _dot_nt(pnb[:half], wvn),
                              _dot_nt(pob[half:], wvo) + _dot_nt(pnb[half:], wvn)], axis=0)
        for g, slab in enumerate(_merge_kv_heads(ow, t)):
            owin_scr[g] = slab

        s2 = _dot(lhs_r, skn_ref[0].astype(BF16)) * ATTN_SCALE
        s2 = _mask_rows(s2, [new_vis, new_vis], t)
        m2 = jnp.max(s2, axis=-1, keepdims=True)
        p2 = jnp.exp(s2 - m2)
        m_scr[...] = m2
        l_scr[...] = jnp.sum(p2, axis=-1, keepdims=True)
        p2b = p2.astype(BF16)
        svn = svn_ref[0].astype(BF16)
        acc_scr[...] = jnp.concatenate([_dot_nt(p2b[:half], svn), _dot_nt(p2b[half:], svn)], axis=0)

    lhs_r = _stack_heads(nqr_ref)
    sc = jnp.concatenate([_dot(lhs_r, r[0].astype(BF16)) for r in k_pages], axis=1) * ATTN_SCALE
    msk = mask_scr[step]
    sc = _mask_rows(sc, [msk[k] > 0.5 for k in range(KV_HEADS)], t)
    m_i = m_scr[...]
    m_new = jnp.maximum(m_i, jnp.max(sc, axis=-1, keepdims=True))
    alpha = jnp.exp(m_i - m_new)
    pr = jnp.exp(sc - m_new)
    l_scr[...] = alpha * l_scr[...] + jnp.sum(pr, axis=-1, keepdims=True)
    m_scr[...] = m_new
    pb = pr.astype(BF16)
    pv = jnp.zeros((rows8, LANES), F32)
    for j, r in enumerate(v_pages):
        vt = r[0].astype(BF16)
        pj = pb[:, j * PAGE_SIZE:(j + 1) * PAGE_SIZE]
        pv = pv + jnp.concatenate([_dot_nt(pj[:half], vt), _dot_nt(pj[half:], vt)], axis=0)
    acc_scr[...] = alpha * acc_scr[...] + pv

    @pl.when(step == n_steps - 1)
    def _():
        o_slc = _merge_kv_heads(acc_scr[...] / l_scr[...], t)
        o_cmp = [ocmp_scr[g] for g in range(NSA_GROUP)]
        o_win = [owin_scr[g] for g in range(NSA_GROUP)]
        _gate_and_store(o_cmp, o_slc, o_win, ngl_ref, ng_ref, y_ref)


def _attn_sample(page_table, cache_sk_t, cache_sv_t, nq, nqr, ngl, ng, kc, vc, skn, svn, wko, wvo, wkn, wvn, t_len):
    db, n_pages = page_table.shape
    g = SAMPLE_PAGES_PER_STEP
    steps = n_pages // g
    past_len = n_pages * PAGE_SIZE
    step_keys = g * PAGE_SIZE
    span = LANES * CMP_BLOCK
    expand = (jnp.arange(span)[None, :] // CMP_BLOCK == jnp.arange(LANES)[:, None]).astype(BF16)

    def page_spec(i):
        return pl.BlockSpec((1, KV_WIDTH, PAGE_SIZE), lambda b, s, pt: (pt[b, s * g + i], 0, 0))

    row = lambda w: pl.BlockSpec((1, t_len, w), lambda b, s, pt: (b, 0, 0))
    per_b = lambda shape: pl.BlockSpec((1,) + shape, lambda b, s, pt: (b, 0, 0))
    rows8 = 2 * NSA_GROUP * t_len
    grid_spec = pltpu.PrefetchScalarGridSpec(
        num_scalar_prefetch=1,
        grid=(db, steps),
        in_specs=[page_spec(i % g) for i in range(2 * g)]
        + [row(512), row(512), row(LANES), row(512), per_b(kc.shape[1:]), per_b(vc.shape[1:]),
           per_b(skn.shape[1:]), per_b(svn.shape[1:]), per_b(wko.shape[1:]), per_b(wvo.shape[1:]),
           per_b(wkn.shape[1:]), per_b(wvn.shape[1:]), pl.BlockSpec(expand.shape, lambda b, s, pt: (0, 0))],
        out_specs=row(512),
        scratch_shapes=[pltpu.VMEM((rows8, 1), F32), pltpu.VMEM((rows8, 1), F32), pltpu.VMEM((rows8, LANES), F32),
                        pltpu.VMEM((NSA_GROUP, t_len, LANES), F32), pltpu.VMEM((NSA_GROUP, t_len, LANES), F32),
                        pltpu.VMEM((steps, KV_HEADS, t_len, step_keys), F32)],
    )
    return pl.pallas_call(
        functools.partial(_attn_sample_kernel, n_pages=g, n_steps=steps, t_len=t_len, past_len=past_len),
        grid_spec=grid_spec,
        out_shape=jax.ShapeDtypeStruct((db, t_len, 512), F32),
        compiler_params=_params(("arbitrary", "arbitrary")),
        name="attn_sample",
    )(page_table, *([cache_sk_t] * g), *([cache_sv_t] * g), nq, nqr, ngl, ng, kc, vc, skn, svn, wko, wvo, wkn, wvn, expand)


def _finish_kernel(x_ref, yr_ref, yn_ref, p_ref, wo_ref, gple_ref, wg_ref, wp_ref, gf_ref, o_ref):
    x = x_ref[...]
    x = x + _dot(yr_ref[...].astype(BF16), wo_ref[0:RET_WIDTH, :]) + _dot(yn_ref[...].astype(BF16), wo_ref[RET_WIDTH:, :])
    ms = jnp.mean(x * x, axis=-1, keepdims=True)
    hn = (x * lax.rsqrt(ms + RMS_EPS) * gple_ref[...]).astype(BF16)
    gate = _sigmoid(_dot(hn, wg_ref[...]))
    x = x + gate * _dot(p_ref[...].astype(BF16), wp_ref[...])
    ms = jnp.mean(x * x, axis=-1, keepdims=True)
    o_ref[...] = x * lax.rsqrt(ms + RMS_EPS) * gf_ref[...]


def _finish(x2d, y_ret, y_nsa, p2d, w_out, norm_ple, w_gate, w_ple, norm_f, tm):
    n = x2d.shape[0]
    row = lambda w: pl.BlockSpec((tm, w), lambda i: (i, 0))
    const = lambda shape: pl.BlockSpec(shape, lambda i: (0, 0))
    return pl.pallas_call(
        _finish_kernel,
        grid=(n // tm,),
        in_specs=[row(D_MODEL), row(RET_WIDTH), row(NSA_WIDTH), row(PLE_DIM), const(w_out.shape), const((1, D_MODEL)),
                  const(w_gate.shape), const(w_ple.shape), const((1, D_MODEL))],
        out_specs=row(D_MODEL),
        out_shape=jax.ShapeDtypeStruct((n, D_MODEL), F32),
        compiler_params=_params(("arbitrary",)),
        name="finish",
    )(x2d, y_ret, y_nsa, p2d, w_out, norm_ple, w_gate, w_ple, norm_f)


def _slab_perm():
    return np.array([(k * NSA_GROUP + g) * NSA_HD + d for g in range(NSA_GROUP) for k in range(KV_HEADS)
                     for d in range(NSA_HD)], np.int32)


def _to_positions_major(x_t):
    lead = x_t.shape[:-2]
    n = len(lead)
    x4 = x_t.reshape(lead + (KV_HEADS, NSA_HD, x_t.shape[-1]))
    return jnp.transpose(x4, tuple(range(n)) + (n + 2, n, n + 1))


def _cache_t(cache):
    n_pool, page = cache.shape[:2]
    return jnp.transpose(cache, (0, 2, 3, 1)).reshape(n_pool, KV_WIDTH, page)


def _layer(xp, xs, c_ck, c_cv, c_sk, c_sv, win_k, win_v, ret_state, page_table, p_p, p_s, norm_mix, w_in, gn_g, gn_b,
           pe_k, w1_k, w2_k, pe_v, w1_v, w2_v, w_out, norm_ple, w_gate, w_ple, norm_f):
    b, s, d = xp.shape
    db, t, _ = xs.shape
    n_pages = page_table.shape[1]
    past = n_pages * PAGE_SIZE

    off = np.cumsum((0,) + SPLIT_SIZES)
    col = lambda i: w_in[:, off[i]:off[i + 1]]
    perm = _slab_perm()
    ngl_w = jnp.pad(col(11), ((0, 0), (0, LANES - N_GATES)))
    ret_cols = [col(0), col(1), col(2), col(3)]
    kv_cols = [col(5), col(6), col(8), col(10), col(7), col(9)]
    w_rm_p = jnp.concatenate(ret_cols + [col(7), col(9)], axis=1).astype(BF16)
    w_t_p = jnp.concatenate(kv_cols + [col(4), col(12), ngl_w], axis=1).T.astype(BF16)
    w_rm_s = jnp.concatenate(ret_cols + [col(4)[:, perm], col(12)[:, perm], ngl_w], axis=1).astype(BF16)
    w_t_s = jnp.concatenate(kv_cols, axis=1).T.astype(BF16)
    w_out_b = w_out.astype(BF16)
    w_out_slab = jnp.concatenate([w_out[:RET_WIDTH], w_out[RET_WIDTH:][perm]], axis=0).astype(BF16)
    g_mix = norm_mix.reshape(1, d)
    cwk = _compress_weights(pe_k, w1_k, w2_k)
    cwv = _compress_weights(pe_v, w1_v, w2_v)
    cwv_t = _compress_weights(pe_v, w1_v, w2_v, transpose_w2=True)
    gn_g2, gn_b2 = gn_g.reshape(1, RET_WIDTH), gn_b.reshape(1, RET_WIDTH)
    fin_w = (norm_ple.reshape(1, d), w_gate.astype(BF16), w_ple.astype(BF16), norm_f.reshape(1, d))

    tm = 512
    (q_ret, k_ret, v_ret, rg, sk_rm, wk_rm, ck_t, cv_t, sv_t, wv_t, sk_t, wk_t, sv_b, wv_b, nq_t, nqr_t, ng_t,
     ngl_t) = _proj(xp.reshape(b * s, d), jnp.arange(s), g_mix, w_rm_p, w_t_p, b, tm, True)
    y_ret, ret_p = _retention(q_ret, k_ret, v_ret, rg, jnp.zeros((b, RET_HEADS, RET_DK, RET_DK), F32), gn_g2, gn_b2,
                              b, s // RET_CHUNK, RET_CHUNK)
    kc, vc_t = _compress_prompt(ck_t, cv_t, cwk, cwv_t)
    y_nsa = _attn_prompt(nq_t, nqr_t, ngl_t, ng_t, kc, vc_t, sk_rm, sv_b, wk_rm, wv_b)
    y_prompt = _finish(xp.reshape(b * s, d), y_ret, y_nsa, p_p.reshape(b * s, PLE_DIM), w_out_b, *fin_w, tm)
    wb_p = min(WINDOW, s)
    prompt_states = (ret_p, _to_positions_major(ck_t), _to_positions_major(cv_t), _to_positions_major(sk_t),
                     _to_positions_major(sv_t), _to_positions_major(wk_t[:, :, s - wb_p:]),
                     _to_positions_major(wv_t[:, :, s - wb_p:]))

    n = db * t
    pos_s = jnp.tile(past + jnp.arange(t), db)
    (q_ret, k_ret, v_ret, rg, nq, nqr, ng, ngl, ck_n, cv_n, sv_n, wv_n, sk_n, wk_n) = _proj(
        xs.reshape(n, d), pos_s, g_mix, w_rm_s, w_t_s, 1, n, False)
    pad = lambda a: jnp.pad(a.reshape(db, t, RET_WIDTH), ((0, 0), (0, RET_CHUNK - t), (0, 0))).reshape(db * RET_CHUNK, RET_WIDTH)
    y_ret_pad, ret_s = _retention(pad(q_ret), pad(k_ret), pad(v_ret), pad(rg), ret_state, gn_g2, gn_b2, db, 1, t)
    y_ret = y_ret_pad.reshape(db, RET_CHUNK, RET_WIDTH)[:, :t].reshape(n, RET_WIDTH)
    kc, vc = _compress_sample(page_table, _cache_t(c_ck), _cache_t(c_cv), cwk, cwv)
    per_batch = lambda a: jnp.transpose(a[0].reshape(KV_WIDTH, db, t), (1, 0, 2))
    lane_pad = lambda a: jnp.pad(per_batch(a), ((0, 0), (0, 0), (0, LANES - t)))
    win_t = lambda w: jnp.transpose(w, (0, 2, 3, 1)).reshape(db, KV_WIDTH, w.shape[1])
    rows3 = lambda a: a.astype(F32).reshape(db, t, a.shape[-1])
    y_nsa = _attn_sample(page_table, _cache_t(c_sk), _cache_t(c_sv), rows3(nq), rows3(nqr), rows3(ngl), rows3(ng), kc, vc,
                         lane_pad(sk_n), lane_pad(sv_n), win_t(win_k), win_t(win_v), lane_pad(wk_n), lane_pad(wv_n), t)
    y_nsa = y_nsa.reshape(n, NSA_WIDTH)
    y_sample = _finish(xs.reshape(n, d), y_ret, y_nsa, p_s.reshape(n, PLE_DIM), w_out_slab, *fin_w, n)
    new_rows = lambda a: _to_positions_major(per_batch(a))
    sample_states = (ret_s, new_rows(ck_n), new_rows(cv_n), new_rows(sk_n), new_rows(sv_n),
                     jnp.concatenate([win_k[:, t:], new_rows(wk_n)], axis=1),
                     jnp.concatenate([win_v[:, t:], new_rows(wv_n)], axis=1))
    return y_prompt.reshape(b, s, d), y_sample.reshape(db, t, d), prompt_states, sample_states


def kernel(x_prompt, x_sample, cache_cmp_k, cache_cmp_v, cache_slc_k, cache_slc_v, state_win_k, state_win_v, state_ret, page_table, p_prompt, p_sample, norm_mix, w_in, ret_gn_g, ret_gn_b, cmp_pe_k, cmp_w1_k, cmp_w2_k, cmp_pe_v, cmp_w1_v, cmp_w2_v, w_out, norm_ple, w_ple_gate, w_ple, norm_f):
    depth = w_in.shape[0]
    assert depth == 1, "single trunk layer"
    l = 0
    yp, ys, sp, ss = _layer(x_prompt, x_sample, cache_cmp_k[l], cache_cmp_v[l], cache_slc_k[l], cache_slc_v[l],
                            state_win_k[l], state_win_v[l], state_ret[l], page_table, p_prompt[l], p_sample[l],
                            norm_mix[l], w_in[l], ret_gn_g[l], ret_gn_b[l], cmp_pe_k[l], cmp_w1_k[l], cmp_w2_k[l],
                            cmp_pe_v[l], cmp_w1_v[l], cmp_w2_v[l], w_out[l], norm_ple[l], w_ple_gate[l], w_ple[l], norm_f)
    return (yp, ys) + tuple(a[None] for a in sp) + tuple(a[None] for a in ss)
```

```python
import functools

import numpy as np
import jax
import jax.numpy as jnp
from jax import lax
from jax.experimental import pallas as pl
from jax.experimental.pallas import tpu as pltpu

F32 = jnp.float32
BF16 = jnp.bfloat16

D_MODEL = 1024
PLE_DIM = 256
PAGE_SIZE = 128
RET_HEADS = 4
RET_DK = 128
RET_WIDTH = 512
RET_CHUNK = 128
NSA_HEADS = 8
KV_HEADS = 2
NSA_GROUP = 4
NSA_HD = 64
NSA_WIDTH = 512
KV_WIDTH = 128
CMP_BLOCK = 32
SLC_BLOCK = 64
TOP_N = 16
WINDOW = 512
Q_BLOCK = 128
ROPE_THETA = 10000.0
RMS_EPS = 1e-6
GN_EPS = 1e-5
NEG_INF = -1e9
FORCED_SCORE = 1e4
M_INIT = -1e30
CAP_OPEN = 3e38
ATTN_SCALE = NSA_HD ** -0.5
SPLIT_SIZES = (512, 512, 512, 512, 512, 128, 128, 128, 128, 128, 128, 24, 512)
N_GATES = 3 * NSA_HEADS

LANES = 128
SUBLANES = 8
SLC_KEY_TILE = 512
RET_CHUNKS_PER_STEP = 4
SAMPLE_PAGES_PER_STEP = 16
COMPRESS_PAGES_PER_STEP = 32
BLOCK_PITCH = 40
VMEM_LIMIT = 56 * 1024 * 1024


def _dot(a, b):
    return jnp.dot(a, b, preferred_element_type=F32)


def _dot_nt(a, b):
    return lax.dot_general(a, b, (((1,), (1,)), ((), ())), preferred_element_type=F32)


def _sigmoid(x):
    return 1.0 / (1.0 + jnp.exp(-x))


def _params(sem):
    return pltpu.CompilerParams(dimension_semantics=sem, vmem_limit_bytes=VMEM_LIMIT)


def _rope_lanes(a, cos, sin_signed, half):
    if 2 * half == LANES:
        partner = pltpu.roll(a, half, 1)
    else:
        lane = lax.broadcasted_iota(jnp.int32, (1, LANES), 1)
        partner = jnp.where((lane % (2 * half)) < half, pltpu.roll(a, LANES - half, 1), pltpu.roll(a, half, 1))
    return a * cos + partner * sin_signed


def _rope_sublanes(blk, cos_t, sin_t):
    outs = []
    for h in range(blk.shape[0] // NSA_HD):
        x1 = blk[h * NSA_HD:h * NSA_HD + NSA_HD // 2]
        x2 = blk[h * NSA_HD + NSA_HD // 2:(h + 1) * NSA_HD]
        outs += [x1 * cos_t - x2 * sin_t, x2 * cos_t + x1 * sin_t]
    return jnp.concatenate(outs, axis=0)


def _proj_kernel(*refs, prompt):
    (x_ref, g_ref, wrm_ref, wt_ref, cosr_ref, sinr_ref, cosn_ref, sinn_ref, cost_ref, sint_ref,
     qret_ref, kret_ref, vret_ref, rg_ref) = refs[:14]
    x = x_ref[...]
    ms = jnp.mean(x * x, axis=-1, keepdims=True)
    h = (x * lax.rsqrt(ms + RMS_EPS) * g_ref[...]).astype(BF16)

    def proj(lo, hi):
        return _dot(h, wrm_ref[:, lo:hi])

    cr, sr = cosr_ref[...], sinr_ref[...]
    cn, sn = cosn_ref[...], sinn_ref[...]
    ct, st = cost_ref[...], sint_ref[...]
    yq = proj(0, 512)
    yk = proj(512, 1024)
    for hh in range(RET_HEADS):
        sl = slice(hh * LANES, (hh + 1) * LANES)
        qret_ref[:, sl] = _rope_lanes(yq[:, sl], cr, sr, RET_DK // 2)
        kret_ref[:, sl] = _rope_lanes(yk[:, sl], cr, sr, RET_DK // 2) * (RET_DK ** -0.5)
    vret_ref[...] = proj(1024, 1536)
    rg_ref[...] = proj(1536, 2048)

    yt = _dot_nt(wt_ref[...], h)
    if prompt:
        (skr_ref, wkr_ref, ck_ref, cv_ref, sv_ref, wv_ref, sk_ref, wk_ref, svb_ref, wvb_ref,
         nqt_ref, nqrt_ref, ngt_ref, nglt_ref) = refs[14:]
    else:
        nq_ref, nqr_ref, ng_ref, ngl_ref, ck_ref, cv_ref, sv_ref, wv_ref, sk_ref, wk_ref = refs[14:]
    ck_ref[0] = yt[0:128]
    cv_ref[0] = yt[128:256]
    sv_ref[0] = yt[256:384]
    wv_ref[0] = yt[384:512]
    sk_ref[0] = _rope_sublanes(yt[512:640], ct, st)
    wk_ref[0] = _rope_sublanes(yt[640:768], ct, st)
    if prompt:
        svb_ref[0] = yt[256:384].astype(BF16)
        wvb_ref[0] = yt[384:512].astype(BF16)
        nqt = yt[768:1280]
        nqt_ref[0] = nqt.astype(BF16)
        nqrt_ref[0] = _rope_sublanes(nqt, ct, st).astype(BF16)
        ngt_ref[0] = yt[1280:1792]
        nglt_ref[0] = yt[1792:1920]
        skr_ref[...] = _rope_lanes(proj(2048, 2176), cn, sn, NSA_HD // 2).astype(BF16)
        wkr_ref[...] = _rope_lanes(proj(2176, 2304), cn, sn, NSA_HD // 2).astype(BF16)
    else:
        yn = proj(2048, 2560)
        for gg in range(NSA_GROUP):
            sl = slice(gg * LANES, (gg + 1) * LANES)
            nq_ref[:, sl] = yn[:, sl].astype(BF16)
            nqr_ref[:, sl] = _rope_lanes(yn[:, sl], cn, sn, NSA_HD // 2).astype(BF16)
        ng_ref[...] = proj(2560, 3072)
        ngl_ref[...] = proj(3072, 3200)


def _rope_tables(pos, half):
    inv = ROPE_THETA ** (-jnp.arange(half, dtype=F32) / half)
    ang = pos.astype(F32)[:, None] * inv[None, :]
    return jnp.cos(ang), jnp.sin(ang)


def _proj(x2d, pos_rows, norm_g, w_rm, w_t, kv_batch, tm, prompt):
    n = x2d.shape[0]
    nt = pos_rows.shape[0] // tm
    skv = n // kv_batch
    nkt = skv // tm
    c64, s64 = _rope_tables(pos_rows, 64)
    c32, s32 = _rope_tables(pos_rows, 32)
    cosr = jnp.concatenate([c64, c64], axis=1)
    sinr = jnp.concatenate([-s64, s64], axis=1)
    cosn = jnp.concatenate([c32, c32, c32, c32], axis=1)
    sinn = jnp.concatenate([-s32, s32, -s32, s32], axis=1)
    cost, sint = c32.T, s32.T
    row = lambda w: pl.BlockSpec((tm, w), lambda i: (i, 0))
    tab = pl.BlockSpec((tm, LANES), lambda i: (i % nt, 0))
    tabt = pl.BlockSpec((32, tm), lambda i: (0, i % nt))
    tspec = lambda r: pl.BlockSpec((1, r, tm), lambda i: (i // nkt, 0, i % nkt))
    f = lambda w, dt: jax.ShapeDtypeStruct((n, w), dt)
    ts = lambda r, dt: jax.ShapeDtypeStruct((kv_batch, r, skv), dt)
    kvo, kvs = tspec(KV_WIDTH), ts(KV_WIDTH, F32)
    out_specs = [row(512)] * 4
    out_shape = [f(512, F32)] * 4
    if prompt:
        out_specs += [row(KV_WIDTH)] * 2 + [kvo] * 8 + [tspec(512)] * 3 + [tspec(LANES)]
        out_shape += [f(KV_WIDTH, BF16)] * 2 + [kvs] * 6 + [ts(KV_WIDTH, BF16)] * 2 + [ts(512, BF16)] * 2 + [
            ts(512, F32), ts(LANES, F32)]
    else:
        out_specs += [row(512)] * 3 + [row(LANES)] + [kvo] * 6
        out_shape += [f(512, BF16)] * 2 + [f(512, F32), f(LANES, F32)] + [kvs] * 6
    return pl.pallas_call(
        functools.partial(_proj_kernel, prompt=prompt),
        grid=(n // tm,),
        in_specs=[row(D_MODEL), pl.BlockSpec((1, D_MODEL), lambda i: (0, 0)),
                  pl.BlockSpec(w_rm.shape, lambda i: (0, 0)), pl.BlockSpec(w_t.shape, lambda i: (0, 0)),
                  tab, tab, tab, tab, tabt, tabt],
        out_specs=out_specs,
        out_shape=out_shape,
        compiler_params=_params(("arbitrary",)),
        name="proj_prompt" if prompt else "proj_sample",
    )(x2d, norm_g, w_rm, w_t, cosr, sinr, cosn, sinn, cost, sint)


def _ret_kernel(q_ref, k_ref, v_ref, rg_ref, st_ref, dm_ref, qd_ref, kd_ref, sd_ref, gg_ref, gb_ref,
                y_ref, so_ref, st_scr, *, n_chunks):
    c = pl.program_id(1)

    @pl.when(c == 0)
    def _():
        st_scr[...] = st_ref[0]

    for h in range(RET_HEADS):
        sl = slice(h * LANES, (h + 1) * LANES)
        state = st_scr[h]
        for cc in range(q_ref.shape[0] // RET_CHUNK):
            rows = slice(cc * RET_CHUNK, (cc + 1) * RET_CHUNK)
            q = q_ref[rows, sl]
            k = k_ref[rows, sl]
            v = v_ref[rows, sl].astype(BF16)
            s = _dot_nt(q.astype(BF16), k.astype(BF16)) * dm_ref[h]
            intra = _dot(s.astype(BF16), v)
            cross = _dot((q * qd_ref[:, sl]).astype(BF16), state.astype(BF16))
            o = intra + cross
            kd_t = (k * kd_ref[:, sl]).T.astype(BF16)
            state = sd_ref[h] * state + _dot(kd_t, v)
            mu = jnp.mean(o, axis=-1, keepdims=True)
            var = jnp.mean(jnp.square(o - mu), axis=-1, keepdims=True)
            on = (o - mu) * lax.rsqrt(var + GN_EPS)
            gate = rg_ref[rows, sl]
            y_ref[rows, sl] = (on * gg_ref[:, sl] + gb_ref[:, sl]) * (gate * _sigmoid(gate))
        st_scr[h] = state

    @pl.when(c == n_chunks - 1)
    def _():
        so_ref[0] = st_scr[...]


def _retention(q, k, v, rg, state, gn_g, gn_b, n_batch, n_chunks, chunk_len):
    c = RET_CHUNK
    log_g = jnp.log(1.0 - 2.0 ** (-5.0 - jnp.arange(RET_HEADS, dtype=F32)))
    i = jnp.arange(c, dtype=F32)
    diff = i[:, None] - i[None, :]
    causal = diff >= 0
    dmask = jnp.where(causal[None], jnp.exp(jnp.where(causal, diff, 0.0)[None] * log_g[:, None, None]), 0.0)
    expand = lambda t: jnp.repeat(t, LANES, axis=1)
    qdec = expand(jnp.exp((i[:, None] + 1.0) * log_g[None, :]))
    kdec = expand(jnp.exp((chunk_len - 1.0 - i)[:, None] * log_g[None, :]))
    sdec = jnp.broadcast_to(jnp.exp(chunk_len * log_g)[:, None, None], (RET_HEADS, 1, LANES))
    per_step = RET_CHUNKS_PER_STEP if n_chunks % RET_CHUNKS_PER_STEP == 0 else 1
    n_steps = n_chunks // per_step
    row = pl.BlockSpec((per_step * c, 512), lambda b, j: (b * n_steps + j, 0))
    stspec = pl.BlockSpec((1, RET_HEADS, 128, 128), lambda b, j: (b, 0, 0, 0))
    const = lambda shape: pl.BlockSpec(shape, lambda b, j: (0,) * len(shape))
    return pl.pallas_call(
        functools.partial(_ret_kernel, n_chunks=n_steps),
        grid=(n_batch, n_steps),
        in_specs=[row, row, row, row, stspec, const((RET_HEADS, c, c)), const((c, 512)), const((c, 512)),
                  const((RET_HEADS, 1, LANES)), const((1, 512)), const((1, 512))],
        out_specs=[row, stspec],
        out_shape=[jax.ShapeDtypeStruct(q.shape, F32), jax.ShapeDtypeStruct(state.shape, F32)],
        scratch_shapes=[pltpu.VMEM((RET_HEADS, 128, 128), F32)],
        compiler_params=_params(("arbitrary", "arbitrary")),
        name="retention",
    )(q, k, v, rg, state, dmask, qdec, kdec, sdec, gn_g, gn_b)


def _gelu_tanh(x):
    return 0.5 * x * (1.0 + jnp.tanh(np.sqrt(2.0 / np.pi).astype(np.float32) * (x + 0.044715 * (x * x * x))))


def _compress_one(tiles, pe_ref, w1_ref, w2_ref, scr, transpose_out=False):
    per_tile = LANES // CMP_BLOCK
    for t, tile in enumerate(tiles):
        rows_pm = tile.T
        for c in range(per_tile):
            r0 = (t * per_tile + c) * BLOCK_PITCH
            scr[r0:r0 + CMP_BLOCK, :] = rows_pm[c * CMP_BLOCK:(c + 1) * CMP_BLOCK]
    n_blk = len(tiles) * per_tile
    flat = [(scr[pl.ds(j, n_blk, stride=BLOCK_PITCH), :] + pe_ref[j:j + 1, :]).astype(BF16) for j in range(CMP_BLOCK)]
    hid = _gelu_tanh(_dot(jnp.concatenate(flat, axis=1), w1_ref[...]))
    if transpose_out:
        return _dot_nt(w2_ref[...], hid.astype(BF16))
    return _dot(hid.astype(BF16), w2_ref[...])


def _compress_prompt_kernel(k_ref, v_ref, pek_ref, w1k_ref, w2k_ref, pev_ref, w1v_ref, w2v_ref,
                            kc_ref, vct_ref, scr_k, scr_v, *, n_tiles):
    for src, pe, w1, w2, dst, scr, tr in ((k_ref, pek_ref, w1k_ref, w2k_ref, kc_ref, scr_k, False),
                                          (v_ref, pev_ref, w1v_ref, w2v_ref, vct_ref, scr_v, True)):
        tiles = [src[0, :, t * LANES:(t + 1) * LANES] for t in range(n_tiles)]
        dst[0] = _compress_one(tiles, pe, w1, w2, scr, tr)


def _compress_sample_kernel(pt_ref, ck_hbm, cv_hbm, pek_ref, w1k_ref, w2k_ref, pev_ref, w1v_ref, w2v_ref,
                            kc_ref, vc_ref, kbuf, vbuf, sems, scr_k, scr_v, *, n_pages, n_steps, n_batch):
    n_blk = n_pages * PAGE_SIZE // CMP_BLOCK

    def on_pages(step, slot):
        rows = pl.ds(pl.multiple_of(step * n_blk, n_blk), n_blk)
        kc_ref[0, rows, :] = _compress_one([kbuf[slot, j] for j in range(n_pages)], pek_ref, w1k_ref, w2k_ref, scr_k)
        vc_ref[0, rows, :] = _compress_one([vbuf[slot, j] for j in range(n_pages)], pev_ref, w1v_ref, w2v_ref, scr_v)

    _stream_pages(pt_ref, (ck_hbm, cv_hbm), (kbuf, vbuf), sems, n_pages, n_steps, n_batch, on_pages)


def _compress_weights(pe, w1, w2, transpose_w2=False):
    pe_rows = pe.reshape(CMP_BLOCK, KV_WIDTH)
    z = jnp.zeros((CMP_BLOCK, NSA_HD, NSA_HD), F32)
    w1bd = jnp.concatenate([jnp.concatenate([w1[0], z], axis=2), jnp.concatenate([z, w1[1]], axis=2)], axis=1)
    z2 = jnp.zeros((NSA_HD, NSA_HD), F32)
    w2bd = jnp.concatenate([jnp.concatenate([w2[0], z2], axis=1), jnp.concatenate([z2, w2[1]], axis=1)], axis=0)
    if transpose_w2:
        w2bd = w2bd.T
    return pe_rows, w1bd.reshape(CMP_BLOCK * KV_WIDTH, KV_WIDTH).astype(BF16), w2bd.astype(BF16)


def _cw_specs():
    zero = lambda *a: (0, 0)
    return [pl.BlockSpec((CMP_BLOCK, KV_WIDTH), zero), pl.BlockSpec((CMP_BLOCK * KV_WIDTH, KV_WIDTH), zero),
            pl.BlockSpec((KV_WIDTH, KV_WIDTH), zero)]


def _compress_prompt(ck_t, cv_t, cwk, cwv_t):
    b, _, s = ck_t.shape
    n_blk = s // CMP_BLOCK
    src = pl.BlockSpec((1, KV_WIDTH, s), lambda i: (i, 0, 0))
    return pl.pallas_call(
        functools.partial(_compress_prompt_kernel, n_tiles=s // LANES),
        grid=(b,),
        in_specs=[src, src] + _cw_specs() + _cw_specs(),
        out_specs=[pl.BlockSpec((1, n_blk, KV_WIDTH), lambda i: (i, 0, 0)),
                   pl.BlockSpec((1, KV_WIDTH, n_blk), lambda i: (i, 0, 0))],
        out_shape=[jax.ShapeDtypeStruct((b, n_blk, KV_WIDTH), F32), jax.ShapeDtypeStruct((b, KV_WIDTH, n_blk), F32)],
        scratch_shapes=[pltpu.VMEM((n_blk * BLOCK_PITCH, KV_WIDTH), F32)] * 2,
        compiler_params=_params(("arbitrary",)),
        name="compress_prompt",
    )(ck_t, cv_t, *cwk, *cwv_t)


def _compress_sample(page_table, cache_k_t, cache_v_t, cwk, cwv):
    db, n_pages = page_table.shape
    g = COMPRESS_PAGES_PER_STEP
    steps = n_pages // g
    blk_per_step = g * PAGE_SIZE // CMP_BLOCK
    n_blk = n_pages * PAGE_SIZE // CMP_BLOCK

    hbm = pl.BlockSpec(memory_space=pl.ANY)
    dst = pl.BlockSpec((1, n_blk, KV_WIDTH), lambda b, pt: (b, 0, 0))
    shp = jax.ShapeDtypeStruct((db, n_blk, KV_WIDTH), F32)
    page_buf = pltpu.VMEM((2, g, KV_WIDTH, PAGE_SIZE), F32)
    grid_spec = pltpu.PrefetchScalarGridSpec(
        num_scalar_prefetch=1,
        grid=(db,),
        in_specs=[hbm, hbm] + _cw_specs() + _cw_specs(),
        out_specs=[dst, dst],
        scratch_shapes=[page_buf, page_buf, pltpu.SemaphoreType.DMA((2, 2))]
        + [pltpu.VMEM((blk_per_step * BLOCK_PITCH, KV_WIDTH), F32)] * 2,
    )
    return pl.pallas_call(
        functools.partial(_compress_sample_kernel, n_pages=g, n_steps=steps, n_batch=db),
        grid_spec=grid_spec,
        out_shape=[shp, shp],
        compiler_params=_params(("arbitrary",)),
        name="compress_sample",
    )(page_table, cache_k_t, cache_v_t, *cwk, *cwv)


def _stack_heads_t(qt_ref):
    nq = qt_ref.shape[2]
    zero = jnp.zeros((NSA_HD, nq), BF16)
    cols = []
    for k in range(KV_HEADS):
        for g in range(NSA_GROUP):
            h = k * NSA_GROUP + g
            tile = qt_ref[0, h * NSA_HD:(h + 1) * NSA_HD, :] * ATTN_SCALE
            cols.append(jnp.concatenate([tile, zero] if k == 0 else [zero, tile], axis=0))
    return jnp.concatenate(cols, axis=1)


def _rank_select_rows(score, n_blk):
    parts = [score[v * SUBLANES:(v + 1) * SUBLANES] for v in range(n_blk // SUBLANES)]
    ranks = [jnp.zeros(p.shape, jnp.int32) for p in parts]
    for j in range(n_blk):
        col = score[j:j + 1, :]
        for v, part in enumerate(parts):
            ge, gt = (col >= part).astype(jnp.int32), (col > part).astype(jnp.int32)
            if v * SUBLANES > j:
                beats = ge
            elif (v + 1) * SUBLANES - 1 <= j:
                beats = gt
            else:
                row = v * SUBLANES + lax.broadcasted_iota(jnp.int32, (SUBLANES, 1), 0)
                beats = jnp.where(row > j, ge, gt)
            ranks[v] = ranks[v] + beats
    return jnp.concatenate([(r < TOP_N).astype(F32) for r in ranks], axis=0)


def _softmax_tile_t(s, v_t, m_ref, l_ref, acc_ref, s_max=None):
    half = s.shape[1] // KV_HEADS
    m_old = m_ref[...]
    m_new = jnp.maximum(m_old, jnp.max(s, axis=0, keepdims=True) if s_max is None else s_max)
    alpha = jnp.exp(m_old - m_new)
    p = jnp.exp(s - m_new)
    l_ref[...] = alpha * l_ref[...] + jnp.sum(p, axis=0, keepdims=True)
    m_ref[...] = m_new
    pb = p.astype(BF16)
    pv = jnp.concatenate([_dot(v_t, pb[:, :half]), _dot(v_t, pb[:, half:])], axis=1)
    acc_ref[...] = alpha * acc_ref[...] + pv


def _attn_prompt_kernel(nqt_ref, nqrt_ref, nglt_ref, ngt_ref, kc_ref, vct_ref, sk_ref, svt_ref, wk_ref, wvt_ref,
                        et_ref, y_ref, lhs_scr, lhsr_scr, ocmp_scr, t_scr, sel_scr, m_scr, l_scr, acc_scr,
                        s_scr, smax_scr):
    i = pl.program_id(1)
    qb = Q_BLOCK
    q0 = i * qb
    cols8 = 2 * NSA_GROUP * qb
    qpos = q0 + lax.broadcasted_iota(jnp.int32, (1, qb), 1)
    lhs_scr[...] = _stack_heads_t(nqt_ref)
    lhsr_scr[...] = _stack_heads_t(nqrt_ref)

    n_cmp = kc_ref.shape[1]
    s_all = _dot(kc_ref[0].astype(BF16), lhs_scr[...])
    cend = (lax.broadcasted_iota(jnp.int32, (n_cmp, 1), 0) + 1) * CMP_BLOCK - 1
    cmask = cend <= qpos
    vct = vct_ref[0].astype(BF16)
    imps = []
    for k in range(KV_HEADS):
        imp = None
        for g in range(NSA_GROUP):
            c0 = (k * NSA_GROUP + g) * qb
            s = jnp.where(cmask, s_all[:, c0:c0 + qb], NEG_INF)
            p = jnp.exp(s - jnp.max(s, axis=0, keepdims=True))
            p = p / jnp.sum(p, axis=0, keepdims=True) * cmask.astype(F32)
            imp = p if imp is None else imp + p
            ocmp_scr[:, c0:c0 + qb] = _dot(vct, p.astype(BF16))
        imps.append(imp)

    @pl.when(q0 + qb <= TOP_N * SLC_BLOCK)
    def _():
        sel_scr[...] = jnp.ones(sel_scr.shape, BF16)

    @pl.when(q0 + qb > TOP_N * SLC_BLOCK)
    def _():
        n_slc = n_cmp // 2
        blk = lax.broadcasted_iota(jnp.int32, (n_slc, 1), 0)
        valid = blk * SLC_BLOCK <= qpos
        forced = (blk == 0) | (blk == qpos // SLC_BLOCK)
        for k in range(KV_HEADS):
            t_scr[...] = imps[k]
            pair = t_scr[pl.ds(0, n_slc, stride=2), :] + t_scr[pl.ds(1, n_slc, stride=2), :]
            score = jnp.where(forced, FORCED_SCORE, jnp.where(valid, pair, NEG_INF))
            sel = _rank_select_rows(score, n_slc)
            sel_scr[k] = jnp.concatenate([sel, jnp.zeros((LANES - n_slc, qb), F32)], axis=0).astype(BF16)

    tk = SLC_KEY_TILE
    n_tiles = (q0 + qb - 1) // tk + 1
    m_scr[...] = jnp.full(m_scr.shape, M_INIT, F32)
    l_scr[...] = jnp.zeros(l_scr.shape, F32)
    acc_scr[...] = jnp.zeros(acc_scr.shape, F32)

    def scores(t):
        k0 = pl.multiple_of(t * tk, tk)
        vis = (k0 + lax.broadcasted_iota(jnp.int32, (tk, 1), 0)) <= qpos
        et = et_ref[pl.ds(k0, tk), :]
        caps = [jnp.where((_dot(et, sel_scr[k]) > 0.5) & vis, CAP_OPEN, NEG_INF) for k in range(KV_HEADS)]
        cap = jnp.concatenate([caps[0]] * NSA_GROUP + [caps[1]] * NSA_GROUP, axis=1)
        s = jnp.minimum(_dot(sk_ref[pl.ds(k0, tk), :], lhsr_scr[...]), cap)
        s_scr[t % 2] = s
        smax_scr[t % 2] = jnp.max(s, axis=0, keepdims=True)

    def update(t):
        k0 = pl.multiple_of(t * tk, tk)
        _softmax_tile_t(s_scr[t % 2], svt_ref[0, :, pl.ds(k0, tk)], m_scr.at[0], l_scr.at[0], acc_scr.at[0],
                        smax_scr[t % 2])

    def slc_step(t, carry):
        scores(t + 1)
        update(t)
        return carry

    scores(0)
    lax.fori_loop(0, n_tiles - 1, slc_step, 0)
    update(n_tiles - 1)

    wlen = WINDOW + qb
    w0 = pl.multiple_of(jnp.maximum(i - WINDOW // qb, 0) * qb, qb)
    dist = qpos - (w0 + lax.broadcasted_iota(jnp.int32, (wlen, 1), 0))
    wcap = jnp.where((dist >= 0) & (dist <= WINDOW), CAP_OPEN, NEG_INF)
    s = jnp.minimum(_dot(wk_ref[pl.ds(w0, wlen), :], lhsr_scr[...]), jnp.concatenate([wcap] * (2 * NSA_GROUP), axis=1))
    _softmax_tile_t(s, wvt_ref[0, :, pl.ds(w0, wlen)], m_scr.at[1], l_scr.at[1], acc_scr.at[1])

    o_slc = acc_scr[0] / l_scr[0]
    o_win = acc_scr[1] / l_scr[1]
    sig = _sigmoid(nglt_ref[0])
    for pair in range(NSA_HEADS // 2):
        tiles = []
        for h in (2 * pair, 2 * pair + 1):
            k = h // NSA_GROUP
            rows = slice(k * NSA_HD, (k + 1) * NSA_HD)
            cols = slice(h * qb, (h + 1) * qb)
            gate = lambda r: sig[r * NSA_HEADS + h:r * NSA_HEADS + h + 1, :]
            o = gate(0) * ocmp_scr[rows, cols] + gate(1) * o_slc[rows, cols] + gate(2) * o_win[rows, cols]
            ng = ngt_ref[0, h * NSA_HD:(h + 1) * NSA_HD, :]
            tiles.append(o * (ng * _sigmoid(ng)))
        y_ref[:, pair * LANES:(pair + 1) * LANES] = jnp.concatenate(tiles, axis=0).T.astype(y_ref.dtype)


def _attn_prompt(nq_t, nqr_t, ngl_t, ng_t, kc, vc_t, sk_rm, sv_t, wk_rm, wv_t):
    b, _, s = sv_t.shape
    assert s // CMP_BLOCK == LANES, "one lane per compressed block"
    qb = Q_BLOCK
    nqb = s // qb
    cols8 = 2 * NSA_GROUP * qb
    expand_t = (jnp.arange(s)[:, None] // SLC_BLOCK == jnp.arange(LANES)[None, :]).astype(BF16)
    qcol = lambda r: pl.BlockSpec((1, r, qb), lambda bb, i: (bb, 0, i))
    per_b = lambda shape: pl.BlockSpec((1,) + shape, lambda bb, i: (bb, 0, 0))
    rows_b = pl.BlockSpec((s, KV_WIDTH), lambda bb, i: (bb, 0))
    return pl.pallas_call(
        _attn_prompt_kernel,
        grid=(b, nqb),
        in_specs=[qcol(512), qcol(512), qcol(LANES), qcol(512), per_b(kc.shape[1:]), per_b(vc_t.shape[1:]),
                  rows_b, per_b((KV_WIDTH, s)), rows_b, per_b((KV_WIDTH, s)),
                  pl.BlockSpec(expand_t.shape, lambda bb, i: (0, 0))],
        out_specs=pl.BlockSpec((qb, 512), lambda bb, i: (bb * nqb + i, 0)),
        out_shape=jax.ShapeDtypeStruct((b * s, 512), BF16),
        scratch_shapes=[pltpu.VMEM((KV_WIDTH, cols8), BF16), pltpu.VMEM((KV_WIDTH, cols8), BF16),
                        pltpu.VMEM((KV_WIDTH, cols8), F32), pltpu.VMEM((LANES, qb), F32),
                        pltpu.VMEM((KV_HEADS, LANES, qb), BF16),
                        pltpu.VMEM((2, 1, cols8), F32), pltpu.VMEM((2, 1, cols8), F32),
                        pltpu.VMEM((2, KV_WIDTH, cols8), F32),
                        pltpu.VMEM((2, SLC_KEY_TILE, cols8), F32), pltpu.VMEM((2, 1, cols8), F32)],
        compiler_params=_params(("arbitrary", "arbitrary")),
        name="attn_prompt",
    )(nq_t, nqr_t, ngl_t, ng_t, kc, vc_t, sk_rm, sv_t, wk_rm, wv_t, expand_t)


def _stack_heads(q_ref):
    lane = lax.broadcasted_iota(jnp.int32, (1, LANES), 1)
    lo = lane < NSA_HD
    slabs = [q_ref[:, g * LANES:(g + 1) * LANES] for g in range(NSA_GROUP)]
    zero = jnp.zeros_like(slabs[0])
    stacked = jnp.concatenate([jnp.where(lo, s, zero) for s in slabs] + [jnp.where(lo, zero, s) for s in slabs], axis=0)
    return stacked.astype(BF16)


def _mask_rows(sc, masks, rows):
    n = sc.shape[-1]
    s4 = sc.reshape(2 * NSA_GROUP, rows, n)
    out = [jnp.where(masks[k][None], s4[k * NSA_GROUP:(k + 1) * NSA_GROUP], NEG_INF) for k in range(KV_HEADS)]
    return jnp.concatenate(out, axis=0).reshape(2 * NSA_GROUP * rows, n)


def _pair_scores(imp, qpos, n_lanes):
    lane = lax.broadcasted_iota(jnp.int32, (1, n_lanes), 1)
    even = (lane % 2) == 0
    ps = imp + jnp.where(even, pltpu.roll(imp, n_lanes - 1, 1), pltpu.roll(imp, 1, 1))
    blk = lane // 2
    valid = blk * SLC_BLOCK <= qpos
    forced = (blk == 0) | (blk == qpos // SLC_BLOCK)
    return jnp.where(forced, FORCED_SCORE, jnp.where(valid, ps, NEG_INF))


def _rank_select(score, n_lanes, extra_forced):
    lane = lax.broadcasted_iota(jnp.int32, (1, n_lanes), 1)
    blk = lane // 2
    rank = jnp.zeros(score.shape, jnp.int32)
    for j in range(0, n_lanes, 2):
        col = score[:, j:j + 1]
        beats = (col > score) | ((col == score) & ((j // 2) < blk))
        rank = rank + beats.astype(jnp.int32)
    if extra_forced:
        rank = rank + (score < FORCED_SCORE).astype(jnp.int32)
    return rank < TOP_N


def _merge_kv_heads(acc, rows):
    lane = lax.broadcasted_iota(jnp.int32, (1, LANES), 1)
    lo = lane < NSA_HD
    half = NSA_GROUP * rows
    return [jnp.where(lo, acc[g * rows:(g + 1) * rows], acc[half + g * rows:half + (g + 1) * rows])
            for g in range(NSA_GROUP)]


def _gate_and_store(o_cmp, o_slc, o_win, ngl_ref, ng_ref, y_ref):
    lane = lax.broadcasted_iota(jnp.int32, (1, LANES), 1)
    lo = lane < NSA_HD
    sig = _sigmoid(ngl_ref[...])
    for g in range(NSA_GROUP):
        gates = [jnp.where(lo, sig[:, r * 8 + g:r * 8 + g + 1], sig[:, r * 8 + 4 + g:r * 8 + 4 + g + 1]) for r in range(3)]
        o = gates[0] * o_cmp[g] + gates[1] * o_slc[g] + gates[2] * o_win[g]
        gate = ng_ref[:, g * LANES:(g + 1) * LANES]
        y_ref[:, g * LANES:(g + 1) * LANES] = (o * (gate * _sigmoid(gate))).astype(y_ref.dtype)


def _page_copies(pt_ref, b, chunk, slot, caches, bufs, sems, g):
    out = []
    for ci, (cache, buf) in enumerate(zip(caches, bufs)):
        for j in range(g):
            page = pt_ref[b, chunk * g + j]
            out.append(pltpu.make_async_copy(cache.at[page], buf.at[slot, j], sems.at[slot, ci]))
    return out


def _stream_pages(pt_ref, caches, bufs, sems, g, n_chunks, n_batch, compute):
    b = pl.program_id(0)
    total = n_batch * n_chunks
    copies = functools.partial(_page_copies, pt_ref, caches=caches, bufs=bufs, sems=sems, g=g)

    @pl.when(b == 0)
    def _():
        for cp in copies(0, 0, 0):
            cp.start()

    def chunk(c, carry):
        flat = b * n_chunks + c
        slot = flat % 2
        nxt = jnp.minimum(flat + 1, total - 1)
        for cp in copies(nxt // n_chunks, nxt % n_chunks, 1 - slot):
            cp.start()
        for cp in copies(b, c, slot):
            cp.wait()
        compute(c, slot)
        return carry

    lax.fori_loop(0, n_chunks, chunk, 0)

    @pl.when(b == n_batch - 1)
    def _():
        for cp in copies(n_batch - 1, n_chunks - 1, total % 2):
            cp.wait()


def _attn_sample_kernel(pt_ref, sk_hbm, sv_hbm, nq_ref, nqr_ref, ngl_ref, ng_ref, kc_ref, vc_ref, skn_ref, svn_ref,
                        wko_ref, wvo_ref, wkn_ref, wvn_ref, e_ref, y_ref, kbuf, vbuf, sems, m_scr, l_scr, acc_scr,
                        ocmp_scr, owin_scr, mask_scr, *, n_pages, n_steps, n_batch, t_len, past_len):
    t = t_len
    nq_ref, nqr_ref, ngl_ref, ng_ref, y_ref = (r.at[0] for r in (nq_ref, nqr_ref, ngl_ref, ng_ref, y_ref))
    rows8 = 2 * NSA_GROUP * t
    half = NSA_GROUP * t
    tpos = lax.broadcasted_iota(jnp.int32, (t, 1), 0)
    qpos = past_len + tpos
    n_cmp = kc_ref.shape[1]
    wb = wko_ref.shape[2]
    step_keys = n_pages * PAGE_SIZE

    def before_pages():
        lhs = _stack_heads(nq_ref)
        s = _dot_nt(lhs, kc_ref[0].astype(BF16)) * ATTN_SCALE
        m = jnp.max(s, axis=-1, keepdims=True)
        p = jnp.exp(s - m)
        p = p / jnp.sum(p, axis=-1, keepdims=True)
        oc = _dot(p.astype(BF16), vc_ref[0].astype(BF16))
        for g, slab in enumerate(_merge_kv_heads(oc, t)):
            ocmp_scr[g] = slab
        p4 = p.reshape(8, t, n_cmp)
        for k in range(KV_HEADS):
            imp = p4[k * NSA_GROUP]
            for g in range(1, NSA_GROUP):
                imp = imp + p4[k * NSA_GROUP + g]
            sel = _rank_select(_pair_scores(imp, qpos, n_cmp), n_cmp, True).astype(BF16)
            for c in range(n_cmp // LANES):
                km = _dot(sel[:, c * LANES:(c + 1) * LANES], e_ref[...])
                per = LANES * CMP_BLOCK // step_keys
                for u in range(per):
                    mask_scr[c * per + u, k] = km[:, u * step_keys:(u + 1) * step_keys]

        lhs_r = _stack_heads(nqr_ref)
        so = _dot(lhs_r, wko_ref[0].astype(BF16)) * ATTN_SCALE
        sn = _dot(lhs_r, wkn_ref[0].astype(BF16)) * ATTN_SCALE
        jo = lax.broadcasted_iota(jnp.int32, (1, wb), 1)
        jn = lax.broadcasted_iota(jnp.int32, (1, LANES), 1)
        old_vis = jo >= tpos
        so = _mask_rows(so, [old_vis, old_vis], t)
        new_vis = jn <= tpos
        sn_w = _mask_rows(sn, [new_vis, new_vis], t)
        m = jnp.maximum(jnp.max(so, axis=-1, keepdims=True), jnp.max(sn_w, axis=-1, keepdims=True))
        po = jnp.exp(so - m)
        pn = jnp.exp(sn_w - m)
        den = jnp.sum(po, axis=-1, keepdims=True) + jnp.sum(pn, axis=-1, keepdims=True)
        pob, pnb = (po / den).astype(BF16), (pn / den).astype(BF16)
        wvo, wvn = wvo_ref[0].astype(BF16), wvn_ref[0].astype(BF16)
        ow = jnp.concatenate([_dot_nt(pob[:half], wvo) + _dot_nt(pnb[:half], wvn),
                              _dot_nt(pob[half:], wvo) + _dot_nt(pnb[half:], wvn)], axis=0)
        for g, slab in enumerate(_merge_kv_heads(ow, t)):
            owin_scr[g] = slab

        s2 = _dot(lhs_r, skn_ref[0].astype(BF16)) * ATTN_SCALE
        s2 = _mask_rows(s2, [new_vis, new_vis], t)
        m2 = jnp.max(s2, axis=-1, keepdims=True)
        p2 = jnp.exp(s2 - m2)
        m_scr[...] = m2
        l_scr[...] = jnp.sum(p2, axis=-1, keepdims=True)
        p2b = p2.astype(BF16)
        svn = svn_ref[0].astype(BF16)
        acc_scr[...] = jnp.concatenate([_dot_nt(p2b[:half], svn), _dot_nt(p2b[half:], svn)], axis=0)

    def on_pages(step, slot):
        lhs_r = _stack_heads(nqr_ref)
        sc = jnp.concatenate([_dot(lhs_r, kbuf[slot, j].astype(BF16)) for j in range(n_pages)], axis=1) * ATTN_SCALE
        msk = mask_scr[step]
        sc = _mask_rows(sc, [msk[k] > 0.5 for k in range(KV_HEADS)], t)
        m_i = m_scr[...]
        m_new = jnp.maximum(m_i, jnp.max(sc, axis=-1, keepdims=True))
        alpha = jnp.exp(m_i - m_new)
        pr = jnp.exp(sc - m_new)
        l_scr[...] = alpha * l_scr[...] + jnp.sum(pr, axis=-1, keepdims=True)
        m_scr[...] = m_new
        pb = pr.astype(BF16)
        pv = jnp.zeros((rows8, LANES), F32)
        for j in range(n_pages):
            vt = vbuf[slot, j].astype(BF16)
            pj = pb[:, j * PAGE_SIZE:(j + 1) * PAGE_SIZE]
            pv = pv + jnp.concatenate([_dot_nt(pj[:half], vt), _dot_nt(pj[half:], vt)], axis=0)
        acc_scr[...] = alpha * acc_scr[...] + pv

    before_pages()
    _stream_pages(pt_ref, (sk_hbm, sv_hbm), (kbuf, vbuf), sems, n_pages, n_steps, n_batch, on_pages)
    o_slc = _merge_kv_heads(acc_scr[...] / l_scr[...], t)
    o_cmp = [ocmp_scr[g] for g in range(NSA_GROUP)]
    o_win = [owin_scr[g] for g in range(NSA_GROUP)]
    _gate_and_store(o_cmp, o_slc, o_win, ngl_ref, ng_ref, y_ref)


def _attn_sample(page_table, cache_sk_t, cache_sv_t, nq, nqr, ngl, ng, kc, vc, skn, svn, wko, wvo, wkn, wvn, t_len):
    db, n_pages = page_table.shape
    g = SAMPLE_PAGES_PER_STEP
    steps = n_pages // g
    past_len = n_pages * PAGE_SIZE
    step_keys = g * PAGE_SIZE
    span = LANES * CMP_BLOCK
    expand = (jnp.arange(span)[None, :] // CMP_BLOCK == jnp.arange(LANES)[:, None]).astype(BF16)

    hbm = pl.BlockSpec(memory_space=pl.ANY)
    row = lambda w: pl.BlockSpec((1, t_len, w), lambda b, pt: (b, 0, 0))
    per_b = lambda shape: pl.BlockSpec((1,) + shape, lambda b, pt: (b, 0, 0))
    rows8 = 2 * NSA_GROUP * t_len
    page_buf = pltpu.VMEM((2, g, KV_WIDTH, PAGE_SIZE), F32)
    grid_spec = pltpu.PrefetchScalarGridSpec(
        num_scalar_prefetch=1,
        grid=(db,),
        in_specs=[hbm, hbm, row(512), row(512), row(LANES), row(512), per_b(kc.shape[1:]), per_b(vc.shape[1:]),
                  per_b(skn.shape[1:]), per_b(svn.shape[1:]), per_b(wko.shape[1:]), per_b(wvo.shape[1:]),
                  per_b(wkn.shape[1:]), per_b(wvn.shape[1:]), pl.BlockSpec(expand.shape, lambda b, pt: (0, 0))],
        out_specs=row(512),
        scratch_shapes=[page_buf, page_buf, pltpu.SemaphoreType.DMA((2, 2)),
                        pltpu.VMEM((rows8, 1), F32), pltpu.VMEM((rows8, 1), F32), pltpu.VMEM((rows8, LANES), F32),
                        pltpu.VMEM((NSA_GROUP, t_len, LANES), F32), pltpu.VMEM((NSA_GROUP, t_len, LANES), F32),
                        pltpu.VMEM((steps, KV_HEADS, t_len, step_keys), F32)],
    )
    return pl.pallas_call(
        functools.partial(_attn_sample_kernel, n_pages=g, n_steps=steps, n_batch=db, t_len=t_len, past_len=past_len),
        grid_spec=grid_spec,
        out_shape=jax.ShapeDtypeStruct((db, t_len, 512), F32),
        compiler_params=_params(("arbitrary",)),
        name="attn_sample",
    )(page_table, cache_sk_t, cache_sv_t, nq, nqr, ngl, ng, kc, vc, skn, svn, wko, wvo, wkn, wvn, expand)


def _finish_kernel(x_ref, yr_ref, yn_ref, p_ref, wo_ref, gple_ref, wg_ref, wp_ref, gf_ref, o_ref):
    x = x_ref[...]
    x = x + _dot(yr_ref[...].astype(BF16), wo_ref[0:RET_WIDTH, :]) + _dot(yn_ref[...].astype(BF16), wo_ref[RET_WIDTH:, :])
    ms = jnp.mean(x * x, axis=-1, keepdims=True)
    hn = (x * lax.rsqrt(ms + RMS_EPS) * gple_ref[...]).astype(BF16)
    gate = _sigmoid(_dot(hn, wg_ref[...]))
    x = x + gate * _dot(p_ref[...].astype(BF16), wp_ref[...])
    ms = jnp.mean(x * x, axis=-1, keepdims=True)
    o_ref[...] = x * lax.rsqrt(ms + RMS_EPS) * gf_ref[...]


def _finish(x2d, y_ret, y_nsa, p2d, w_out, norm_ple, w_gate, w_ple, norm_f, tm):
    n = x2d.shape[0]
    row = lambda w: pl.BlockSpec((tm, w), lambda i: (i, 0))
    const = lambda shape: pl.BlockSpec(shape, lambda i: (0, 0))
    return pl.pallas_call(
        _finish_kernel,
        grid=(n // tm,),
        in_specs=[row(D_MODEL), row(RET_WIDTH), row(NSA_WIDTH), row(PLE_DIM), const(w_out.shape), const((1, D_MODEL)),
                  const(w_gate.shape), const(w_ple.shape), const((1, D_MODEL))],
        out_specs=row(D_MODEL),
        out_shape=jax.ShapeDtypeStruct((n, D_MODEL), F32),
        compiler_params=_params(("arbitrary",)),
        name="finish",
    )(x2d, y_ret, y_nsa, p2d, w_out, norm_ple, w_gate, w_ple, norm_f)


def _slab_perm():
    return np.array([(k * NSA_GROUP + g) * NSA_HD + d for g in range(NSA_GROUP) for k in range(KV_HEADS)
                     for d in range(NSA_HD)], np.int32)


def _to_positions_major(x_t):
    lead = x_t.shape[:-2]
    n = len(lead)
    x4 = x_t.reshape(lead + (KV_HEADS, NSA_HD, x_t.shape[-1]))
    return jnp.transpose(x4, tuple(range(n)) + (n + 2, n, n + 1))


def _cache_t(cache):
    n_pool, page = cache.shape[:2]
    return jnp.transpose(cache, (0, 2, 3, 1)).reshape(n_pool, KV_WIDTH, page)


def _layer(xp, xs, c_ck, c_cv, c_sk, c_sv, win_k, win_v, ret_state, page_table, p_p, p_s, norm_mix, w_in, gn_g, gn_b,
           pe_k, w1_k, w2_k, pe_v, w1_v, w2_v, w_out, norm_ple, w_gate, w_ple, norm_f):
    b, s, d = xp.shape
    db, t, _ = xs.shape
    n_pages = page_table.shape[1]
    past = n_pages * PAGE_SIZE

    off = np.cumsum((0,) + SPLIT_SIZES)
    col = lambda i: w_in[:, off[i]:off[i + 1]]
    perm = _slab_perm()
    ngl_w = jnp.pad(col(11), ((0, 0), (0, LANES - N_GATES)))
    ret_cols = [col(0), col(1), col(2), col(3)]
    kv_cols = [col(5), col(6), col(8), col(10), col(7), col(9)]
    w_rm_p = jnp.concatenate(ret_cols + [col(7), col(9)], axis=1).astype(BF16)
    w_t_p = jnp.concatenate(kv_cols + [col(4), col(12), ngl_w], axis=1).T.astype(BF16)
    w_rm_s = jnp.concatenate(ret_cols + [col(4)[:, perm], col(12)[:, perm], ngl_w], axis=1).astype(BF16)
    w_t_s = jnp.concatenate(kv_cols, axis=1).T.astype(BF16)
    w_out_b = w_out.astype(BF16)
    w_out_slab = jnp.concatenate([w_out[:RET_WIDTH], w_out[RET_WIDTH:][perm]], axis=0).astype(BF16)
    g_mix = norm_mix.reshape(1, d)
    cwk = _compress_weights(pe_k, w1_k, w2_k)
    cwv = _compress_weights(pe_v, w1_v, w2_v)
    cwv_t = _compress_weights(pe_v, w1_v, w2_v, transpose_w2=True)
    gn_g2, gn_b2 = gn_g.reshape(1, RET_WIDTH), gn_b.reshape(1, RET_WIDTH)
    fin_w = (norm_ple.reshape(1, d), w_gate.astype(BF16), w_ple.astype(BF16), norm_f.reshape(1, d))

    tm = 512
    (q_ret, k_ret, v_ret, rg, sk_rm, wk_rm, ck_t, cv_t, sv_t, wv_t, sk_t, wk_t, sv_b, wv_b, nq_t, nqr_t, ng_t,
     ngl_t) = _proj(xp.reshape(b * s, d), jnp.arange(s), g_mix, w_rm_p, w_t_p, b, tm, True)
    y_ret, ret_p = _retention(q_ret, k_ret, v_ret, rg, jnp.zeros((b, RET_HEADS, RET_DK, RET_DK), F32), gn_g2, gn_b2,
                              b, s // RET_CHUNK, RET_CHUNK)
    kc, vc_t = _compress_prompt(ck_t, cv_t, cwk, cwv_t)
    y_nsa = _attn_prompt(nq_t, nqr_t, ngl_t, ng_t, kc, vc_t, sk_rm, sv_b, wk_rm, wv_b)
    y_prompt = _finish(xp.reshape(b * s, d), y_ret, y_nsa, p_p.reshape(b * s, PLE_DIM), w_out_b, *fin_w, tm)
    wb_p = min(WINDOW, s)
    prompt_states = (ret_p, _to_positions_major(ck_t), _to_positions_major(cv_t), _to_positions_major(sk_t),
                     _to_positions_major(sv_t), _to_positions_major(wk_t[:, :, s - wb_p:]),
                     _to_positions_major(wv_t[:, :, s - wb_p:]))

    n = db * t
    pos_s = jnp.tile(past + jnp.arange(t), db)
    (q_ret, k_ret, v_ret, rg, nq, nqr, ng, ngl, ck_n, cv_n, sv_n, wv_n, sk_n, wk_n) = _proj(
        xs.reshape(n, d), pos_s, g_mix, w_rm_s, w_t_s, 1, n, False)
    pad = lambda a: jnp.pad(a.reshape(db, t, RET_WIDTH), ((0, 0), (0, RET_CHUNK - t), (0, 0))).reshape(db * RET_CHUNK, RET_WIDTH)
    y_ret_pad, ret_s = _retention(pad(q_ret), pad(k_ret), pad(v_ret), pad(rg), ret_state, gn_g2, gn_b2, db, 1, t)
    y_ret = y_ret_pad.reshape(db, RET_CHUNK, RET_WIDTH)[:, :t].reshape(n, RET_WIDTH)
    kc, vc = _compress_sample(page_table, _cache_t(c_ck), _cache_t(c_cv), cwk, cwv)
    per_batch = lambda a: jnp.transpose(a[0].reshape(KV_WIDTH, db, t), (1, 0, 2))
    lane_pad = lambda a: jnp.pad(per_batch(a), ((0, 0), (0, 0), (0, LANES - t)))
    win_t = lambda w: jnp.transpose(w, (0, 2, 3, 1)).reshape(db, KV_WIDTH, w.shape[1])
    rows3 = lambda a: a.astype(F32).reshape(db, t, a.shape[-1])
    y_nsa = _attn_sample(page_table, _cache_t(c_sk), _cache_t(c_sv), rows3(nq), rows3(nqr), rows3(ngl), rows3(ng), kc, vc,
                         lane_pad(sk_n), lane_pad(sv_n), win_t(win_k), win_t(win_v), lane_pad(wk_n), lane_pad(wv_n), t)
    y_nsa = y_nsa.reshape(n, NSA_WIDTH)
    y_sample = _finish(xs.reshape(n, d), y_ret, y_nsa, p_s.reshape(n, PLE_DIM), w_out_slab, *fin_w, n)
    new_rows = lambda a: _to_positions_major(per_batch(a))
    sample_states = (ret_s, new_rows(ck_n), new_rows(cv_n), new_rows(sk_n), new_rows(sv_n),
                     jnp.concatenate([win_k[:, t:], new_rows(wk_n)], axis=1),
                     jnp.concatenate([win_v[:, t:], new_rows(wv_n)], axis=1))
    return y_prompt.reshape(b, s, d), y_sample.reshape(db, t, d), prompt_states, sample_states


def kernel(x_prompt, x_sample, cache_cmp_k, cache_cmp_v, cache_slc_k, cache_slc_v, state_win_k, state_win_v, state_ret, page_table, p_prompt, p_sample, norm_mix, w_in, ret_gn_g, ret_gn_b, cmp_pe_k, cmp_w1_k, cmp_w2_k, cmp_pe_v, cmp_w1_v, cmp_w2_v, w_out, norm_ple, w_ple_gate, w_ple, norm_f):
    depth = w_in.shape[0]
    assert depth == 1, "single trunk layer"
    l = 0
    yp, ys, sp, ss = _layer(x_prompt, x_sample, cache_cmp_k[l], cache_cmp_v[l], cache_slc_k[l], cache_slc_v[l],
                            state_win_k[l], state_win_v[l], state_ret[l], page_table, p_prompt[l], p_sample[l],
                            norm_mix[l], w_in[l], ret_gn_g[l], ret_gn_b[l], cmp_pe_k[l], cmp_w1_k[l], cmp_w2_k[l],
                            cmp_pe_v[l], cmp_w1_v[l], cmp_w2_v[l], w_out[l], norm_ple[l], w_ple_gate[l], w_ple[l], norm_f)
    return (yp, ys) + tuple(a[None] for a in sp) + tuple(a[None] for a in ss)
```

```python
import functools

import numpy as np
import jax
import jax.numpy as jnp
from jax import lax
from jax.experimental import pallas as pl
from jax.experimental.pallas import tpu as pltpu

F32 = jnp.float32
BF16 = jnp.bfloat16

D_MODEL = 1024
PLE_DIM = 256
PAGE_SIZE = 128
RET_HEADS = 4
RET_DK = 128
RET_WIDTH = 512
RET_CHUNK = 128
NSA_HEADS = 8
KV_HEADS = 2
NSA_GROUP = 4
NSA_HD = 64
NSA_WIDTH = 512
KV_WIDTH = 128
CMP_BLOCK = 32
SLC_BLOCK = 64
TOP_N = 16
WINDOW = 512
Q_BLOCK = 128
ROPE_THETA = 10000.0
RMS_EPS = 1e-6
GN_EPS = 1e-5
NEG_INF = -1e9
FORCED_SCORE = 1e4
M_INIT = -1e30
CAP_OPEN = 3e38
ATTN_SCALE = NSA_HD ** -0.5
SPLIT_SIZES = (512, 512, 512, 512, 512, 128, 128, 128, 128, 128, 128, 24, 512)
N_GATES = 3 * NSA_HEADS

LANES = 128
SUBLANES = 8
SLC_KEY_TILE = 512
RET_CHUNKS_PER_STEP = 4
SAMPLE_PAGES_PER_STEP = 64
SAMPLE_STREAMS = 1
COMPRESS_PAGES_PER_STEP = 32
PAGE_SLOTS = 3
BLOCK_PITCH = 40
VMEM_LIMIT = 56 * 1024 * 1024


def _dot(a, b):
    return jnp.dot(a, b, preferred_element_type=F32)


def _dot_nt(a, b):
    return lax.dot_general(a, b, (((1,), (1,)), ((), ())), preferred_element_type=F32)


def _sigmoid(x):
    return 1.0 / (1.0 + jnp.exp(-x))


def _params(sem):
    return pltpu.CompilerParams(dimension_semantics=sem, vmem_limit_bytes=VMEM_LIMIT)


def _rope_lanes(a, cos, sin_signed, half):
    if 2 * half == LANES:
        partner = pltpu.roll(a, half, 1)
    else:
        lane = lax.broadcasted_iota(jnp.int32, (1, LANES), 1)
        partner = jnp.where((lane % (2 * half)) < half, pltpu.roll(a, LANES - half, 1), pltpu.roll(a, half, 1))
    return a * cos + partner * sin_signed


def _rope_sublanes(blk, cos_t, sin_t):
    outs = []
    for h in range(blk.shape[0] // NSA_HD):
        x1 = blk[h * NSA_HD:h * NSA_HD + NSA_HD // 2]
        x2 = blk[h * NSA_HD + NSA_HD // 2:(h + 1) * NSA_HD]
        outs += [x1 * cos_t - x2 * sin_t, x2 * cos_t + x1 * sin_t]
    return jnp.concatenate(outs, axis=0)


def _proj_kernel(*refs, prompt):
    (x_ref, g_ref, wrm_ref, wt_ref, cosr_ref, sinr_ref, cosn_ref, sinn_ref, cost_ref, sint_ref,
     qret_ref, kret_ref, vret_ref, rg_ref) = refs[:14]
    x = x_ref[...]
    ms = jnp.mean(x * x, axis=-1, keepdims=True)
    h = (x * lax.rsqrt(ms + RMS_EPS) * g_ref[...]).astype(BF16)

    def proj(lo, hi):
        return _dot(h, wrm_ref[:, lo:hi])

    cr, sr = cosr_ref[...], sinr_ref[...]
    cn, sn = cosn_ref[...], sinn_ref[...]
    ct, st = cost_ref[...], sint_ref[...]
    yq = proj(0, 512)
    yk = proj(512, 1024)
    for hh in range(RET_HEADS):
        sl = slice(hh * LANES, (hh + 1) * LANES)
        qret_ref[:, sl] = _rope_lanes(yq[:, sl], cr, sr, RET_DK // 2)
        kret_ref[:, sl] = _rope_lanes(yk[:, sl], cr, sr, RET_DK // 2) * (RET_DK ** -0.5)
    vret_ref[...] = proj(1024, 1536)
    rg_ref[...] = proj(1536, 2048)

    yt = _dot_nt(wt_ref[...], h)
    if prompt:
        (skr_ref, wkr_ref, ck_ref, cv_ref, sv_ref, wv_ref, sk_ref, wk_ref, svb_ref, wvb_ref,
         nqt_ref, nqrt_ref, ngt_ref, nglt_ref) = refs[14:]
    else:
        nq_ref, nqr_ref, ng_ref, ngl_ref, ck_ref, cv_ref, sv_ref, wv_ref, sk_ref, wk_ref = refs[14:]
    ck_ref[0] = yt[0:128]
    cv_ref[0] = yt[128:256]
    sv_ref[0] = yt[256:384]
    wv_ref[0] = yt[384:512]
    sk_ref[0] = _rope_sublanes(yt[512:640], ct, st)
    wk_ref[0] = _rope_sublanes(yt[640:768], ct, st)
    if prompt:
        svb_ref[0] = yt[256:384].astype(BF16)
        wvb_ref[0] = yt[384:512].astype(BF16)
        nqt = yt[768:1280]
        nqt_ref[0] = nqt.astype(BF16)
        nqrt_ref[0] = _rope_sublanes(nqt, ct, st).astype(BF16)
        ngt_ref[0] = yt[1280:1792]
        nglt_ref[0] = yt[1792:1920]
        skr_ref[...] = _rope_lanes(proj(2048, 2176), cn, sn, NSA_HD // 2).astype(BF16)
        wkr_ref[...] = _rope_lanes(proj(2176, 2304), cn, sn, NSA_HD // 2).astype(BF16)
    else:
        yn = proj(2048, 2560)
        for gg in range(NSA_GROUP):
            sl = slice(gg * LANES, (gg + 1) * LANES)
            nq_ref[:, sl] = yn[:, sl].astype(BF16)
            nqr_ref[:, sl] = _rope_lanes(yn[:, sl], cn, sn, NSA_HD // 2).astype(BF16)
        ng_ref[...] = proj(2560, 3072)
        ngl_ref[...] = proj(3072, 3200)


def _rope_tables(pos, half):
    inv = ROPE_THETA ** (-jnp.arange(half, dtype=F32) / half)
    ang = pos.astype(F32)[:, None] * inv[None, :]
    return jnp.cos(ang), jnp.sin(ang)


def _proj(x2d, pos_rows, norm_g, w_rm, w_t, kv_batch, tm, prompt):
    n = x2d.shape[0]
    nt = pos_rows.shape[0] // tm
    skv = n // kv_batch
    nkt = skv // tm
    c64, s64 = _rope_tables(pos_rows, 64)
    c32, s32 = _rope_tables(pos_rows, 32)
    cosr = jnp.concatenate([c64, c64], axis=1)
    sinr = jnp.concatenate([-s64, s64], axis=1)
    cosn = jnp.concatenate([c32, c32, c32, c32], axis=1)
    sinn = jnp.concatenate([-s32, s32, -s32, s32], axis=1)
    cost, sint = c32.T, s32.T
    row = lambda w: pl.BlockSpec((tm, w), lambda i: (i, 0))
    tab = pl.BlockSpec((tm, LANES), lambda i: (i % nt, 0))
    tabt = pl.BlockSpec((32, tm), lambda i: (0, i % nt))
    tspec = lambda r: pl.BlockSpec((1, r, tm), lambda i: (i // nkt, 0, i % nkt))
    f = lambda w, dt: jax.ShapeDtypeStruct((n, w), dt)
    ts = lambda r, dt: jax.ShapeDtypeStruct((kv_batch, r, skv), dt)
    kvo, kvs = tspec(KV_WIDTH), ts(KV_WIDTH, F32)
    out_specs = [row(512)] * 4
    out_shape = [f(512, F32)] * 4
    if prompt:
        out_specs += [row(KV_WIDTH)] * 2 + [kvo] * 8 + [tspec(512)] * 3 + [tspec(LANES)]
        out_shape += [f(KV_WIDTH, BF16)] * 2 + [kvs] * 6 + [ts(KV_WIDTH, BF16)] * 2 + [ts(512, BF16)] * 2 + [
            ts(512, F32), ts(LANES, F32)]
    else:
        out_specs += [row(512)] * 3 + [row(LANES)] + [kvo] * 6
        out_shape += [f(512, BF16)] * 2 + [f(512, F32), f(LANES, F32)] + [kvs] * 6
    return pl.pallas_call(
        functools.partial(_proj_kernel, prompt=prompt),
        grid=(n // tm,),
        in_specs=[row(D_MODEL), pl.BlockSpec((1, D_MODEL), lambda i: (0, 0)),
                  pl.BlockSpec(w_rm.shape, lambda i: (0, 0)), pl.BlockSpec(w_t.shape, lambda i: (0, 0)),
                  tab, tab, tab, tab, tabt, tabt],
        out_specs=out_specs,
        out_shape=out_shape,
        compiler_params=_params(("arbitrary",)),
        name="proj_prompt" if prompt else "proj_sample",
    )(x2d, norm_g, w_rm, w_t, cosr, sinr, cosn, sinn, cost, sint)


def _ret_kernel(q_ref, k_ref, v_ref, rg_ref, st_ref, dm_ref, qd_ref, kd_ref, sd_ref, gg_ref, gb_ref,
                y_ref, so_ref, st_scr, *, n_chunks):
    c = pl.program_id(1)

    @pl.when(c == 0)
    def _():
        st_scr[...] = st_ref[0]

    for h in range(RET_HEADS):
        sl = slice(h * LANES, (h + 1) * LANES)
        state = st_scr[h]
        for cc in range(q_ref.shape[0] // RET_CHUNK):
            rows = slice(cc * RET_CHUNK, (cc + 1) * RET_CHUNK)
            q = q_ref[rows, sl]
            k = k_ref[rows, sl]
            v = v_ref[rows, sl].astype(BF16)
            s = _dot_nt(q.astype(BF16), k.astype(BF16)) * dm_ref[h]
            intra = _dot(s.astype(BF16), v)
            cross = _dot((q * qd_ref[:, sl]).astype(BF16), state.astype(BF16))
            o = intra + cross
            kd_t = (k * kd_ref[:, sl]).T.astype(BF16)
            state = sd_ref[h] * state + _dot(kd_t, v)
            mu = jnp.mean(o, axis=-1, keepdims=True)
            var = jnp.mean(jnp.square(o - mu), axis=-1, keepdims=True)
            on = (o - mu) * lax.rsqrt(var + GN_EPS)
            gate = rg_ref[rows, sl]
            y_ref[rows, sl] = (on * gg_ref[:, sl] + gb_ref[:, sl]) * (gate * _sigmoid(gate))
        st_scr[h] = state

    @pl.when(c == n_chunks - 1)
    def _():
        so_ref[0] = st_scr[...]


def _retention(q, k, v, rg, state, gn_g, gn_b, n_batch, n_chunks, chunk_len):
    c = RET_CHUNK
    log_g = jnp.log(1.0 - 2.0 ** (-5.0 - jnp.arange(RET_HEADS, dtype=F32)))
    i = jnp.arange(c, dtype=F32)
    diff = i[:, None] - i[None, :]
    causal = diff >= 0
    dmask = jnp.where(causal[None], jnp.exp(jnp.where(causal, diff, 0.0)[None] * log_g[:, None, None]), 0.0)
    expand = lambda t: jnp.repeat(t, LANES, axis=1)
    qdec = expand(jnp.exp((i[:, None] + 1.0) * log_g[None, :]))
    kdec = expand(jnp.exp((chunk_len - 1.0 - i)[:, None] * log_g[None, :]))
    sdec = jnp.broadcast_to(jnp.exp(chunk_len * log_g)[:, None, None], (RET_HEADS, 1, LANES))
    per_step = RET_CHUNKS_PER_STEP if n_chunks % RET_CHUNKS_PER_STEP == 0 else 1
    n_steps = n_chunks // per_step
    row = pl.BlockSpec((per_step * c, 512), lambda b, j: (b * n_steps + j, 0))
    stspec = pl.BlockSpec((1, RET_HEADS, 128, 128), lambda b, j: (b, 0, 0, 0))
    const = lambda shape: pl.BlockSpec(shape, lambda b, j: (0,) * len(shape))
    return pl.pallas_call(
        functools.partial(_ret_kernel, n_chunks=n_steps),
        grid=(n_batch, n_steps),
        in_specs=[row, row, row, row, stspec, const((RET_HEADS, c, c)), const((c, 512)), const((c, 512)),
                  const((RET_HEADS, 1, LANES)), const((1, 512)), const((1, 512))],
        out_specs=[row, stspec],
        out_shape=[jax.ShapeDtypeStruct(q.shape, F32), jax.ShapeDtypeStruct(state.shape, F32)],
        scratch_shapes=[pltpu.VMEM((RET_HEADS, 128, 128), F32)],
        compiler_params=_params(("arbitrary", "arbitrary")),
        name="retention",
    )(q, k, v, rg, state, dmask, qdec, kdec, sdec, gn_g, gn_b)


def _gelu_tanh(x):
    return 0.5 * x * (1.0 + jnp.tanh(np.sqrt(2.0 / np.pi).astype(np.float32) * (x + 0.044715 * (x * x * x))))


def _compress_one(tiles, pe_ref, w1_ref, w2_ref, scr, transpose_out=False):
    per_tile = LANES // CMP_BLOCK
    for t, tile in enumerate(tiles):
        rows_pm = tile.T
        for c in range(per_tile):
            r0 = (t * per_tile + c) * BLOCK_PITCH
            scr[r0:r0 + CMP_BLOCK, :] = rows_pm[c * CMP_BLOCK:(c + 1) * CMP_BLOCK]
    n_blk = len(tiles) * per_tile
    flat = [(scr[pl.ds(j, n_blk, stride=BLOCK_PITCH), :] + pe_ref[j:j + 1, :]).astype(BF16) for j in range(CMP_BLOCK)]
    hid = _gelu_tanh(_dot(jnp.concatenate(flat, axis=1), w1_ref[...]))
    if transpose_out:
        return _dot_nt(w2_ref[...], hid.astype(BF16))
    return _dot(hid.astype(BF16), w2_ref[...])


def _compress_prompt_kernel(k_ref, v_ref, pek_ref, w1k_ref, w2k_ref, pev_ref, w1v_ref, w2v_ref,
                            kc_ref, vct_ref, scr_k, scr_v, *, n_tiles):
    for src, pe, w1, w2, dst, scr, tr in ((k_ref, pek_ref, w1k_ref, w2k_ref, kc_ref, scr_k, False),
                                          (v_ref, pev_ref, w1v_ref, w2v_ref, vct_ref, scr_v, True)):
        tiles = [src[0, :, t * LANES:(t + 1) * LANES] for t in range(n_tiles)]
        dst[0] = _compress_one(tiles, pe, w1, w2, scr, tr)


def _compress_sample_kernel(pt_ref, ck_hbm, cv_hbm, pek_ref, w1k_ref, w2k_ref, pev_ref, w1v_ref, w2v_ref,
                            kc_ref, vc_ref, kbuf, vbuf, sems, scr_k, scr_v, *, n_pages, n_steps, n_batch):
    n_blk = n_pages * PAGE_SIZE // CMP_BLOCK

    def on_pages(step, slot):
        rows = pl.ds(pl.multiple_of(step * n_blk, n_blk), n_blk)
        kc_ref[0, rows, :] = _compress_one([kbuf[slot, j] for j in range(n_pages)], pek_ref, w1k_ref, w2k_ref, scr_k)
        vc_ref[0, rows, :] = _compress_one([vbuf[slot, j] for j in range(n_pages)], pev_ref, w1v_ref, w2v_ref, scr_v)

    _stream_pages(pt_ref, (ck_hbm, cv_hbm), (kbuf, vbuf), sems, n_pages, n_steps, n_batch, on_pages)


def _compress_weights(pe, w1, w2, transpose_w2=False):
    pe_rows = pe.reshape(CMP_BLOCK, KV_WIDTH)
    z = jnp.zeros((CMP_BLOCK, NSA_HD, NSA_HD), F32)
    w1bd = jnp.concatenate([jnp.concatenate([w1[0], z], axis=2), jnp.concatenate([z, w1[1]], axis=2)], axis=1)
    z2 = jnp.zeros((NSA_HD, NSA_HD), F32)
    w2bd = jnp.concatenate([jnp.concatenate([w2[0], z2], axis=1), jnp.concatenate([z2, w2[1]], axis=1)], axis=0)
    if transpose_w2:
        w2bd = w2bd.T
    return pe_rows, w1bd.reshape(CMP_BLOCK * KV_WIDTH, KV_WIDTH).astype(BF16), w2bd.astype(BF16)


def _cw_specs():
    zero = lambda *a: (0, 0)
    return [pl.BlockSpec((CMP_BLOCK, KV_WIDTH), zero), pl.BlockSpec((CMP_BLOCK * KV_WIDTH, KV_WIDTH), zero),
            pl.BlockSpec((KV_WIDTH, KV_WIDTH), zero)]


def _compress_prompt(ck_t, cv_t, cwk, cwv_t):
    b, _, s = ck_t.shape
    n_blk = s // CMP_BLOCK
    src = pl.BlockSpec((1, KV_WIDTH, s), lambda i: (i, 0, 0))
    return pl.pallas_call(
        functools.partial(_compress_prompt_kernel, n_tiles=s // LANES),
        grid=(b,),
        in_specs=[src, src] + _cw_specs() + _cw_specs(),
        out_specs=[pl.BlockSpec((1, n_blk, KV_WIDTH), lambda i: (i, 0, 0)),
                   pl.BlockSpec((1, KV_WIDTH, n_blk), lambda i: (i, 0, 0))],
        out_shape=[jax.ShapeDtypeStruct((b, n_blk, KV_WIDTH), F32), jax.ShapeDtypeStruct((b, KV_WIDTH, n_blk), F32)],
        scratch_shapes=[pltpu.VMEM((n_blk * BLOCK_PITCH, KV_WIDTH), F32)] * 2,
        compiler_params=_params(("arbitrary",)),
        name="compress_prompt",
    )(ck_t, cv_t, *cwk, *cwv_t)


def _compress_sample(page_table, cache_k_t, cache_v_t, cwk, cwv):
    db, n_pages = page_table.shape
    g = COMPRESS_PAGES_PER_STEP
    steps = n_pages // g
    blk_per_step = g * PAGE_SIZE // CMP_BLOCK
    n_blk = n_pages * PAGE_SIZE // CMP_BLOCK

    hbm = pl.BlockSpec(memory_space=pl.ANY)
    dst = pl.BlockSpec((1, n_blk, KV_WIDTH), lambda b, pt: (b, 0, 0))
    shp = jax.ShapeDtypeStruct((db, n_blk, KV_WIDTH), F32)
    page_buf = pltpu.VMEM((PAGE_SLOTS, g, KV_WIDTH, PAGE_SIZE), F32)
    grid_spec = pltpu.PrefetchScalarGridSpec(
        num_scalar_prefetch=1,
        grid=(db,),
        in_specs=[hbm, hbm] + _cw_specs() + _cw_specs(),
        out_specs=[dst, dst],
        scratch_shapes=[page_buf, page_buf, pltpu.SemaphoreType.DMA((PAGE_SLOTS, 2))]
        + [pltpu.VMEM((blk_per_step * BLOCK_PITCH, KV_WIDTH), F32)] * 2,
    )
    return pl.pallas_call(
        functools.partial(_compress_sample_kernel, n_pages=g, n_steps=steps, n_batch=db),
        grid_spec=grid_spec,
        out_shape=[shp, shp],
        compiler_params=_params(("arbitrary",)),
        name="compress_sample",
    )(page_table, cache_k_t, cache_v_t, *cwk, *cwv)


def _stack_heads_t(qt_ref):
    nq = qt_ref.shape[2]
    zero = jnp.zeros((NSA_HD, nq), BF16)
    cols = []
    for k in range(KV_HEADS):
        for g in range(NSA_GROUP):
            h = k * NSA_GROUP + g
            tile = qt_ref[0, h * NSA_HD:(h + 1) * NSA_HD, :] * ATTN_SCALE
            cols.append(jnp.concatenate([tile, zero] if k == 0 else [zero, tile], axis=0))
    return jnp.concatenate(cols, axis=1)


def _rank_select_rows(score, n_blk):
    parts = [score[v * SUBLANES:(v + 1) * SUBLANES] for v in range(n_blk // SUBLANES)]
    ranks = [jnp.zeros(p.shape, jnp.int32) for p in parts]
    for j in range(n_blk):
        col = score[j:j + 1, :]
        for v, part in enumerate(parts):
            ge, gt = (col >= part).astype(jnp.int32), (col > part).astype(jnp.int32)
            if v * SUBLANES > j:
                beats = ge
            elif (v + 1) * SUBLANES - 1 <= j:
                beats = gt
            else:
                row = v * SUBLANES + lax.broadcasted_iota(jnp.int32, (SUBLANES, 1), 0)
                beats = jnp.where(row > j, ge, gt)
            ranks[v] = ranks[v] + beats
    return jnp.concatenate([(r < TOP_N).astype(F32) for r in ranks], axis=0)


def _softmax_tile_t(s, v_t, m_ref, acc_ref, s_max=None):
    half = s.shape[1] // KV_HEADS
    m_old = m_ref[...]
    m_new = jnp.maximum(m_old, jnp.max(s, axis=0, keepdims=True) if s_max is None else s_max)
    alpha = jnp.exp(m_old - m_new)
    pb = jnp.exp(s - m_new).astype(BF16)
    m_ref[...] = m_new
    ones = jnp.ones((NSA_HD, v_t.shape[1]), BF16)
    v0 = jnp.concatenate([v_t[:NSA_HD], ones], axis=0)
    v1 = jnp.concatenate([ones, v_t[NSA_HD:]], axis=0)
    pv = jnp.concatenate([_dot(v0, pb[:, :half]), _dot(v1, pb[:, half:])], axis=1)
    acc_ref[...] = alpha * acc_ref[...] + pv


def _normalized_head_t(acc, k, cols):
    den = (1 - k) * NSA_HD
    return acc[k * NSA_HD:(k + 1) * NSA_HD, cols] / acc[den:den + 1, cols]


def _attn_prompt_kernel(nqt_ref, nqrt_ref, nglt_ref, ngt_ref, kc_ref, vct_ref, sk_ref, svt_ref, wk_ref, wvt_ref,
                        et_ref, y_ref, lhs_scr, lhsr_scr, ocmp_scr, t_scr, sel_scr, m_scr, acc_scr, s_scr, smax_scr):
    i = pl.program_id(1)
    qb = Q_BLOCK
    q0 = i * qb
    cols8 = 2 * NSA_GROUP * qb
    qpos = q0 + lax.broadcasted_iota(jnp.int32, (1, qb), 1)
    lhs_scr[...] = _stack_heads_t(nqt_ref)
    lhsr_scr[...] = _stack_heads_t(nqrt_ref)

    n_cmp = kc_ref.shape[1]
    s_all = _dot(kc_ref[0].astype(BF16), lhs_scr[...])
    cend = (lax.broadcasted_iota(jnp.int32, (n_cmp, 1), 0) + 1) * CMP_BLOCK - 1
    cmask = cend <= qpos
    vct = vct_ref[0].astype(BF16)
    imps = []
    for k in range(KV_HEADS):
        imp = None
        for g in range(NSA_GROUP):
            c0 = (k * NSA_GROUP + g) * qb
            s = jnp.where(cmask, s_all[:, c0:c0 + qb], NEG_INF)
            p = jnp.exp(s - jnp.max(s, axis=0, keepdims=True))
            p = p / jnp.sum(p, axis=0, keepdims=True) * cmask.astype(F32)
            imp = p if imp is None else imp + p
            ocmp_scr[:, c0:c0 + qb] = _dot(vct, p.astype(BF16))
        imps.append(imp)

    @pl.when(q0 + qb <= TOP_N * SLC_BLOCK)
    def _():
        sel_scr[...] = jnp.ones(sel_scr.shape, BF16)

    @pl.when(q0 + qb > TOP_N * SLC_BLOCK)
    def _():
        n_slc = n_cmp // 2
        blk = lax.broadcasted_iota(jnp.int32, (n_slc, 1), 0)
        valid = blk * SLC_BLOCK <= qpos
        forced = (blk == 0) | (blk == qpos // SLC_BLOCK)
        for k in range(KV_HEADS):
            t_scr[...] = imps[k]
            pair = t_scr[pl.ds(0, n_slc, stride=2), :] + t_scr[pl.ds(1, n_slc, stride=2), :]
            score = jnp.where(forced, FORCED_SCORE, jnp.where(valid, pair, NEG_INF))
            sel = _rank_select_rows(score, n_slc)
            sel_scr[k] = jnp.concatenate([sel, jnp.zeros((LANES - n_slc, qb), F32)], axis=0).astype(BF16)

    tk = SLC_KEY_TILE
    n_tiles = (q0 + qb - 1) // tk + 1
    m_scr[...] = jnp.full(m_scr.shape, M_INIT, F32)
    acc_scr[...] = jnp.zeros(acc_scr.shape, F32)

    def scores(t):
        k0 = pl.multiple_of(t * tk, tk)
        vis = (k0 + lax.broadcasted_iota(jnp.int32, (tk, 1), 0)) <= qpos
        et = et_ref[pl.ds(k0, tk), :]
        caps = [jnp.where((_dot(et, sel_scr[k]) > 0.5) & vis, CAP_OPEN, NEG_INF) for k in range(KV_HEADS)]
        cap = jnp.concatenate([caps[0]] * NSA_GROUP + [caps[1]] * NSA_GROUP, axis=1)
        s = jnp.minimum(_dot(sk_ref[pl.ds(k0, tk), :], lhsr_scr[...]), cap)
        s_scr[t % 2] = s
        smax_scr[t % 2] = jnp.max(s, axis=0, keepdims=True)

    def update(t):
        k0 = pl.multiple_of(t * tk, tk)
        _softmax_tile_t(s_scr[t % 2], svt_ref[0, :, pl.ds(k0, tk)], m_scr.at[0], acc_scr.at[0], smax_scr[t % 2])

    def slc_step(t, carry):
        scores(t + 1)
        update(t)
        return carry

    scores(0)
    lax.fori_loop(0, n_tiles - 1, slc_step, 0)
    update(n_tiles - 1)

    wlen = WINDOW + qb
    w0 = pl.multiple_of(jnp.maximum(i - WINDOW // qb, 0) * qb, qb)
    dist = qpos - (w0 + lax.broadcasted_iota(jnp.int32, (wlen, 1), 0))
    wcap = jnp.where((dist >= 0) & (dist <= WINDOW), CAP_OPEN, NEG_INF)
    s = jnp.minimum(_dot(wk_ref[pl.ds(w0, wlen), :], lhsr_scr[...]), jnp.concatenate([wcap] * (2 * NSA_GROUP), axis=1))
    _softmax_tile_t(s, wvt_ref[0, :, pl.ds(w0, wlen)], m_scr.at[1], acc_scr.at[1])

    acc_slc, acc_win = acc_scr[0], acc_scr[1]
    sig = _sigmoid(nglt_ref[0])
    for pair in range(NSA_HEADS // 2):
        tiles = []
        for h in (2 * pair, 2 * pair + 1):
            k = h // NSA_GROUP
            rows = slice(k * NSA_HD, (k + 1) * NSA_HD)
            cols = slice(h * qb, (h + 1) * qb)
            gate = lambda r: sig[r * NSA_HEADS + h:r * NSA_HEADS + h + 1, :]
            o = (gate(0) * ocmp_scr[rows, cols] + gate(1) * _normalized_head_t(acc_slc, k, cols)
                 + gate(2) * _normalized_head_t(acc_win, k, cols))
            ng = ngt_ref[0, h * NSA_HD:(h + 1) * NSA_HD, :]
            tiles.append(o * (ng * _sigmoid(ng)))
        y_ref[:, pair * LANES:(pair + 1) * LANES] = jnp.concatenate(tiles, axis=0).T.astype(y_ref.dtype)


def _attn_prompt(nq_t, nqr_t, ngl_t, ng_t, kc, vc_t, sk_rm, sv_t, wk_rm, wv_t):
    b, _, s = sv_t.shape
    assert s // CMP_BLOCK == LANES, "one lane per compressed block"
    qb = Q_BLOCK
    nqb = s // qb
    cols8 = 2 * NSA_GROUP * qb
    expand_t = (jnp.arange(s)[:, None] // SLC_BLOCK == jnp.arange(LANES)[None, :]).astype(BF16)
    qcol = lambda r: pl.BlockSpec((1, r, qb), lambda bb, i: (bb, 0, i))
    per_b = lambda shape: pl.BlockSpec((1,) + shape, lambda bb, i: (bb, 0, 0))
    rows_b = pl.BlockSpec((s, KV_WIDTH), lambda bb, i: (bb, 0))
    return pl.pallas_call(
        _attn_prompt_kernel,
        grid=(b, nqb),
        in_specs=[qcol(512), qcol(512), qcol(LANES), qcol(512), per_b(kc.shape[1:]), per_b(vc_t.shape[1:]),
                  rows_b, per_b((KV_WIDTH, s)), rows_b, per_b((KV_WIDTH, s)),
                  pl.BlockSpec(expand_t.shape, lambda bb, i: (0, 0))],
        out_specs=pl.BlockSpec((qb, 512), lambda bb, i: (bb * nqb + i, 0)),
        out_shape=jax.ShapeDtypeStruct((b * s, 512), BF16),
        scratch_shapes=[pltpu.VMEM((KV_WIDTH, cols8), BF16), pltpu.VMEM((KV_WIDTH, cols8), BF16),
                        pltpu.VMEM((KV_WIDTH, cols8), F32), pltpu.VMEM((LANES, qb), F32),
                        pltpu.VMEM((KV_HEADS, LANES, qb), BF16),
                        pltpu.VMEM((2, 1, cols8), F32), pltpu.VMEM((2, KV_WIDTH, cols8), F32),
                        pltpu.VMEM((2, SLC_KEY_TILE, cols8), F32), pltpu.VMEM((2, 1, cols8), F32)],
        compiler_params=_params(("arbitrary", "arbitrary")),
        name="attn_prompt",
    )(nq_t, nqr_t, ngl_t, ng_t, kc, vc_t, sk_rm, sv_t, wk_rm, wv_t, expand_t)


def _stack_heads(q_ref):
    lane = lax.broadcasted_iota(jnp.int32, (1, LANES), 1)
    lo = lane < NSA_HD
    slabs = [q_ref[:, g * LANES:(g + 1) * LANES] for g in range(NSA_GROUP)]
    zero = jnp.zeros_like(slabs[0])
    stacked = jnp.concatenate([jnp.where(lo, s, zero) for s in slabs] + [jnp.where(lo, zero, s) for s in slabs], axis=0)
    return stacked.astype(BF16)


def _mask_rows(sc, masks, rows):
    n = sc.shape[-1]
    s4 = sc.reshape(2 * NSA_GROUP, rows, n)
    out = [jnp.where(masks[k][None], s4[k * NSA_GROUP:(k + 1) * NSA_GROUP], NEG_INF) for k in range(KV_HEADS)]
    return jnp.concatenate(out, axis=0).reshape(2 * NSA_GROUP * rows, n)


def _pair_scores(imp, qpos, n_lanes):
    lane = lax.broadcasted_iota(jnp.int32, (1, n_lanes), 1)
    even = (lane % 2) == 0
    ps = imp + jnp.where(even, pltpu.roll(imp, n_lanes - 1, 1), pltpu.roll(imp, 1, 1))
    blk = lane // 2
    valid = blk * SLC_BLOCK <= qpos
    forced = (blk == 0) | (blk == qpos // SLC_BLOCK)
    return jnp.where(forced, FORCED_SCORE, jnp.where(valid, ps, NEG_INF))


def _rank_select(score, n_lanes, extra_forced):
    lane = lax.broadcasted_iota(jnp.int32, (1, n_lanes), 1)
    blk = lane // 2
    rank = jnp.zeros(score.shape, jnp.int32)
    for j in range(0, n_lanes, 2):
        col = score[:, j:j + 1]
        beats = (col > score) | ((col == score) & ((j // 2) < blk))
        rank = rank + beats.astype(jnp.int32)
    if extra_forced:
        rank = rank + (score < FORCED_SCORE).astype(jnp.int32)
    return rank < TOP_N


def _merge_kv_heads(acc, rows):
    lane = lax.broadcasted_iota(jnp.int32, (1, LANES), 1)
    lo = lane < NSA_HD
    half = NSA_GROUP * rows
    return [jnp.where(lo, acc[g * rows:(g + 1) * rows], acc[half + g * rows:half + (g + 1) * rows])
            for g in range(NSA_GROUP)]


def _gate_and_store(o_cmp, o_slc, o_win, ngl_ref, ng_ref, y_ref):
    lane = lax.broadcasted_iota(jnp.int32, (1, LANES), 1)
    lo = lane < NSA_HD
    sig = _sigmoid(ngl_ref[...])
    for g in range(NSA_GROUP):
        gates = [jnp.where(lo, sig[:, r * 8 + g:r * 8 + g + 1], sig[:, r * 8 + 4 + g:r * 8 + 4 + g + 1]) for r in range(3)]
        o = gates[0] * o_cmp[g] + gates[1] * o_slc[g] + gates[2] * o_win[g]
        gate = ng_ref[:, g * LANES:(g + 1) * LANES]
        y_ref[:, g * LANES:(g + 1) * LANES] = (o * (gate * _sigmoid(gate))).astype(y_ref.dtype)


def _page_copies(pt_ref, b, chunk, slot, caches, bufs, sems, g):
    out = []
    for ci, (cache, buf) in enumerate(zip(caches, bufs)):
        for j in range(g):
            page = pt_ref[b, chunk * g + j]
            out.append(pltpu.make_async_copy(cache.at[page], buf.at[slot, j], sems.at[slot, ci]))
    return out


def _stream_pages(pt_ref, caches, bufs, sems, g, n_chunks, n_batch, compute):
    b = pl.program_id(0)
    total = n_batch * n_chunks
    n_slots = bufs[0].shape[0]
    depth = n_slots - 1
    assert n_chunks >= depth
    copies = functools.partial(_page_copies, pt_ref, caches=caches, bufs=bufs, sems=sems, g=g)

    @pl.when(b == 0)
    def _():
        for d in range(depth):
            for cp in copies(0, d, d):
                cp.start()

    def chunk(c, carry):
        flat = b * n_chunks + c
        ahead = jnp.minimum(flat + depth, total - 1)
        for cp in copies(ahead // n_chunks, ahead % n_chunks, (flat + depth) % n_slots):
            cp.start()
        for cp in copies(b, c, flat % n_slots):
            cp.wait()
        compute(c, flat % n_slots)
        return carry

    lax.fori_loop(0, n_chunks, chunk, 0)

    @pl.when(b == n_batch - 1)
    def _():
        for d in range(depth):
            for cp in copies(n_batch - 1, n_chunks - 1, (total + d) % n_slots):
                cp.wait()


def _attn_sample_kernel(pt_ref, sk_hbm, sv_hbm, nq_ref, nqr_ref, ngl_ref, ng_ref, kc_ref, vc_ref, skn_ref, svn_ref,
                        wko_ref, wvo_ref, wkn_ref, wvn_ref, e_ref, y_ref, kbuf, vbuf, sems, m_scr, l_scr, acc_scr,
                        ocmp_scr, owin_scr, mask_scr, *, n_pages, n_steps, n_batch, t_len, past_len):
    t = t_len
    nq_ref, nqr_ref, ngl_ref, ng_ref, y_ref = (r.at[0] for r in (nq_ref, nqr_ref, ngl_ref, ng_ref, y_ref))
    rows8 = 2 * NSA_GROUP * t
    half = NSA_GROUP * t
    tpos = lax.broadcasted_iota(jnp.int32, (t, 1), 0)
    qpos = past_len + tpos
    n_cmp = kc_ref.shape[1]
    wb = wko_ref.shape[2]
    step_keys = n_pages * PAGE_SIZE

    def before_pages():
        lhs = _stack_heads(nq_ref)
        s = _dot_nt(lhs, kc_ref[0].astype(BF16)) * ATTN_SCALE
        m = jnp.max(s, axis=-1, keepdims=True)
        p = jnp.exp(s - m)
        p = p / jnp.sum(p, axis=-1, keepdims=True)
        oc = _dot(p.astype(BF16), vc_ref[0].astype(BF16))
        for g, slab in enumerate(_merge_kv_heads(oc, t)):
            ocmp_scr[g] = slab
        p4 = p.reshape(8, t, n_cmp)
        for k in range(KV_HEADS):
            imp = p4[k * NSA_GROUP]
            for g in range(1, NSA_GROUP):
                imp = imp + p4[k * NSA_GROUP + g]
            sel = _rank_select(_pair_scores(imp, qpos, n_cmp), n_cmp, True).astype(BF16)
            for c in range(n_cmp // LANES):
                km = _dot(sel[:, c * LANES:(c + 1) * LANES], e_ref[...])
                span = LANES * CMP_BLOCK
                if step_keys >= span:
                    off = (c * span) % step_keys
                    mask_scr[(c * span) // step_keys, k, :, off:off + span] = km
                else:
                    per = span // step_keys
                    for u in range(per):
                        mask_scr[c * per + u, k] = km[:, u * step_keys:(u + 1) * step_keys]

        lhs_r = _stack_heads(nqr_ref)
        so = _dot(lhs_r, wko_ref[0].astype(BF16)) * ATTN_SCALE
        sn = _dot(lhs_r, wkn_ref[0].astype(BF16)) * ATTN_SCALE
        jo = lax.broadcasted_iota(jnp.int32, (1, wb), 1)
        jn = lax.broadcasted_iota(jnp.int32, (1, LANES), 1)
        old_vis = jo >= tpos
        so = _mask_rows(so, [old_vis, old_vis], t)
        new_vis = jn <= tpos
        sn_w = _mask_rows(sn, [new_vis, new_vis], t)
        m = jnp.maximum(jnp.max(so, axis=-1, keepdims=True), jnp.max(sn_w, axis=-1, keepdims=True))
        po = jnp.exp(so - m)
        pn = jnp.exp(sn_w - m)
        den = jnp.sum(po, axis=-1, keepdims=True) + jnp.sum(pn, axis=-1, keepdims=True)
        pob, pnb = (po / den).astype(BF16), (pn / den).astype(BF16)
        wvo, wvn = wvo_ref[0].astype(BF16), wvn_ref[0].astype(BF16)
        ow = jnp.concatenate([_dot_nt(pob[:half], wvo) + _dot_nt(pnb[:half], wvn),
                              _dot_nt(pob[half:], wvo) + _dot_nt(pnb[half:], wvn)], axis=0)
        for g, slab in enumerate(_merge_kv_heads(ow, t)):
            owin_scr[g] = slab

        s2 = _dot(lhs_r, skn_ref[0].astype(BF16)) * ATTN_SCALE
        s2 = _mask_rows(s2, [new_vis, new_vis], t)
        m2 = jnp.max(s2, axis=-1, keepdims=True)
        p2 = jnp.exp(s2 - m2)
        m_scr[0] = m2
        l_scr[0] = jnp.sum(p2, axis=-1, keepdims=True)
        p2b = p2.astype(BF16)
        svn = svn_ref[0].astype(BF16)
        acc_scr[0] = jnp.concatenate([_dot_nt(p2b[:half], svn), _dot_nt(p2b[half:], svn)], axis=0)
        for st in range(1, SAMPLE_STREAMS):
            m_scr[st] = jnp.full((rows8, 1), M_INIT, F32)
            l_scr[st] = jnp.zeros((rows8, 1), F32)
            acc_scr[st] = jnp.zeros((rows8, LANES), F32)

    def on_pages(step, slot):
        lhs_r = _stack_heads(nqr_ref)
        per = n_pages // SAMPLE_STREAMS
        msk = mask_scr[step]
        for st in range(SAMPLE_STREAMS):
            pages = range(st * per, (st + 1) * per)
            keys = slice(st * per * PAGE_SIZE, (st + 1) * per * PAGE_SIZE)
            kt = jnp.concatenate([kbuf[slot, j].astype(BF16) for j in pages], axis=1)
            vt = jnp.concatenate([vbuf[slot, j].astype(BF16) for j in pages], axis=1)
            sc = _dot(lhs_r, kt) * ATTN_SCALE
            sc = _mask_rows(sc, [msk[k][:, keys] > 0.5 for k in range(KV_HEADS)], t)
            m_i = m_scr[st]
            m_new = jnp.maximum(m_i, jnp.max(sc, axis=-1, keepdims=True))
            alpha = jnp.exp(m_i - m_new)
            pr = jnp.exp(sc - m_new)
            l_scr[st] = alpha * l_scr[st] + jnp.sum(pr, axis=-1, keepdims=True)
            m_scr[st] = m_new
            pb = pr.astype(BF16)
            pv = jnp.concatenate([_dot_nt(pb[:half], vt), _dot_nt(pb[half:], vt)], axis=0)
            acc_scr[st] = alpha * acc_scr[st] + pv

    before_pages()
    _stream_pages(pt_ref, (sk_hbm, sv_hbm), (kbuf, vbuf), sems, n_pages, n_steps, n_batch, on_pages)
    m_all = m_scr[0]
    for st in range(1, SAMPLE_STREAMS):
        m_all = jnp.maximum(m_all, m_scr[st])
    l_all = jnp.zeros((rows8, 1), F32)
    acc_all = jnp.zeros((rows8, LANES), F32)
    for st in range(SAMPLE_STREAMS):
        w = jnp.exp(m_scr[st] - m_all)
        l_all = l_all + w * l_scr[st]
        acc_all = acc_all + w * acc_scr[st]
    o_slc = _merge_kv_heads(acc_all / l_all, t)
    o_cmp = [ocmp_scr[g] for g in range(NSA_GROUP)]
    o_win = [owin_scr[g] for g in range(NSA_GROUP)]
    _gate_and_store(o_cmp, o_slc, o_win, ngl_ref, ng_ref, y_ref)


def _attn_sample(page_table, cache_sk_t, cache_sv_t, nq, nqr, ngl, ng, kc, vc, skn, svn, wko, wvo, wkn, wvn, t_len):
    db, n_pages = page_table.shape
    g = SAMPLE_PAGES_PER_STEP
    steps = n_pages // g
    past_len = n_pages * PAGE_SIZE
    step_keys = g * PAGE_SIZE
    span = LANES * CMP_BLOCK
    expand = (jnp.arange(span)[None, :] // CMP_BLOCK == jnp.arange(LANES)[:, None]).astype(BF16)

    hbm = pl.BlockSpec(memory_space=pl.ANY)
    row = lambda w: pl.BlockSpec((1, t_len, w), lambda b, pt: (b, 0, 0))
    per_b = lambda shape: pl.BlockSpec((1,) + shape, lambda b, pt: (b, 0, 0))
    rows8 = 2 * NSA_GROUP * t_len
    page_buf = pltpu.VMEM((PAGE_SLOTS, g, KV_WIDTH, PAGE_SIZE), F32)
    grid_spec = pltpu.PrefetchScalarGridSpec(
        num_scalar_prefetch=1,
        grid=(db,),
        in_specs=[hbm, hbm, row(512), row(512), row(LANES), row(512), per_b(kc.shape[1:]), per_b(vc.shape[1:]),
                  per_b(skn.shape[1:]), per_b(svn.shape[1:]), per_b(wko.shape[1:]), per_b(wvo.shape[1:]),
                  per_b(wkn.shape[1:]), per_b(wvn.shape[1:]), pl.BlockSpec(expand.shape, lambda b, pt: (0, 0))],
        out_specs=row(512),
        scratch_shapes=[page_buf, page_buf, pltpu.SemaphoreType.DMA((PAGE_SLOTS, 2)),
                        pltpu.VMEM((SAMPLE_STREAMS, rows8, 1), F32), pltpu.VMEM((SAMPLE_STREAMS, rows8, 1), F32),
                        pltpu.VMEM((SAMPLE_STREAMS, rows8, LANES), F32),
                        pltpu.VMEM((NSA_GROUP, t_len, LANES), F32), pltpu.VMEM((NSA_GROUP, t_len, LANES), F32),
                        pltpu.VMEM((steps, KV_HEADS, t_len, step_keys), F32)],
    )
    return pl.pallas_call(
        functools.partial(_attn_sample_kernel, n_pages=g, n_steps=steps, n_batch=db, t_len=t_len, past_len=past_len),
        grid_spec=grid_spec,
        out_shape=jax.ShapeDtypeStruct((db, t_len, 512), F32),
        compiler_params=_params(("arbitrary",)),
        name="attn_sample",
    )(page_table, cache_sk_t, cache_sv_t, nq, nqr, ngl, ng, kc, vc, skn, svn, wko, wvo, wkn, wvn, expand)


def _finish_kernel(x_ref, yr_ref, yn_ref, p_ref, wo_ref, gple_ref, wg_ref, wp_ref, gf_ref, o_ref):
    x = x_ref[...]
    x = x + _dot(yr_ref[...].astype(BF16), wo_ref[0:RET_WIDTH, :]) + _dot(yn_ref[...].astype(BF16), wo_ref[RET_WIDTH:, :])
    ms = jnp.mean(x * x, axis=-1, keepdims=True)
    hn = (x * lax.rsqrt(ms + RMS_EPS) * gple_ref[...]).astype(BF16)
    gate = _sigmoid(_dot(hn, wg_ref[...]))
    x = x + gate * _dot(p_ref[...].astype(BF16), wp_ref[...])
    ms = jnp.mean(x * x, axis=-1, keepdims=True)
    o_ref[...] = x * lax.rsqrt(ms + RMS_EPS) * gf_ref[...]


def _finish(x2d, y_ret, y_nsa, p2d, w_out, norm_ple, w_gate, w_ple, norm_f, tm):
    n = x2d.shape[0]
    row = lambda w: pl.BlockSpec((tm, w), lambda i: (i, 0))
    const = lambda shape: pl.BlockSpec(shape, lambda i: (0, 0))
    return pl.pallas_call(
        _finish_kernel,
        grid=(n // tm,),
        in_specs=[row(D_MODEL), row(RET_WIDTH), row(NSA_WIDTH), row(PLE_DIM), const(w_out.shape), const((1, D_MODEL)),
                  const(w_gate.shape), const(w_ple.shape), const((1, D_MODEL))],
        out_specs=row(D_MODEL),
        out_shape=jax.ShapeDtypeStruct((n, D_MODEL), F32),
        compiler_params=_params(("arbitrary",)),
        name="finish",
    )(x2d, y_ret, y_nsa, p2d, w_out, norm_ple, w_gate, w_ple, norm_f)


def _slab_perm():
    return np.array([(k * NSA_GROUP + g) * NSA_HD + d for g in range(NSA_GROUP) for k in range(KV_HEADS)
                     for d in range(NSA_HD)], np.int32)


def _to_positions_major(x_t):
    lead = x_t.shape[:-2]
    n = len(lead)
    x4 = x_t.reshape(lead + (KV_HEADS, NSA_HD, x_t.shape[-1]))
    return jnp.transpose(x4, tuple(range(n)) + (n + 2, n, n + 1))


def _cache_t(cache):
    n_pool, page = cache.shape[:2]
    return jnp.transpose(cache, (0, 2, 3, 1)).reshape(n_pool, KV_WIDTH, page)


def _layer(xp, xs, c_ck, c_cv, c_sk, c_sv, win_k, win_v, ret_state, page_table, p_p, p_s, norm_mix, w_in, gn_g, gn_b,
           pe_k, w1_k, w2_k, pe_v, w1_v, w2_v, w_out, norm_ple, w_gate, w_ple, norm_f):
    b, s, d = xp.shape
    db, t, _ = xs.shape
    n_pages = page_table.shape[1]
    past = n_pages * PAGE_SIZE

    off = np.cumsum((0,) + SPLIT_SIZES)
    col = lambda i: w_in[:, off[i]:off[i + 1]]
    perm = _slab_perm()
    ngl_w = jnp.pad(col(11), ((0, 0), (0, LANES - N_GATES)))
    ret_cols = [col(0), col(1), col(2), col(3)]
    kv_cols = [col(5), col(6), col(8), col(10), col(7), col(9)]
    w_rm_p = jnp.concatenate(ret_cols + [col(7), col(9)], axis=1).astype(BF16)
    w_t_p = jnp.concatenate(kv_cols + [col(4), col(12), ngl_w], axis=1).T.astype(BF16)
    w_rm_s = jnp.concatenate(ret_cols + [col(4)[:, perm], col(12)[:, perm], ngl_w], axis=1).astype(BF16)
    w_t_s = jnp.concatenate(kv_cols, axis=1).T.astype(BF16)
    w_out_b = w_out.astype(BF16)
    w_out_slab = jnp.concatenate([w_out[:RET_WIDTH], w_out[RET_WIDTH:][perm]], axis=0).astype(BF16)
    g_mix = norm_mix.reshape(1, d)
    cwk = _compress_weights(pe_k, w1_k, w2_k)
    cwv = _compress_weights(pe_v, w1_v, w2_v)
    cwv_t = _compress_weights(pe_v, w1_v, w2_v, transpose_w2=True)
    gn_g2, gn_b2 = gn_g.reshape(1, RET_WIDTH), gn_b.reshape(1, RET_WIDTH)
    fin_w = (norm_ple.reshape(1, d), w_gate.astype(BF16), w_ple.astype(BF16), norm_f.reshape(1, d))

    tm = 512
    (q_ret, k_ret, v_ret, rg, sk_rm, wk_rm, ck_t, cv_t, sv_t, wv_t, sk_t, wk_t, sv_b, wv_b, nq_t, nqr_t, ng_t,
     ngl_t) = _proj(xp.reshape(b * s, d), jnp.arange(s), g_mix, w_rm_p, w_t_p, b, tm, True)
    y_ret, ret_p = _retention(q_ret, k_ret, v_ret, rg, jnp.zeros((b, RET_HEADS, RET_DK, RET_DK), F32), gn_g2, gn_b2,
                              b, s // RET_CHUNK, RET_CHUNK)
    kc, vc_t = _compress_prompt(ck_t, cv_t, cwk, cwv_t)
    y_nsa = _attn_prompt(nq_t, nqr_t, ngl_t, ng_t, kc, vc_t, sk_rm, sv_b, wk_rm, wv_b)
    y_prompt = _finish(xp.reshape(b * s, d), y_ret, y_nsa, p_p.reshape(b * s, PLE_DIM), w_out_b, *fin_w, tm)
    wb_p = min(WINDOW, s)
    prompt_states = (ret_p, _to_positions_major(ck_t), _to_positions_major(cv_t), _to_positions_major(sk_t),
                     _to_positions_major(sv_t), _to_positions_major(wk_t[:, :, s - wb_p:]),
                     _to_positions_major(wv_t[:, :, s - wb_p:]))

    n = db * t
    pos_s = jnp.tile(past + jnp.arange(t), db)
    (q_ret, k_ret, v_ret, rg, nq, nqr, ng, ngl, ck_n, cv_n, sv_n, wv_n, sk_n, wk_n) = _proj(
        xs.reshape(n, d), pos_s, g_mix, w_rm_s, w_t_s, 1, n, False)
    pad = lambda a: jnp.pad(a.reshape(db, t, RET_WIDTH), ((0, 0), (0, RET_CHUNK - t), (0, 0))).reshape(db * RET_CHUNK, RET_WIDTH)
    y_ret_pad, ret_s = _retention(pad(q_ret), pad(k_ret), pad(v_ret), pad(rg), ret_state, gn_g2, gn_b2, db, 1, t)
    y_ret = y_ret_pad.reshape(db, RET_CHUNK, RET_WIDTH)[:, :t].reshape(n, RET_WIDTH)
    kc, vc = _compress_sample(page_table, _cache_t(c_ck), _cache_t(c_cv), cwk, cwv)
    per_batch = lambda a: jnp.transpose(a[0].reshape(KV_WIDTH, db, t), (1, 0, 2))
    lane_pad = lambda a: jnp.pad(per_batch(a), ((0, 0), (0, 0), (0, LANES - t)))
    win_t = lambda w: jnp.transpose(w, (0, 2, 3, 1)).reshape(db, KV_WIDTH, w.shape[1])
    rows3 = lambda a: a.astype(F32).reshape(db, t, a.shape[-1])
    y_nsa = _attn_sample(page_table, _cache_t(c_sk), _cache_t(c_sv), rows3(nq), rows3(nqr), rows3(ngl), rows3(ng), kc, vc,
                         lane_pad(sk_n), lane_pad(sv_n), win_t(win_k), win_t(win_v), lane_pad(wk_n), lane_pad(wv_n), t)
    y_nsa = y_nsa.reshape(n, NSA_WIDTH)
    y_sample = _finish(xs.reshape(n, d), y_ret, y_nsa, p_s.reshape(n, PLE_DIM), w_out_slab, *fin_w, n)
    new_rows = lambda a: _to_positions_major(per_batch(a))
    sample_states = (ret_s, new_rows(ck_n), new_rows(cv_n), new_rows(sk_n), new_rows(sv_n),
                     jnp.concatenate([win_k[:, t:], new_rows(wk_n)], axis=1),
                     jnp.concatenate([win_v[:, t:], new_rows(wv_n)], axis=1))
    return y_prompt.reshape(b, s, d), y_sample.reshape(db, t, d), prompt_states, sample_states


def kernel(x_prompt, x_sample, cache_cmp_k, cache_cmp_v, cache_slc_k, cache_slc_v, state_win_k, state_win_v, state_ret, page_table, p_prompt, p_sample, norm_mix, w_in, ret_gn_g, ret_gn_b, cmp_pe_k, cmp_w1_k, cmp_w2_k, cmp_pe_v, cmp_w1_v, cmp_w2_v, w_out, norm_ple, w_ple_gate, w_ple, norm_f):
    depth = w_in.shape[0]
    assert depth == 1, "single trunk layer"
    l = 0
    yp, ys, sp, ss = _layer(x_prompt, x_sample, cache_cmp_k[l], cache_cmp_v[l], cache_slc_k[l], cache_slc_v[l],
                            state_win_k[l], state_win_v[l], state_ret[l], page_table, p_prompt[l], p_sample[l],
                            norm_mix[l], w_in[l], ret_gn_g[l], ret_gn_b[l], cmp_pe_k[l], cmp_w1_k[l], cmp_w2_k[l],
                            cmp_pe_v[l], cmp_w1_v[l], cmp_w2_v[l], w_out[l], norm_ple[l], w_ple_gate[l], w_ple[l], norm_f)
    return (yp, ys) + tuple(a[None] for a in sp) + tuple(a[None] for a in ss)
```

```python
import functools

import numpy as np
import jax
import jax.numpy as jnp
from jax import lax
from jax.experimental import pallas as pl
from jax.experimental.pallas import tpu as pltpu

F32 = jnp.float32
BF16 = jnp.bfloat16

D_MODEL = 1024
PLE_DIM = 256
PAGE_SIZE = 128
RET_HEADS = 4
RET_DK = 128
RET_WIDTH = 512
RET_CHUNK = 128
NSA_HEADS = 8
KV_HEADS = 2
NSA_GROUP = 4
NSA_HD = 64
NSA_WIDTH = 512
KV_WIDTH = 128
CMP_BLOCK = 32
SLC_BLOCK = 64
TOP_N = 16
WINDOW = 512
Q_BLOCK = 128
ROPE_THETA = 10000.0
RMS_EPS = 1e-6
GN_EPS = 1e-5
NEG_INF = -1e9
FORCED_SCORE = 1e4
M_INIT = -1e30
CAP_OPEN = 3e38
ATTN_SCALE = NSA_HD ** -0.5
SPLIT_SIZES = (512, 512, 512, 512, 512, 128, 128, 128, 128, 128, 128, 24, 512)
N_GATES = 3 * NSA_HEADS

LANES = 128
SUBLANES = 8
SLC_KEY_TILE = 512
RET_CHUNKS_PER_STEP = 4
SAMPLE_PAGES_PER_STEP = 64
SAMPLE_STREAMS = 1
COMPRESS_PAGES_PER_STEP = 32
PAGE_SLOTS = 3
BLOCK_PITCH = 40
VMEM_LIMIT = 56 * 1024 * 1024


def _dot(a, b):
    return jnp.dot(a, b, preferred_element_type=F32)


def _dot_nt(a, b):
    return lax.dot_general(a, b, (((1,), (1,)), ((), ())), preferred_element_type=F32)


def _sigmoid(x):
    return 1.0 / (1.0 + jnp.exp(-x))


def _params(sem):
    return pltpu.CompilerParams(dimension_semantics=sem, vmem_limit_bytes=VMEM_LIMIT)


def _rope_lanes(a, cos, sin_signed, half):
    if 2 * half == LANES:
        partner = pltpu.roll(a, half, 1)
    else:
        lane = lax.broadcasted_iota(jnp.int32, (1, LANES), 1)
        partner = jnp.where((lane % (2 * half)) < half, pltpu.roll(a, LANES - half, 1), pltpu.roll(a, half, 1))
    return a * cos + partner * sin_signed


def _rope_sublanes(blk, cos_t, sin_t):
    outs = []
    for h in range(blk.shape[0] // NSA_HD):
        x1 = blk[h * NSA_HD:h * NSA_HD + NSA_HD // 2]
        x2 = blk[h * NSA_HD + NSA_HD // 2:(h + 1) * NSA_HD]
        outs += [x1 * cos_t - x2 * sin_t, x2 * cos_t + x1 * sin_t]
    return jnp.concatenate(outs, axis=0)


def _proj_kernel(*refs, prompt):
    (x_ref, g_ref, wrm_ref, wt_ref, cosr_ref, sinr_ref, cosn_ref, sinn_ref, cost_ref, sint_ref,
     qret_ref, kret_ref, vret_ref, rg_ref) = refs[:14]
    x = x_ref[...]
    ms = jnp.mean(x * x, axis=-1, keepdims=True)
    h = (x * lax.rsqrt(ms + RMS_EPS) * g_ref[...]).astype(BF16)

    def proj(lo, hi):
        return _dot(h, wrm_ref[:, lo:hi])

    cr, sr = cosr_ref[...], sinr_ref[...]
    cn, sn = cosn_ref[...], sinn_ref[...]
    ct, st = cost_ref[...], sint_ref[...]
    yq = proj(0, 512)
    yk = proj(512, 1024)
    for hh in range(RET_HEADS):
        sl = slice(hh * LANES, (hh + 1) * LANES)
        qret_ref[:, sl] = _rope_lanes(yq[:, sl], cr, sr, RET_DK // 2)
        kret_ref[:, sl] = _rope_lanes(yk[:, sl], cr, sr, RET_DK // 2) * (RET_DK ** -0.5)
    vret_ref[...] = proj(1024, 1536)
    rg_ref[...] = proj(1536, 2048)

    yt = _dot_nt(wt_ref[...], h)
    if prompt:
        (skr_ref, wkr_ref, ck_ref, cv_ref, sv_ref, wv_ref, sk_ref, wk_ref, svb_ref, wvb_ref,
         nqt_ref, nqrt_ref, ngt_ref, nglt_ref) = refs[14:]
    else:
        nq_ref, nqr_ref, ng_ref, ngl_ref, ck_ref, cv_ref, sv_ref, wv_ref, sk_ref, wk_ref = refs[14:]
    ck_ref[0] = yt[0:128]
    cv_ref[0] = yt[128:256]
    sv_ref[0] = yt[256:384]
    wv_ref[0] = yt[384:512]
    sk_ref[0] = _rope_sublanes(yt[512:640], ct, st)
    wk_ref[0] = _rope_sublanes(yt[640:768], ct, st)
    if prompt:
        svb_ref[0] = yt[256:384].astype(BF16)
        wvb_ref[0] = yt[384:512].astype(BF16)
        nqt = yt[768:1280]
        nqt_ref[0] = nqt.astype(BF16)
        nqrt_ref[0] = _rope_sublanes(nqt, ct, st).astype(BF16)
        ngt_ref[0] = yt[1280:1792]
        nglt_ref[0] = yt[1792:1920]
        skr_ref[...] = _rope_lanes(proj(2048, 2176), cn, sn, NSA_HD // 2).astype(BF16)
        wkr_ref[...] = _rope_lanes(proj(2176, 2304), cn, sn, NSA_HD // 2).astype(BF16)
    else:
        yn = proj(2048, 2560)
        for gg in range(NSA_GROUP):
            sl = slice(gg * LANES, (gg + 1) * LANES)
            nq_ref[:, sl] = yn[:, sl].astype(BF16)
            nqr_ref[:, sl] = _rope_lanes(yn[:, sl], cn, sn, NSA_HD // 2).astype(BF16)
        ng_ref[...] = proj(2560, 3072)
        ngl_ref[...] = proj(3072, 3200)


def _rope_tables(pos, half):
    inv = ROPE_THETA ** (-jnp.arange(half, dtype=F32) / half)
    ang = pos.astype(F32)[:, None] * inv[None, :]
    return jnp.cos(ang), jnp.sin(ang)


def _proj(x2d, pos_rows, norm_g, w_rm, w_t, kv_batch, tm, prompt):
    n = x2d.shape[0]
    nt = pos_rows.shape[0] // tm
    skv = n // kv_batch
    nkt = skv // tm
    c64, s64 = _rope_tables(pos_rows, 64)
    c32, s32 = _rope_tables(pos_rows, 32)
    cosr = jnp.concatenate([c64, c64], axis=1)
    sinr = jnp.concatenate([-s64, s64], axis=1)
    cosn = jnp.concatenate([c32, c32, c32, c32], axis=1)
    sinn = jnp.concatenate([-s32, s32, -s32, s32], axis=1)
    cost, sint = c32.T, s32.T
    row = lambda w: pl.BlockSpec((tm, w), lambda i: (i, 0))
    tab = pl.BlockSpec((tm, LANES), lambda i: (i % nt, 0))
    tabt = pl.BlockSpec((32, tm), lambda i: (0, i % nt))
    tspec = lambda r: pl.BlockSpec((1, r, tm), lambda i: (i // nkt, 0, i % nkt))
    f = lambda w, dt: jax.ShapeDtypeStruct((n, w), dt)
    ts = lambda r, dt: jax.ShapeDtypeStruct((kv_batch, r, skv), dt)
    kvo, kvs = tspec(KV_WIDTH), ts(KV_WIDTH, F32)
    out_specs = [row(512)] * 4
    out_shape = [f(512, F32)] * 4
    if prompt:
        out_specs += [row(KV_WIDTH)] * 2 + [kvo] * 8 + [tspec(512)] * 3 + [tspec(LANES)]
        out_shape += [f(KV_WIDTH, BF16)] * 2 + [kvs] * 6 + [ts(KV_WIDTH, BF16)] * 2 + [ts(512, BF16)] * 2 + [
            ts(512, F32), ts(LANES, F32)]
    else:
        out_specs += [row(512)] * 3 + [row(LANES)] + [kvo] * 6
        out_shape += [f(512, BF16)] * 2 + [f(512, F32), f(LANES, F32)] + [kvs] * 6
    return pl.pallas_call(
        functools.partial(_proj_kernel, prompt=prompt),
        grid=(n // tm,),
        in_specs=[row(D_MODEL), pl.BlockSpec((1, D_MODEL), lambda i: (0, 0)),
                  pl.BlockSpec(w_rm.shape, lambda i: (0, 0)), pl.BlockSpec(w_t.shape, lambda i: (0, 0)),
                  tab, tab, tab, tab, tabt, tabt],
        out_specs=out_specs,
        out_shape=out_shape,
        compiler_params=_params(("arbitrary",)),
        name="proj_prompt" if prompt else "proj_sample",
    )(x2d, norm_g, w_rm, w_t, cosr, sinr, cosn, sinn, cost, sint)


def _ret_kernel(q_ref, k_ref, v_ref, rg_ref, st_ref, dm_ref, qd_ref, kd_ref, sd_ref, gg_ref, gb_ref,
                y_ref, so_ref, st_scr, *, n_chunks):
    c = pl.program_id(1)

    @pl.when(c == 0)
    def _():
        st_scr[...] = st_ref[0]

    for h in range(RET_HEADS):
        sl = slice(h * LANES, (h + 1) * LANES)
        state = st_scr[h]
        for cc in range(q_ref.shape[0] // RET_CHUNK):
            rows = slice(cc * RET_CHUNK, (cc + 1) * RET_CHUNK)
            q = q_ref[rows, sl]
            k = k_ref[rows, sl]
            v = v_ref[rows, sl].astype(BF16)
            s = _dot_nt(q.astype(BF16), k.astype(BF16)) * dm_ref[h]
            intra = _dot(s.astype(BF16), v)
            cross = _dot((q * qd_ref[:, sl]).astype(BF16), state.astype(BF16))
            o = intra + cross
            kd_t = (k * kd_ref[:, sl]).T.astype(BF16)
            state = sd_ref[h] * state + _dot(kd_t, v)
            mu = jnp.mean(o, axis=-1, keepdims=True)
            var = jnp.mean(jnp.square(o - mu), axis=-1, keepdims=True)
            on = (o - mu) * lax.rsqrt(var + GN_EPS)
            gate = rg_ref[rows, sl]
            y_ref[rows, sl] = (on * gg_ref[:, sl] + gb_ref[:, sl]) * (gate * _sigmoid(gate))
        st_scr[h] = state

    @pl.when(c == n_chunks - 1)
    def _():
        so_ref[0] = st_scr[...]


def _retention(q, k, v, rg, state, gn_g, gn_b, n_batch, n_chunks, chunk_len):
    c = RET_CHUNK
    log_g = jnp.log(1.0 - 2.0 ** (-5.0 - jnp.arange(RET_HEADS, dtype=F32)))
    i = jnp.arange(c, dtype=F32)
    diff = i[:, None] - i[None, :]
    causal = diff >= 0
    dmask = jnp.where(causal[None], jnp.exp(jnp.where(causal, diff, 0.0)[None] * log_g[:, None, None]), 0.0)
    expand = lambda t: jnp.repeat(t, LANES, axis=1)
    qdec = expand(jnp.exp((i[:, None] + 1.0) * log_g[None, :]))
    kdec = expand(jnp.exp((chunk_len - 1.0 - i)[:, None] * log_g[None, :]))
    sdec = jnp.broadcast_to(jnp.exp(chunk_len * log_g)[:, None, None], (RET_HEADS, 1, LANES))
    stspec1 = pl.BlockSpec((1, RET_HEADS, 128, 128), lambda b: (b, 0, 0, 0))
    const1 = lambda shape: pl.BlockSpec(shape, lambda b: (0,) * len(shape))
    if chunk_len < c:
        assert n_chunks == 1
        row1 = pl.BlockSpec((chunk_len, 512), lambda b: (b, 0))
        return pl.pallas_call(
            _ret_short_kernel,
            grid=(n_batch,),
            in_specs=[row1, row1, row1, row1, stspec1, const1((RET_HEADS, c, c)), const1((c, 512)), const1((c, 512)),
                      const1((RET_HEADS, 1, LANES)), const1((1, 512)), const1((1, 512))],
            out_specs=[row1, stspec1],
            out_shape=[jax.ShapeDtypeStruct(q.shape, F32), jax.ShapeDtypeStruct(state.shape, F32)],
            scratch_shapes=[pltpu.VMEM((c, 512), F32)] * 3,
            compiler_params=_params(("arbitrary",)),
            name="retention_short",
        )(q, k, v, rg, state, dmask, qdec, kdec, sdec, gn_g, gn_b)
    per_step = RET_CHUNKS_PER_STEP if n_chunks % RET_CHUNKS_PER_STEP == 0 else 1
    n_steps = n_chunks // per_step
    row = pl.BlockSpec((per_step * c, 512), lambda b, j: (b * n_steps + j, 0))
    stspec = pl.BlockSpec((1, RET_HEADS, 128, 128), lambda b, j: (b, 0, 0, 0))
    const = lambda shape: pl.BlockSpec(shape, lambda b, j: (0,) * len(shape))
    return pl.pallas_call(
        functools.partial(_ret_kernel, n_chunks=n_steps),
        grid=(n_batch, n_steps),
        in_specs=[row, row, row, row, stspec, const((RET_HEADS, c, c)), const((c, 512)), const((c, 512)),
                  const((RET_HEADS, 1, LANES)), const((1, 512)), const((1, 512))],
        out_specs=[row, stspec],
        out_shape=[jax.ShapeDtypeStruct(q.shape, F32), jax.ShapeDtypeStruct(state.shape, F32)],
        scratch_shapes=[pltpu.VMEM((RET_HEADS, 128, 128), F32)],
        compiler_params=_params(("arbitrary", "arbitrary")),
        name="retention",
    )(q, k, v, rg, state, dmask, qdec, kdec, sdec, gn_g, gn_b)


def _ret_short_kernel(q_ref, k_ref, v_ref, rg_ref, st_ref, dm_ref, qd_ref, kd_ref, sd_ref, gg_ref, gb_ref,
                      y_ref, so_ref, k_pad, kd_pad, v_pad):
    t = q_ref.shape[0]

    @pl.when(pl.program_id(0) == 0)
    def _():
        for pad in (k_pad, kd_pad, v_pad):
            pad[...] = jnp.zeros(pad.shape, F32)

    k_pad[0:t, :] = k_ref[...]
    kd_pad[0:t, :] = k_ref[...] * kd_ref[0:t, :]
    v_pad[0:t, :] = v_ref[...]
    for h in range(RET_HEADS):
        sl = slice(h * LANES, (h + 1) * LANES)
        q = q_ref[:, sl]
        v = v_pad[:, sl].astype(BF16)
        state = st_ref[0, h]
        s = _dot_nt(q.astype(BF16), k_pad[:, sl].astype(BF16)) * dm_ref[h, 0:t, :]
        intra = _dot(s.astype(BF16), v)
        cross = _dot((q * qd_ref[0:t, sl]).astype(BF16), state.astype(BF16))
        o = intra + cross
        so_ref[0, h] = sd_ref[h] * state + _dot(kd_pad[:, sl].T.astype(BF16), v)
        mu = jnp.mean(o, axis=-1, keepdims=True)
        var = jnp.mean(jnp.square(o - mu), axis=-1, keepdims=True)
        on = (o - mu) * lax.rsqrt(var + GN_EPS)
        gate = rg_ref[:, sl]
        y_ref[:, sl] = (on * gg_ref[:, sl] + gb_ref[:, sl]) * (gate * _sigmoid(gate))


def _gelu_tanh(x):
    return 0.5 * x * (1.0 + jnp.tanh(np.sqrt(2.0 / np.pi).astype(np.float32) * (x + 0.044715 * (x * x * x))))


def _compress_one(tiles, pe_ref, w1_ref, w2_ref, scr, transpose_out=False):
    per_tile = LANES // CMP_BLOCK
    for t, tile in enumerate(tiles):
        rows_pm = tile.T
        for c in range(per_tile):
            r0 = (t * per_tile + c) * BLOCK_PITCH
            scr[r0:r0 + CMP_BLOCK, :] = rows_pm[c * CMP_BLOCK:(c + 1) * CMP_BLOCK]
    n_blk = len(tiles) * per_tile
    flat = [(scr[pl.ds(j, n_blk, stride=BLOCK_PITCH), :] + pe_ref[j:j + 1, :]).astype(BF16) for j in range(CMP_BLOCK)]
    hid = _gelu_tanh(_dot(jnp.concatenate(flat, axis=1), w1_ref[...]))
    if transpose_out:
        return _dot_nt(w2_ref[...], hid.astype(BF16))
    return _dot(hid.astype(BF16), w2_ref[...])


def _compress_prompt_kernel(k_ref, v_ref, pek_ref, w1k_ref, w2k_ref, pev_ref, w1v_ref, w2v_ref,
                            kc_ref, vct_ref, scr_k, scr_v, *, n_tiles):
    for src, pe, w1, w2, dst, scr, tr in ((k_ref, pek_ref, w1k_ref, w2k_ref, kc_ref, scr_k, False),
                                          (v_ref, pev_ref, w1v_ref, w2v_ref, vct_ref, scr_v, True)):
        tiles = [src[0, :, t * LANES:(t + 1) * LANES] for t in range(n_tiles)]
        dst[0] = _compress_one(tiles, pe, w1, w2, scr, tr)


def _compress_sample_kernel(pt_ref, ck_hbm, cv_hbm, pek_ref, w1k_ref, w2k_ref, pev_ref, w1v_ref, w2v_ref,
                            kc_ref, vc_ref, kbuf, vbuf, sems, scr_k, scr_v, *, n_pages, n_steps, n_batch):
    n_blk = n_pages * PAGE_SIZE // CMP_BLOCK

    def on_pages(step, slot):
        rows = pl.ds(pl.multiple_of(step * n_blk, n_blk), n_blk)
        kc_ref[0, rows, :] = _compress_one([kbuf[slot, j] for j in range(n_pages)], pek_ref, w1k_ref, w2k_ref, scr_k)
        vc_ref[0, rows, :] = _compress_one([vbuf[slot, j] for j in range(n_pages)], pev_ref, w1v_ref, w2v_ref, scr_v)

    _stream_pages(pt_ref, (ck_hbm, cv_hbm), (kbuf, vbuf), sems, n_pages, n_steps, n_batch, on_pages)


def _compress_weights(pe, w1, w2, transpose_w2=False):
    pe_rows = pe.reshape(CMP_BLOCK, KV_WIDTH)
    z = jnp.zeros((CMP_BLOCK, NSA_HD, NSA_HD), F32)
    w1bd = jnp.concatenate([jnp.concatenate([w1[0], z], axis=2), jnp.concatenate([z, w1[1]], axis=2)], axis=1)
    z2 = jnp.zeros((NSA_HD, NSA_HD), F32)
    w2bd = jnp.concatenate([jnp.concatenate([w2[0], z2], axis=1), jnp.concatenate([z2, w2[1]], axis=1)], axis=0)
    if transpose_w2:
        w2bd = w2bd.T
    return pe_rows, w1bd.reshape(CMP_BLOCK * KV_WIDTH, KV_WIDTH).astype(BF16), w2bd.astype(BF16)


def _cw_specs():
    zero = lambda *a: (0, 0)
    return [pl.BlockSpec((CMP_BLOCK, KV_WIDTH), zero), pl.BlockSpec((CMP_BLOCK * KV_WIDTH, KV_WIDTH), zero),
            pl.BlockSpec((KV_WIDTH, KV_WIDTH), zero)]


def _compress_prompt(ck_t, cv_t, cwk, cwv_t):
    b, _, s = ck_t.shape
    n_blk = s // CMP_BLOCK
    src = pl.BlockSpec((1, KV_WIDTH, s), lambda i: (i, 0, 0))
    return pl.pallas_call(
        functools.partial(_compress_prompt_kernel, n_tiles=s // LANES),
        grid=(b,),
        in_specs=[src, src] + _cw_specs() + _cw_specs(),
        out_specs=[pl.BlockSpec((1, n_blk, KV_WIDTH), lambda i: (i, 0, 0)),
                   pl.BlockSpec((1, KV_WIDTH, n_blk), lambda i: (i, 0, 0))],
        out_shape=[jax.ShapeDtypeStruct((b, n_blk, KV_WIDTH), F32), jax.ShapeDtypeStruct((b, KV_WIDTH, n_blk), F32)],
        scratch_shapes=[pltpu.VMEM((n_blk * BLOCK_PITCH, KV_WIDTH), F32)] * 2,
        compiler_params=_params(("arbitrary",)),
        name="compress_prompt",
    )(ck_t, cv_t, *cwk, *cwv_t)


def _compress_sample(page_table, cache_k_t, cache_v_t, cwk, cwv):
    db, n_pages = page_table.shape
    g = COMPRESS_PAGES_PER_STEP
    steps = n_pages // g
    blk_per_step = g * PAGE_SIZE // CMP_BLOCK
    n_blk = n_pages * PAGE_SIZE // CMP_BLOCK

    hbm = pl.BlockSpec(memory_space=pl.ANY)
    dst = pl.BlockSpec((1, n_blk, KV_WIDTH), lambda b, pt: (b, 0, 0))
    shp = jax.ShapeDtypeStruct((db, n_blk, KV_WIDTH), F32)
    page_buf = pltpu.VMEM((PAGE_SLOTS, g, KV_WIDTH, PAGE_SIZE), F32)
    grid_spec = pltpu.PrefetchScalarGridSpec(
        num_scalar_prefetch=1,
        grid=(db,),
        in_specs=[hbm, hbm] + _cw_specs() + _cw_specs(),
        out_specs=[dst, dst],
        scratch_shapes=[page_buf, page_buf, pltpu.SemaphoreType.DMA((PAGE_SLOTS, 2))]
        + [pltpu.VMEM((blk_per_step * BLOCK_PITCH, KV_WIDTH), F32)] * 2,
    )
    return pl.pallas_call(
        functools.partial(_compress_sample_kernel, n_pages=g, n_steps=steps, n_batch=db),
        grid_spec=grid_spec,
        out_shape=[shp, shp],
        compiler_params=_params(("arbitrary",)),
        name="compress_sample",
    )(page_table, cache_k_t, cache_v_t, *cwk, *cwv)


def _stack_heads_t(qt_ref):
    nq = qt_ref.shape[2]
    zero = jnp.zeros((NSA_HD, nq), BF16)
    cols = []
    for k in range(KV_HEADS):
        for g in range(NSA_GROUP):
            h = k * NSA_GROUP + g
            tile = qt_ref[0, h * NSA_HD:(h + 1) * NSA_HD, :] * ATTN_SCALE
            cols.append(jnp.concatenate([tile, zero] if k == 0 else [zero, tile], axis=0))
    return jnp.concatenate(cols, axis=1)


def _rank_select_rows(score, n_blk):
    parts = [score[v * SUBLANES:(v + 1) * SUBLANES] for v in range(n_blk // SUBLANES)]
    ranks = [jnp.zeros(p.shape, jnp.int32) for p in parts]
    for j in range(n_blk):
        col = score[j:j + 1, :]
        for v, part in enumerate(parts):
            ge, gt = (col >= part).astype(jnp.int32), (col > part).astype(jnp.int32)
            if v * SUBLANES > j:
                beats = ge
            elif (v + 1) * SUBLANES - 1 <= j:
                beats = gt
            else:
                row = v * SUBLANES + lax.broadcasted_iota(jnp.int32, (SUBLANES, 1), 0)
                beats = jnp.where(row > j, ge, gt)
            ranks[v] = ranks[v] + beats
    return jnp.concatenate([(r < TOP_N).astype(F32) for r in ranks], axis=0)


def _softmax_tile_t(s, v_t, m_ref, acc_ref, s_max=None):
    half = s.shape[1] // KV_HEADS
    m_old = m_ref[...]
    m_new = jnp.maximum(m_old, jnp.max(s, axis=0, keepdims=True) if s_max is None else s_max)
    alpha = jnp.exp(m_old - m_new)
    pb = jnp.exp(s - m_new).astype(BF16)
    m_ref[...] = m_new
    ones = jnp.ones((NSA_HD, v_t.shape[1]), BF16)
    v0 = jnp.concatenate([v_t[:NSA_HD], ones], axis=0)
    v1 = jnp.concatenate([ones, v_t[NSA_HD:]], axis=0)
    pv = jnp.concatenate([_dot(v0, pb[:, :half]), _dot(v1, pb[:, half:])], axis=1)
    acc_ref[...] = alpha * acc_ref[...] + pv


def _normalized_head_t(acc, k, cols):
    den = (1 - k) * NSA_HD
    return acc[k * NSA_HD:(k + 1) * NSA_HD, cols] / acc[den:den + 1, cols]


def _attn_prompt_kernel(nqt_ref, nqrt_ref, nglt_ref, ngt_ref, kc_ref, vct_ref, sk_ref, svt_ref, wk_ref, wvt_ref,
                        et_ref, ewin_ref, y_ref, lhs_scr, qslc_scr, qwin_scr, ocmp_scr, t_scr, m_scr, acc_scr,
                        s_scr, smax_scr, sw_scr):
    i = pl.program_id(1)
    qb = Q_BLOCK
    q0 = i * qb
    cols8 = 2 * NSA_GROUP * qb
    qpos = q0 + lax.broadcasted_iota(jnp.int32, (1, qb), 1)
    lhs_scr[...] = _stack_heads_t(nqt_ref)
    q_rot = _stack_heads_t(nqrt_ref)
    qslc_scr[0:KV_WIDTH, :] = q_rot
    qwin_scr[0:KV_WIDTH, :] = q_rot
    def causal_cap():
        own_keys = q0 + lax.broadcasted_iota(jnp.int32, (qb, 1), 0)
        return jnp.concatenate([jnp.where(own_keys <= qpos, CAP_OPEN, NEG_INF)] * (2 * NSA_GROUP), axis=1)

    n_cmp = kc_ref.shape[1]
    s_all = _dot(kc_ref[0].astype(BF16), lhs_scr[...])
    cend = (lax.broadcasted_iota(jnp.int32, (n_cmp, 1), 0) + 1) * CMP_BLOCK - 1
    cmask = cend <= qpos
    vct = vct_ref[0].astype(BF16)
    imps = []
    for k in range(KV_HEADS):
        imp = None
        for g in range(NSA_GROUP):
            c0 = (k * NSA_GROUP + g) * qb
            s = jnp.where(cmask, s_all[:, c0:c0 + qb], NEG_INF)
            p = jnp.exp(s - jnp.max(s, axis=0, keepdims=True))
            p = p / jnp.sum(p, axis=0, keepdims=True) * cmask.astype(F32)
            imp = p if imp is None else imp + p
            ocmp_scr[:, c0:c0 + qb] = _dot(vct, p.astype(BF16))
        imps.append(imp)

    n_slc = n_cmp // 2
    blk = lax.broadcasted_iota(jnp.int32, (n_slc, 1), 0)
    valid = blk * SLC_BLOCK <= qpos

    def store_bias(k, sel):
        bias = jnp.concatenate([(sel - 1.0) * (-NEG_INF), jnp.zeros((LANES - n_slc, qb), F32)], axis=0).astype(BF16)
        c0 = k * NSA_GROUP * qb
        qslc_scr[KV_WIDTH:, c0:c0 + NSA_GROUP * qb] = jnp.concatenate([bias] * NSA_GROUP, axis=1)

    @pl.when(q0 + qb <= TOP_N * SLC_BLOCK)
    def _():
        for k in range(KV_HEADS):
            store_bias(k, valid.astype(F32))

    @pl.when(q0 + qb > TOP_N * SLC_BLOCK)
    def _():
        forced = (blk == 0) | (blk == qpos // SLC_BLOCK)
        for k in range(KV_HEADS):
            t_scr[...] = imps[k]
            pair = t_scr[pl.ds(0, n_slc, stride=2), :] + t_scr[pl.ds(1, n_slc, stride=2), :]
            score = jnp.where(forced, FORCED_SCORE, jnp.where(valid, pair, NEG_INF))
            store_bias(k, _rank_select_rows(score, n_slc))

    tk = SLC_KEY_TILE
    n_tiles = (q0 + qb - 1) // tk + 1
    m_scr[...] = jnp.full(m_scr.shape, M_INIT, F32)
    acc_scr[...] = jnp.zeros(acc_scr.shape, F32)

    def scores(t):
        k0 = pl.multiple_of(t * tk, tk)
        keys = jnp.concatenate([sk_ref[pl.ds(k0, tk), :], et_ref[pl.ds(k0, tk), :]], axis=1)
        s = _dot(keys, qslc_scr[...])
        s_scr[t % 2] = s
        smax_scr[t % 2] = jnp.max(s, axis=0, keepdims=True)

    def update(t):
        k0 = pl.multiple_of(t * tk, tk)
        _softmax_tile_t(s_scr[t % 2], svt_ref[0, :, pl.ds(k0, tk)], m_scr.at[0], acc_scr.at[0], smax_scr[t % 2])

    def slc_step(t, carry):
        scores(t + 1)
        update(t)
        return carry

    scores(0)
    lax.fori_loop(0, n_tiles - 1, slc_step, 0)
    last = n_tiles - 1
    own = pl.ds(pl.multiple_of(q0 - last * tk, qb), qb)
    s_scr[last % 2, own, :] = jnp.minimum(s_scr[last % 2, own, :], causal_cap())
    smax_scr[last % 2] = jnp.max(s_scr[last % 2], axis=0, keepdims=True)
    update(last)

    wlen = WINDOW + qb
    w0 = pl.multiple_of(jnp.maximum(i - WINDOW // qb, 0) * qb, qb)
    later = (w0 + lax.broadcasted_iota(jnp.int32, (LANES, 1), 0) * qb) > q0
    qwin_scr[KV_WIDTH:, :] = jnp.broadcast_to(jnp.where(later, NEG_INF, 0.0), (LANES, cols8)).astype(BF16)
    keys = jnp.concatenate([wk_ref[pl.ds(w0, wlen), :], ewin_ref[...]], axis=1)
    sw_scr[...] = _dot(keys, qwin_scr[...])
    edge_keys = w0 + lax.broadcasted_iota(jnp.int32, (qb, 1), 0)
    edge_cap = jnp.concatenate([jnp.where(qpos - edge_keys <= WINDOW, CAP_OPEN, NEG_INF)] * (2 * NSA_GROUP), axis=1)
    sw_scr[0:qb, :] = jnp.minimum(sw_scr[0:qb, :], edge_cap)
    own = pl.ds(pl.multiple_of(q0 - w0, qb), qb)
    sw_scr[own, :] = jnp.minimum(sw_scr[own, :], causal_cap())
    _softmax_tile_t(sw_scr[...], wvt_ref[0, :, pl.ds(w0, wlen)], m_scr.at[1], acc_scr.at[1])

    acc_slc, acc_win = acc_scr[0], acc_scr[1]
    sig = _sigmoid(nglt_ref[0])
    for pair in range(NSA_HEADS // 2):
        tiles = []
        for h in (2 * pair, 2 * pair + 1):
            k = h // NSA_GROUP
            rows = slice(k * NSA_HD, (k + 1) * NSA_HD)
            cols = slice(h * qb, (h + 1) * qb)
            gate = lambda r: sig[r * NSA_HEADS + h:r * NSA_HEADS + h + 1, :]
            o = (gate(0) * ocmp_scr[rows, cols] + gate(1) * _normalized_head_t(acc_slc, k, cols)
                 + gate(2) * _normalized_head_t(acc_win, k, cols))
            ng = ngt_ref[0, h * NSA_HD:(h + 1) * NSA_HD, :]
            tiles.append(o * (ng * _sigmoid(ng)))
        y_ref[:, pair * LANES:(pair + 1) * LANES] = jnp.concatenate(tiles, axis=0).T.astype(y_ref.dtype)


def _attn_prompt(nq_t, nqr_t, ngl_t, ng_t, kc, vc_t, sk_rm, sv_t, wk_rm, wv_t):
    b, _, s = sv_t.shape
    assert s // CMP_BLOCK == LANES, "one lane per compressed block"
    qb = Q_BLOCK
    nqb = s // qb
    cols8 = 2 * NSA_GROUP * qb
    expand_t = (jnp.arange(s)[:, None] // SLC_BLOCK == jnp.arange(LANES)[None, :]).astype(BF16)
    wlen = WINDOW + qb
    win_blocks = (jnp.arange(wlen)[:, None] // qb == jnp.arange(LANES)[None, :]).astype(BF16)
    qcol = lambda r: pl.BlockSpec((1, r, qb), lambda bb, i: (bb, 0, i))
    per_b = lambda shape: pl.BlockSpec((1,) + shape, lambda bb, i: (bb, 0, 0))
    rows_b = pl.BlockSpec((s, KV_WIDTH), lambda bb, i: (bb, 0))
    return pl.pallas_call(
        _attn_prompt_kernel,
        grid=(b, nqb),
        in_specs=[qcol(512), qcol(512), qcol(LANES), qcol(512), per_b(kc.shape[1:]), per_b(vc_t.shape[1:]),
                  rows_b, per_b((KV_WIDTH, s)), rows_b, per_b((KV_WIDTH, s)),
                  pl.BlockSpec(expand_t.shape, lambda bb, i: (0, 0)),
                  pl.BlockSpec(win_blocks.shape, lambda bb, i: (0, 0))],
        out_specs=pl.BlockSpec((qb, 512), lambda bb, i: (bb * nqb + i, 0)),
        out_shape=jax.ShapeDtypeStruct((b * s, 512), BF16),
        scratch_shapes=[pltpu.VMEM((KV_WIDTH, cols8), BF16), pltpu.VMEM((2 * KV_WIDTH, cols8), BF16),
                        pltpu.VMEM((2 * KV_WIDTH, cols8), BF16),
                        pltpu.VMEM((KV_WIDTH, cols8), F32), pltpu.VMEM((LANES, qb), F32),
                        pltpu.VMEM((2, 1, cols8), F32), pltpu.VMEM((2, KV_WIDTH, cols8), F32),
                        pltpu.VMEM((2, SLC_KEY_TILE, cols8), F32), pltpu.VMEM((2, 1, cols8), F32),
                        pltpu.VMEM((wlen, cols8), F32)],
        compiler_params=_params(("arbitrary", "arbitrary")),
        name="attn_prompt",
    )(nq_t, nqr_t, ngl_t, ng_t, kc, vc_t, sk_rm, sv_t, wk_rm, wv_t, expand_t, win_blocks)


def _stack_heads(q_ref):
    lane = lax.broadcasted_iota(jnp.int32, (1, LANES), 1)
    lo = lane < NSA_HD
    slabs = [q_ref[:, g * LANES:(g + 1) * LANES] for g in range(NSA_GROUP)]
    zero = jnp.zeros_like(slabs[0])
    stacked = jnp.concatenate([jnp.where(lo, s, zero) for s in slabs] + [jnp.where(lo, zero, s) for s in slabs], axis=0)
    return stacked.astype(BF16)


def _mask_rows(sc, masks, rows):
    n = sc.shape[-1]
    s4 = sc.reshape(2 * NSA_GROUP, rows, n)
    out = [jnp.where(masks[k][None], s4[k * NSA_GROUP:(k + 1) * NSA_GROUP], NEG_INF) for k in range(KV_HEADS)]
    return jnp.concatenate(out, axis=0).reshape(2 * NSA_GROUP * rows, n)


def _pair_scores(imp, qpos, n_lanes):
    lane = lax.broadcasted_iota(jnp.int32, (1, n_lanes), 1)
    even = (lane % 2) == 0
    ps = imp + jnp.where(even, pltpu.roll(imp, n_lanes - 1, 1), pltpu.roll(imp, 1, 1))
    blk = lane // 2
    valid = blk * SLC_BLOCK <= qpos
    forced = (blk == 0) | (blk == qpos // SLC_BLOCK)
    return jnp.where(forced, FORCED_SCORE, jnp.where(valid, ps, NEG_INF))


def _rank_select(score, n_lanes, extra_forced):
    lane = lax.broadcasted_iota(jnp.int32, (1, n_lanes), 1)
    blk = lane // 2
    rank = jnp.zeros(score.shape, jnp.int32)
    for j in range(0, n_lanes, 2):
        col = score[:, j:j + 1]
        beats = (col > score) | ((col == score) & ((j // 2) < blk))
        rank = rank + beats.astype(jnp.int32)
    if extra_forced:
        rank = rank + (score < FORCED_SCORE).astype(jnp.int32)
    return rank < TOP_N


def _merge_kv_heads(acc, rows):
    lane = lax.broadcasted_iota(jnp.int32, (1, LANES), 1)
    lo = lane < NSA_HD
    half = NSA_GROUP * rows
    return [jnp.where(lo, acc[g * rows:(g + 1) * rows], acc[half + g * rows:half + (g + 1) * rows])
            for g in range(NSA_GROUP)]


def _gate_and_store(o_cmp, o_slc, o_win, ngl_ref, ng_ref, y_ref):
    lane = lax.broadcasted_iota(jnp.int32, (1, LANES), 1)
    lo = lane < NSA_HD
    sig = _sigmoid(ngl_ref[...])
    for g in range(NSA_GROUP):
        gates = [jnp.where(lo, sig[:, r * 8 + g:r * 8 + g + 1], sig[:, r * 8 + 4 + g:r * 8 + 4 + g + 1]) for r in range(3)]
        o = gates[0] * o_cmp[g] + gates[1] * o_slc[g] + gates[2] * o_win[g]
        gate = ng_ref[:, g * LANES:(g + 1) * LANES]
        y_ref[:, g * LANES:(g + 1) * LANES] = (o * (gate * _sigmoid(gate))).astype(y_ref.dtype)


def _page_copies(pt_ref, b, chunk, slot, caches, bufs, sems, g):
    out = []
    for ci, (cache, buf) in enumerate(zip(caches, bufs)):
        for j in range(g):
            page = pt_ref[b, chunk * g + j]
            out.append(pltpu.make_async_copy(cache.at[page], buf.at[slot, j], sems.at[slot, ci]))
    return out


def _stream_pages(pt_ref, caches, bufs, sems, g, n_chunks, n_batch, compute):
    b = pl.program_id(0)
    total = n_batch * n_chunks
    n_slots = bufs[0].shape[0]
    depth = n_slots - 1
    assert n_chunks >= depth
    copies = functools.partial(_page_copies, pt_ref, caches=caches, bufs=bufs, sems=sems, g=g)

    @pl.when(b == 0)
    def _():
        for d in range(depth):
            for cp in copies(0, d, d):
                cp.start()

    def chunk(c, carry):
        flat = b * n_chunks + c
        ahead = jnp.minimum(flat + depth, total - 1)
        for cp in copies(ahead // n_chunks, ahead % n_chunks, (flat + depth) % n_slots):
            cp.start()
        for cp in copies(b, c, flat % n_slots):
            cp.wait()
        compute(c, flat % n_slots)
        return carry

    lax.fori_loop(0, n_chunks, chunk, 0)

    @pl.when(b == n_batch - 1)
    def _():
        for d in range(depth):
            for cp in copies(n_batch - 1, n_chunks - 1, (total + d) % n_slots):
                cp.wait()


def _attn_sample_kernel(pt_ref, sk_hbm, sv_hbm, nq_ref, nqr_ref, ngl_ref, ng_ref, kc_ref, vc_ref, skn_ref, svn_ref,
                        wko_ref, wvo_ref, wkn_ref, wvn_ref, e_ref, y_ref, kbuf, vbuf, sems, m_scr, l_scr, acc_scr,
                        ocmp_scr, owin_scr, mask_scr, *, n_pages, n_steps, n_batch, t_len, past_len):
    t = t_len
    nq_ref, nqr_ref, ngl_ref, ng_ref, y_ref = (r.at[0] for r in (nq_ref, nqr_ref, ngl_ref, ng_ref, y_ref))
    rows8 = 2 * NSA_GROUP * t
    half = NSA_GROUP * t
    tpos = lax.broadcasted_iota(jnp.int32, (t, 1), 0)
    qpos = past_len + tpos
    n_cmp = kc_ref.shape[1]
    wb = wko_ref.shape[2]
    step_keys = n_pages * PAGE_SIZE

    def before_pages():
        lhs = _stack_heads(nq_ref)
        s = _dot_nt(lhs, kc_ref[0].astype(BF16)) * ATTN_SCALE
        m = jnp.max(s, axis=-1, keepdims=True)
        p = jnp.exp(s - m)
        p = p / jnp.sum(p, axis=-1, keepdims=True)
        oc = _dot(p.astype(BF16), vc_ref[0].astype(BF16))
        for g, slab in enumerate(_merge_kv_heads(oc, t)):
            ocmp_scr[g] = slab
        p4 = p.reshape(8, t, n_cmp)
        for k in range(KV_HEADS):
            imp = p4[k * NSA_GROUP]
            for g in range(1, NSA_GROUP):
                imp = imp + p4[k * NSA_GROUP + g]
            sel = _rank_select(_pair_scores(imp, qpos, n_cmp), n_cmp, True).astype(BF16)
            for c in range(n_cmp // LANES):
                km = _dot(sel[:, c * LANES:(c + 1) * LANES], e_ref[...])
                span = LANES * CMP_BLOCK
                if step_keys >= span:
                    off = (c * span) % step_keys
                    mask_scr[(c * span) // step_keys, k, :, off:off + span] = km
                else:
                    per = span // step_keys
                    for u in range(per):
                        mask_scr[c * per + u, k] = km[:, u * step_keys:(u + 1) * step_keys]

        lhs_r = _stack_heads(nqr_ref)
        so = _dot(lhs_r, wko_ref[0].astype(BF16)) * ATTN_SCALE
        sn = _dot(lhs_r, wkn_ref[0].astype(BF16)) * ATTN_SCALE
        jo = lax.broadcasted_iota(jnp.int32, (1, wb), 1)
        jn = lax.broadcasted_iota(jnp.int32, (1, LANES), 1)
        old_vis = jo >= tpos
        so = _mask_rows(so, [old_vis, old_vis], t)
        new_vis = jn <= tpos
        sn_w = _mask_rows(sn, [new_vis, new_vis], t)
        m = jnp.maximum(jnp.max(so, axis=-1, keepdims=True), jnp.max(sn_w, axis=-1, keepdims=True))
        po = jnp.exp(so - m)
        pn = jnp.exp(sn_w - m)
        den = jnp.sum(po, axis=-1, keepdims=True) + jnp.sum(pn, axis=-1, keepdims=True)
        pob, pnb = (po / den).astype(BF16), (pn / den).astype(BF16)
        wvo, wvn = wvo_ref[0].astype(BF16), wvn_ref[0].astype(BF16)
        ow = jnp.concatenate([_dot_nt(pob[:half], wvo) + _dot_nt(pnb[:half], wvn),
                              _dot_nt(pob[half:], wvo) + _dot_nt(pnb[half:], wvn)], axis=0)
        for g, slab in enumerate(_merge_kv_heads(ow, t)):
            owin_scr[g] = slab

        s2 = _dot(lhs_r, skn_ref[0].astype(BF16)) * ATTN_SCALE
        s2 = _mask_rows(s2, [new_vis, new_vis], t)
        m2 = jnp.max(s2, axis=-1, keepdims=True)
        p2 = jnp.exp(s2 - m2)
        m_scr[0] = m2
        l_scr[0] = jnp.sum(p2, axis=-1, keepdims=True)
        p2b = p2.astype(BF16)
        svn = svn_ref[0].astype(BF16)
        acc_scr[0] = jnp.concatenate([_dot_nt(p2b[:half], svn), _dot_nt(p2b[half:], svn)], axis=0)
        for st in range(1, SAMPLE_STREAMS):
            m_scr[st] = jnp.full((rows8, 1), M_INIT, F32)
            l_scr[st] = jnp.zeros((rows8, 1), F32)
            acc_scr[st] = jnp.zeros((rows8, LANES), F32)

    def on_pages(step, slot):
        lhs_r = _stack_heads(nqr_ref)
        per = n_pages // SAMPLE_STREAMS
        msk = mask_scr[step]
        for st in range(SAMPLE_STREAMS):
            pages = range(st * per, (st + 1) * per)
            keys = slice(st * per * PAGE_SIZE, (st + 1) * per * PAGE_SIZE)
            kt = jnp.concatenate([kbuf[slot, j].astype(BF16) for j in pages], axis=1)
            vt = jnp.concatenate([vbuf[slot, j].astype(BF16) for j in pages], axis=1)
            sc = _dot(lhs_r, kt) * ATTN_SCALE
            sc = _mask_rows(sc, [msk[k][:, keys] > 0.5 for k in range(KV_HEADS)], t)
            m_i = m_scr[st]
            m_new = jnp.maximum(m_i, jnp.max(sc, axis=-1, keepdims=True))
            alpha = jnp.exp(m_i - m_new)
            pr = jnp.exp(sc - m_new)
            l_scr[st] = alpha * l_scr[st] + jnp.sum(pr, axis=-1, keepdims=True)
            m_scr[st] = m_new
            pb = pr.astype(BF16)
            pv = jnp.concatenate([_dot_nt(pb[:half], vt), _dot_nt(pb[half:], vt)], axis=0)
            acc_scr[st] = alpha * acc_scr[st] + pv

    before_pages()
    _stream_pages(pt_ref, (sk_hbm, sv_hbm), (kbuf, vbuf), sems, n_pages, n_steps, n_batch, on_pages)
    m_all = m_scr[0]
    for st in range(1, SAMPLE_STREAMS):
        m_all = jnp.maximum(m_all, m_scr[st])
    l_all = jnp.zeros((rows8, 1), F32)
    acc_all = jnp.zeros((rows8, LANES), F32)
    for st in range(SAMPLE_STREAMS):
        w = jnp.exp(m_scr[st] - m_all)
        l_all = l_all + w * l_scr[st]
        acc_all = acc_all + w * acc_scr[st]
    o_slc = _merge_kv_heads(acc_all / l_all, t)
    o_cmp = [ocmp_scr[g] for g in range(NSA_GROUP)]
    o_win = [owin_scr[g] for g in range(NSA_GROUP)]
    _gate_and_store(o_cmp, o_slc, o_win, ngl_ref, ng_ref, y_ref)


def _attn_sample(page_table, cache_sk_t, cache_sv_t, nq, nqr, ngl, ng, kc, vc, skn, svn, wko, wvo, wkn, wvn, t_len):
    db, n_pages = page_table.shape
    g = SAMPLE_PAGES_PER_STEP
    steps = n_pages // g
    past_len = n_pages * PAGE_SIZE
    step_keys = g * PAGE_SIZE
    span = LANES * CMP_BLOCK
    expand = (jnp.arange(span)[None, :] // CMP_BLOCK == jnp.arange(LANES)[:, None]).astype(BF16)

    hbm = pl.BlockSpec(memory_space=pl.ANY)
    row = lambda w: pl.BlockSpec((1, t_len, w), lambda b, pt: (b, 0, 0))
    per_b = lambda shape: pl.BlockSpec((1,) + shape, lambda b, pt: (b, 0, 0))
    rows8 = 2 * NSA_GROUP * t_len
    page_buf = pltpu.VMEM((PAGE_SLOTS, g, KV_WIDTH, PAGE_SIZE), F32)
    grid_spec = pltpu.PrefetchScalarGridSpec(
        num_scalar_prefetch=1,
        grid=(db,),
        in_specs=[hbm, hbm, row(512), row(512), row(LANES), row(512), per_b(kc.shape[1:]), per_b(vc.shape[1:]),
                  per_b(skn.shape[1:]), per_b(svn.shape[1:]), per_b(wko.shape[1:]), per_b(wvo.shape[1:]),
                  per_b(wkn.shape[1:]), per_b(wvn.shape[1:]), pl.BlockSpec(expand.shape, lambda b, pt: (0, 0))],
        out_specs=row(512),
        scratch_shapes=[page_buf, page_buf, pltpu.SemaphoreType.DMA((PAGE_SLOTS, 2)),
                        pltpu.VMEM((SAMPLE_STREAMS, rows8, 1), F32), pltpu.VMEM((SAMPLE_STREAMS, rows8, 1), F32),
                        pltpu.VMEM((SAMPLE_STREAMS, rows8, LANES), F32),
                        pltpu.VMEM((NSA_GROUP, t_len, LANES), F32), pltpu.VMEM((NSA_GROUP, t_len, LANES), F32),
                        pltpu.VMEM((steps, KV_HEADS, t_len, step_keys), F32)],
    )
    return pl.pallas_call(
        functools.partial(_attn_sample_kernel, n_pages=g, n_steps=steps, n_batch=db, t_len=t_len, past_len=past_len),
        grid_spec=grid_spec,
        out_shape=jax.ShapeDtypeStruct((db, t_len, 512), F32),
        compiler_params=_params(("arbitrary",)),
        name="attn_sample",
    )(page_table, cache_sk_t, cache_sv_t, nq, nqr, ngl, ng, kc, vc, skn, svn, wko, wvo, wkn, wvn, expand)


def _finish_kernel(x_ref, yr_ref, yn_ref, p_ref, wo_ref, gple_ref, wg_ref, wp_ref, gf_ref, o_ref):
    x = x_ref[...]
    x = x + _dot(yr_ref[...].astype(BF16), wo_ref[0:RET_WIDTH, :]) + _dot(yn_ref[...].astype(BF16), wo_ref[RET_WIDTH:, :])
    ms = jnp.mean(x * x, axis=-1, keepdims=True)
    hn = (x * lax.rsqrt(ms + RMS_EPS) * gple_ref[...]).astype(BF16)
    gate = _sigmoid(_dot(hn, wg_ref[...]))
    x = x + gate * _dot(p_ref[...].astype(BF16), wp_ref[...])
    ms = jnp.mean(x * x, axis=-1, keepdims=True)
    o_ref[...] = x * lax.rsqrt(ms + RMS_EPS) * gf_ref[...]


def _finish(x2d, y_ret, y_nsa, p2d, w_out, norm_ple, w_gate, w_ple, norm_f, tm):
    n = x2d.shape[0]
    row = lambda w: pl.BlockSpec((tm, w), lambda i: (i, 0))
    const = lambda shape: pl.BlockSpec(shape, lambda i: (0, 0))
    return pl.pallas_call(
        _finish_kernel,
        grid=(n // tm,),
        in_specs=[row(D_MODEL), row(RET_WIDTH), row(NSA_WIDTH), row(PLE_DIM), const(w_out.shape), const((1, D_MODEL)),
                  const(w_gate.shape), const(w_ple.shape), const((1, D_MODEL))],
        out_specs=row(D_MODEL),
        out_shape=jax.ShapeDtypeStruct((n, D_MODEL), F32),
        compiler_params=_params(("arbitrary",)),
        name="finish",
    )(x2d, y_ret, y_nsa, p2d, w_out, norm_ple, w_gate, w_ple, norm_f)


def _slab_perm():
    return np.array([(k * NSA_GROUP + g) * NSA_HD + d for g in range(NSA_GROUP) for k in range(KV_HEADS)
                     for d in range(NSA_HD)], np.int32)


def _to_positions_major(x_t):
    lead = x_t.shape[:-2]
    n = len(lead)
    x4 = x_t.reshape(lead + (KV_HEADS, NSA_HD, x_t.shape[-1]))
    return jnp.transpose(x4, tuple(range(n)) + (n + 2, n, n + 1))


def _cache_t(cache):
    n_pool, page = cache.shape[:2]
    return jnp.transpose(cache, (0, 2, 3, 1)).reshape(n_pool, KV_WIDTH, page)


def _layer(xp, xs, c_ck, c_cv, c_sk, c_sv, win_k, win_v, ret_state, page_table, p_p, p_s, norm_mix, w_in, gn_g, gn_b,
           pe_k, w1_k, w2_k, pe_v, w1_v, w2_v, w_out, norm_ple, w_gate, w_ple, norm_f):
    b, s, d = xp.shape
    db, t, _ = xs.shape
    n_pages = page_table.shape[1]
    past = n_pages * PAGE_SIZE

    off = np.cumsum((0,) + SPLIT_SIZES)
    col = lambda i: w_in[:, off[i]:off[i + 1]]
    perm = _slab_perm()
    ngl_w = jnp.pad(col(11), ((0, 0), (0, LANES - N_GATES)))
    ret_cols = [col(0), col(1), col(2), col(3)]
    kv_cols = [col(5), col(6), col(8), col(10), col(7), col(9)]
    w_rm_p = jnp.concatenate(ret_cols + [col(7), col(9)], axis=1).astype(BF16)
    w_t_p = jnp.concatenate(kv_cols + [col(4), col(12), ngl_w], axis=1).T.astype(BF16)
    w_rm_s = jnp.concatenate(ret_cols + [col(4)[:, perm], col(12)[:, perm], ngl_w], axis=1).astype(BF16)
    w_t_s = jnp.concatenate(kv_cols, axis=1).T.astype(BF16)
    w_out_b = w_out.astype(BF16)
    w_out_slab = jnp.concatenate([w_out[:RET_WIDTH], w_out[RET_WIDTH:][perm]], axis=0).astype(BF16)
    g_mix = norm_mix.reshape(1, d)
    cwk = _compress_weights(pe_k, w1_k, w2_k)
    cwv = _compress_weights(pe_v, w1_v, w2_v)
    cwv_t = _compress_weights(pe_v, w1_v, w2_v, transpose_w2=True)
    gn_g2, gn_b2 = gn_g.reshape(1, RET_WIDTH), gn_b.reshape(1, RET_WIDTH)
    fin_w = (norm_ple.reshape(1, d), w_gate.astype(BF16), w_ple.astype(BF16), norm_f.reshape(1, d))

    tm = 512
    (q_ret, k_ret, v_ret, rg, sk_rm, wk_rm, ck_t, cv_t, sv_t, wv_t, sk_t, wk_t, sv_b, wv_b, nq_t, nqr_t, ng_t,
     ngl_t) = _proj(xp.reshape(b * s, d), jnp.arange(s), g_mix, w_rm_p, w_t_p, b, tm, True)
    y_ret, ret_p = _retention(q_ret, k_ret, v_ret, rg, jnp.zeros((b, RET_HEADS, RET_DK, RET_DK), F32), gn_g2, gn_b2,
                              b, s // RET_CHUNK, RET_CHUNK)
    kc, vc_t = _compress_prompt(ck_t, cv_t, cwk, cwv_t)
    y_nsa = _attn_prompt(nq_t, nqr_t, ngl_t, ng_t, kc, vc_t, sk_rm, sv_b, wk_rm, wv_b)
    y_prompt = _finish(xp.reshape(b * s, d), y_ret, y_nsa, p_p.reshape(b * s, PLE_DIM), w_out_b, *fin_w, tm)
    wb_p = min(WINDOW, s)
    prompt_states = (ret_p, _to_positions_major(ck_t), _to_positions_major(cv_t), _to_positions_major(sk_t),
                     _to_positions_major(sv_t), _to_positions_major(wk_t[:, :, s - wb_p:]),
                     _to_positions_major(wv_t[:, :, s - wb_p:]))

    n = db * t
    pos_s = jnp.tile(past + jnp.arange(t), db)
    (q_ret, k_ret, v_ret, rg, nq, nqr, ng, ngl, ck_n, cv_n, sv_n, wv_n, sk_n, wk_n) = _proj(
        xs.reshape(n, d), pos_s, g_mix, w_rm_s, w_t_s, 1, n, False)
    y_ret, ret_s = _retention(q_ret, k_ret, v_ret, rg, ret_state, gn_g2, gn_b2, db, 1, t)
    kc, vc = _compress_sample(page_table, _cache_t(c_ck), _cache_t(c_cv), cwk, cwv)
    per_batch = lambda a: jnp.transpose(a[0].reshape(KV_WIDTH, db, t), (1, 0, 2))
    lane_pad = lambda a: jnp.pad(per_batch(a), ((0, 0), (0, 0), (0, LANES - t)))
    win_t = lambda w: jnp.transpose(w, (0, 2, 3, 1)).reshape(db, KV_WIDTH, w.shape[1])
    rows3 = lambda a: a.astype(F32).reshape(db, t, a.shape[-1])
    y_nsa = _attn_sample(page_table, _cache_t(c_sk), _cache_t(c_sv), rows3(nq), rows3(nqr), rows3(ngl), rows3(ng), kc, vc,
                         lane_pad(sk_n), lane_pad(sv_n), win_t(win_k), win_t(win_v), lane_pad(wk_n), lane_pad(wv_n), t)
    y_nsa = y_nsa.reshape(n, NSA_WIDTH)
    y_sample = _finish(xs.reshape(n, d), y_ret, y_nsa, p_s.reshape(n, PLE_DIM), w_out_slab, *fin_w, n)
    new_rows = lambda a: _to_positions_major(per_batch(a))
    sample_states = (ret_s, new_rows(ck_n), new_rows(cv_n), new_rows(sk_n), new_rows(sv_n),
                     jnp.concatenate([win_k[:, t:], new_rows(wk_n)], axis=1),
                     jnp.concatenate([win_v[:, t:], new_rows(wv_n)], axis=1))
    return y_prompt.reshape(b, s, d), y_sample.reshape(db, t, d), prompt_states, sample_states


def kernel(x_prompt, x_sample, cache_cmp_k, cache_cmp_v, cache_slc_k, cache_slc_v, state_win_k, state_win_v, state_ret, page_table, p_prompt, p_sample, norm_mix, w_in, ret_gn_g, ret_gn_b, cmp_pe_k, cmp_w1_k, cmp_w2_k, cmp_pe_v, cmp_w1_v, cmp_w2_v, w_out, norm_ple, w_ple_gate, w_ple, norm_f):
    depth = w_in.shape[0]
    assert depth == 1, "single trunk layer"
    l = 0
    yp, ys, sp, ss = _layer(x_prompt, x_sample, cache_cmp_k[l], cache_cmp_v[l], cache_slc_k[l], cache_slc_v[l],
                            state_win_k[l], state_win_v[l], state_ret[l], page_table, p_prompt[l], p_sample[l],
                            norm_mix[l], w_in[l], ret_gn_g[l], ret_gn_b[l], cmp_pe_k[l], cmp_w1_k[l], cmp_w2_k[l],
                            cmp_pe_v[l], cmp_w1_v[l], cmp_w2_v[l], w_out[l], norm_ple[l], w_ple_gate[l], w_ple[l], norm_f)
    return (yp, ys) + tuple(a[None] for a in sp) + tuple(a[None] for a in ss)
```

```python
import functools

import numpy as np
import jax
import jax.numpy as jnp
from jax import lax
from jax.experimental import pallas as pl
from jax.experimental.pallas import tpu as pltpu

F32 = jnp.float32
BF16 = jnp.bfloat16

D_MODEL = 1024
PLE_DIM = 256
PAGE_SIZE = 128
RET_HEADS = 4
RET_DK = 128
RET_WIDTH = 512
RET_CHUNK = 128
NSA_HEADS = 8
KV_HEADS = 2
NSA_GROUP = 4
NSA_HD = 64
NSA_WIDTH = 512
KV_WIDTH = 128
CMP_BLOCK = 32
SLC_BLOCK = 64
TOP_N = 16
WINDOW = 512
ROPE_THETA = 10000.0
RMS_EPS = 1e-6
GN_EPS = 1e-5
NEG_INF = -1e9
FORCED_SCORE = 1e4
M_INIT = -1e30
CAP_OPEN = 3e38
ATTN_SCALE = NSA_HD ** -0.5
SPLIT_SIZES = (512, 512, 512, 512, 512, 128, 128, 128, 128, 128, 128, 24, 512)
N_GATES = 3 * NSA_HEADS

LANES = 128
SUBLANES = 8
PROMPT_Q_TILE = 128
SLC_KEY_TILE = 512
RET_CHUNKS_PER_STEP = 4
SAMPLE_PAGES_PER_STEP = 64
SAMPLE_STREAMS = 1
COMPRESS_PAGES_PER_STEP = 32
PAGE_SLOTS = 3
BLOCK_PITCH = 40
VMEM_LIMIT = 56 * 1024 * 1024


def _dot(a, b):
    return jnp.dot(a, b, preferred_element_type=F32)


def _dot_nt(a, b):
    return lax.dot_general(a, b, (((1,), (1,)), ((), ())), preferred_element_type=F32)


def _sigmoid(x):
    return 1.0 / (1.0 + jnp.exp(-x))


def _params(sem):
    return pltpu.CompilerParams(dimension_semantics=sem, vmem_limit_bytes=VMEM_LIMIT)


def _rope_lanes(a, cos, sin_signed, half):
    if 2 * half == LANES:
        partner = pltpu.roll(a, half, 1)
    else:
        lane = lax.broadcasted_iota(jnp.int32, (1, LANES), 1)
        partner = jnp.where((lane % (2 * half)) < half, pltpu.roll(a, LANES - half, 1), pltpu.roll(a, half, 1))
    return a * cos + partner * sin_signed


def _rope_sublanes(blk, cos_t, sin_t):
    outs = []
    for h in range(blk.shape[0] // NSA_HD):
        x1 = blk[h * NSA_HD:h * NSA_HD + NSA_HD // 2]
        x2 = blk[h * NSA_HD + NSA_HD // 2:(h + 1) * NSA_HD]
        outs += [x1 * cos_t - x2 * sin_t, x2 * cos_t + x1 * sin_t]
    return jnp.concatenate(outs, axis=0)


def _proj_kernel(*refs, prompt):
    (x_ref, g_ref, wrm_ref, wt_ref, cosr_ref, sinr_ref, cosn_ref, sinn_ref, cost_ref, sint_ref,
     qret_ref, kret_ref, vret_ref, rg_ref) = refs[:14]
    x = x_ref[...]
    ms = jnp.mean(x * x, axis=-1, keepdims=True)
    h = (x * lax.rsqrt(ms + RMS_EPS) * g_ref[...]).astype(BF16)

    def proj(lo, hi):
        return _dot(h, wrm_ref[:, lo:hi]) if prompt else _dot_nt(h, wrm_ref[lo:hi, :])

    cr, sr = cosr_ref[...], sinr_ref[...]
    cn, sn = cosn_ref[...], sinn_ref[...]
    ct, st = cost_ref[...], sint_ref[...]
    yq = proj(0, 512)
    yk = proj(512, 1024)
    for hh in range(RET_HEADS):
        sl = slice(hh * LANES, (hh + 1) * LANES)
        qret_ref[:, sl] = _rope_lanes(yq[:, sl], cr, sr, RET_DK // 2)
        kret_ref[:, sl] = _rope_lanes(yk[:, sl], cr, sr, RET_DK // 2) * (RET_DK ** -0.5)
    vret_ref[...] = proj(1024, 1536)
    rg_ref[...] = proj(1536, 2048)

    yt = _dot_nt(wt_ref[...], h)
    if prompt:
        (skr_ref, wkr_ref, ck_ref, cv_ref, sv_ref, wv_ref, sk_ref, wk_ref, svb_ref, wvb_ref,
         nqt_ref, nqrt_ref, ngt_ref, nglt_ref) = refs[14:]
    else:
        nq_ref, nqr_ref, ng_ref, ngl_ref, ck_ref, cv_ref, sv_ref, wv_ref, sk_ref, wk_ref = refs[14:]
    ck_ref[0] = yt[0:128]
    cv_ref[0] = yt[128:256]
    sv_ref[0] = yt[256:384]
    wv_ref[0] = yt[384:512]
    sk_ref[0] = _rope_sublanes(yt[512:640], ct, st)
    wk_ref[0] = _rope_sublanes(yt[640:768], ct, st)
    if prompt:
        svb_ref[0] = yt[256:384].astype(BF16)
        wvb_ref[0] = yt[384:512].astype(BF16)
        nqt = yt[768:1280]
        nqt_ref[0] = nqt.astype(BF16)
        nqrt_ref[0] = _rope_sublanes(nqt, ct, st).astype(BF16)
        ngt_ref[0] = yt[1280:1792]
        nglt_ref[0] = yt[1792:1920]
        skr_ref[...] = _rope_lanes(proj(2048, 2176), cn, sn, NSA_HD // 2).astype(BF16)
        wkr_ref[...] = _rope_lanes(proj(2176, 2304), cn, sn, NSA_HD // 2).astype(BF16)
    else:
        yn = proj(2048, 2560)
        for gg in range(NSA_GROUP):
            sl = slice(gg * LANES, (gg + 1) * LANES)
            nq_ref[:, sl] = yn[:, sl].astype(BF16)
            nqr_ref[:, sl] = _rope_lanes(yn[:, sl], cn, sn, NSA_HD // 2).astype(BF16)
        ng_ref[...] = proj(2560, 3072)
        ngl_ref[...] = proj(3072, 3200)


def _rope_tables(pos, half):
    inv = ROPE_THETA ** (-np.arange(half, dtype=np.float64) / half)
    ang = np.asarray(pos, np.float64)[:, None] * inv[None, :]
    return np.cos(ang).astype(np.float32), np.sin(ang).astype(np.float32)


def _proj(x2d, pos_rows, norm_g, w_rm, w_t, kv_batch, tm, prompt):
    n = x2d.shape[0]
    nt = pos_rows.shape[0] // tm
    skv = n // kv_batch
    nkt = skv // tm
    c64, s64 = _rope_tables(pos_rows, 64)
    c32, s32 = _rope_tables(pos_rows, 32)
    cosr = np.concatenate([c64, c64], axis=1)
    sinr = np.concatenate([-s64, s64], axis=1)
    cosn = np.concatenate([c32, c32, c32, c32], axis=1)
    sinn = np.concatenate([-s32, s32, -s32, s32], axis=1)
    cost, sint = np.ascontiguousarray(c32.T), np.ascontiguousarray(s32.T)
    row = lambda w: pl.BlockSpec((tm, w), lambda i: (i, 0))
    tab = pl.BlockSpec((tm, LANES), lambda i: (i % nt, 0))
    tabt = pl.BlockSpec((32, tm), lambda i: (0, i % nt))
    tspec = lambda r: pl.BlockSpec((1, r, tm), lambda i: (i // nkt, 0, i % nkt))
    f = lambda w, dt: jax.ShapeDtypeStruct((n, w), dt)
    ts = lambda r, dt: jax.ShapeDtypeStruct((kv_batch, r, skv), dt)
    kvo, kvs = tspec(KV_WIDTH), ts(KV_WIDTH, F32)
    out_specs = [row(512)] * 4
    out_shape = [f(512, F32)] * 4
    if prompt:
        out_specs += [row(KV_WIDTH)] * 2 + [kvo] * 8 + [tspec(512)] * 3 + [tspec(LANES)]
        out_shape += [f(KV_WIDTH, BF16)] * 2 + [kvs] * 6 + [ts(KV_WIDTH, BF16)] * 2 + [ts(512, BF16)] * 2 + [
            ts(512, F32), ts(LANES, F32)]
    else:
        out_specs += [row(512)] * 3 + [row(LANES)] + [kvo] * 6
        out_shape += [f(512, BF16)] * 2 + [f(512, F32), f(LANES, F32)] + [kvs] * 6
    return pl.pallas_call(
        functools.partial(_proj_kernel, prompt=prompt),
        grid=(n // tm,),
        in_specs=[row(D_MODEL), pl.BlockSpec((1, D_MODEL), lambda i: (0, 0)),
                  pl.BlockSpec(w_rm.shape, lambda i: (0, 0)), pl.BlockSpec(w_t.shape, lambda i: (0, 0)),
                  tab, tab, tab, tab, tabt, tabt],
        out_specs=out_specs,
        out_shape=out_shape,
        compiler_params=_params(("arbitrary",)),
        name="proj_prompt" if prompt else "proj_sample",
    )(x2d, norm_g, w_rm, w_t, cosr, sinr, cosn, sinn, cost, sint)


def _ret_kernel(q_ref, k_ref, v_ref, rg_ref, st_ref, dm_ref, qd_ref, kd_ref, sd_ref, gg_ref, gb_ref,
                y_ref, so_ref, st_scr, *, n_chunks):
    c = pl.program_id(1)

    @pl.when(c == 0)
    def _():
        st_scr[...] = st_ref[0]

    for h in range(RET_HEADS):
        sl = slice(h * LANES, (h + 1) * LANES)
        state = st_scr[h]
        for cc in range(q_ref.shape[0] // RET_CHUNK):
            rows = slice(cc * RET_CHUNK, (cc + 1) * RET_CHUNK)
            q = q_ref[rows, sl]
            k = k_ref[rows, sl]
            v = v_ref[rows, sl].astype(BF16)
            s = _dot_nt(q.astype(BF16), k.astype(BF16)) * dm_ref[h]
            intra = _dot(s.astype(BF16), v)
            cross = _dot((q * qd_ref[:, sl]).astype(BF16), state.astype(BF16))
            o = intra + cross
            kd_t = (k * kd_ref[:, sl]).T.astype(BF16)
            state = sd_ref[h] * state + _dot(kd_t, v)
            mu = jnp.mean(o, axis=-1, keepdims=True)
            var = jnp.mean(jnp.square(o - mu), axis=-1, keepdims=True)
            on = (o - mu) * lax.rsqrt(var + GN_EPS)
            gate = rg_ref[rows, sl]
            y_ref[rows, sl] = (on * gg_ref[:, sl] + gb_ref[:, sl]) * (gate * _sigmoid(gate))
        st_scr[h] = state

    @pl.when(c == n_chunks - 1)
    def _():
        so_ref[0] = st_scr[...]


def _retention(q, k, v, rg, state, gn_g, gn_b, n_batch, n_chunks, chunk_len):
    c = RET_CHUNK
    log_g = np.log(1.0 - 2.0 ** (-5.0 - np.arange(RET_HEADS, dtype=np.float64)))
    i = np.arange(c, dtype=np.float64)
    diff = i[:, None] - i[None, :]
    causal = diff >= 0
    f32 = lambda a: np.ascontiguousarray(a, dtype=np.float32)
    dmask = f32(np.where(causal[None], np.exp(np.where(causal, diff, 0.0)[None] * log_g[:, None, None]), 0.0))
    expand = lambda t: f32(np.repeat(t, LANES, axis=1))
    qdec = expand(np.exp((i[:, None] + 1.0) * log_g[None, :]))
    kdec = expand(np.exp((chunk_len - 1.0 - i)[:, None] * log_g[None, :]))
    sdec = f32(np.broadcast_to(np.exp(chunk_len * log_g)[:, None, None], (RET_HEADS, 1, LANES)))
    stspec1 = pl.BlockSpec((1, RET_HEADS, 128, 128), lambda b: (b, 0, 0, 0))
    const1 = lambda shape: pl.BlockSpec(shape, lambda b: (0,) * len(shape))
    if chunk_len < c:
        assert n_chunks == 1
        row1 = pl.BlockSpec((chunk_len, 512), lambda b: (b, 0))
        return pl.pallas_call(
            _ret_short_kernel,
            grid=(n_batch,),
            in_specs=[row1, row1, row1, row1, stspec1, const1((RET_HEADS, c, c)), const1((c, 512)), const1((c, 512)),
                      const1((RET_HEADS, 1, LANES)), const1((1, 512)), const1((1, 512))],
            out_specs=[row1, stspec1],
            out_shape=[jax.ShapeDtypeStruct(q.shape, F32), jax.ShapeDtypeStruct(state.shape, F32)],
            scratch_shapes=[pltpu.VMEM((c, 512), F32)] * 3,
            compiler_params=_params(("arbitrary",)),
            name="retention_short",
        )(q, k, v, rg, state, dmask, qdec, kdec, sdec, gn_g, gn_b)
    per_step = RET_CHUNKS_PER_STEP if n_chunks % RET_CHUNKS_PER_STEP == 0 else 1
    n_steps = n_chunks // per_step
    row = pl.BlockSpec((per_step * c, 512), lambda b, j: (b * n_steps + j, 0))
    stspec = pl.BlockSpec((1, RET_HEADS, 128, 128), lambda b, j: (b, 0, 0, 0))
    const = lambda shape: pl.BlockSpec(shape, lambda b, j: (0,) * len(shape))
    return pl.pallas_call(
        functools.partial(_ret_kernel, n_chunks=n_steps),
        grid=(n_batch, n_steps),
        in_specs=[row, row, row, row, stspec, const((RET_HEADS, c, c)), const((c, 512)), const((c, 512)),
                  const((RET_HEADS, 1, LANES)), const((1, 512)), const((1, 512))],
        out_specs=[row, stspec],
        out_shape=[jax.ShapeDtypeStruct(q.shape, F32), jax.ShapeDtypeStruct(state.shape, F32)],
        scratch_shapes=[pltpu.VMEM((RET_HEADS, 128, 128), F32)],
        compiler_params=_params(("arbitrary", "arbitrary")),
        name="retention",
    )(q, k, v, rg, state, dmask, qdec, kdec, sdec, gn_g, gn_b)


def _ret_short_kernel(q_ref, k_ref, v_ref, rg_ref, st_ref, dm_ref, qd_ref, kd_ref, sd_ref, gg_ref, gb_ref,
                      y_ref, so_ref, k_pad, kd_pad, v_pad):
    t = q_ref.shape[0]

    @pl.when(pl.program_id(0) == 0)
    def _():
        for pad in (k_pad, kd_pad, v_pad):
            pad[...] = jnp.zeros(pad.shape, F32)

    k_pad[0:t, :] = k_ref[...]
    kd_pad[0:t, :] = k_ref[...] * kd_ref[0:t, :]
    v_pad[0:t, :] = v_ref[...]
    for h in range(RET_HEADS):
        sl = slice(h * LANES, (h + 1) * LANES)
        q = q_ref[:, sl]
        v = v_pad[:, sl].astype(BF16)
        state = st_ref[0, h]
        s = _dot_nt(q.astype(BF16), k_pad[:, sl].astype(BF16)) * dm_ref[h, 0:t, :]
        intra = _dot(s.astype(BF16), v)
        cross = _dot((q * qd_ref[0:t, sl]).astype(BF16), state.astype(BF16))
        o = intra + cross
        so_ref[0, h] = sd_ref[h] * state + _dot(kd_pad[:, sl].T.astype(BF16), v)
        mu = jnp.mean(o, axis=-1, keepdims=True)
        var = jnp.mean(jnp.square(o - mu), axis=-1, keepdims=True)
        on = (o - mu) * lax.rsqrt(var + GN_EPS)
        gate = rg_ref[:, sl]
        y_ref[:, sl] = (on * gg_ref[:, sl] + gb_ref[:, sl]) * (gate * _sigmoid(gate))


def _gelu_tanh(x):
    return 0.5 * x * (1.0 + jnp.tanh(np.sqrt(2.0 / np.pi).astype(np.float32) * (x + 0.044715 * (x * x * x))))


def _compress_one(tiles, pe_ref, w1_ref, w2_ref, scr, transpose_out=False):
    per_tile = LANES // CMP_BLOCK
    for t, tile in enumerate(tiles):
        rows_pm = tile.T
        for c in range(per_tile):
            r0 = (t * per_tile + c) * BLOCK_PITCH
            scr[r0:r0 + CMP_BLOCK, :] = rows_pm[c * CMP_BLOCK:(c + 1) * CMP_BLOCK]
    n_blk = len(tiles) * per_tile
    flat = [(scr[pl.ds(j, n_blk, stride=BLOCK_PITCH), :] + pe_ref[j:j + 1, :]).astype(BF16) for j in range(CMP_BLOCK)]
    hid = _gelu_tanh(_dot(jnp.concatenate(flat, axis=1), w1_ref[...]))
    if transpose_out:
        return _dot_nt(w2_ref[...], hid.astype(BF16))
    return _dot(hid.astype(BF16), w2_ref[...])


def _compress_prompt_kernel(k_ref, v_ref, pek_ref, w1k_ref, w2k_ref, pev_ref, w1v_ref, w2v_ref,
                            kc_ref, vct_ref, scr_k, scr_v, *, n_tiles):
    for src, pe, w1, w2, dst, scr, tr in ((k_ref, pek_ref, w1k_ref, w2k_ref, kc_ref, scr_k, False),
                                          (v_ref, pev_ref, w1v_ref, w2v_ref, vct_ref, scr_v, True)):
        tiles = [src[0, :, t * LANES:(t + 1) * LANES] for t in range(n_tiles)]
        dst[0] = _compress_one(tiles, pe, w1, w2, scr, tr)


def _compress_sample_kernel(pt_ref, ck_hbm, cv_hbm, pek_ref, w1k_ref, w2k_ref, pev_ref, w1v_ref, w2v_ref,
                            kc_ref, vc_ref, kbuf, vbuf, sems, scr_k, scr_v, *, n_pages, n_steps, n_batch):
    n_blk = n_pages * PAGE_SIZE // CMP_BLOCK

    def on_pages(step, slot):
        rows = pl.ds(pl.multiple_of(step * n_blk, n_blk), n_blk)
        kc_ref[0, rows, :] = _compress_one([kbuf[slot, j] for j in range(n_pages)], pek_ref, w1k_ref, w2k_ref, scr_k)
        vc_ref[0, rows, :] = _compress_one([vbuf[slot, j] for j in range(n_pages)], pev_ref, w1v_ref, w2v_ref, scr_v)

    _stream_pages(pt_ref, (ck_hbm, cv_hbm), (kbuf, vbuf), sems, n_pages, n_steps, n_batch, on_pages)


def _compress_weights(pe, w1, w2, transpose_w2=False):
    pe_rows = pe.reshape(CMP_BLOCK, KV_WIDTH)
    z = jnp.zeros((CMP_BLOCK, NSA_HD, NSA_HD), F32)
    w1bd = jnp.concatenate([jnp.concatenate([w1[0], z], axis=2), jnp.concatenate([z, w1[1]], axis=2)], axis=1)
    z2 = jnp.zeros((NSA_HD, NSA_HD), F32)
    w2bd = jnp.concatenate([jnp.concatenate([w2[0], z2], axis=1), jnp.concatenate([z2, w2[1]], axis=1)], axis=0)
    if transpose_w2:
        w2bd = w2bd.T
    return pe_rows, w1bd.reshape(CMP_BLOCK * KV_WIDTH, KV_WIDTH).astype(BF16), w2bd.astype(BF16)


def _cw_specs():
    zero = lambda *a: (0, 0)
    return [pl.BlockSpec((CMP_BLOCK, KV_WIDTH), zero), pl.BlockSpec((CMP_BLOCK * KV_WIDTH, KV_WIDTH), zero),
            pl.BlockSpec((KV_WIDTH, KV_WIDTH), zero)]


def _compress_prompt(ck_t, cv_t, cwk, cwv_t):
    b, _, s = ck_t.shape
    n_blk = s // CMP_BLOCK
    src = pl.BlockSpec((1, KV_WIDTH, s), lambda i: (i, 0, 0))
    return pl.pallas_call(
        functools.partial(_compress_prompt_kernel, n_tiles=s // LANES),
        grid=(b,),
        in_specs=[src, src] + _cw_specs() + _cw_specs(),
        out_specs=[pl.BlockSpec((1, n_blk, KV_WIDTH), lambda i: (i, 0, 0)),
                   pl.BlockSpec((1, KV_WIDTH, n_blk), lambda i: (i, 0, 0))],
        out_shape=[jax.ShapeDtypeStruct((b, n_blk, KV_WIDTH), F32), jax.ShapeDtypeStruct((b, KV_WIDTH, n_blk), F32)],
        scratch_shapes=[pltpu.VMEM((n_blk * BLOCK_PITCH, KV_WIDTH), F32)] * 2,
        compiler_params=_params(("arbitrary",)),
        name="compress_prompt",
    )(ck_t, cv_t, *cwk, *cwv_t)


def _compress_sample(page_table, cache_k_t, cache_v_t, cwk, cwv):
    db, n_pages = page_table.shape
    g = COMPRESS_PAGES_PER_STEP
    steps = n_pages // g
    blk_per_step = g * PAGE_SIZE // CMP_BLOCK
    n_blk = n_pages * PAGE_SIZE // CMP_BLOCK

    hbm = pl.BlockSpec(memory_space=pl.ANY)
    dst = pl.BlockSpec((1, n_blk, KV_WIDTH), lambda b, pt: (b, 0, 0))
    shp = jax.ShapeDtypeStruct((db, n_blk, KV_WIDTH), F32)
    page_buf = pltpu.VMEM((PAGE_SLOTS, g, KV_WIDTH, PAGE_SIZE), F32)
    grid_spec = pltpu.PrefetchScalarGridSpec(
        num_scalar_prefetch=1,
        grid=(db,),
        in_specs=[hbm, hbm] + _cw_specs() + _cw_specs(),
        out_specs=[dst, dst],
        scratch_shapes=[page_buf, page_buf, pltpu.SemaphoreType.DMA((PAGE_SLOTS, 2))]
        + [pltpu.VMEM((blk_per_step * BLOCK_PITCH, KV_WIDTH), F32)] * 2,
    )
    return pl.pallas_call(
        functools.partial(_compress_sample_kernel, n_pages=g, n_steps=steps, n_batch=db),
        grid_spec=grid_spec,
        out_shape=[shp, shp],
        compiler_params=_params(("arbitrary",)),
        name="compress_sample",
    )(page_table, cache_k_t, cache_v_t, *cwk, *cwv)


def _stack_heads_t(qt_ref):
    nq = qt_ref.shape[2]
    zero = jnp.zeros((NSA_HD, nq), BF16)
    cols = []
    for k in range(KV_HEADS):
        for g in range(NSA_GROUP):
            h = k * NSA_GROUP + g
            tile = qt_ref[0, h * NSA_HD:(h + 1) * NSA_HD, :] * ATTN_SCALE
            cols.append(jnp.concatenate([tile, zero] if k == 0 else [zero, tile], axis=0))
    return jnp.concatenate(cols, axis=1)


def _rank_select_rows(score, n_blk):
    parts = [score[v * SUBLANES:(v + 1) * SUBLANES] for v in range(n_blk // SUBLANES)]
    ranks = [jnp.zeros(p.shape, jnp.int32) for p in parts]
    for j in range(n_blk):
        col = score[j:j + 1, :]
        for v, part in enumerate(parts):
            ge, gt = (col >= part).astype(jnp.int32), (col > part).astype(jnp.int32)
            if v * SUBLANES > j:
                beats = ge
            elif (v + 1) * SUBLANES - 1 <= j:
                beats = gt
            else:
                row = v * SUBLANES + lax.broadcasted_iota(jnp.int32, (SUBLANES, 1), 0)
                beats = jnp.where(row > j, ge, gt)
            ranks[v] = ranks[v] + beats
    return jnp.concatenate([(r < TOP_N).astype(F32) for r in ranks], axis=0)


def _softmax_tile_t(s, v_t, m_ref, acc_ref, s_max=None):
    half = s.shape[1] // KV_HEADS
    m_old = m_ref[...]
    m_new = jnp.maximum(m_old, jnp.max(s, axis=0, keepdims=True) if s_max is None else s_max)
    alpha = jnp.exp(m_old - m_new)
    pb = jnp.exp(s - m_new).astype(BF16)
    m_ref[...] = m_new
    ones = jnp.ones((NSA_HD, v_t.shape[1]), BF16)
    v0 = jnp.concatenate([v_t[:NSA_HD], ones], axis=0)
    v1 = jnp.concatenate([ones, v_t[NSA_HD:]], axis=0)
    pv = jnp.concatenate([_dot(v0, pb[:, :half]), _dot(v1, pb[:, half:])], axis=1)
    acc_ref[...] = alpha * acc_ref[...] + pv


def _normalized_head_t(acc, k, cols):
    den = (1 - k) * NSA_HD
    return acc[k * NSA_HD:(k + 1) * NSA_HD, cols] / acc[den:den + 1, cols]


def _attn_prompt_kernel(nqt_ref, nqrt_ref, nglt_ref, ngt_ref, kc_ref, vct_ref, sk_ref, svt_ref, wk_ref, wvt_ref,
                        et_ref, ewin_ref, y_ref, lhs_scr, qslc_scr, qwin_scr, ocmp_scr, t_scr, m_scr, acc_scr,
                        s_scr, smax_scr, sw_scr):
    i = pl.program_id(1)
    qb = y_ref.shape[0]
    q0 = i * qb
    cols8 = 2 * NSA_GROUP * qb
    qpos = q0 + lax.broadcasted_iota(jnp.int32, (1, qb), 1)
    lhs_scr[...] = _stack_heads_t(nqt_ref)
    q_rot = _stack_heads_t(nqrt_ref)
    qslc_scr[0:KV_WIDTH, :] = q_rot
    qwin_scr[0:KV_WIDTH, :] = q_rot
    def causal_cap():
        own_keys = q0 + lax.broadcasted_iota(jnp.int32, (qb, 1), 0)
        return jnp.concatenate([jnp.where(own_keys <= qpos, CAP_OPEN, NEG_INF)] * (2 * NSA_GROUP), axis=1)

    n_cmp = kc_ref.shape[1]
    s_all = _dot(kc_ref[0].astype(BF16), lhs_scr[...])
    cend = (lax.broadcasted_iota(jnp.int32, (n_cmp, 1), 0) + 1) * CMP_BLOCK - 1
    cmask = cend <= qpos
    vct = vct_ref[0].astype(BF16)
    imps = []
    for k in range(KV_HEADS):
        imp = None
        for g in range(NSA_GROUP):
            c0 = (k * NSA_GROUP + g) * qb
            s = jnp.where(cmask, s_all[:, c0:c0 + qb], NEG_INF)
            p = jnp.exp(s - jnp.max(s, axis=0, keepdims=True))
            p = p / jnp.sum(p, axis=0, keepdims=True) * cmask.astype(F32)
            imp = p if imp is None else imp + p
            ocmp_scr[:, c0:c0 + qb] = _dot(vct, p.astype(BF16))
        imps.append(imp)

    n_slc = n_cmp // 2
    blk = lax.broadcasted_iota(jnp.int32, (n_slc, 1), 0)
    valid = blk * SLC_BLOCK <= qpos

    def store_bias(k, sel):
        bias = jnp.concatenate([(sel - 1.0) * (-NEG_INF), jnp.zeros((LANES - n_slc, qb), F32)], axis=0).astype(BF16)
        c0 = k * NSA_GROUP * qb
        qslc_scr[KV_WIDTH:, c0:c0 + NSA_GROUP * qb] = jnp.concatenate([bias] * NSA_GROUP, axis=1)

    @pl.when(q0 + qb <= TOP_N * SLC_BLOCK)
    def _():
        for k in range(KV_HEADS):
            store_bias(k, valid.astype(F32))

    @pl.when(q0 + qb > TOP_N * SLC_BLOCK)
    def _():
        forced = (blk == 0) | (blk == qpos // SLC_BLOCK)
        for k in range(KV_HEADS):
            pairs = []
            for u in range(qb // LANES):
                t_scr[u] = imps[k][:, u * LANES:(u + 1) * LANES]
                pairs.append(t_scr[u, pl.ds(0, n_slc, stride=2), :] + t_scr[u, pl.ds(1, n_slc, stride=2), :])
            pair = jnp.concatenate(pairs, axis=1)
            score = jnp.where(forced, FORCED_SCORE, jnp.where(valid, pair, NEG_INF))
            store_bias(k, _rank_select_rows(score, n_slc))

    tk = SLC_KEY_TILE
    n_tiles = (q0 + qb - 1) // tk + 1
    m_scr[...] = jnp.full(m_scr.shape, M_INIT, F32)
    acc_scr[...] = jnp.zeros(acc_scr.shape, F32)

    def scores(t):
        k0 = pl.multiple_of(t * tk, tk)
        keys = jnp.concatenate([sk_ref[pl.ds(k0, tk), :], et_ref[pl.ds(k0, tk), :]], axis=1)
        s = _dot(keys, qslc_scr[...])
        s_scr[t % 2] = s
        smax_scr[t % 2] = jnp.max(s, axis=0, keepdims=True)

    def update(t):
        k0 = pl.multiple_of(t * tk, tk)
        _softmax_tile_t(s_scr[t % 2], svt_ref[0, :, pl.ds(k0, tk)], m_scr.at[0], acc_scr.at[0], smax_scr[t % 2])

    def slc_step(t, carry):
        scores(t + 1)
        update(t)
        return carry

    scores(0)
    lax.fori_loop(0, n_tiles - 1, slc_step, 0)
    last = n_tiles - 1
    own = pl.ds(pl.multiple_of(q0 - last * tk, qb), qb)
    s_scr[last % 2, own, :] = jnp.minimum(s_scr[last % 2, own, :], causal_cap())
    smax_scr[last % 2] = jnp.max(s_scr[last % 2], axis=0, keepdims=True)
    update(last)

    wlen = WINDOW + qb
    w0 = pl.multiple_of(jnp.maximum(i - WINDOW // qb, 0) * qb, qb)
    later = (w0 + lax.broadcasted_iota(jnp.int32, (LANES, 1), 0) * qb) > q0
    qwin_scr[KV_WIDTH:, :] = jnp.broadcast_to(jnp.where(later, NEG_INF, 0.0), (LANES, cols8)).astype(BF16)
    keys = jnp.concatenate([wk_ref[pl.ds(w0, wlen), :], ewin_ref[...]], axis=1)
    sw_scr[...] = _dot(keys, qwin_scr[...])
    edge_keys = w0 + lax.broadcasted_iota(jnp.int32, (qb, 1), 0)
    edge_cap = jnp.concatenate([jnp.where(qpos - edge_keys <= WINDOW, CAP_OPEN, NEG_INF)] * (2 * NSA_GROUP), axis=1)
    sw_scr[0:qb, :] = jnp.minimum(sw_scr[0:qb, :], edge_cap)
    own = pl.ds(pl.multiple_of(q0 - w0, qb), qb)
    sw_scr[own, :] = jnp.minimum(sw_scr[own, :], causal_cap())
    _softmax_tile_t(sw_scr[...], wvt_ref[0, :, pl.ds(w0, wlen)], m_scr.at[1], acc_scr.at[1])

    acc_slc, acc_win = acc_scr[0], acc_scr[1]
    sig = _sigmoid(nglt_ref[0])
    for pair in range(NSA_HEADS // 2):
        tiles = []
        for h in (2 * pair, 2 * pair + 1):
            k = h // NSA_GROUP
            rows = slice(k * NSA_HD, (k + 1) * NSA_HD)
            cols = slice(h * qb, (h + 1) * qb)
            gate = lambda r: sig[r * NSA_HEADS + h:r * NSA_HEADS + h + 1, :]
            o = (gate(0) * ocmp_scr[rows, cols] + gate(1) * _normalized_head_t(acc_slc, k, cols)
                 + gate(2) * _normalized_head_t(acc_win, k, cols))
            ng = ngt_ref[0, h * NSA_HD:(h + 1) * NSA_HD, :]
            tiles.append(o * (ng * _sigmoid(ng)))
        y_ref[:, pair * LANES:(pair + 1) * LANES] = jnp.concatenate(tiles, axis=0).T.astype(y_ref.dtype)


def _attn_prompt(nq_t, nqr_t, ngl_t, ng_t, kc, vc_t, sk_rm, sv_t, wk_rm, wv_t):
    b, _, s = sv_t.shape
    assert s // CMP_BLOCK == LANES, "one lane per compressed block"
    qb = PROMPT_Q_TILE
    nqb = s // qb
    cols8 = 2 * NSA_GROUP * qb
    expand_t = (np.arange(s)[:, None] // SLC_BLOCK == np.arange(LANES)[None, :]).astype(BF16)
    wlen = WINDOW + qb
    win_blocks = (np.arange(wlen)[:, None] // qb == np.arange(LANES)[None, :]).astype(BF16)
    qcol = lambda r: pl.BlockSpec((1, r, qb), lambda bb, i: (bb, 0, i))
    per_b = lambda shape: pl.BlockSpec((1,) + shape, lambda bb, i: (bb, 0, 0))
    rows_b = pl.BlockSpec((s, KV_WIDTH), lambda bb, i: (bb, 0))
    return pl.pallas_call(
        _attn_prompt_kernel,
        grid=(b, nqb),
        in_specs=[qcol(512), qcol(512), qcol(LANES), qcol(512), per_b(kc.shape[1:]), per_b(vc_t.shape[1:]),
                  rows_b, per_b((KV_WIDTH, s)), rows_b, per_b((KV_WIDTH, s)),
                  pl.BlockSpec(expand_t.shape, lambda bb, i: (0, 0)),
                  pl.BlockSpec(win_blocks.shape, lambda bb, i: (0, 0))],
        out_specs=pl.BlockSpec((qb, 512), lambda bb, i: (bb * nqb + i, 0)),
        out_shape=jax.ShapeDtypeStruct((b * s, 512), BF16),
        scratch_shapes=[pltpu.VMEM((KV_WIDTH, cols8), BF16), pltpu.VMEM((2 * KV_WIDTH, cols8), BF16),
                        pltpu.VMEM((2 * KV_WIDTH, cols8), BF16),
                        pltpu.VMEM((KV_WIDTH, cols8), F32), pltpu.VMEM((qb // LANES, LANES, LANES), F32),
                        pltpu.VMEM((2, 1, cols8), F32), pltpu.VMEM((2, KV_WIDTH, cols8), F32),
                        pltpu.VMEM((2, SLC_KEY_TILE, cols8), F32), pltpu.VMEM((2, 1, cols8), F32),
                        pltpu.VMEM((wlen, cols8), F32)],
        compiler_params=_params(("arbitrary", "arbitrary")),
        name="attn_prompt",
    )(nq_t, nqr_t, ngl_t, ng_t, kc, vc_t, sk_rm, sv_t, wk_rm, wv_t, expand_t, win_blocks)


def _stack_heads(q_ref):
    lane = lax.broadcasted_iota(jnp.int32, (1, LANES), 1)
    lo = lane < NSA_HD
    slabs = [q_ref[:, g * LANES:(g + 1) * LANES] for g in range(NSA_GROUP)]
    zero = jnp.zeros_like(slabs[0])
    stacked = jnp.concatenate([jnp.where(lo, s, zero) for s in slabs] + [jnp.where(lo, zero, s) for s in slabs], axis=0)
    return stacked.astype(BF16)


def _mask_rows(sc, masks, rows):
    n = sc.shape[-1]
    s4 = sc.reshape(2 * NSA_GROUP, rows, n)
    out = [jnp.where(masks[k][None], s4[k * NSA_GROUP:(k + 1) * NSA_GROUP], NEG_INF) for k in range(KV_HEADS)]
    return jnp.concatenate(out, axis=0).reshape(2 * NSA_GROUP * rows, n)


def _pair_scores(imp, qpos):
    n_slc = imp.shape[1] // 2
    blk = lax.broadcasted_iota(jnp.int32, (1, n_slc), 1)
    valid = blk * SLC_BLOCK <= qpos
    forced = (blk == 0) | (blk == qpos // SLC_BLOCK)
    return jnp.where(forced, FORCED_SCORE, jnp.where(valid, imp[:, :n_slc] + imp[:, n_slc:], NEG_INF))


def _rank_select(score, extra_forced):
    n_blk = score.shape[1]
    blk = lax.broadcasted_iota(jnp.int32, (1, n_blk), 1)
    rank = jnp.zeros(score.shape, jnp.int32)
    for j in range(n_blk):
        col = score[:, j:j + 1]
        beats = (col > score) | ((col == score) & (j < blk))
        rank = rank + beats.astype(jnp.int32)
    if extra_forced:
        rank = rank + (score < FORCED_SCORE).astype(jnp.int32)
    return rank < TOP_N


def _merge_kv_heads(acc, rows):
    lane = lax.broadcasted_iota(jnp.int32, (1, LANES), 1)
    lo = lane < NSA_HD
    half = NSA_GROUP * rows
    return [jnp.where(lo, acc[g * rows:(g + 1) * rows], acc[half + g * rows:half + (g + 1) * rows])
            for g in range(NSA_GROUP)]


def _gate_and_store(o_cmp, o_slc, o_win, ngl_ref, ng_ref, y_ref):
    lane = lax.broadcasted_iota(jnp.int32, (1, LANES), 1)
    lo = lane < NSA_HD
    sig = _sigmoid(ngl_ref[...])
    for g in range(NSA_GROUP):
        gates = [jnp.where(lo, sig[:, r * 8 + g:r * 8 + g + 1], sig[:, r * 8 + 4 + g:r * 8 + 4 + g + 1]) for r in range(3)]
        o = gates[0] * o_cmp[g] + gates[1] * o_slc[g] + gates[2] * o_win[g]
        gate = ng_ref[:, g * LANES:(g + 1) * LANES]
        y_ref[:, g * LANES:(g + 1) * LANES] = (o * (gate * _sigmoid(gate))).astype(y_ref.dtype)


def _page_copies(pt_ref, b, chunk, slot, caches, bufs, sems, g):
    out = []
    for ci, (cache, buf) in enumerate(zip(caches, bufs)):
        for j in range(g):
            page = pt_ref[b, chunk * g + j]
            out.append(pltpu.make_async_copy(cache.at[page], buf.at[slot, j], sems.at[slot, ci]))
    return out


def _stream_pages(pt_ref, caches, bufs, sems, g, n_chunks, n_batch, compute):
    b = pl.program_id(0)
    total = n_batch * n_chunks
    n_slots = bufs[0].shape[0]
    depth = n_slots - 1
    assert n_chunks >= depth
    copies = functools.partial(_page_copies, pt_ref, caches=caches, bufs=bufs, sems=sems, g=g)

    @pl.when(b == 0)
    def _():
        for d in range(depth):
            for cp in copies(0, d, d):
                cp.start()

    def chunk(c, carry):
        flat = b * n_chunks + c
        ahead = jnp.minimum(flat + depth, total - 1)
        for cp in copies(ahead // n_chunks, ahead % n_chunks, (flat + depth) % n_slots):
            cp.start()
        for cp in copies(b, c, flat % n_slots):
            cp.wait()
        compute(c, flat % n_slots)
        return carry

    lax.fori_loop(0, n_chunks, chunk, 0)

    @pl.when(b == n_batch - 1)
    def _():
        for d in range(depth):
            for cp in copies(n_batch - 1, n_chunks - 1, (total + d) % n_slots):
                cp.wait()


def _attn_sample_kernel(pt_ref, sk_hbm, sv_hbm, nq_ref, nqr_ref, ngl_ref, ng_ref, kc_ref, vc_ref, skn_ref, svn_ref,
                        wko_ref, wvo_ref, wkn_ref, wvn_ref, e_ref, y_ref, kbuf, vbuf, sems, m_scr, l_scr, acc_scr,
                        ocmp_scr, owin_scr, mask_scr, *, n_pages, n_steps, n_batch, t_len, past_len):
    t = t_len
    nq_ref, nqr_ref, ngl_ref, ng_ref, y_ref = (r.at[0] for r in (nq_ref, nqr_ref, ngl_ref, ng_ref, y_ref))
    rows8 = 2 * NSA_GROUP * t
    half = NSA_GROUP * t
    tpos = lax.broadcasted_iota(jnp.int32, (t, 1), 0)
    qpos = past_len + tpos
    n_cmp = kc_ref.shape[1]
    wb = wko_ref.shape[2]
    step_keys = n_pages * PAGE_SIZE

    def before_pages():
        n_slc = n_cmp // 2
        halves = lambda ref: jnp.concatenate([ref[0, pl.ds(0, n_slc, stride=2), :],
                                              ref[0, pl.ds(1, n_slc, stride=2), :]], axis=0).astype(BF16)
        lhs = _stack_heads(nq_ref)
        s = _dot_nt(lhs, halves(kc_ref)) * ATTN_SCALE
        m = jnp.max(s, axis=-1, keepdims=True)
        p = jnp.exp(s - m)
        p = p / jnp.sum(p, axis=-1, keepdims=True)
        oc = _dot(p.astype(BF16), halves(vc_ref))
        for g, slab in enumerate(_merge_kv_heads(oc, t)):
            ocmp_scr[g] = slab
        p4 = p.reshape(8, t, n_cmp)
        sels = []
        for k in range(KV_HEADS):
            imp = p4[k * NSA_GROUP]
            for g in range(1, NSA_GROUP):
                imp = imp + p4[k * NSA_GROUP + g]
            sels.append(_rank_select(_pair_scores(imp, qpos), True).astype(F32))
        sel = jnp.concatenate(sels, axis=0).astype(BF16)
        span = LANES * SLC_BLOCK
        for c in range(n_slc // LANES):
            km = _dot(sel[:, c * LANES:(c + 1) * LANES], e_ref[...])
            for k in range(KV_HEADS):
                kmk = km[k * t:(k + 1) * t]
                if step_keys >= span:
                    off = (c * span) % step_keys
                    mask_scr[(c * span) // step_keys, k, :, off:off + span] = kmk
                else:
                    per = span // step_keys
                    for u in range(per):
                        mask_scr[c * per + u, k] = kmk[:, u * step_keys:(u + 1) * step_keys]

        lhs_r = _stack_heads(nqr_ref)
        so = _dot(lhs_r, wko_ref[0].astype(BF16)) * ATTN_SCALE
        sn = _dot(lhs_r, wkn_ref[0].astype(BF16)) * ATTN_SCALE
        jo = lax.broadcasted_iota(jnp.int32, (1, wb), 1)
        jn = lax.broadcasted_iota(jnp.int32, (1, LANES), 1)
        old_vis = jo >= tpos
        so = _mask_rows(so, [old_vis, old_vis], t)
        new_vis = jn <= tpos
        sn_w = _mask_rows(sn, [new_vis, new_vis], t)
        m = jnp.maximum(jnp.max(so, axis=-1, keepdims=True), jnp.max(sn_w, axis=-1, keepdims=True))
        po = jnp.exp(so - m)
        pn = jnp.exp(sn_w - m)
        den = jnp.sum(po, axis=-1, keepdims=True) + jnp.sum(pn, axis=-1, keepdims=True)
        pob, pnb = (po / den).astype(BF16), (pn / den).astype(BF16)
        wvo, wvn = wvo_ref[0].astype(BF16), wvn_ref[0].astype(BF16)
        ow = jnp.concatenate([_dot_nt(pob[:half], wvo) + _dot_nt(pnb[:half], wvn),
                              _dot_nt(pob[half:], wvo) + _dot_nt(pnb[half:], wvn)], axis=0)
        for g, slab in enumerate(_merge_kv_heads(ow, t)):
            owin_scr[g] = slab

        s2 = _dot(lhs_r, skn_ref[0].astype(BF16)) * ATTN_SCALE
        s2 = _mask_rows(s2, [new_vis, new_vis], t)
        m2 = jnp.max(s2, axis=-1, keepdims=True)
        p2 = jnp.exp(s2 - m2)
        m_scr[0] = m2
        l_scr[0] = jnp.sum(p2, axis=-1, keepdims=True)
        p2b = p2.astype(BF16)
        svn = svn_ref[0].astype(BF16)
        acc_scr[0] = jnp.concatenate([_dot_nt(p2b[:half], svn), _dot_nt(p2b[half:], svn)], axis=0)
        for st in range(1, SAMPLE_STREAMS):
            m_scr[st] = jnp.full((rows8, 1), M_INIT, F32)
            l_scr[st] = jnp.zeros((rows8, 1), F32)
            acc_scr[st] = jnp.zeros((rows8, LANES), F32)

    def on_pages(step, slot):
        lhs_r = _stack_heads(nqr_ref)
        per = n_pages // SAMPLE_STREAMS
        msk = mask_scr[step]
        for st in range(SAMPLE_STREAMS):
            pages = range(st * per, (st + 1) * per)
            keys = slice(st * per * PAGE_SIZE, (st + 1) * per * PAGE_SIZE)
            kt = jnp.concatenate([kbuf[slot, j].astype(BF16) for j in pages], axis=1)
            vt = jnp.concatenate([vbuf[slot, j].astype(BF16) for j in pages], axis=1)
            sc = _dot(lhs_r, kt) * ATTN_SCALE
            sc = _mask_rows(sc, [msk[k][:, keys] > 0.5 for k in range(KV_HEADS)], t)
            m_i = m_scr[st]
            m_new = jnp.maximum(m_i, jnp.max(sc, axis=-1, keepdims=True))
            alpha = jnp.exp(m_i - m_new)
            pr = jnp.exp(sc - m_new)
            l_scr[st] = alpha * l_scr[st] + jnp.sum(pr, axis=-1, keepdims=True)
            m_scr[st] = m_new
            pb = pr.astype(BF16)
            pv = jnp.concatenate([_dot_nt(pb[:half], vt), _dot_nt(pb[half:], vt)], axis=0)
            acc_scr[st] = alpha * acc_scr[st] + pv

    before_pages()
    _stream_pages(pt_ref, (sk_hbm, sv_hbm), (kbuf, vbuf), sems, n_pages, n_steps, n_batch, on_pages)
    m_all = m_scr[0]
    for st in range(1, SAMPLE_STREAMS):
        m_all = jnp.maximum(m_all, m_scr[st])
    l_all = jnp.zeros((rows8, 1), F32)
    acc_all = jnp.zeros((rows8, LANES), F32)
    for st in range(SAMPLE_STREAMS):
        w = jnp.exp(m_scr[st] - m_all)
        l_all = l_all + w * l_scr[st]
        acc_all = acc_all + w * acc_scr[st]
    o_slc = _merge_kv_heads(acc_all / l_all, t)
    o_cmp = [ocmp_scr[g] for g in range(NSA_GROUP)]
    o_win = [owin_scr[g] for g in range(NSA_GROUP)]
    _gate_and_store(o_cmp, o_slc, o_win, ngl_ref, ng_ref, y_ref)


def _attn_sample(page_table, cache_sk_t, cache_sv_t, nq, nqr, ngl, ng, kc, vc, skn, svn, wko, wvo, wkn, wvn, t_len):
    db, n_pages = page_table.shape
    g = SAMPLE_PAGES_PER_STEP
    steps = n_pages // g
    past_len = n_pages * PAGE_SIZE
    step_keys = g * PAGE_SIZE
    span = LANES * SLC_BLOCK
    expand = (np.arange(span)[None, :] // SLC_BLOCK == np.arange(LANES)[:, None]).astype(BF16)

    hbm = pl.BlockSpec(memory_space=pl.ANY)
    row = lambda w: pl.BlockSpec((1, t_len, w), lambda b, pt: (b, 0, 0))
    per_b = lambda shape: pl.BlockSpec((1,) + shape, lambda b, pt: (b, 0, 0))
    rows8 = 2 * NSA_GROUP * t_len
    page_buf = pltpu.VMEM((PAGE_SLOTS, g, KV_WIDTH, PAGE_SIZE), F32)
    grid_spec = pltpu.PrefetchScalarGridSpec(
        num_scalar_prefetch=1,
        grid=(db,),
        in_specs=[hbm, hbm, row(512), row(512), row(LANES), row(512), per_b(kc.shape[1:]), per_b(vc.shape[1:]),
                  per_b(skn.shape[1:]), per_b(svn.shape[1:]), per_b(wko.shape[1:]), per_b(wvo.shape[1:]),
                  per_b(wkn.shape[1:]), per_b(wvn.shape[1:]), pl.BlockSpec(expand.shape, lambda b, pt: (0, 0))],
        out_specs=row(512),
        scratch_shapes=[page_buf, page_buf, pltpu.SemaphoreType.DMA((PAGE_SLOTS, 2)),
                        pltpu.VMEM((SAMPLE_STREAMS, rows8, 1), F32), pltpu.VMEM((SAMPLE_STREAMS, rows8, 1), F32),
                        pltpu.VMEM((SAMPLE_STREAMS, rows8, LANES), F32),
                        pltpu.VMEM((NSA_GROUP, t_len, LANES), F32), pltpu.VMEM((NSA_GROUP, t_len, LANES), F32),
                        pltpu.VMEM((steps, KV_HEADS, t_len, step_keys), F32)],
    )
    return pl.pallas_call(
        functools.partial(_attn_sample_kernel, n_pages=g, n_steps=steps, n_batch=db, t_len=t_len, past_len=past_len),
        grid_spec=grid_spec,
        out_shape=jax.ShapeDtypeStruct((db, t_len, 512), F32),
        compiler_params=_params(("arbitrary",)),
        name="attn_sample",
    )(page_table, cache_sk_t, cache_sv_t, nq, nqr, ngl, ng, kc, vc, skn, svn, wko, wvo, wkn, wvn, expand)


def _finish_kernel(x_ref, yr_ref, yn_ref, p_ref, wo_ref, gple_ref, wg_ref, wp_ref, gf_ref, o_ref):
    x = x_ref[...]
    x = x + _dot(yr_ref[...].astype(BF16), wo_ref[0:RET_WIDTH, :]) + _dot(yn_ref[...].astype(BF16), wo_ref[RET_WIDTH:, :])
    ms = jnp.mean(x * x, axis=-1, keepdims=True)
    hn = (x * lax.rsqrt(ms + RMS_EPS) * gple_ref[...]).astype(BF16)
    gate = _sigmoid(_dot(hn, wg_ref[...]))
    x = x + gate * _dot(p_ref[...].astype(BF16), wp_ref[...])
    ms = jnp.mean(x * x, axis=-1, keepdims=True)
    o_ref[...] = x * lax.rsqrt(ms + RMS_EPS) * gf_ref[...]


def _finish(x2d, y_ret, y_nsa, p2d, w_out, norm_ple, w_gate, w_ple, norm_f, tm):
    n = x2d.shape[0]
    row = lambda w: pl.BlockSpec((tm, w), lambda i: (i, 0))
    const = lambda shape: pl.BlockSpec(shape, lambda i: (0, 0))
    return pl.pallas_call(
        _finish_kernel,
        grid=(n // tm,),
        in_specs=[row(D_MODEL), row(RET_WIDTH), row(NSA_WIDTH), row(PLE_DIM), const(w_out.shape), const((1, D_MODEL)),
                  const(w_gate.shape), const(w_ple.shape), const((1, D_MODEL))],
        out_specs=row(D_MODEL),
        out_shape=jax.ShapeDtypeStruct((n, D_MODEL), F32),
        compiler_params=_params(("arbitrary",)),
        name="finish",
    )(x2d, y_ret, y_nsa, p2d, w_out, norm_ple, w_gate, w_ple, norm_f)


def _slab_perm():
    return np.array([(k * NSA_GROUP + g) * NSA_HD + d for g in range(NSA_GROUP) for k in range(KV_HEADS)
                     for d in range(NSA_HD)], np.int32)


def _to_positions_major(x_t):
    lead = x_t.shape[:-2]
    n = len(lead)
    x4 = x_t.reshape(lead + (KV_HEADS, NSA_HD, x_t.shape[-1]))
    return jnp.transpose(x4, tuple(range(n)) + (n + 2, n, n + 1))


def _cache_t(cache):
    n_pool, page = cache.shape[:2]
    return jnp.transpose(cache, (0, 2, 3, 1)).reshape(n_pool, KV_WIDTH, page)


def _layer(xp, xs, c_ck, c_cv, c_sk, c_sv, win_k, win_v, ret_state, page_table, p_p, p_s, norm_mix, w_in, gn_g, gn_b,
           pe_k, w1_k, w2_k, pe_v, w1_v, w2_v, w_out, norm_ple, w_gate, w_ple, norm_f):
    b, s, d = xp.shape
    db, t, _ = xs.shape
    n_pages = page_table.shape[1]
    past = n_pages * PAGE_SIZE

    off = np.cumsum((0,) + SPLIT_SIZES)
    w_in_t = w_in.T
    col = lambda i: w_in[:, off[i]:off[i + 1]]
    rows = lambda i: w_in_t[off[i]:off[i + 1]]
    perm = _slab_perm()
    ngl_rows = jnp.pad(rows(11), ((0, LANES - N_GATES), (0, 0)))
    ret_rows = [rows(0), rows(1), rows(2), rows(3)]
    kv_rows = [rows(5), rows(6), rows(8), rows(10), rows(7), rows(9)]
    w_rm_p = jnp.concatenate([col(0), col(1), col(2), col(3), col(7), col(9)], axis=1).astype(BF16)
    w_t_p = jnp.concatenate(kv_rows + [rows(4), rows(12), ngl_rows], axis=0).astype(BF16)
    w_rm_s = jnp.concatenate(ret_rows + [rows(4)[perm], rows(12)[perm], ngl_rows], axis=0).astype(BF16)
    w_t_s = jnp.concatenate(kv_rows, axis=0).astype(BF16)
    w_out_b = w_out.astype(BF16)
    w_out_slab = jnp.concatenate([w_out[:RET_WIDTH], w_out[RET_WIDTH:][perm]], axis=0).astype(BF16)
    g_mix = norm_mix.reshape(1, d)
    cwk = _compress_weights(pe_k, w1_k, w2_k)
    cwv = _compress_weights(pe_v, w1_v, w2_v)
    cwv_t = _compress_weights(pe_v, w1_v, w2_v, transpose_w2=True)
    gn_g2, gn_b2 = gn_g.reshape(1, RET_WIDTH), gn_b.reshape(1, RET_WIDTH)
    fin_w = (norm_ple.reshape(1, d), w_gate.astype(BF16), w_ple.astype(BF16), norm_f.reshape(1, d))

    tm = 512
    (q_ret, k_ret, v_ret, rg, sk_rm, wk_rm, ck_t, cv_t, sv_t, wv_t, sk_t, wk_t, sv_b, wv_b, nq_t, nqr_t, ng_t,
     ngl_t) = _proj(xp.reshape(b * s, d), np.arange(s), g_mix, w_rm_p, w_t_p, b, tm, True)
    y_ret, ret_p = _retention(q_ret, k_ret, v_ret, rg, jnp.zeros((b, RET_HEADS, RET_DK, RET_DK), F32), gn_g2, gn_b2,
                              b, s // RET_CHUNK, RET_CHUNK)
    kc, vc_t = _compress_prompt(ck_t, cv_t, cwk, cwv_t)
    y_nsa = _attn_prompt(nq_t, nqr_t, ngl_t, ng_t, kc, vc_t, sk_rm, sv_b, wk_rm, wv_b)
    y_prompt = _finish(xp.reshape(b * s, d), y_ret, y_nsa, p_p.reshape(b * s, PLE_DIM), w_out_b, *fin_w, tm)
    wb_p = min(WINDOW, s)
    prompt_states = (ret_p, _to_positions_major(ck_t), _to_positions_major(cv_t), _to_positions_major(sk_t),
                     _to_positions_major(sv_t), _to_positions_major(wk_t[:, :, s - wb_p:]),
                     _to_positions_major(wv_t[:, :, s - wb_p:]))

    n = db * t
    pos_s = np.tile(past + np.arange(t), db)
    (q_ret, k_ret, v_ret, rg, nq, nqr, ng, ngl, ck_n, cv_n, sv_n, wv_n, sk_n, wk_n) = _proj(
        xs.reshape(n, d), pos_s, g_mix, w_rm_s, w_t_s, 1, n, False)
    y_ret, ret_s = _retention(q_ret, k_ret, v_ret, rg, ret_state, gn_g2, gn_b2, db, 1, t)
    kc, vc = _compress_sample(page_table, _cache_t(c_ck), _cache_t(c_cv), cwk, cwv)
    per_batch = lambda a: jnp.transpose(a[0].reshape(KV_WIDTH, db, t), (1, 0, 2))
    lane_pad = lambda a: jnp.pad(per_batch(a), ((0, 0), (0, 0), (0, LANES - t)))
    win_t = lambda w: jnp.transpose(w, (0, 2, 3, 1)).reshape(db, KV_WIDTH, w.shape[1])
    rows3 = lambda a: a.astype(F32).reshape(db, t, a.shape[-1])
    y_nsa = _attn_sample(page_table, _cache_t(c_sk), _cache_t(c_sv), rows3(nq), rows3(nqr), rows3(ngl), rows3(ng), kc, vc,
                         lane_pad(sk_n), lane_pad(sv_n), win_t(win_k), win_t(win_v), lane_pad(wk_n), lane_pad(wv_n), t)
    y_nsa = y_nsa.reshape(n, NSA_WIDTH)
    y_sample = _finish(xs.reshape(n, d), y_ret, y_nsa, p_s.reshape(n, PLE_DIM), w_out_slab, *fin_w, n)
    new_rows = lambda a: _to_positions_major(per_batch(a))
    sample_states = (ret_s, new_rows(ck_n), new_rows(cv_n), new_rows(sk_n), new_rows(sv_n),
                     jnp.concatenate([win_k[:, t:], new_rows(wk_n)], axis=1),
                     jnp.concatenate([win_v[:, t:], new_rows(wv_n)], axis=1))
    return y_prompt.reshape(b, s, d), y_sample.reshape(db, t, d), prompt_states, sample_states


def kernel(x_prompt, x_sample, cache_cmp_k, cache_cmp_v, cache_slc_k, cache_slc_v, state_win_k, state_win_v, state_ret, page_table, p_prompt, p_sample, norm_mix, w_in, ret_gn_g, ret_gn_b, cmp_pe_k, cmp_w1_k, cmp_w2_k, cmp_pe_v, cmp_w1_v, cmp_w2_v, w_out, norm_ple, w_ple_gate, w_ple, norm_f):
    depth = w_in.shape[0]
    assert depth == 1, "single trunk layer"
    l = 0
    yp, ys, sp, ss = _layer(x_prompt, x_sample, cache_cmp_k[l], cache_cmp_v[l], cache_slc_k[l], cache_slc_v[l],
                            state_win_k[l], state_win_v[l], state_ret[l], page_table, p_prompt[l], p_sample[l],
                            norm_mix[l], w_in[l], ret_gn_g[l], ret_gn_b[l], cmp_pe_k[l], cmp_w1_k[l], cmp_w2_k[l],
                            cmp_pe_v[l], cmp_w1_v[l], cmp_w2_v[l], w_out[l], norm_ple[l], w_ple_gate[l], w_ple[l], norm_f)
    return (yp, ys) + tuple(a[None] for a in sp) + tuple(a[None] for a in ss)
```

```python
import functools

import numpy as np
import jax
import jax.numpy as jnp
from jax import lax
from jax.experimental import pallas as pl
from jax.experimental.pallas import tpu as pltpu

F32 = jnp.float32
BF16 = jnp.bfloat16

D_MODEL = 1024
PLE_DIM = 256
PAGE_SIZE = 128
RET_HEADS = 4
RET_DK = 128
RET_WIDTH = 512
RET_CHUNK = 128
NSA_HEADS = 8
KV_HEADS = 2
NSA_GROUP = 4
NSA_HD = 64
NSA_WIDTH = 512
KV_WIDTH = 128
CMP_BLOCK = 32
SLC_BLOCK = 64
TOP_N = 16
WINDOW = 512
ROPE_THETA = 10000.0
RMS_EPS = 1e-6
GN_EPS = 1e-5
NEG_INF = -1e9
FORCED_SCORE = 1e4
M_INIT = -1e30
CAP_OPEN = 3e38
ATTN_SCALE = NSA_HD ** -0.5
SPLIT_SIZES = (512, 512, 512, 512, 512, 128, 128, 128, 128, 128, 128, 24, 512)
N_GATES = 3 * NSA_HEADS

LANES = 128
SUBLANES = 8
PROMPT_Q_TILE = 128
SLC_KEY_TILE = 1024
RET_CHUNKS_PER_STEP = 4
SAMPLE_PAGES_PER_STEP = 64
SAMPLE_STREAMS = 1
COMPRESS_PAGES_PER_STEP = 32
PAGE_SLOTS = 3
BLOCK_PITCH = 40
VMEM_LIMIT = 56 * 1024 * 1024


def _dot(a, b):
    return jnp.dot(a, b, preferred_element_type=F32)


def _dot_nt(a, b):
    return lax.dot_general(a, b, (((1,), (1,)), ((), ())), preferred_element_type=F32)


def _sigmoid(x):
    return 1.0 / (1.0 + jnp.exp(-x))


def _params(sem):
    return pltpu.CompilerParams(dimension_semantics=sem, vmem_limit_bytes=VMEM_LIMIT)


def _rope_lanes(a, cos, sin_signed, half):
    if 2 * half == LANES:
        partner = pltpu.roll(a, half, 1)
    else:
        lane = lax.broadcasted_iota(jnp.int32, (1, LANES), 1)
        partner = jnp.where((lane % (2 * half)) < half, pltpu.roll(a, LANES - half, 1), pltpu.roll(a, half, 1))
    return a * cos + partner * sin_signed


def _rope_sublanes(blk, cos_t, sin_t):
    outs = []
    for h in range(blk.shape[0] // NSA_HD):
        x1 = blk[h * NSA_HD:h * NSA_HD + NSA_HD // 2]
        x2 = blk[h * NSA_HD + NSA_HD // 2:(h + 1) * NSA_HD]
        outs += [x1 * cos_t - x2 * sin_t, x2 * cos_t + x1 * sin_t]
    return jnp.concatenate(outs, axis=0)


def _proj_kernel(*refs, prompt):
    (x_ref, g_ref, wrm_ref, wt_ref, cosr_ref, sinr_ref, cosn_ref, sinn_ref, cost_ref, sint_ref,
     qret_ref, kret_ref, vret_ref, rg_ref) = refs[:14]
    x = x_ref[...]
    ms = jnp.mean(x * x, axis=-1, keepdims=True)
    h = (x * lax.rsqrt(ms + RMS_EPS) * g_ref[...]).astype(BF16)

    def proj(lo, hi):
        return _dot(h, wrm_ref[:, lo:hi]) if prompt else _dot_nt(h, wrm_ref[lo:hi, :])

    cr, sr = cosr_ref[...], sinr_ref[...]
    cn, sn = cosn_ref[...], sinn_ref[...]
    ct, st = cost_ref[...], sint_ref[...]
    yq = proj(0, 512)
    yk = proj(512, 1024)
    for hh in range(RET_HEADS):
        sl = slice(hh * LANES, (hh + 1) * LANES)
        qret_ref[:, sl] = _rope_lanes(yq[:, sl], cr, sr, RET_DK // 2)
        kret_ref[:, sl] = _rope_lanes(yk[:, sl], cr, sr, RET_DK // 2) * (RET_DK ** -0.5)
    vret_ref[...] = proj(1024, 1536)
    rg_ref[...] = proj(1536, 2048)

    yt = _dot_nt(wt_ref[...], h)
    if prompt:
        (skr_ref, wkr_ref, ck_ref, cv_ref, sv_ref, wv_ref, sk_ref, wk_ref, svb_ref, wvb_ref,
         nqt_ref, nqrt_ref, ngt_ref, nglt_ref) = refs[14:]
    else:
        nq_ref, nqr_ref, ng_ref, ngl_ref, ck_ref, cv_ref, sv_ref, wv_ref, sk_ref, wk_ref = refs[14:]
    ck_ref[0] = yt[0:128]
    cv_ref[0] = yt[128:256]
    sv_ref[0] = yt[256:384]
    wv_ref[0] = yt[384:512]
    sk_ref[0] = _rope_sublanes(yt[512:640], ct, st)
    wk_ref[0] = _rope_sublanes(yt[640:768], ct, st)
    if prompt:
        svb_ref[0] = yt[256:384].astype(BF16)
        wvb_ref[0] = yt[384:512].astype(BF16)
        nqt = yt[768:1280]
        nqt_ref[0] = nqt.astype(BF16)
        nqrt_ref[0] = _rope_sublanes(nqt, ct, st).astype(BF16)
        ngt_ref[0] = yt[1280:1792]
        nglt_ref[0] = yt[1792:1920]
        skr_ref[...] = _rope_lanes(proj(2048, 2176), cn, sn, NSA_HD // 2).astype(BF16)
        wkr_ref[...] = _rope_lanes(proj(2176, 2304), cn, sn, NSA_HD // 2).astype(BF16)
    else:
        yn = proj(2048, 2560)
        for gg in range(NSA_GROUP):
            sl = slice(gg * LANES, (gg + 1) * LANES)
            nq_ref[:, sl] = yn[:, sl].astype(BF16)
            nqr_ref[:, sl] = _rope_lanes(yn[:, sl], cn, sn, NSA_HD // 2).astype(BF16)
        ng_ref[...] = proj(2560, 3072)
        ngl_ref[...] = proj(3072, 3200)


def _rope_tables(pos, half):
    inv = ROPE_THETA ** (-np.arange(half, dtype=np.float64) / half)
    ang = np.asarray(pos, np.float64)[:, None] * inv[None, :]
    return np.cos(ang).astype(np.float32), np.sin(ang).astype(np.float32)


def _proj(x2d, pos_rows, norm_g, w_rm, w_t, kv_batch, tm, prompt):
    n = x2d.shape[0]
    nt = pos_rows.shape[0] // tm
    skv = n // kv_batch
    nkt = skv // tm
    c64, s64 = _rope_tables(pos_rows, 64)
    c32, s32 = _rope_tables(pos_rows, 32)
    cosr = np.concatenate([c64, c64], axis=1)
    sinr = np.concatenate([-s64, s64], axis=1)
    cosn = np.concatenate([c32, c32, c32, c32], axis=1)
    sinn = np.concatenate([-s32, s32, -s32, s32], axis=1)
    cost, sint = np.ascontiguousarray(c32.T), np.ascontiguousarray(s32.T)
    row = lambda w: pl.BlockSpec((tm, w), lambda i: (i, 0))
    tab = pl.BlockSpec((tm, LANES), lambda i: (i % nt, 0))
    tabt = pl.BlockSpec((32, tm), lambda i: (0, i % nt))
    tspec = lambda r: pl.BlockSpec((1, r, tm), lambda i: (i // nkt, 0, i % nkt))
    f = lambda w, dt: jax.ShapeDtypeStruct((n, w), dt)
    ts = lambda r, dt: jax.ShapeDtypeStruct((kv_batch, r, skv), dt)
    kvo, kvs = tspec(KV_WIDTH), ts(KV_WIDTH, F32)
    out_specs = [row(512)] * 4
    out_shape = [f(512, F32)] * 4
    if prompt:
        out_specs += [row(KV_WIDTH)] * 2 + [kvo] * 8 + [tspec(512)] * 3 + [tspec(LANES)]
        out_shape += [f(KV_WIDTH, BF16)] * 2 + [kvs] * 6 + [ts(KV_WIDTH, BF16)] * 2 + [ts(512, BF16)] * 2 + [
            ts(512, F32), ts(LANES, F32)]
    else:
        out_specs += [row(512)] * 3 + [row(LANES)] + [kvo] * 6
        out_shape += [f(512, BF16)] * 2 + [f(512, F32), f(LANES, F32)] + [kvs] * 6
    return pl.pallas_call(
        functools.partial(_proj_kernel, prompt=prompt),
        grid=(n // tm,),
        in_specs=[row(D_MODEL), pl.BlockSpec((1, D_MODEL), lambda i: (0, 0)),
                  pl.BlockSpec(w_rm.shape, lambda i: (0, 0)), pl.BlockSpec(w_t.shape, lambda i: (0, 0)),
                  tab, tab, tab, tab, tabt, tabt],
        out_specs=out_specs,
        out_shape=out_shape,
        compiler_params=_params(("arbitrary",)),
        name="proj_prompt" if prompt else "proj_sample",
    )(x2d, norm_g, w_rm, w_t, cosr, sinr, cosn, sinn, cost, sint)


def _ret_kernel(q_ref, k_ref, v_ref, rg_ref, st_ref, dm_ref, qd_ref, kd_ref, sd_ref, gg_ref, gb_ref,
                y_ref, so_ref, st_scr, *, n_chunks):
    c = pl.program_id(1)

    @pl.when(c == 0)
    def _():
        st_scr[...] = st_ref[0]

    for h in range(RET_HEADS):
        sl = slice(h * LANES, (h + 1) * LANES)
        state = st_scr[h]
        for cc in range(q_ref.shape[0] // RET_CHUNK):
            rows = slice(cc * RET_CHUNK, (cc + 1) * RET_CHUNK)
            q = q_ref[rows, sl]
            k = k_ref[rows, sl]
            v = v_ref[rows, sl].astype(BF16)
            s = _dot_nt(q.astype(BF16), k.astype(BF16)) * dm_ref[h]
            intra = _dot(s.astype(BF16), v)
            cross = _dot((q * qd_ref[:, sl]).astype(BF16), state.astype(BF16))
            o = intra + cross
            kd_t = (k * kd_ref[:, sl]).T.astype(BF16)
            state = sd_ref[h] * state + _dot(kd_t, v)
            mu = jnp.mean(o, axis=-1, keepdims=True)
            var = jnp.mean(jnp.square(o - mu), axis=-1, keepdims=True)
            on = (o - mu) * lax.rsqrt(var + GN_EPS)
            gate = rg_ref[rows, sl]
            y_ref[rows, sl] = (on * gg_ref[:, sl] + gb_ref[:, sl]) * (gate * _sigmoid(gate))
        st_scr[h] = state

    @pl.when(c == n_chunks - 1)
    def _():
        so_ref[0] = st_scr[...]


def _retention(q, k, v, rg, state, gn_g, gn_b, n_batch, n_chunks, chunk_len):
    c = RET_CHUNK
    log_g = np.log(1.0 - 2.0 ** (-5.0 - np.arange(RET_HEADS, dtype=np.float64)))
    i = np.arange(c, dtype=np.float64)
    diff = i[:, None] - i[None, :]
    causal = diff >= 0
    f32 = lambda a: np.ascontiguousarray(a, dtype=np.float32)
    dmask = f32(np.where(causal[None], np.exp(np.where(causal, diff, 0.0)[None] * log_g[:, None, None]), 0.0))
    expand = lambda t: f32(np.repeat(t, LANES, axis=1))
    qdec = expand(np.exp((i[:, None] + 1.0) * log_g[None, :]))
    kdec = expand(np.exp((chunk_len - 1.0 - i)[:, None] * log_g[None, :]))
    sdec = f32(np.broadcast_to(np.exp(chunk_len * log_g)[:, None, None], (RET_HEADS, 1, LANES)))
    stspec1 = pl.BlockSpec((1, RET_HEADS, 128, 128), lambda b: (b, 0, 0, 0))
    const1 = lambda shape: pl.BlockSpec(shape, lambda b: (0,) * len(shape))
    if chunk_len < c:
        assert n_chunks == 1
        row1 = pl.BlockSpec((chunk_len, 512), lambda b: (b, 0))
        return pl.pallas_call(
            _ret_short_kernel,
            grid=(n_batch,),
            in_specs=[row1, row1, row1, row1, stspec1, const1((RET_HEADS, c, c)), const1((c, 512)), const1((c, 512)),
                      const1((RET_HEADS, 1, LANES)), const1((1, 512)), const1((1, 512))],
            out_specs=[row1, stspec1],
            out_shape=[jax.ShapeDtypeStruct(q.shape, F32), jax.ShapeDtypeStruct(state.shape, F32)],
            scratch_shapes=[pltpu.VMEM((c, 512), F32)] * 3,
            compiler_params=_params(("arbitrary",)),
            name="retention_short",
        )(q, k, v, rg, state, dmask, qdec, kdec, sdec, gn_g, gn_b)
    per_step = RET_CHUNKS_PER_STEP if n_chunks % RET_CHUNKS_PER_STEP == 0 else 1
    n_steps = n_chunks // per_step
    row = pl.BlockSpec((per_step * c, 512), lambda b, j: (b * n_steps + j, 0))
    stspec = pl.BlockSpec((1, RET_HEADS, 128, 128), lambda b, j: (b, 0, 0, 0))
    const = lambda shape: pl.BlockSpec(shape, lambda b, j: (0,) * len(shape))
    return pl.pallas_call(
        functools.partial(_ret_kernel, n_chunks=n_steps),
        grid=(n_batch, n_steps),
        in_specs=[row, row, row, row, stspec, const((RET_HEADS, c, c)), const((c, 512)), const((c, 512)),
                  const((RET_HEADS, 1, LANES)), const((1, 512)), const((1, 512))],
        out_specs=[row, stspec],
        out_shape=[jax.ShapeDtypeStruct(q.shape, F32), jax.ShapeDtypeStruct(state.shape, F32)],
        scratch_shapes=[pltpu.VMEM((RET_HEADS, 128, 128), F32)],
        compiler_params=_params(("arbitrary", "arbitrary")),
        name="retention",
    )(q, k, v, rg, state, dmask, qdec, kdec, sdec, gn_g, gn_b)


def _ret_short_kernel(q_ref, k_ref, v_ref, rg_ref, st_ref, dm_ref, qd_ref, kd_ref, sd_ref, gg_ref, gb_ref,
                      y_ref, so_ref, k_pad, kd_pad, v_pad):
    t = q_ref.shape[0]

    @pl.when(pl.program_id(0) == 0)
    def _():
        for pad in (k_pad, kd_pad, v_pad):
            pad[...] = jnp.zeros(pad.shape, F32)

    k_pad[0:t, :] = k_ref[...]
    kd_pad[0:t, :] = k_ref[...] * kd_ref[0:t, :]
    v_pad[0:t, :] = v_ref[...]
    for h in range(RET_HEADS):
        sl = slice(h * LANES, (h + 1) * LANES)
        q = q_ref[:, sl]
        v = v_pad[:, sl].astype(BF16)
        state = st_ref[0, h]
        s = _dot_nt(q.astype(BF16), k_pad[:, sl].astype(BF16)) * dm_ref[h, 0:t, :]
        intra = _dot(s.astype(BF16), v)
        cross = _dot((q * qd_ref[0:t, sl]).astype(BF16), state.astype(BF16))
        o = intra + cross
        so_ref[0, h] = sd_ref[h] * state + _dot(kd_pad[:, sl].T.astype(BF16), v)
        mu = jnp.mean(o, axis=-1, keepdims=True)
        var = jnp.mean(jnp.square(o - mu), axis=-1, keepdims=True)
        on = (o - mu) * lax.rsqrt(var + GN_EPS)
        gate = rg_ref[:, sl]
        y_ref[:, sl] = (on * gg_ref[:, sl] + gb_ref[:, sl]) * (gate * _sigmoid(gate))


def _gelu_tanh(x):
    return 0.5 * x * (1.0 + jnp.tanh(np.sqrt(2.0 / np.pi).astype(np.float32) * (x + 0.044715 * (x * x * x))))


def _rows_to_scratch(tiles, scr):
    per_tile = LANES // CMP_BLOCK
    for t, tile in enumerate(tiles):
        rows_pm = tile.T
        for c in range(per_tile):
            r0 = (t * per_tile + c) * BLOCK_PITCH
            scr[r0:r0 + CMP_BLOCK, :] = rows_pm[c * CMP_BLOCK:(c + 1) * CMP_BLOCK]


def _compress_scratch(scr, n_blk, pe_ref, w1_ref, w2_ref, transpose_out=False):
    flat = [(scr[pl.ds(j, n_blk, stride=BLOCK_PITCH), :] + pe_ref[j:j + 1, :]).astype(BF16) for j in range(CMP_BLOCK)]
    hid = _gelu_tanh(_dot(jnp.concatenate(flat, axis=1), w1_ref[...]))
    if transpose_out:
        return _dot_nt(w2_ref[...], hid.astype(BF16))
    return _dot(hid.astype(BF16), w2_ref[...])


def _compress_one(tiles, pe_ref, w1_ref, w2_ref, scr, transpose_out=False):
    _rows_to_scratch(tiles, scr)
    return _compress_scratch(scr, len(tiles) * (LANES // CMP_BLOCK), pe_ref, w1_ref, w2_ref, transpose_out)


def _compress_prompt_kernel(k_ref, v_ref, pek_ref, w1k_ref, w2k_ref, pev_ref, w1v_ref, w2v_ref,
                            kc_ref, vct_ref, scr_k, scr_v, *, n_tiles):
    for src, pe, w1, w2, dst, scr, tr in ((k_ref, pek_ref, w1k_ref, w2k_ref, kc_ref, scr_k, False),
                                          (v_ref, pev_ref, w1v_ref, w2v_ref, vct_ref, scr_v, True)):
        tiles = [src[0, :, t * LANES:(t + 1) * LANES] for t in range(n_tiles)]
        dst[0] = _compress_one(tiles, pe, w1, w2, scr, tr)


def _compress_sample_kernel(pt_ref, ck_hbm, cv_hbm, pek_ref, w1k_ref, w2k_ref, pev_ref, w1v_ref, w2v_ref,
                            kc_ref, vc_ref, kbuf, vbuf, sems, scr_k, scr_v, *, n_pages, n_steps, n_batch):
    n_blk = n_pages * PAGE_SIZE // CMP_BLOCK

    def to_rows(step, slot):
        _rows_to_scratch([kbuf[slot, j] for j in range(n_pages)], scr_k.at[step % 2])
        _rows_to_scratch([vbuf[slot, j] for j in range(n_pages)], scr_v.at[step % 2])

    def compress(step):
        rows = pl.ds(pl.multiple_of(step * n_blk, n_blk), n_blk)
        kc_ref[0, rows, :] = _compress_scratch(scr_k.at[step % 2], n_blk, pek_ref, w1k_ref, w2k_ref)
        vc_ref[0, rows, :] = _compress_scratch(scr_v.at[step % 2], n_blk, pev_ref, w1v_ref, w2v_ref)

    _stream_pages(pt_ref, (ck_hbm, cv_hbm), (kbuf, vbuf), sems, n_pages, n_steps, n_batch, to_rows, compress)


def _compress_weights(pe, w1, w2, transpose_w2=False):
    pe_rows = pe.reshape(CMP_BLOCK, KV_WIDTH)
    z = jnp.zeros((CMP_BLOCK, NSA_HD, NSA_HD), F32)
    w1bd = jnp.concatenate([jnp.concatenate([w1[0], z], axis=2), jnp.concatenate([z, w1[1]], axis=2)], axis=1)
    z2 = jnp.zeros((NSA_HD, NSA_HD), F32)
    w2bd = jnp.concatenate([jnp.concatenate([w2[0], z2], axis=1), jnp.concatenate([z2, w2[1]], axis=1)], axis=0)
    if transpose_w2:
        w2bd = w2bd.T
    return pe_rows, w1bd.reshape(CMP_BLOCK * KV_WIDTH, KV_WIDTH).astype(BF16), w2bd.astype(BF16)


def _cw_specs():
    zero = lambda *a: (0, 0)
    return [pl.BlockSpec((CMP_BLOCK, KV_WIDTH), zero), pl.BlockSpec((CMP_BLOCK * KV_WIDTH, KV_WIDTH), zero),
            pl.BlockSpec((KV_WIDTH, KV_WIDTH), zero)]


def _compress_prompt(ck_t, cv_t, cwk, cwv_t):
    b, _, s = ck_t.shape
    n_blk = s // CMP_BLOCK
    src = pl.BlockSpec((1, KV_WIDTH, s), lambda i: (i, 0, 0))
    return pl.pallas_call(
        functools.partial(_compress_prompt_kernel, n_tiles=s // LANES),
        grid=(b,),
        in_specs=[src, src] + _cw_specs() + _cw_specs(),
        out_specs=[pl.BlockSpec((1, n_blk, KV_WIDTH), lambda i: (i, 0, 0)),
                   pl.BlockSpec((1, KV_WIDTH, n_blk), lambda i: (i, 0, 0))],
        out_shape=[jax.ShapeDtypeStruct((b, n_blk, KV_WIDTH), F32), jax.ShapeDtypeStruct((b, KV_WIDTH, n_blk), F32)],
        scratch_shapes=[pltpu.VMEM((n_blk * BLOCK_PITCH, KV_WIDTH), F32)] * 2,
        compiler_params=_params(("arbitrary",)),
        name="compress_prompt",
    )(ck_t, cv_t, *cwk, *cwv_t)


def _compress_sample(page_table, cache_k_t, cache_v_t, cwk, cwv):
    db, n_pages = page_table.shape
    g = COMPRESS_PAGES_PER_STEP
    steps = n_pages // g
    blk_per_step = g * PAGE_SIZE // CMP_BLOCK
    n_blk = n_pages * PAGE_SIZE // CMP_BLOCK

    hbm = pl.BlockSpec(memory_space=pl.ANY)
    dst = pl.BlockSpec((1, n_blk, KV_WIDTH), lambda b, pt: (b, 0, 0))
    shp = jax.ShapeDtypeStruct((db, n_blk, KV_WIDTH), F32)
    page_buf = pltpu.VMEM((PAGE_SLOTS, g, KV_WIDTH, PAGE_SIZE), F32)
    grid_spec = pltpu.PrefetchScalarGridSpec(
        num_scalar_prefetch=1,
        grid=(db,),
        in_specs=[hbm, hbm] + _cw_specs() + _cw_specs(),
        out_specs=[dst, dst],
        scratch_shapes=[page_buf, page_buf, pltpu.SemaphoreType.DMA((PAGE_SLOTS, 2))]
        + [pltpu.VMEM((2, blk_per_step * BLOCK_PITCH, KV_WIDTH), F32)] * 2,
    )
    return pl.pallas_call(
        functools.partial(_compress_sample_kernel, n_pages=g, n_steps=steps, n_batch=db),
        grid_spec=grid_spec,
        out_shape=[shp, shp],
        compiler_params=_params(("arbitrary",)),
        name="compress_sample",
    )(page_table, cache_k_t, cache_v_t, *cwk, *cwv)


def _stack_heads_t(qt_ref):
    nq = qt_ref.shape[2]
    zero = jnp.zeros((NSA_HD, nq), BF16)
    cols = []
    for k in range(KV_HEADS):
        for g in range(NSA_GROUP):
            h = k * NSA_GROUP + g
            tile = qt_ref[0, h * NSA_HD:(h + 1) * NSA_HD, :] * ATTN_SCALE
            cols.append(jnp.concatenate([tile, zero] if k == 0 else [zero, tile], axis=0))
    return jnp.concatenate(cols, axis=1)


def _rank_select_rows(score, n_blk):
    parts = [score[v * SUBLANES:(v + 1) * SUBLANES] for v in range(n_blk // SUBLANES)]
    ranks = [jnp.zeros(p.shape, jnp.int32) for p in parts]
    for j in range(n_blk):
        col = score[j:j + 1, :]
        for v, part in enumerate(parts):
            ge, gt = (col >= part).astype(jnp.int32), (col > part).astype(jnp.int32)
            if v * SUBLANES > j:
                beats = ge
            elif (v + 1) * SUBLANES - 1 <= j:
                beats = gt
            else:
                row = v * SUBLANES + lax.broadcasted_iota(jnp.int32, (SUBLANES, 1), 0)
                beats = jnp.where(row > j, ge, gt)
            ranks[v] = ranks[v] + beats
    return jnp.concatenate([(r < TOP_N).astype(F32) for r in ranks], axis=0)


def _softmax_tile_t(s, v_t, m_ref, acc_ref, s_max=None):
    half = s.shape[1] // KV_HEADS
    m_old = m_ref[...]
    m_new = jnp.maximum(m_old, jnp.max(s, axis=0, keepdims=True) if s_max is None else s_max)
    alpha = jnp.exp(m_old - m_new)
    pb = jnp.exp(s - m_new).astype(BF16)
    m_ref[...] = m_new
    ones = jnp.ones((NSA_HD, v_t.shape[1]), BF16)
    v0 = jnp.concatenate([v_t[:NSA_HD], ones], axis=0)
    v1 = jnp.concatenate([ones, v_t[NSA_HD:]], axis=0)
    pv = jnp.concatenate([_dot(v0, pb[:, :half]), _dot(v1, pb[:, half:])], axis=1)
    acc_ref[...] = alpha * acc_ref[...] + pv


def _normalized_head_t(acc, k, cols):
    den = (1 - k) * NSA_HD
    return acc[k * NSA_HD:(k + 1) * NSA_HD, cols] / acc[den:den + 1, cols]


def _attn_prompt_kernel(nqt_ref, nqrt_ref, nglt_ref, ngt_ref, kc_ref, vct_ref, sk_ref, svt_ref, wk_ref, wvt_ref,
                        et_ref, ewin_ref, y_ref, lhs_scr, qslc_scr, qwin_scr, ocmp_scr, t_scr, m_scr, acc_scr,
                        s_scr, smax_scr, sw_scr):
    i = pl.program_id(1)
    qb = y_ref.shape[0]
    q0 = i * qb
    cols8 = 2 * NSA_GROUP * qb
    qpos = q0 + lax.broadcasted_iota(jnp.int32, (1, qb), 1)
    lhs_scr[...] = _stack_heads_t(nqt_ref)
    q_rot = _stack_heads_t(nqrt_ref)
    qslc_scr[0:KV_WIDTH, :] = q_rot
    qwin_scr[0:KV_WIDTH, :] = q_rot
    def causal_cap():
        own_keys = q0 + lax.broadcasted_iota(jnp.int32, (qb, 1), 0)
        return jnp.concatenate([jnp.where(own_keys <= qpos, CAP_OPEN, NEG_INF)] * (2 * NSA_GROUP), axis=1)

    n_cmp = kc_ref.shape[1]
    s_all = _dot(kc_ref[0].astype(BF16), lhs_scr[...])
    cend = (lax.broadcasted_iota(jnp.int32, (n_cmp, 1), 0) + 1) * CMP_BLOCK - 1
    cmask = cend <= qpos
    vct = vct_ref[0].astype(BF16)
    imps = []
    for k in range(KV_HEADS):
        imp = None
        for g in range(NSA_GROUP):
            c0 = (k * NSA_GROUP + g) * qb
            s = jnp.where(cmask, s_all[:, c0:c0 + qb], NEG_INF)
            p = jnp.exp(s - jnp.max(s, axis=0, keepdims=True))
            p = p / jnp.sum(p, axis=0, keepdims=True) * cmask.astype(F32)
            imp = p if imp is None else imp + p
            ocmp_scr[:, c0:c0 + qb] = _dot(vct, p.astype(BF16))
        imps.append(imp)

    n_slc = n_cmp // 2
    blk = lax.broadcasted_iota(jnp.int32, (n_slc, 1), 0)
    valid = blk * SLC_BLOCK <= qpos

    def store_bias(k, sel):
        bias = jnp.concatenate([(sel - 1.0) * (-NEG_INF), jnp.zeros((LANES - n_slc, qb), F32)], axis=0).astype(BF16)
        c0 = k * NSA_GROUP * qb
        qslc_scr[KV_WIDTH:, c0:c0 + NSA_GROUP * qb] = jnp.concatenate([bias] * NSA_GROUP, axis=1)

    @pl.when(q0 + qb <= TOP_N * SLC_BLOCK)
    def _():
        for k in range(KV_HEADS):
            store_bias(k, valid.astype(F32))

    @pl.when(q0 + qb > TOP_N * SLC_BLOCK)
    def _():
        forced = (blk == 0) | (blk == qpos // SLC_BLOCK)
        for k in range(KV_HEADS):
            pairs = []
            for u in range(qb // LANES):
                t_scr[u] = imps[k][:, u * LANES:(u + 1) * LANES]
                pairs.append(t_scr[u, pl.ds(0, n_slc, stride=2), :] + t_scr[u, pl.ds(1, n_slc, stride=2), :])
            pair = jnp.concatenate(pairs, axis=1)
            score = jnp.where(forced, FORCED_SCORE, jnp.where(valid, pair, NEG_INF))
            store_bias(k, _rank_select_rows(score, n_slc))

    tk = SLC_KEY_TILE
    n_tiles = (q0 + qb - 1) // tk + 1
    m_scr[...] = jnp.full(m_scr.shape, M_INIT, F32)
    acc_scr[...] = jnp.zeros(acc_scr.shape, F32)

    def scores(t):
        k0 = pl.multiple_of(t * tk, tk)
        keys = jnp.concatenate([sk_ref[pl.ds(k0, tk), :], et_ref[pl.ds(k0, tk), :]], axis=1)
        s = _dot(keys, qslc_scr[...])
        s_scr[t % 2] = s
        smax_scr[t % 2] = jnp.max(s, axis=0, keepdims=True)

    def update(t):
        k0 = pl.multiple_of(t * tk, tk)
        _softmax_tile_t(s_scr[t % 2], svt_ref[0, :, pl.ds(k0, tk)], m_scr.at[0], acc_scr.at[0], smax_scr[t % 2])

    def slc_step(t, carry):
        scores(t + 1)
        update(t)
        return carry

    scores(0)
    lax.fori_loop(0, n_tiles - 1, slc_step, 0)
    last = n_tiles - 1
    own = pl.ds(pl.multiple_of(q0 - last * tk, qb), qb)
    s_scr[last % 2, own, :] = jnp.minimum(s_scr[last % 2, own, :], causal_cap())
    smax_scr[last % 2] = jnp.max(s_scr[last % 2], axis=0, keepdims=True)
    update(last)

    wlen = WINDOW + qb
    w0 = pl.multiple_of(jnp.maximum(i - WINDOW // qb, 0) * qb, qb)
    later = (w0 + lax.broadcasted_iota(jnp.int32, (LANES, 1), 0) * qb) > q0
    qwin_scr[KV_WIDTH:, :] = jnp.broadcast_to(jnp.where(later, NEG_INF, 0.0), (LANES, cols8)).astype(BF16)
    keys = jnp.concatenate([wk_ref[pl.ds(w0, wlen), :], ewin_ref[...]], axis=1)
    sw_scr[...] = _dot(keys, qwin_scr[...])
    edge_keys = w0 + lax.broadcasted_iota(jnp.int32, (qb, 1), 0)
    edge_cap = jnp.concatenate([jnp.where(qpos - edge_keys <= WINDOW, CAP_OPEN, NEG_INF)] * (2 * NSA_GROUP), axis=1)
    sw_scr[0:qb, :] = jnp.minimum(sw_scr[0:qb, :], edge_cap)
    own = pl.ds(pl.multiple_of(q0 - w0, qb), qb)
    sw_scr[own, :] = jnp.minimum(sw_scr[own, :], causal_cap())
    _softmax_tile_t(sw_scr[...], wvt_ref[0, :, pl.ds(w0, wlen)], m_scr.at[1], acc_scr.at[1])

    acc_slc, acc_win = acc_scr[0], acc_scr[1]
    sig = _sigmoid(nglt_ref[0])
    for pair in range(NSA_HEADS // 2):
        tiles = []
        for h in (2 * pair, 2 * pair + 1):
            k = h // NSA_GROUP
            rows = slice(k * NSA_HD, (k + 1) * NSA_HD)
            cols = slice(h * qb, (h + 1) * qb)
            gate = lambda r: sig[r * NSA_HEADS + h:r * NSA_HEADS + h + 1, :]
            o = (gate(0) * ocmp_scr[rows, cols] + gate(1) * _normalized_head_t(acc_slc, k, cols)
                 + gate(2) * _normalized_head_t(acc_win, k, cols))
            ng = ngt_ref[0, h * NSA_HD:(h + 1) * NSA_HD, :]
            tiles.append(o * (ng * _sigmoid(ng)))
        y_ref[:, pair * LANES:(pair + 1) * LANES] = jnp.concatenate(tiles, axis=0).T.astype(y_ref.dtype)


def _attn_prompt(nq_t, nqr_t, ngl_t, ng_t, kc, vc_t, sk_rm, sv_t, wk_rm, wv_t):
    b, _, s = sv_t.shape
    assert s // CMP_BLOCK == LANES, "one lane per compressed block"
    qb = PROMPT_Q_TILE
    nqb = s // qb
    cols8 = 2 * NSA_GROUP * qb
    expand_t = (np.arange(s)[:, None] // SLC_BLOCK == np.arange(LANES)[None, :]).astype(BF16)
    wlen = WINDOW + qb
    win_blocks = (np.arange(wlen)[:, None] // qb == np.arange(LANES)[None, :]).astype(BF16)
    qcol = lambda r: pl.BlockSpec((1, r, qb), lambda bb, i: (bb, 0, i))
    per_b = lambda shape: pl.BlockSpec((1,) + shape, lambda bb, i: (bb, 0, 0))
    rows_b = pl.BlockSpec((s, KV_WIDTH), lambda bb, i: (bb, 0))
    return pl.pallas_call(
        _attn_prompt_kernel,
        grid=(b, nqb),
        in_specs=[qcol(512), qcol(512), qcol(LANES), qcol(512), per_b(kc.shape[1:]), per_b(vc_t.shape[1:]),
                  rows_b, per_b((KV_WIDTH, s)), rows_b, per_b((KV_WIDTH, s)),
                  pl.BlockSpec(expand_t.shape, lambda bb, i: (0, 0)),
                  pl.BlockSpec(win_blocks.shape, lambda bb, i: (0, 0))],
        out_specs=pl.BlockSpec((qb, 512), lambda bb, i: (bb * nqb + i, 0)),
        out_shape=jax.ShapeDtypeStruct((b * s, 512), BF16),
        scratch_shapes=[pltpu.VMEM((KV_WIDTH, cols8), BF16), pltpu.VMEM((2 * KV_WIDTH, cols8), BF16),
                        pltpu.VMEM((2 * KV_WIDTH, cols8), BF16),
                        pltpu.VMEM((KV_WIDTH, cols8), F32), pltpu.VMEM((qb // LANES, LANES, LANES), F32),
                        pltpu.VMEM((2, 1, cols8), F32), pltpu.VMEM((2, KV_WIDTH, cols8), F32),
                        pltpu.VMEM((2, SLC_KEY_TILE, cols8), F32), pltpu.VMEM((2, 1, cols8), F32),
                        pltpu.VMEM((wlen, cols8), F32)],
        compiler_params=_params(("arbitrary", "arbitrary")),
        name="attn_prompt",
    )(nq_t, nqr_t, ngl_t, ng_t, kc, vc_t, sk_rm, sv_t, wk_rm, wv_t, expand_t, win_blocks)


def _stack_heads(q_ref):
    lane = lax.broadcasted_iota(jnp.int32, (1, LANES), 1)
    lo = lane < NSA_HD
    slabs = [q_ref[:, g * LANES:(g + 1) * LANES] for g in range(NSA_GROUP)]
    zero = jnp.zeros_like(slabs[0])
    stacked = jnp.concatenate([jnp.where(lo, s, zero) for s in slabs] + [jnp.where(lo, zero, s) for s in slabs], axis=0)
    return stacked.astype(BF16)


def _mask_rows(sc, masks, rows):
    n = sc.shape[-1]
    s4 = sc.reshape(2 * NSA_GROUP, rows, n)
    out = [jnp.where(masks[k][None], s4[k * NSA_GROUP:(k + 1) * NSA_GROUP], NEG_INF) for k in range(KV_HEADS)]
    return jnp.concatenate(out, axis=0).reshape(2 * NSA_GROUP * rows, n)


def _pair_scores(imp, qpos):
    n_slc = imp.shape[1] // 2
    blk = lax.broadcasted_iota(jnp.int32, (1, n_slc), 1)
    valid = blk * SLC_BLOCK <= qpos
    forced = (blk == 0) | (blk == qpos // SLC_BLOCK)
    return jnp.where(forced, FORCED_SCORE, jnp.where(valid, imp[:, :n_slc] + imp[:, n_slc:], NEG_INF))


def _rank_select(score, extra_forced):
    n_blk = score.shape[1]
    blk = lax.broadcasted_iota(jnp.int32, (1, n_blk), 1)
    rank = jnp.zeros(score.shape, jnp.int32)
    for j in range(n_blk):
        col = score[:, j:j + 1]
        beats = (col > score) | ((col == score) & (j < blk))
        rank = rank + beats.astype(jnp.int32)
    if extra_forced:
        rank = rank + (score < FORCED_SCORE).astype(jnp.int32)
    return rank < TOP_N


def _merge_kv_heads(acc, rows):
    lane = lax.broadcasted_iota(jnp.int32, (1, LANES), 1)
    lo = lane < NSA_HD
    half = NSA_GROUP * rows
    return [jnp.where(lo, acc[g * rows:(g + 1) * rows], acc[half + g * rows:half + (g + 1) * rows])
            for g in range(NSA_GROUP)]


def _gate_and_store(o_cmp, o_slc, o_win, ngl_ref, ng_ref, y_ref):
    lane = lax.broadcasted_iota(jnp.int32, (1, LANES), 1)
    lo = lane < NSA_HD
    sig = _sigmoid(ngl_ref[...])
    for g in range(NSA_GROUP):
        gates = [jnp.where(lo, sig[:, r * 8 + g:r * 8 + g + 1], sig[:, r * 8 + 4 + g:r * 8 + 4 + g + 1]) for r in range(3)]
        o = gates[0] * o_cmp[g] + gates[1] * o_slc[g] + gates[2] * o_win[g]
        gate = ng_ref[:, g * LANES:(g + 1) * LANES]
        y_ref[:, g * LANES:(g + 1) * LANES] = (o * (gate * _sigmoid(gate))).astype(y_ref.dtype)


def _page_copies(pt_ref, b, chunk, slot, caches, bufs, sems, g):
    out = []
    for ci, (cache, buf) in enumerate(zip(caches, bufs)):
        for j in range(g):
            page = pt_ref[b, chunk * g + j]
            out.append(pltpu.make_async_copy(cache.at[page], buf.at[slot, j], sems.at[slot, ci]))
    return out


def _stream_pages(pt_ref, caches, bufs, sems, g, n_chunks, n_batch, compute, finish=None):
    b = pl.program_id(0)
    total = n_batch * n_chunks
    n_slots = bufs[0].shape[0]
    depth = n_slots - 1
    assert n_chunks >= depth
    copies = functools.partial(_page_copies, pt_ref, caches=caches, bufs=bufs, sems=sems, g=g)

    @pl.when(b == 0)
    def _():
        for d in range(depth):
            for cp in copies(0, d, d):
                cp.start()

    def visit(c):
        flat = b * n_chunks + c
        ahead = jnp.minimum(flat + depth, total - 1)
        for cp in copies(ahead // n_chunks, ahead % n_chunks, (flat + depth) % n_slots):
            cp.start()
        for cp in copies(b, c, flat % n_slots):
            cp.wait()
        compute(c, flat % n_slots)

    def chunk(c, carry):
        visit(c)
        if finish is not None:
            finish(c - 1)
        return carry

    if finish is None:
        lax.fori_loop(0, n_chunks, chunk, 0)
    else:
        visit(0)
        for c in range(1, n_chunks):
            chunk(c, 0)
        finish(n_chunks - 1)

    @pl.when(b == n_batch - 1)
    def _():
        for d in range(depth):
            for cp in copies(n_batch - 1, n_chunks - 1, (total + d) % n_slots):
                cp.wait()


def _attn_sample_kernel(pt_ref, sk_hbm, sv_hbm, nq_ref, nqr_ref, ngl_ref, ng_ref, kc_ref, vc_ref, skn_ref, svn_ref,
                        wko_ref, wvo_ref, wkn_ref, wvn_ref, e_ref, y_ref, kbuf, vbuf, sems, m_scr, l_scr, acc_scr,
                        ocmp_scr, owin_scr, mask_scr, *, n_pages, n_steps, n_batch, t_len, past_len):
    t = t_len
    nq_ref, nqr_ref, ngl_ref, ng_ref, y_ref = (r.at[0] for r in (nq_ref, nqr_ref, ngl_ref, ng_ref, y_ref))
    rows8 = 2 * NSA_GROUP * t
    half = NSA_GROUP * t
    tpos = lax.broadcasted_iota(jnp.int32, (t, 1), 0)
    qpos = past_len + tpos
    n_cmp = kc_ref.shape[1]
    wb = wko_ref.shape[2]
    step_keys = n_pages * PAGE_SIZE

    def before_pages():
        n_slc = n_cmp // 2
        halves = lambda ref: jnp.concatenate([ref[0, pl.ds(0, n_slc, stride=2), :],
                                              ref[0, pl.ds(1, n_slc, stride=2), :]], axis=0).astype(BF16)
        lhs = _stack_heads(nq_ref)
        s = _dot_nt(lhs, halves(kc_ref)) * ATTN_SCALE
        m = jnp.max(s, axis=-1, keepdims=True)
        p = jnp.exp(s - m)
        p = p / jnp.sum(p, axis=-1, keepdims=True)
        oc = _dot(p.astype(BF16), halves(vc_ref))
        for g, slab in enumerate(_merge_kv_heads(oc, t)):
            ocmp_scr[g] = slab
        p4 = p.reshape(8, t, n_cmp)
        sels = []
        for k in range(KV_HEADS):
            imp = p4[k * NSA_GROUP]
            for g in range(1, NSA_GROUP):
                imp = imp + p4[k * NSA_GROUP + g]
            sels.append(_rank_select(_pair_scores(imp, qpos), True).astype(F32))
        sel = jnp.concatenate(sels, axis=0).astype(BF16)
        span = LANES * SLC_BLOCK
        for c in range(n_slc // LANES):
            km = _dot(sel[:, c * LANES:(c + 1) * LANES], e_ref[...])
            for k in range(KV_HEADS):
                kmk = km[k * t:(k + 1) * t]
                if step_keys >= span:
                    off = (c * span) % step_keys
                    mask_scr[(c * span) // step_keys, k, :, off:off + span] = kmk
                else:
                    per = span // step_keys
                    for u in range(per):
                        mask_scr[c * per + u, k] = kmk[:, u * step_keys:(u + 1) * step_keys]

        lhs_r = _stack_heads(nqr_ref)
        so = _dot(lhs_r, wko_ref[0].astype(BF16)) * ATTN_SCALE
        sn = _dot(lhs_r, wkn_ref[0].astype(BF16)) * ATTN_SCALE
        jo = lax.broadcasted_iota(jnp.int32, (1, wb), 1)
        jn = lax.broadcasted_iota(jnp.int32, (1, LANES), 1)
        old_vis = jo >= tpos
        so = _mask_rows(so, [old_vis, old_vis], t)
        new_vis = jn <= tpos
        sn_w = _mask_rows(sn, [new_vis, new_vis], t)
        m = jnp.maximum(jnp.max(so, axis=-1, keepdims=True), jnp.max(sn_w, axis=-1, keepdims=True))
        po = jnp.exp(so - m)
        pn = jnp.exp(sn_w - m)
        den = jnp.sum(po, axis=-1, keepdims=True) + jnp.sum(pn, axis=-1, keepdims=True)
        pob, pnb = (po / den).astype(BF16), (pn / den).astype(BF16)
        wvo, wvn = wvo_ref[0].astype(BF16), wvn_ref[0].astype(BF16)
        ow = jnp.concatenate([_dot_nt(pob[:half], wvo) + _dot_nt(pnb[:half], wvn),
                              _dot_nt(pob[half:], wvo) + _dot_nt(pnb[half:], wvn)], axis=0)
        for g, slab in enumerate(_merge_kv_heads(ow, t)):
            owin_scr[g] = slab

        s2 = _dot(lhs_r, skn_ref[0].astype(BF16)) * ATTN_SCALE
        s2 = _mask_rows(s2, [new_vis, new_vis], t)
        m2 = jnp.max(s2, axis=-1, keepdims=True)
        p2 = jnp.exp(s2 - m2)
        m_scr[0] = m2
        l_scr[0] = jnp.sum(p2, axis=-1, keepdims=True)
        p2b = p2.astype(BF16)
        svn = svn_ref[0].astype(BF16)
        acc_scr[0] = jnp.concatenate([_dot_nt(p2b[:half], svn), _dot_nt(p2b[half:], svn)], axis=0)
        for st in range(1, SAMPLE_STREAMS):
            m_scr[st] = jnp.full((rows8, 1), M_INIT, F32)
            l_scr[st] = jnp.zeros((rows8, 1), F32)
            acc_scr[st] = jnp.zeros((rows8, LANES), F32)

    def on_pages(step, slot):
        lhs_r = _stack_heads(nqr_ref)
        per = n_pages // SAMPLE_STREAMS
        msk = mask_scr[step]
        for st in range(SAMPLE_STREAMS):
            pages = range(st * per, (st + 1) * per)
            keys = slice(st * per * PAGE_SIZE, (st + 1) * per * PAGE_SIZE)
            kt = jnp.concatenate([kbuf[slot, j].astype(BF16) for j in pages], axis=1)
            vt = jnp.concatenate([vbuf[slot, j].astype(BF16) for j in pages], axis=1)
            sc = _dot(lhs_r, kt) * ATTN_SCALE
            sc = _mask_rows(sc, [msk[k][:, keys] > 0.5 for k in range(KV_HEADS)], t)
            m_i = m_scr[st]
            m_new = jnp.maximum(m_i, jnp.max(sc, axis=-1, keepdims=True))
            alpha = jnp.exp(m_i - m_new)
            pr = jnp.exp(sc - m_new)
            l_scr[st] = alpha * l_scr[st] + jnp.sum(pr, axis=-1, keepdims=True)
            m_scr[st] = m_new
            pb = pr.astype(BF16)
            pv = jnp.concatenate([_dot_nt(pb[:half], vt), _dot_nt(pb[half:], vt)], axis=0)
            acc_scr[st] = alpha * acc_scr[st] + pv

    before_pages()
    _stream_pages(pt_ref, (sk_hbm, sv_hbm), (kbuf, vbuf), sems, n_pages, n_steps, n_batch, on_pages)
    m_all = m_scr[0]
    for st in range(1, SAMPLE_STREAMS):
        m_all = jnp.maximum(m_all, m_scr[st])
    l_all = jnp.zeros((rows8, 1), F32)
    acc_all = jnp.zeros((rows8, LANES), F32)
    for st in range(SAMPLE_STREAMS):
        w = jnp.exp(m_scr[st] - m_all)
        l_all = l_all + w * l_scr[st]
        acc_all = acc_all + w * acc_scr[st]
    o_slc = _merge_kv_heads(acc_all / l_all, t)
    o_cmp = [ocmp_scr[g] for g in range(NSA_GROUP)]
    o_win = [owin_scr[g] for g in range(NSA_GROUP)]
    _gate_and_store(o_cmp, o_slc, o_win, ngl_ref, ng_ref, y_ref)


def _attn_sample(page_table, cache_sk_t, cache_sv_t, nq, nqr, ngl, ng, kc, vc, skn, svn, wko, wvo, wkn, wvn, t_len):
    db, n_pages = page_table.shape
    g = SAMPLE_PAGES_PER_STEP
    steps = n_pages // g
    past_len = n_pages * PAGE_SIZE
    step_keys = g * PAGE_SIZE
    span = LANES * SLC_BLOCK
    expand = (np.arange(span)[None, :] // SLC_BLOCK == np.arange(LANES)[:, None]).astype(BF16)

    hbm = pl.BlockSpec(memory_space=pl.ANY)
    row = lambda w: pl.BlockSpec((1, t_len, w), lambda b, pt: (b, 0, 0))
    per_b = lambda shape: pl.BlockSpec((1,) + shape, lambda b, pt: (b, 0, 0))
    rows8 = 2 * NSA_GROUP * t_len
    page_buf = pltpu.VMEM((PAGE_SLOTS, g, KV_WIDTH, PAGE_SIZE), F32)
    grid_spec = pltpu.PrefetchScalarGridSpec(
        num_scalar_prefetch=1,
        grid=(db,),
        in_specs=[hbm, hbm, row(512), row(512), row(LANES), row(512), per_b(kc.shape[1:]), per_b(vc.shape[1:]),
                  per_b(skn.shape[1:]), per_b(svn.shape[1:]), per_b(wko.shape[1:]), per_b(wvo.shape[1:]),
                  per_b(wkn.shape[1:]), per_b(wvn.shape[1:]), pl.BlockSpec(expand.shape, lambda b, pt: (0, 0))],
        out_specs=row(512),
        scratch_shapes=[page_buf, page_buf, pltpu.SemaphoreType.DMA((PAGE_SLOTS, 2)),
                        pltpu.VMEM((SAMPLE_STREAMS, rows8, 1), F32), pltpu.VMEM((SAMPLE_STREAMS, rows8, 1), F32),
                        pltpu.VMEM((SAMPLE_STREAMS, rows8, LANES), F32),
                        pltpu.VMEM((NSA_GROUP, t_len, LANES), F32), pltpu.VMEM((NSA_GROUP, t_len, LANES), F32),
                        pltpu.VMEM((steps, KV_HEADS, t_len, step_keys), F32)],
    )
    return pl.pallas_call(
        functools.partial(_attn_sample_kernel, n_pages=g, n_steps=steps, n_batch=db, t_len=t_len, past_len=past_len),
        grid_spec=grid_spec,
        out_shape=jax.ShapeDtypeStruct((db, t_len, 512), F32),
        compiler_params=_params(("arbitrary",)),
        name="attn_sample",
    )(page_table, cache_sk_t, cache_sv_t, nq, nqr, ngl, ng, kc, vc, skn, svn, wko, wvo, wkn, wvn, expand)


def _finish_kernel(x_ref, yr_ref, yn_ref, p_ref, wo_ref, gple_ref, wg_ref, wp_ref, gf_ref, o_ref):
    x = x_ref[...]
    x = x + _dot(yr_ref[...].astype(BF16), wo_ref[0:RET_WIDTH, :]) + _dot(yn_ref[...].astype(BF16), wo_ref[RET_WIDTH:, :])
    ms = jnp.mean(x * x, axis=-1, keepdims=True)
    hn = (x * lax.rsqrt(ms + RMS_EPS) * gple_ref[...]).astype(BF16)
    gate = _sigmoid(_dot(hn, wg_ref[...]))
    x = x + gate * _dot(p_ref[...].astype(BF16), wp_ref[...])
    ms = jnp.mean(x * x, axis=-1, keepdims=True)
    o_ref[...] = x * lax.rsqrt(ms + RMS_EPS) * gf_ref[...]


def _finish(x2d, y_ret, y_nsa, p2d, w_out, norm_ple, w_gate, w_ple, norm_f, tm):
    n = x2d.shape[0]
    row = lambda w: pl.BlockSpec((tm, w), lambda i: (i, 0))
    const = lambda shape: pl.BlockSpec(shape, lambda i: (0, 0))
    return pl.pallas_call(
        _finish_kernel,
        grid=(n // tm,),
        in_specs=[row(D_MODEL), row(RET_WIDTH), row(NSA_WIDTH), row(PLE_DIM), const(w_out.shape), const((1, D_MODEL)),
                  const(w_gate.shape), const(w_ple.shape), const((1, D_MODEL))],
        out_specs=row(D_MODEL),
        out_shape=jax.ShapeDtypeStruct((n, D_MODEL), F32),
        compiler_params=_params(("arbitrary",)),
        name="finish",
    )(x2d, y_ret, y_nsa, p2d, w_out, norm_ple, w_gate, w_ple, norm_f)


def _slab_perm():
    return np.array([(k * NSA_GROUP + g) * NSA_HD + d for g in range(NSA_GROUP) for k in range(KV_HEADS)
                     for d in range(NSA_HD)], np.int32)


def _to_positions_major(x_t):
    lead = x_t.shape[:-2]
    n = len(lead)
    x4 = x_t.reshape(lead + (KV_HEADS, NSA_HD, x_t.shape[-1]))
    return jnp.transpose(x4, tuple(range(n)) + (n + 2, n, n + 1))


def _cache_t(cache):
    n_pool, page = cache.shape[:2]
    return jnp.transpose(cache, (0, 2, 3, 1)).reshape(n_pool, KV_WIDTH, page)


def _layer(xp, xs, c_ck, c_cv, c_sk, c_sv, win_k, win_v, ret_state, page_table, p_p, p_s, norm_mix, w_in, gn_g, gn_b,
           pe_k, w1_k, w2_k, pe_v, w1_v, w2_v, w_out, norm_ple, w_gate, w_ple, norm_f):
    b, s, d = xp.shape
    db, t, _ = xs.shape
    n_pages = page_table.shape[1]
    past = n_pages * PAGE_SIZE

    off = np.cumsum((0,) + SPLIT_SIZES)
    w_in_t = w_in.T
    col = lambda i: w_in[:, off[i]:off[i + 1]]
    rows = lambda i: w_in_t[off[i]:off[i + 1]]
    perm = _slab_perm()
    ngl_rows = jnp.pad(rows(11), ((0, LANES - N_GATES), (0, 0)))
    ret_rows = [rows(0), rows(1), rows(2), rows(3)]
    kv_rows = [rows(5), rows(6), rows(8), rows(10), rows(7), rows(9)]
    w_rm_p = jnp.concatenate([col(0), col(1), col(2), col(3), col(7), col(9)], axis=1).astype(BF16)
    w_t_p = jnp.concatenate(kv_rows + [rows(4), rows(12), ngl_rows], axis=0).astype(BF16)
    w_rm_s = jnp.concatenate(ret_rows + [rows(4)[perm], rows(12)[perm], ngl_rows], axis=0).astype(BF16)
    w_t_s = jnp.concatenate(kv_rows, axis=0).astype(BF16)
    w_out_b = w_out.astype(BF16)
    w_out_slab = jnp.concatenate([w_out[:RET_WIDTH], w_out[RET_WIDTH:][perm]], axis=0).astype(BF16)
    g_mix = norm_mix.reshape(1, d)
    cwk = _compress_weights(pe_k, w1_k, w2_k)
    cwv = _compress_weights(pe_v, w1_v, w2_v)
    cwv_t = _compress_weights(pe_v, w1_v, w2_v, transpose_w2=True)
    gn_g2, gn_b2 = gn_g.reshape(1, RET_WIDTH), gn_b.reshape(1, RET_WIDTH)
    fin_w = (norm_ple.reshape(1, d), w_gate.astype(BF16), w_ple.astype(BF16), norm_f.reshape(1, d))

    tm = 512
    (q_ret, k_ret, v_ret, rg, sk_rm, wk_rm, ck_t, cv_t, sv_t, wv_t, sk_t, wk_t, sv_b, wv_b, nq_t, nqr_t, ng_t,
     ngl_t) = _proj(xp.reshape(b * s, d), np.arange(s), g_mix, w_rm_p, w_t_p, b, tm, True)
    y_ret, ret_p = _retention(q_ret, k_ret, v_ret, rg, jnp.zeros((b, RET_HEADS, RET_DK, RET_DK), F32), gn_g2, gn_b2,
                              b, s // RET_CHUNK, RET_CHUNK)
    kc, vc_t = _compress_prompt(ck_t, cv_t, cwk, cwv_t)
    y_nsa = _attn_prompt(nq_t, nqr_t, ngl_t, ng_t, kc, vc_t, sk_rm, sv_b, wk_rm, wv_b)
    y_prompt = _finish(xp.reshape(b * s, d), y_ret, y_nsa, p_p.reshape(b * s, PLE_DIM), w_out_b, *fin_w, tm)
    wb_p = min(WINDOW, s)
    prompt_states = (ret_p, _to_positions_major(ck_t), _to_positions_major(cv_t), _to_positions_major(sk_t),
                     _to_positions_major(sv_t), _to_positions_major(wk_t[:, :, s - wb_p:]),
                     _to_positions_major(wv_t[:, :, s - wb_p:]))

    n = db * t
    pos_s = np.tile(past + np.arange(t), db)
    (q_ret, k_ret, v_ret, rg, nq, nqr, ng, ngl, ck_n, cv_n, sv_n, wv_n, sk_n, wk_n) = _proj(
        xs.reshape(n, d), pos_s, g_mix, w_rm_s, w_t_s, 1, n, False)
    y_ret, ret_s = _retention(q_ret, k_ret, v_ret, rg, ret_state, gn_g2, gn_b2, db, 1, t)
    kc, vc = _compress_sample(page_table, _cache_t(c_ck), _cache_t(c_cv), cwk, cwv)
    per_batch = lambda a: jnp.transpose(a[0].reshape(KV_WIDTH, db, t), (1, 0, 2))
    lane_pad = lambda a: jnp.pad(per_batch(a), ((0, 0), (0, 0), (0, LANES - t)))
    win_t = lambda w: jnp.transpose(w, (0, 2, 3, 1)).reshape(db, KV_WIDTH, w.shape[1])
    rows3 = lambda a: a.astype(F32).reshape(db, t, a.shape[-1])
    y_nsa = _attn_sample(page_table, _cache_t(c_sk), _cache_t(c_sv), rows3(nq), rows3(nqr), rows3(ngl), rows3(ng), kc, vc,
                         lane_pad(sk_n), lane_pad(sv_n), win_t(win_k), win_t(win_v), lane_pad(wk_n), lane_pad(wv_n), t)
    y_nsa = y_nsa.reshape(n, NSA_WIDTH)
    y_sample = _finish(xs.reshape(n, d), y_ret, y_nsa, p_s.reshape(n, PLE_DIM), w_out_slab, *fin_w, n)
    new_rows = lambda a: _to_positions_major(per_batch(a))
    sample_states = (ret_s, new_rows(ck_n), new_rows(cv_n), new_rows(sk_n), new_rows(sv_n),
                     jnp.concatenate([win_k[:, t:], new_rows(wk_n)], axis=1),
                     jnp.concatenate([win_v[:, t:], new_rows(wv_n)], axis=1))
    return y_prompt.reshape(b, s, d), y_sample.reshape(db, t, d), prompt_states, sample_states


def kernel(x_prompt, x_sample, cache_cmp_k, cache_cmp_v, cache_slc_k, cache_slc_v, state_win_k, state_win_v, state_ret, page_table, p_prompt, p_sample, norm_mix, w_in, ret_gn_g, ret_gn_b, cmp_pe_k, cmp_w1_k, cmp_w2_k, cmp_pe_v, cmp_w1_v, cmp_w2_v, w_out, norm_ple, w_ple_gate, w_ple, norm_f):
    depth = w_in.shape[0]
    assert depth == 1, "single trunk layer"
    l = 0
    yp, ys, sp, ss = _layer(x_prompt, x_sample, cache_cmp_k[l], cache_cmp_v[l], cache_slc_k[l], cache_slc_v[l],
                            state_win_k[l], state_win_v[l], state_ret[l], page_table, p_prompt[l], p_sample[l],
                            norm_mix[l], w_in[l], ret_gn_g[l], ret_gn_b[l], cmp_pe_k[l], cmp_w1_k[l], cmp_w2_k[l],
                            cmp_pe_v[l], cmp_w1_v[l], cmp_w2_v[l], w_out[l], norm_ple[l], w_ple_gate[l], w_ple[l], norm_f)
    return (yp, ys) + tuple(a[None] for a in sp) + tuple(a[None] for a in ss)
```

```python
import functools

import numpy as np
import jax
import jax.numpy as jnp
from jax import lax
from jax.experimental import pallas as pl
from jax.experimental.pallas import tpu as pltpu

F32 = jnp.float32
BF16 = jnp.bfloat16

D_MODEL = 1024
PLE_DIM = 256
PAGE_SIZE = 128
RET_HEADS = 4
RET_DK = 128
RET_WIDTH = 512
RET_CHUNK = 128
NSA_HEADS = 8
KV_HEADS = 2
NSA_GROUP = 4
NSA_HD = 64
NSA_WIDTH = 512
KV_WIDTH = 128
CMP_BLOCK = 32
SLC_BLOCK = 64
TOP_N = 16
WINDOW = 512
ROPE_THETA = 10000.0
RMS_EPS = 1e-6
GN_EPS = 1e-5
NEG_INF = -1e9
FORCED_SCORE = 1e4
M_INIT = -1e30
CAP_OPEN = 3e38
ATTN_SCALE = NSA_HD ** -0.5
SPLIT_SIZES = (512, 512, 512, 512, 512, 128, 128, 128, 128, 128, 128, 24, 512)
N_GATES = 3 * NSA_HEADS

LANES = 128
SUBLANES = 8
PROMPT_Q_TILE = 128
SLC_KEY_TILE = 1024
RET_CHUNKS_PER_STEP = 8
PROMPT_ROW_TILE = 512
FINISH_ROW_TILE = 512
SAMPLE_PAGES_PER_STEP = 64
SAMPLE_STREAMS = 1
COMPRESS_PAGES_PER_STEP = 32
PAGE_SLOTS = 3
BLOCK_PITCH = 40
VMEM_LIMIT = 56 * 1024 * 1024


def _dot(a, b):
    return jnp.dot(a, b, preferred_element_type=F32)


def _dot_nt(a, b):
    return lax.dot_general(a, b, (((1,), (1,)), ((), ())), preferred_element_type=F32)


def _sigmoid(x):
    return 1.0 / (1.0 + jnp.exp(-x))


def _params(sem):
    return pltpu.CompilerParams(dimension_semantics=sem, vmem_limit_bytes=VMEM_LIMIT)


def _rope_lanes(a, cos, sin_signed, half):
    if 2 * half == LANES:
        partner = pltpu.roll(a, half, 1)
    else:
        lane = lax.broadcasted_iota(jnp.int32, (1, LANES), 1)
        partner = jnp.where((lane % (2 * half)) < half, pltpu.roll(a, LANES - half, 1), pltpu.roll(a, half, 1))
    return a * cos + partner * sin_signed


def _rope_sublanes(blk, cos_t, sin_t):
    outs = []
    for h in range(blk.shape[0] // NSA_HD):
        x1 = blk[h * NSA_HD:h * NSA_HD + NSA_HD // 2]
        x2 = blk[h * NSA_HD + NSA_HD // 2:(h + 1) * NSA_HD]
        outs += [x1 * cos_t - x2 * sin_t, x2 * cos_t + x1 * sin_t]
    return jnp.concatenate(outs, axis=0)


def _proj_kernel(*refs, prompt):
    (x_ref, g_ref, wrm_ref, wt_ref, cosr_ref, sinr_ref, cosn_ref, sinn_ref, cost_ref, sint_ref,
     qret_ref, kret_ref, vret_ref, rg_ref) = refs[:14]
    x = x_ref[...]
    ms = jnp.mean(x * x, axis=-1, keepdims=True)
    h = (x * lax.rsqrt(ms + RMS_EPS) * g_ref[...]).astype(BF16)

    def proj(lo, hi):
        return _dot(h, wrm_ref[:, lo:hi]) if prompt else _dot_nt(h, wrm_ref[lo:hi, :])

    cr, sr = cosr_ref[...], sinr_ref[...]
    cn, sn = cosn_ref[...], sinn_ref[...]
    ct, st = cost_ref[...], sint_ref[...]
    yq = proj(0, 512)
    yk = proj(512, 1024)
    for hh in range(RET_HEADS):
        sl = slice(hh * LANES, (hh + 1) * LANES)
        qret_ref[:, sl] = _rope_lanes(yq[:, sl], cr, sr, RET_DK // 2)
        kret_ref[:, sl] = _rope_lanes(yk[:, sl], cr, sr, RET_DK // 2) * (RET_DK ** -0.5)
    vret_ref[...] = proj(1024, 1536)
    rg_ref[...] = proj(1536, 2048)

    yt = _dot_nt(wt_ref[...], h)
    if prompt:
        (skr_ref, wkr_ref, ck_ref, cv_ref, sv_ref, wv_ref, sk_ref, wk_ref, svb_ref, wvb_ref,
         nqt_ref, nqrt_ref, ngt_ref, nglt_ref) = refs[14:]
    else:
        nq_ref, nqr_ref, ng_ref, ngl_ref, ck_ref, cv_ref, sv_ref, wv_ref, sk_ref, wk_ref = refs[14:]
    ck_ref[0] = yt[0:128]
    cv_ref[0] = yt[128:256]
    sv_ref[0] = yt[256:384]
    wv_ref[0] = yt[384:512]
    sk_ref[0] = _rope_sublanes(yt[512:640], ct, st)
    wk_ref[0] = _rope_sublanes(yt[640:768], ct, st)
    if prompt:
        svb_ref[0] = yt[256:384].astype(BF16)
        wvb_ref[0] = yt[384:512].astype(BF16)
        nqt = yt[768:1280]
        nqt_ref[0] = nqt.astype(BF16)
        nqrt_ref[0] = _rope_sublanes(nqt, ct, st).astype(BF16)
        ngt_ref[0] = yt[1280:1792]
        nglt_ref[0] = yt[1792:1920]
        skr_ref[...] = _rope_lanes(proj(2048, 2176), cn, sn, NSA_HD // 2).astype(BF16)
        wkr_ref[...] = _rope_lanes(proj(2176, 2304), cn, sn, NSA_HD // 2).astype(BF16)
    else:
        yn = proj(2048, 2560)
        for gg in range(NSA_GROUP):
            sl = slice(gg * LANES, (gg + 1) * LANES)
            nq_ref[:, sl] = yn[:, sl].astype(BF16)
            nqr_ref[:, sl] = _rope_lanes(yn[:, sl], cn, sn, NSA_HD // 2).astype(BF16)
        ng_ref[...] = proj(2560, 3072)
        ngl_ref[...] = proj(3072, 3200)


def _rope_tables(pos, half):
    inv = ROPE_THETA ** (-np.arange(half, dtype=np.float64) / half)
    ang = np.asarray(pos, np.float64)[:, None] * inv[None, :]
    return np.cos(ang).astype(np.float32), np.sin(ang).astype(np.float32)


def _proj(x2d, pos_rows, norm_g, w_rm, w_t, kv_batch, tm, prompt):
    n = x2d.shape[0]
    nt = pos_rows.shape[0] // tm
    skv = n // kv_batch
    nkt = skv // tm
    c64, s64 = _rope_tables(pos_rows, 64)
    c32, s32 = _rope_tables(pos_rows, 32)
    cosr = np.concatenate([c64, c64], axis=1)
    sinr = np.concatenate([-s64, s64], axis=1)
    cosn = np.concatenate([c32, c32, c32, c32], axis=1)
    sinn = np.concatenate([-s32, s32, -s32, s32], axis=1)
    cost, sint = np.ascontiguousarray(c32.T), np.ascontiguousarray(s32.T)
    row = lambda w: pl.BlockSpec((tm, w), lambda i: (i, 0))
    tab = pl.BlockSpec((tm, LANES), lambda i: (i % nt, 0))
    tabt = pl.BlockSpec((32, tm), lambda i: (0, i % nt))
    tspec = lambda r: pl.BlockSpec((1, r, tm), lambda i: (i // nkt, 0, i % nkt))
    f = lambda w, dt: jax.ShapeDtypeStruct((n, w), dt)
    ts = lambda r, dt: jax.ShapeDtypeStruct((kv_batch, r, skv), dt)
    kvo, kvs = tspec(KV_WIDTH), ts(KV_WIDTH, F32)
    out_specs = [row(512)] * 4
    out_shape = [f(512, F32)] * 4
    if prompt:
        out_specs += [row(KV_WIDTH)] * 2 + [kvo] * 8 + [tspec(512)] * 3 + [tspec(LANES)]
        out_shape += [f(KV_WIDTH, BF16)] * 2 + [kvs] * 6 + [ts(KV_WIDTH, BF16)] * 2 + [ts(512, BF16)] * 2 + [
            ts(512, F32), ts(LANES, F32)]
    else:
        out_specs += [row(512)] * 3 + [row(LANES)] + [kvo] * 6
        out_shape += [f(512, BF16)] * 2 + [f(512, F32), f(LANES, F32)] + [kvs] * 6
    return pl.pallas_call(
        functools.partial(_proj_kernel, prompt=prompt),
        grid=(n // tm,),
        in_specs=[row(D_MODEL), pl.BlockSpec((1, D_MODEL), lambda i: (0, 0)),
                  pl.BlockSpec(w_rm.shape, lambda i: (0, 0)), pl.BlockSpec(w_t.shape, lambda i: (0, 0)),
                  tab, tab, tab, tab, tabt, tabt],
        out_specs=out_specs,
        out_shape=out_shape,
        compiler_params=_params(("arbitrary",)),
        name="proj_prompt" if prompt else "proj_sample",
    )(x2d, norm_g, w_rm, w_t, cosr, sinr, cosn, sinn, cost, sint)


def _ret_kernel(q_ref, k_ref, v_ref, rg_ref, st_ref, dm_ref, qd_ref, kd_ref, sd_ref, gg_ref, gb_ref,
                y_ref, so_ref, st_scr, *, n_chunks):
    c = pl.program_id(1)

    @pl.when(c == 0)
    def _():
        st_scr[...] = st_ref[0]

    for h in range(RET_HEADS):
        sl = slice(h * LANES, (h + 1) * LANES)
        state = st_scr[h]
        for cc in range(q_ref.shape[0] // RET_CHUNK):
            rows = slice(cc * RET_CHUNK, (cc + 1) * RET_CHUNK)
            q = q_ref[rows, sl]
            k = k_ref[rows, sl]
            v = v_ref[rows, sl].astype(BF16)
            s = _dot_nt(q.astype(BF16), k.astype(BF16)) * dm_ref[h]
            intra = _dot(s.astype(BF16), v)
            cross = _dot((q * qd_ref[:, sl]).astype(BF16), state.astype(BF16))
            o = intra + cross
            kd_t = (k * kd_ref[:, sl]).T.astype(BF16)
            state = sd_ref[h] * state + _dot(kd_t, v)
            mu = jnp.mean(o, axis=-1, keepdims=True)
            var = jnp.mean(jnp.square(o - mu), axis=-1, keepdims=True)
            on = (o - mu) * lax.rsqrt(var + GN_EPS)
            gate = rg_ref[rows, sl]
            y_ref[rows, sl] = (on * gg_ref[:, sl] + gb_ref[:, sl]) * (gate * _sigmoid(gate))
        st_scr[h] = state

    @pl.when(c == n_chunks - 1)
    def _():
        so_ref[0] = st_scr[...]


def _retention(q, k, v, rg, state, gn_g, gn_b, n_batch, n_chunks, chunk_len):
    c = RET_CHUNK
    log_g = np.log(1.0 - 2.0 ** (-5.0 - np.arange(RET_HEADS, dtype=np.float64)))
    i = np.arange(c, dtype=np.float64)
    diff = i[:, None] - i[None, :]
    causal = diff >= 0
    f32 = lambda a: np.ascontiguousarray(a, dtype=np.float32)
    dmask = f32(np.where(causal[None], np.exp(np.where(causal, diff, 0.0)[None] * log_g[:, None, None]), 0.0))
    expand = lambda t: f32(np.repeat(t, LANES, axis=1))
    qdec = expand(np.exp((i[:, None] + 1.0) * log_g[None, :]))
    kdec = expand(np.exp((chunk_len - 1.0 - i)[:, None] * log_g[None, :]))
    sdec = f32(np.broadcast_to(np.exp(chunk_len * log_g)[:, None, None], (RET_HEADS, 1, LANES)))
    stspec1 = pl.BlockSpec((1, RET_HEADS, 128, 128), lambda b: (b, 0, 0, 0))
    const1 = lambda shape: pl.BlockSpec(shape, lambda b: (0,) * len(shape))
    if chunk_len < c:
        assert n_chunks == 1
        row1 = pl.BlockSpec((chunk_len, 512), lambda b: (b, 0))
        return pl.pallas_call(
            _ret_short_kernel,
            grid=(n_batch,),
            in_specs=[row1, row1, row1, row1, stspec1, const1((RET_HEADS, c, c)), const1((c, 512)), const1((c, 512)),
                      const1((RET_HEADS, 1, LANES)), const1((1, 512)), const1((1, 512))],
            out_specs=[row1, stspec1],
            out_shape=[jax.ShapeDtypeStruct(q.shape, F32), jax.ShapeDtypeStruct(state.shape, F32)],
            scratch_shapes=[pltpu.VMEM((c, 512), F32)] * 3,
            compiler_params=_params(("arbitrary",)),
            name="retention_short",
        )(q, k, v, rg, state, dmask, qdec, kdec, sdec, gn_g, gn_b)
    per_step = RET_CHUNKS_PER_STEP if n_chunks % RET_CHUNKS_PER_STEP == 0 else 1
    n_steps = n_chunks // per_step
    row = pl.BlockSpec((per_step * c, 512), lambda b, j: (b * n_steps + j, 0))
    stspec = pl.BlockSpec((1, RET_HEADS, 128, 128), lambda b, j: (b, 0, 0, 0))
    const = lambda shape: pl.BlockSpec(shape, lambda b, j: (0,) * len(shape))
    return pl.pallas_call(
        functools.partial(_ret_kernel, n_chunks=n_steps),
        grid=(n_batch, n_steps),
        in_specs=[row, row, row, row, stspec, const((RET_HEADS, c, c)), const((c, 512)), const((c, 512)),
                  const((RET_HEADS, 1, LANES)), const((1, 512)), const((1, 512))],
        out_specs=[row, stspec],
        out_shape=[jax.ShapeDtypeStruct(q.shape, F32), jax.ShapeDtypeStruct(state.shape, F32)],
        scratch_shapes=[pltpu.VMEM((RET_HEADS, 128, 128), F32)],
        compiler_params=_params(("arbitrary", "arbitrary")),
        name="retention",
    )(q, k, v, rg, state, dmask, qdec, kdec, sdec, gn_g, gn_b)


def _ret_short_kernel(q_ref, k_ref, v_ref, rg_ref, st_ref, dm_ref, qd_ref, kd_ref, sd_ref, gg_ref, gb_ref,
                      y_ref, so_ref, k_pad, kd_pad, v_pad):
    t = q_ref.shape[0]

    @pl.when(pl.program_id(0) == 0)
    def _():
        for pad in (k_pad, kd_pad, v_pad):
            pad[...] = jnp.zeros(pad.shape, F32)

    k_pad[0:t, :] = k_ref[...]
    kd_pad[0:t, :] = k_ref[...] * kd_ref[0:t, :]
    v_pad[0:t, :] = v_ref[...]
    for h in range(RET_HEADS):
        sl = slice(h * LANES, (h + 1) * LANES)
        q = q_ref[:, sl]
        v = v_pad[:, sl].astype(BF16)
        state = st_ref[0, h]
        s = _dot_nt(q.astype(BF16), k_pad[:, sl].astype(BF16)) * dm_ref[h, 0:t, :]
        intra = _dot(s.astype(BF16), v)
        cross = _dot((q * qd_ref[0:t, sl]).astype(BF16), state.astype(BF16))
        o = intra + cross
        so_ref[0, h] = sd_ref[h] * state + _dot(kd_pad[:, sl].T.astype(BF16), v)
        mu = jnp.mean(o, axis=-1, keepdims=True)
        var = jnp.mean(jnp.square(o - mu), axis=-1, keepdims=True)
        on = (o - mu) * lax.rsqrt(var + GN_EPS)
        gate = rg_ref[:, sl]
        y_ref[:, sl] = (on * gg_ref[:, sl] + gb_ref[:, sl]) * (gate * _sigmoid(gate))


def _gelu_tanh(x):
    return 0.5 * x * (1.0 + jnp.tanh(np.sqrt(2.0 / np.pi).astype(np.float32) * (x + 0.044715 * (x * x * x))))


def _rows_to_scratch(tiles, scr):
    per_tile = LANES // CMP_BLOCK
    for t, tile in enumerate(tiles):
        rows_pm = tile.T
        for c in range(per_tile):
            r0 = (t * per_tile + c) * BLOCK_PITCH
            scr[r0:r0 + CMP_BLOCK, :] = rows_pm[c * CMP_BLOCK:(c + 1) * CMP_BLOCK]


def _compress_scratch(scr, n_blk, pe_ref, w1_ref, w2_ref, transpose_out=False):
    flat = [(scr[pl.ds(j, n_blk, stride=BLOCK_PITCH), :] + pe_ref[j:j + 1, :]).astype(BF16) for j in range(CMP_BLOCK)]
    hid = _gelu_tanh(_dot(jnp.concatenate(flat, axis=1), w1_ref[...]))
    if transpose_out:
        return _dot_nt(w2_ref[...], hid.astype(BF16))
    return _dot(hid.astype(BF16), w2_ref[...])


def _compress_one(tiles, pe_ref, w1_ref, w2_ref, scr, transpose_out=False):
    _rows_to_scratch(tiles, scr)
    return _compress_scratch(scr, len(tiles) * (LANES // CMP_BLOCK), pe_ref, w1_ref, w2_ref, transpose_out)


def _compress_prompt_kernel(k_ref, v_ref, pek_ref, w1k_ref, w2k_ref, pev_ref, w1v_ref, w2v_ref,
                            kc_ref, vct_ref, scr_k, scr_v, *, n_tiles):
    for src, pe, w1, w2, dst, scr, tr in ((k_ref, pek_ref, w1k_ref, w2k_ref, kc_ref, scr_k, False),
                                          (v_ref, pev_ref, w1v_ref, w2v_ref, vct_ref, scr_v, True)):
        tiles = [src[0, :, t * LANES:(t + 1) * LANES] for t in range(n_tiles)]
        dst[0] = _compress_one(tiles, pe, w1, w2, scr, tr)


def _compress_sample_kernel(pt_ref, ck_hbm, cv_hbm, pek_ref, w1k_ref, w2k_ref, pev_ref, w1v_ref, w2v_ref,
                            kc_ref, vc_ref, kbuf, vbuf, sems, scr_k, scr_v, *, n_pages, n_steps, n_batch):
    n_blk = n_pages * PAGE_SIZE // CMP_BLOCK

    def to_rows(step, slot):
        _rows_to_scratch([kbuf[slot, j] for j in range(n_pages)], scr_k.at[step % 2])
        _rows_to_scratch([vbuf[slot, j] for j in range(n_pages)], scr_v.at[step % 2])

    def compress(step):
        rows = pl.ds(pl.multiple_of(step * n_blk, n_blk), n_blk)
        kc_ref[0, rows, :] = _compress_scratch(scr_k.at[step % 2], n_blk, pek_ref, w1k_ref, w2k_ref)
        vc_ref[0, rows, :] = _compress_scratch(scr_v.at[step % 2], n_blk, pev_ref, w1v_ref, w2v_ref)

    _stream_pages(pt_ref, (ck_hbm, cv_hbm), (kbuf, vbuf), sems, n_pages, n_steps, n_batch, to_rows, compress)


def _compress_weights(pe, w1, w2, transpose_w2=False):
    pe_rows = pe.reshape(CMP_BLOCK, KV_WIDTH)
    z = jnp.zeros((CMP_BLOCK, NSA_HD, NSA_HD), F32)
    w1bd = jnp.concatenate([jnp.concatenate([w1[0], z], axis=2), jnp.concatenate([z, w1[1]], axis=2)], axis=1)
    z2 = jnp.zeros((NSA_HD, NSA_HD), F32)
    w2bd = jnp.concatenate([jnp.concatenate([w2[0], z2], axis=1), jnp.concatenate([z2, w2[1]], axis=1)], axis=0)
    if transpose_w2:
        w2bd = w2bd.T
    return pe_rows, w1bd.reshape(CMP_BLOCK * KV_WIDTH, KV_WIDTH).astype(BF16), w2bd.astype(BF16)


def _cw_specs():
    zero = lambda *a: (0, 0)
    return [pl.BlockSpec((CMP_BLOCK, KV_WIDTH), zero), pl.BlockSpec((CMP_BLOCK * KV_WIDTH, KV_WIDTH), zero),
            pl.BlockSpec((KV_WIDTH, KV_WIDTH), zero)]


def _compress_prompt(ck_t, cv_t, cwk, cwv_t):
    b, _, s = ck_t.shape
    n_blk = s // CMP_BLOCK
    src = pl.BlockSpec((1, KV_WIDTH, s), lambda i: (i, 0, 0))
    return pl.pallas_call(
        functools.partial(_compress_prompt_kernel, n_tiles=s // LANES),
        grid=(b,),
        in_specs=[src, src] + _cw_specs() + _cw_specs(),
        out_specs=[pl.BlockSpec((1, n_blk, KV_WIDTH), lambda i: (i, 0, 0)),
                   pl.BlockSpec((1, KV_WIDTH, n_blk), lambda i: (i, 0, 0))],
        out_shape=[jax.ShapeDtypeStruct((b, n_blk, KV_WIDTH), F32), jax.ShapeDtypeStruct((b, KV_WIDTH, n_blk), F32)],
        scratch_shapes=[pltpu.VMEM((n_blk * BLOCK_PITCH, KV_WIDTH), F32)] * 2,
        compiler_params=_params(("arbitrary",)),
        name="compress_prompt",
    )(ck_t, cv_t, *cwk, *cwv_t)


def _compress_sample(page_table, cache_k_t, cache_v_t, cwk, cwv):
    db, n_pages = page_table.shape
    g = COMPRESS_PAGES_PER_STEP
    steps = n_pages // g
    blk_per_step = g * PAGE_SIZE // CMP_BLOCK
    n_blk = n_pages * PAGE_SIZE // CMP_BLOCK

    hbm = pl.BlockSpec(memory_space=pl.ANY)
    dst = pl.BlockSpec((1, n_blk, KV_WIDTH), lambda b, pt: (b, 0, 0))
    shp = jax.ShapeDtypeStruct((db, n_blk, KV_WIDTH), F32)
    page_buf = pltpu.VMEM((PAGE_SLOTS, g, KV_WIDTH, PAGE_SIZE), F32)
    grid_spec = pltpu.PrefetchScalarGridSpec(
        num_scalar_prefetch=1,
        grid=(db,),
        in_specs=[hbm, hbm] + _cw_specs() + _cw_specs(),
        out_specs=[dst, dst],
        scratch_shapes=[page_buf, page_buf, pltpu.SemaphoreType.DMA((PAGE_SLOTS, 2))]
        + [pltpu.VMEM((2, blk_per_step * BLOCK_PITCH, KV_WIDTH), F32)] * 2,
    )
    return pl.pallas_call(
        functools.partial(_compress_sample_kernel, n_pages=g, n_steps=steps, n_batch=db),
        grid_spec=grid_spec,
        out_shape=[shp, shp],
        compiler_params=_params(("arbitrary",)),
        name="compress_sample",
    )(page_table, cache_k_t, cache_v_t, *cwk, *cwv)


def _stack_heads_t(qt_ref):
    nq = qt_ref.shape[2]
    zero = jnp.zeros((NSA_HD, nq), BF16)
    cols = []
    for k in range(KV_HEADS):
        for g in range(NSA_GROUP):
            h = k * NSA_GROUP + g
            tile = qt_ref[0, h * NSA_HD:(h + 1) * NSA_HD, :] * ATTN_SCALE
            cols.append(jnp.concatenate([tile, zero] if k == 0 else [zero, tile], axis=0))
    return jnp.concatenate(cols, axis=1)


def _rank_select_rows(score, n_blk):
    parts = [score[v * SUBLANES:(v + 1) * SUBLANES] for v in range(n_blk // SUBLANES)]
    ranks = [jnp.zeros(p.shape, jnp.int32) for p in parts]
    for j in range(n_blk):
        col = score[j:j + 1, :]
        for v, part in enumerate(parts):
            ge, gt = (col >= part).astype(jnp.int32), (col > part).astype(jnp.int32)
            if v * SUBLANES > j:
                beats = ge
            elif (v + 1) * SUBLANES - 1 <= j:
                beats = gt
            else:
                row = v * SUBLANES + lax.broadcasted_iota(jnp.int32, (SUBLANES, 1), 0)
                beats = jnp.where(row > j, ge, gt)
            ranks[v] = ranks[v] + beats
    return jnp.concatenate([(r < TOP_N).astype(F32) for r in ranks], axis=0)


def _softmax_tile_t(s, v_t, m_ref, acc_ref, s_max=None):
    half = s.shape[1] // KV_HEADS
    m_old = m_ref[...]
    m_new = jnp.maximum(m_old, jnp.max(s, axis=0, keepdims=True) if s_max is None else s_max)
    alpha = jnp.exp(m_old - m_new)
    pb = jnp.exp(s - m_new).astype(BF16)
    m_ref[...] = m_new
    ones = jnp.ones((NSA_HD, v_t.shape[1]), BF16)
    v0 = jnp.concatenate([v_t[:NSA_HD], ones], axis=0)
    v1 = jnp.concatenate([ones, v_t[NSA_HD:]], axis=0)
    pv = jnp.concatenate([_dot(v0, pb[:, :half]), _dot(v1, pb[:, half:])], axis=1)
    acc_ref[...] = alpha * acc_ref[...] + pv


def _normalized_head_t(acc, k, cols):
    den = (1 - k) * NSA_HD
    return acc[k * NSA_HD:(k + 1) * NSA_HD, cols] / acc[den:den + 1, cols]


def _attn_prompt_kernel(nqt_ref, nqrt_ref, nglt_ref, ngt_ref, kc_ref, vct_ref, sk_ref, svt_ref, wk_ref, wvt_ref,
                        et_ref, ewin_ref, y_ref, lhs_scr, qslc_scr, qwin_scr, ocmp_scr, t_scr, m_scr, acc_scr,
                        s_scr, smax_scr, sw_scr):
    i = pl.program_id(1)
    qb = y_ref.shape[0]
    q0 = i * qb
    cols8 = 2 * NSA_GROUP * qb
    qpos = q0 + lax.broadcasted_iota(jnp.int32, (1, qb), 1)
    lhs_scr[...] = _stack_heads_t(nqt_ref)
    q_rot = _stack_heads_t(nqrt_ref)
    qslc_scr[0:KV_WIDTH, :] = q_rot
    qwin_scr[0:KV_WIDTH, :] = q_rot
    def causal_cap():
        own_keys = q0 + lax.broadcasted_iota(jnp.int32, (qb, 1), 0)
        return jnp.concatenate([jnp.where(own_keys <= qpos, CAP_OPEN, NEG_INF)] * (2 * NSA_GROUP), axis=1)

    n_cmp = kc_ref.shape[1]
    s_all = _dot(kc_ref[0].astype(BF16), lhs_scr[...])
    cend = (lax.broadcasted_iota(jnp.int32, (n_cmp, 1), 0) + 1) * CMP_BLOCK - 1
    cmask = cend <= qpos
    vct = vct_ref[0].astype(BF16)
    imps = []
    for k in range(KV_HEADS):
        imp = None
        for g in range(NSA_GROUP):
            c0 = (k * NSA_GROUP + g) * qb
            s = jnp.where(cmask, s_all[:, c0:c0 + qb], NEG_INF)
            p = jnp.exp(s - jnp.max(s, axis=0, keepdims=True))
            p = p / jnp.sum(p, axis=0, keepdims=True) * cmask.astype(F32)
            imp = p if imp is None else imp + p
            ocmp_scr[:, c0:c0 + qb] = _dot(vct, p.astype(BF16))
        imps.append(imp)

    n_slc = n_cmp // 2
    blk = lax.broadcasted_iota(jnp.int32, (n_slc, 1), 0)
    valid = blk * SLC_BLOCK <= qpos

    def store_bias(k, sel):
        bias = jnp.concatenate([(sel - 1.0) * (-NEG_INF), jnp.zeros((LANES - n_slc, qb), F32)], axis=0).astype(BF16)
        c0 = k * NSA_GROUP * qb
        qslc_scr[KV_WIDTH:, c0:c0 + NSA_GROUP * qb] = jnp.concatenate([bias] * NSA_GROUP, axis=1)

    @pl.when(q0 + qb <= TOP_N * SLC_BLOCK)
    def _():
        for k in range(KV_HEADS):
            store_bias(k, valid.astype(F32))

    @pl.when(q0 + qb > TOP_N * SLC_BLOCK)
    def _():
        forced = (blk == 0) | (blk == qpos // SLC_BLOCK)
        for k in range(KV_HEADS):
            pairs = []
            for u in range(qb // LANES):
                t_scr[u] = imps[k][:, u * LANES:(u + 1) * LANES]
                pairs.append(t_scr[u, pl.ds(0, n_slc, stride=2), :] + t_scr[u, pl.ds(1, n_slc, stride=2), :])
            pair = jnp.concatenate(pairs, axis=1)
            score = jnp.where(forced, FORCED_SCORE, jnp.where(valid, pair, NEG_INF))
            store_bias(k, _rank_select_rows(score, n_slc))

    tk = SLC_KEY_TILE
    m_scr[...] = jnp.full(m_scr.shape, M_INIT, F32)
    acc_scr[...] = jnp.zeros(acc_scr.shape, F32)

    def scores(t):
        k0 = pl.multiple_of(t * tk, tk)
        keys = jnp.concatenate([sk_ref[pl.ds(k0, tk), :], et_ref[pl.ds(k0, tk), :]], axis=1)
        s = _dot(keys, qslc_scr[...])
        s_scr[t % 2] = s
        smax_scr[t % 2] = jnp.max(s, axis=0, keepdims=True)

    def update(t):
        k0 = pl.multiple_of(t * tk, tk)
        _softmax_tile_t(s_scr[t % 2], svt_ref[0, :, pl.ds(k0, tk)], m_scr.at[0], acc_scr.at[0], smax_scr[t % 2])

    def slc_step(t, carry):
        scores(t + 1)
        update(t)
        return carry

    n_full = q0 // tk

    @pl.when(n_full >= 1)
    def _():
        scores(0)
        lax.fori_loop(0, n_full - 1, slc_step, 0)
        update(n_full - 1)

    k_rem = pl.multiple_of(n_full * tk, tk)
    for v in range(tk // qb):
        @pl.when((q0 - k_rem) // qb == v)
        def _():
            r = (v + 1) * qb
            keys = jnp.concatenate([sk_ref[pl.ds(k_rem, r), :], et_ref[pl.ds(k_rem, r), :]], axis=1)
            s = _dot(keys, qslc_scr[...])
            own = jnp.minimum(s[r - qb:], causal_cap())
            s = own if v == 0 else jnp.concatenate([s[:r - qb], own], axis=0)
            _softmax_tile_t(s, svt_ref[0, :, pl.ds(k_rem, r)], m_scr.at[0], acc_scr.at[0])

    wlen = WINDOW + qb
    w0 = pl.multiple_of(jnp.maximum(i - WINDOW // qb, 0) * qb, qb)
    later = (w0 + lax.broadcasted_iota(jnp.int32, (LANES, 1), 0) * qb) > q0
    qwin_scr[KV_WIDTH:, :] = jnp.broadcast_to(jnp.where(later, NEG_INF, 0.0), (LANES, cols8)).astype(BF16)
    keys = jnp.concatenate([wk_ref[pl.ds(w0, wlen), :], ewin_ref[...]], axis=1)
    sw_scr[...] = _dot(keys, qwin_scr[...])
    edge_keys = w0 + lax.broadcasted_iota(jnp.int32, (qb, 1), 0)
    edge_cap = jnp.concatenate([jnp.where(qpos - edge_keys <= WINDOW, CAP_OPEN, NEG_INF)] * (2 * NSA_GROUP), axis=1)
    sw_scr[0:qb, :] = jnp.minimum(sw_scr[0:qb, :], edge_cap)
    own = pl.ds(pl.multiple_of(q0 - w0, qb), qb)
    sw_scr[own, :] = jnp.minimum(sw_scr[own, :], causal_cap())
    _softmax_tile_t(sw_scr[...], wvt_ref[0, :, pl.ds(w0, wlen)], m_scr.at[1], acc_scr.at[1])

    acc_slc, acc_win = acc_scr[0], acc_scr[1]
    sig = _sigmoid(nglt_ref[0])
    for pair in range(NSA_HEADS // 2):
        tiles = []
        for h in (2 * pair, 2 * pair + 1):
            k = h // NSA_GROUP
            rows = slice(k * NSA_HD, (k + 1) * NSA_HD)
            cols = slice(h * qb, (h + 1) * qb)
            gate = lambda r: sig[r * NSA_HEADS + h:r * NSA_HEADS + h + 1, :]
            o = (gate(0) * ocmp_scr[rows, cols] + gate(1) * _normalized_head_t(acc_slc, k, cols)
                 + gate(2) * _normalized_head_t(acc_win, k, cols))
            ng = ngt_ref[0, h * NSA_HD:(h + 1) * NSA_HD, :]
            tiles.append(o * (ng * _sigmoid(ng)))
        y_ref[:, pair * LANES:(pair + 1) * LANES] = jnp.concatenate(tiles, axis=0).T.astype(y_ref.dtype)


def _attn_prompt(nq_t, nqr_t, ngl_t, ng_t, kc, vc_t, sk_rm, sv_t, wk_rm, wv_t):
    b, _, s = sv_t.shape
    assert s // CMP_BLOCK == LANES, "one lane per compressed block"
    qb = PROMPT_Q_TILE
    nqb = s // qb
    cols8 = 2 * NSA_GROUP * qb
    expand_t = (np.arange(s)[:, None] // SLC_BLOCK == np.arange(LANES)[None, :]).astype(BF16)
    wlen = WINDOW + qb
    win_blocks = (np.arange(wlen)[:, None] // qb == np.arange(LANES)[None, :]).astype(BF16)
    qcol = lambda r: pl.BlockSpec((1, r, qb), lambda bb, i: (bb, 0, i))
    per_b = lambda shape: pl.BlockSpec((1,) + shape, lambda bb, i: (bb, 0, 0))
    rows_b = pl.BlockSpec((s, KV_WIDTH), lambda bb, i: (bb, 0))
    return pl.pallas_call(
        _attn_prompt_kernel,
        grid=(b, nqb),
        in_specs=[qcol(512), qcol(512), qcol(LANES), qcol(512), per_b(kc.shape[1:]), per_b(vc_t.shape[1:]),
                  rows_b, per_b((KV_WIDTH, s)), rows_b, per_b((KV_WIDTH, s)),
                  pl.BlockSpec(expand_t.shape, lambda bb, i: (0, 0)),
                  pl.BlockSpec(win_blocks.shape, lambda bb, i: (0, 0))],
        out_specs=pl.BlockSpec((qb, 512), lambda bb, i: (bb * nqb + i, 0)),
        out_shape=jax.ShapeDtypeStruct((b * s, 512), BF16),
        scratch_shapes=[pltpu.VMEM((KV_WIDTH, cols8), BF16), pltpu.VMEM((2 * KV_WIDTH, cols8), BF16),
                        pltpu.VMEM((2 * KV_WIDTH, cols8), BF16),
                        pltpu.VMEM((KV_WIDTH, cols8), F32), pltpu.VMEM((qb // LANES, LANES, LANES), F32),
                        pltpu.VMEM((2, 1, cols8), F32), pltpu.VMEM((2, KV_WIDTH, cols8), F32),
                        pltpu.VMEM((2, SLC_KEY_TILE, cols8), F32), pltpu.VMEM((2, 1, cols8), F32),
                        pltpu.VMEM((wlen, cols8), F32)],
        compiler_params=_params(("arbitrary", "arbitrary")),
        name="attn_prompt",
    )(nq_t, nqr_t, ngl_t, ng_t, kc, vc_t, sk_rm, sv_t, wk_rm, wv_t, expand_t, win_blocks)


def _stack_heads(q_ref):
    lane = lax.broadcasted_iota(jnp.int32, (1, LANES), 1)
    lo = lane < NSA_HD
    slabs = [q_ref[:, g * LANES:(g + 1) * LANES] for g in range(NSA_GROUP)]
    zero = jnp.zeros_like(slabs[0])
    stacked = jnp.concatenate([jnp.where(lo, s, zero) for s in slabs] + [jnp.where(lo, zero, s) for s in slabs], axis=0)
    return stacked.astype(BF16)


def _mask_rows(sc, masks, rows):
    n = sc.shape[-1]
    s4 = sc.reshape(2 * NSA_GROUP, rows, n)
    out = [jnp.where(masks[k][None], s4[k * NSA_GROUP:(k + 1) * NSA_GROUP], NEG_INF) for k in range(KV_HEADS)]
    return jnp.concatenate(out, axis=0).reshape(2 * NSA_GROUP * rows, n)


def _pair_scores(imp, qpos):
    n_slc = imp.shape[1] // 2
    blk = lax.broadcasted_iota(jnp.int32, (1, n_slc), 1)
    valid = blk * SLC_BLOCK <= qpos
    forced = (blk == 0) | (blk == qpos // SLC_BLOCK)
    return jnp.where(forced, FORCED_SCORE, jnp.where(valid, imp[:, :n_slc] + imp[:, n_slc:], NEG_INF))


def _rank_select(score, extra_forced):
    n_blk = score.shape[1]
    blk = lax.broadcasted_iota(jnp.int32, (1, n_blk), 1)
    rank = jnp.zeros(score.shape, jnp.int32)
    for j in range(n_blk):
        col = score[:, j:j + 1]
        beats = (col > score) | ((col == score) & (j < blk))
        rank = rank + beats.astype(jnp.int32)
    if extra_forced:
        rank = rank + (score < FORCED_SCORE).astype(jnp.int32)
    return rank < TOP_N


def _merge_kv_heads(acc, rows):
    lane = lax.broadcasted_iota(jnp.int32, (1, LANES), 1)
    lo = lane < NSA_HD
    half = NSA_GROUP * rows
    return [jnp.where(lo, acc[g * rows:(g + 1) * rows], acc[half + g * rows:half + (g + 1) * rows])
            for g in range(NSA_GROUP)]


def _gate_and_store(o_cmp, o_slc, o_win, ngl_ref, ng_ref, y_ref):
    lane = lax.broadcasted_iota(jnp.int32, (1, LANES), 1)
    lo = lane < NSA_HD
    sig = _sigmoid(ngl_ref[...])
    for g in range(NSA_GROUP):
        gates = [jnp.where(lo, sig[:, r * 8 + g:r * 8 + g + 1], sig[:, r * 8 + 4 + g:r * 8 + 4 + g + 1]) for r in range(3)]
        o = gates[0] * o_cmp[g] + gates[1] * o_slc[g] + gates[2] * o_win[g]
        gate = ng_ref[:, g * LANES:(g + 1) * LANES]
        y_ref[:, g * LANES:(g + 1) * LANES] = (o * (gate * _sigmoid(gate))).astype(y_ref.dtype)


def _page_copies(pt_ref, b, chunk, slot, caches, bufs, sems, g):
    out = []
    for ci, (cache, buf) in enumerate(zip(caches, bufs)):
        for j in range(g):
            page = pt_ref[b, chunk * g + j]
            out.append(pltpu.make_async_copy(cache.at[page], buf.at[slot, j], sems.at[slot, ci]))
    return out


def _stream_pages(pt_ref, caches, bufs, sems, g, n_chunks, n_batch, compute, finish=None):
    b = pl.program_id(0)
    total = n_batch * n_chunks
    n_slots = bufs[0].shape[0]
    depth = n_slots - 1
    assert n_chunks >= depth
    copies = functools.partial(_page_copies, pt_ref, caches=caches, bufs=bufs, sems=sems, g=g)

    @pl.when(b == 0)
    def _():
        for d in range(depth):
            for cp in copies(0, d, d):
                cp.start()

    def visit(c):
        flat = b * n_chunks + c
        ahead = jnp.minimum(flat + depth, total - 1)
        for cp in copies(ahead // n_chunks, ahead % n_chunks, (flat + depth) % n_slots):
            cp.start()
        for cp in copies(b, c, flat % n_slots):
            cp.wait()
        compute(c, flat % n_slots)

    def chunk(c, carry):
        visit(c)
        if finish is not None:
            finish(c - 1)
        return carry

    if finish is None:
        lax.fori_loop(0, n_chunks, chunk, 0)
    else:
        visit(0)
        for c in range(1, n_chunks):
            chunk(c, 0)
        finish(n_chunks - 1)

    @pl.when(b == n_batch - 1)
    def _():
        for d in range(depth):
            for cp in copies(n_batch - 1, n_chunks - 1, (total + d) % n_slots):
                cp.wait()


def _attn_sample_kernel(pt_ref, sk_hbm, sv_hbm, nq_ref, nqr_ref, ngl_ref, ng_ref, kc_ref, vc_ref, skn_ref, svn_ref,
                        wko_ref, wvo_ref, wkn_ref, wvn_ref, e_ref, y_ref, kbuf, vbuf, sems, m_scr, l_scr, acc_scr,
                        ocmp_scr, owin_scr, mask_scr, *, n_pages, n_steps, n_batch, t_len, past_len):
    t = t_len
    nq_ref, nqr_ref, ngl_ref, ng_ref, y_ref = (r.at[0] for r in (nq_ref, nqr_ref, ngl_ref, ng_ref, y_ref))
    rows8 = 2 * NSA_GROUP * t
    half = NSA_GROUP * t
    tpos = lax.broadcasted_iota(jnp.int32, (t, 1), 0)
    qpos = past_len + tpos
    n_cmp = kc_ref.shape[1]
    wb = wko_ref.shape[2]
    step_keys = n_pages * PAGE_SIZE

    def before_pages():
        n_slc = n_cmp // 2
        halves = lambda ref: jnp.concatenate([ref[0, pl.ds(0, n_slc, stride=2), :],
                                              ref[0, pl.ds(1, n_slc, stride=2), :]], axis=0).astype(BF16)
        lhs = _stack_heads(nq_ref)
        s = _dot_nt(lhs, halves(kc_ref)) * ATTN_SCALE
        m = jnp.max(s, axis=-1, keepdims=True)
        p = jnp.exp(s - m)
        p = p / jnp.sum(p, axis=-1, keepdims=True)
        oc = _dot(p.astype(BF16), halves(vc_ref))
        for g, slab in enumerate(_merge_kv_heads(oc, t)):
            ocmp_scr[g] = slab
        p4 = p.reshape(8, t, n_cmp)
        sels = []
        for k in range(KV_HEADS):
            imp = p4[k * NSA_GROUP]
            for g in range(1, NSA_GROUP):
                imp = imp + p4[k * NSA_GROUP + g]
            sels.append(_rank_select(_pair_scores(imp, qpos), True).astype(F32))
        sel = jnp.concatenate(sels, axis=0).astype(BF16)
        span = LANES * SLC_BLOCK
        for c in range(n_slc // LANES):
            km = _dot(sel[:, c * LANES:(c + 1) * LANES], e_ref[...])
            for k in range(KV_HEADS):
                kmk = km[k * t:(k + 1) * t]
                if step_keys >= span:
                    off = (c * span) % step_keys
                    mask_scr[(c * span) // step_keys, k, :, off:off + span] = kmk
                else:
                    per = span // step_keys
                    for u in range(per):
                        mask_scr[c * per + u, k] = kmk[:, u * step_keys:(u + 1) * step_keys]

        lhs_r = _stack_heads(nqr_ref)
        so = _dot(lhs_r, wko_ref[0].astype(BF16)) * ATTN_SCALE
        sn = _dot(lhs_r, wkn_ref[0].astype(BF16)) * ATTN_SCALE
        jo = lax.broadcasted_iota(jnp.int32, (1, wb), 1)
        jn = lax.broadcasted_iota(jnp.int32, (1, LANES), 1)
        old_vis = jo >= tpos
        so = _mask_rows(so, [old_vis, old_vis], t)
        new_vis = jn <= tpos
        sn_w = _mask_rows(sn, [new_vis, new_vis], t)
        m = jnp.maximum(jnp.max(so, axis=-1, keepdims=True), jnp.max(sn_w, axis=-1, keepdims=True))
        po = jnp.exp(so - m)
        pn = jnp.exp(sn_w - m)
        den = jnp.sum(po, axis=-1, keepdims=True) + jnp.sum(pn, axis=-1, keepdims=True)
        pob, pnb = (po / den).astype(BF16), (pn / den).astype(BF16)
        wvo, wvn = wvo_ref[0].astype(BF16), wvn_ref[0].astype(BF16)
        ow = jnp.concatenate([_dot_nt(pob[:half], wvo) + _dot_nt(pnb[:half], wvn),
                              _dot_nt(pob[half:], wvo) + _dot_nt(pnb[half:], wvn)], axis=0)
        for g, slab in enumerate(_merge_kv_heads(ow, t)):
            owin_scr[g] = slab

        s2 = _dot(lhs_r, skn_ref[0].astype(BF16)) * ATTN_SCALE
        s2 = _mask_rows(s2, [new_vis, new_vis], t)
        m2 = jnp.max(s2, axis=-1, keepdims=True)
        p2 = jnp.exp(s2 - m2)
        m_scr[0] = m2
        l_scr[0] = jnp.sum(p2, axis=-1, keepdims=True)
        p2b = p2.astype(BF16)
        svn = svn_ref[0].astype(BF16)
        acc_scr[0] = jnp.concatenate([_dot_nt(p2b[:half], svn), _dot_nt(p2b[half:], svn)], axis=0)
        for st in range(1, SAMPLE_STREAMS):
            m_scr[st] = jnp.full((rows8, 1), M_INIT, F32)
            l_scr[st] = jnp.zeros((rows8, 1), F32)
            acc_scr[st] = jnp.zeros((rows8, LANES), F32)

    def on_pages(step, slot):
        lhs_r = _stack_heads(nqr_ref)
        per = n_pages // SAMPLE_STREAMS
        msk = mask_scr[step]
        for st in range(SAMPLE_STREAMS):
            pages = range(st * per, (st + 1) * per)
            keys = slice(st * per * PAGE_SIZE, (st + 1) * per * PAGE_SIZE)
            kt = jnp.concatenate([kbuf[slot, j].astype(BF16) for j in pages], axis=1)
            vt = jnp.concatenate([vbuf[slot, j].astype(BF16) for j in pages], axis=1)
            sc = _dot(lhs_r, kt) * ATTN_SCALE
            sc = _mask_rows(sc, [msk[k][:, keys] > 0.5 for k in range(KV_HEADS)], t)
            m_i = m_scr[st]
            m_new = jnp.maximum(m_i, jnp.max(sc, axis=-1, keepdims=True))
            alpha = jnp.exp(m_i - m_new)
            pr = jnp.exp(sc - m_new)
            l_scr[st] = alpha * l_scr[st] + jnp.sum(pr, axis=-1, keepdims=True)
            m_scr[st] = m_new
            pb = pr.astype(BF16)
            pv = jnp.concatenate([_dot_nt(pb[:half], vt), _dot_nt(pb[half:], vt)], axis=0)
            acc_scr[st] = alpha * acc_scr[st] + pv

    before_pages()
    _stream_pages(pt_ref, (sk_hbm, sv_hbm), (kbuf, vbuf), sems, n_pages, n_steps, n_batch, on_pages)
    m_all = m_scr[0]
    for st in range(1, SAMPLE_STREAMS):
        m_all = jnp.maximum(m_all, m_scr[st])
    l_all = jnp.zeros((rows8, 1), F32)
    acc_all = jnp.zeros((rows8, LANES), F32)
    for st in range(SAMPLE_STREAMS):
        w = jnp.exp(m_scr[st] - m_all)
        l_all = l_all + w * l_scr[st]
        acc_all = acc_all + w * acc_scr[st]
    o_slc = _merge_kv_heads(acc_all / l_all, t)
    o_cmp = [ocmp_scr[g] for g in range(NSA_GROUP)]
    o_win = [owin_scr[g] for g in range(NSA_GROUP)]
    _gate_and_store(o_cmp, o_slc, o_win, ngl_ref, ng_ref, y_ref)


def _attn_sample(page_table, cache_sk_t, cache_sv_t, nq, nqr, ngl, ng, kc, vc, skn, svn, wko, wvo, wkn, wvn, t_len):
    db, n_pages = page_table.shape
    g = SAMPLE_PAGES_PER_STEP
    steps = n_pages // g
    past_len = n_pages * PAGE_SIZE
    step_keys = g * PAGE_SIZE
    span = LANES * SLC_BLOCK
    expand = (np.arange(span)[None, :] // SLC_BLOCK == np.arange(LANES)[:, None]).astype(BF16)

    hbm = pl.BlockSpec(memory_space=pl.ANY)
    row = lambda w: pl.BlockSpec((1, t_len, w), lambda b, pt: (b, 0, 0))
    per_b = lambda shape: pl.BlockSpec((1,) + shape, lambda b, pt: (b, 0, 0))
    rows8 = 2 * NSA_GROUP * t_len
    page_buf = pltpu.VMEM((PAGE_SLOTS, g, KV_WIDTH, PAGE_SIZE), F32)
    grid_spec = pltpu.PrefetchScalarGridSpec(
        num_scalar_prefetch=1,
        grid=(db,),
        in_specs=[hbm, hbm, row(512), row(512), row(LANES), row(512), per_b(kc.shape[1:]), per_b(vc.shape[1:]),
                  per_b(skn.shape[1:]), per_b(svn.shape[1:]), per_b(wko.shape[1:]), per_b(wvo.shape[1:]),
                  per_b(wkn.shape[1:]), per_b(wvn.shape[1:]), pl.BlockSpec(expand.shape, lambda b, pt: (0, 0))],
        out_specs=row(512),
        scratch_shapes=[page_buf, page_buf, pltpu.SemaphoreType.DMA((PAGE_SLOTS, 2)),
                        pltpu.VMEM((SAMPLE_STREAMS, rows8, 1), F32), pltpu.VMEM((SAMPLE_STREAMS, rows8, 1), F32),
                        pltpu.VMEM((SAMPLE_STREAMS, rows8, LANES), F32),
                        pltpu.VMEM((NSA_GROUP, t_len, LANES), F32), pltpu.VMEM((NSA_GROUP, t_len, LANES), F32),
                        pltpu.VMEM((steps, KV_HEADS, t_len, step_keys), F32)],
    )
    return pl.pallas_call(
        functools.partial(_attn_sample_kernel, n_pages=g, n_steps=steps, n_batch=db, t_len=t_len, past_len=past_len),
        grid_spec=grid_spec,
        out_shape=jax.ShapeDtypeStruct((db, t_len, 512), F32),
        compiler_params=_params(("arbitrary",)),
        name="attn_sample",
    )(page_table, cache_sk_t, cache_sv_t, nq, nqr, ngl, ng, kc, vc, skn, svn, wko, wvo, wkn, wvn, expand)


def _finish_kernel(x_ref, yr_ref, yn_ref, p_ref, wo_ref, gple_ref, wg_ref, wp_ref, gf_ref, o_ref):
    x = x_ref[...]
    x = x + _dot(yr_ref[...].astype(BF16), wo_ref[0:RET_WIDTH, :]) + _dot(yn_ref[...].astype(BF16), wo_ref[RET_WIDTH:, :])
    ms = jnp.mean(x * x, axis=-1, keepdims=True)
    hn = (x * lax.rsqrt(ms + RMS_EPS) * gple_ref[...]).astype(BF16)
    gate = _sigmoid(_dot(hn, wg_ref[...]))
    x = x + gate * _dot(p_ref[...].astype(BF16), wp_ref[...])
    ms = jnp.mean(x * x, axis=-1, keepdims=True)
    o_ref[...] = x * lax.rsqrt(ms + RMS_EPS) * gf_ref[...]


def _finish(x2d, y_ret, y_nsa, p2d, w_out, norm_ple, w_gate, w_ple, norm_f, tm):
    n = x2d.shape[0]
    row = lambda w: pl.BlockSpec((tm, w), lambda i: (i, 0))
    const = lambda shape: pl.BlockSpec(shape, lambda i: (0, 0))
    return pl.pallas_call(
        _finish_kernel,
        grid=(n // tm,),
        in_specs=[row(D_MODEL), row(RET_WIDTH), row(NSA_WIDTH), row(PLE_DIM), const(w_out.shape), const((1, D_MODEL)),
                  const(w_gate.shape), const(w_ple.shape), const((1, D_MODEL))],
        out_specs=row(D_MODEL),
        out_shape=jax.ShapeDtypeStruct((n, D_MODEL), F32),
        compiler_params=_params(("arbitrary",)),
        name="finish",
    )(x2d, y_ret, y_nsa, p2d, w_out, norm_ple, w_gate, w_ple, norm_f)


def _slab_perm():
    return np.array([(k * NSA_GROUP + g) * NSA_HD + d for g in range(NSA_GROUP) for k in range(KV_HEADS)
                     for d in range(NSA_HD)], np.int32)


def _to_positions_major(x_t):
    lead = x_t.shape[:-2]
    n = len(lead)
    x4 = x_t.reshape(lead + (KV_HEADS, NSA_HD, x_t.shape[-1]))
    return jnp.transpose(x4, tuple(range(n)) + (n + 2, n, n + 1))


def _cache_t(cache):
    n_pool, page = cache.shape[:2]
    return jnp.transpose(cache, (0, 2, 3, 1)).reshape(n_pool, KV_WIDTH, page)


def _layer(xp, xs, c_ck, c_cv, c_sk, c_sv, win_k, win_v, ret_state, page_table, p_p, p_s, norm_mix, w_in, gn_g, gn_b,
           pe_k, w1_k, w2_k, pe_v, w1_v, w2_v, w_out, norm_ple, w_gate, w_ple, norm_f):
    b, s, d = xp.shape
    db, t, _ = xs.shape
    n_pages = page_table.shape[1]
    past = n_pages * PAGE_SIZE

    off = np.cumsum((0,) + SPLIT_SIZES)
    w_in_t = w_in.T
    col = lambda i: w_in[:, off[i]:off[i + 1]]
    rows = lambda i: w_in_t[off[i]:off[i + 1]]
    perm = _slab_perm()
    ngl_rows = jnp.pad(rows(11), ((0, LANES - N_GATES), (0, 0)))
    ret_rows = [rows(0), rows(1), rows(2), rows(3)]
    kv_rows = [rows(5), rows(6), rows(8), rows(10), rows(7), rows(9)]
    w_rm_p = jnp.concatenate([col(0), col(1), col(2), col(3), col(7), col(9)], axis=1).astype(BF16)
    w_t_p = jnp.concatenate(kv_rows + [rows(4), rows(12), ngl_rows], axis=0).astype(BF16)
    w_rm_s = jnp.concatenate(ret_rows + [rows(4)[perm], rows(12)[perm], ngl_rows], axis=0).astype(BF16)
    w_t_s = jnp.concatenate(kv_rows, axis=0).astype(BF16)
    w_out_b = w_out.astype(BF16)
    w_out_slab = jnp.concatenate([w_out[:RET_WIDTH], w_out[RET_WIDTH:][perm]], axis=0).astype(BF16)
    g_mix = norm_mix.reshape(1, d)
    cwk = _compress_weights(pe_k, w1_k, w2_k)
    cwv = _compress_weights(pe_v, w1_v, w2_v)
    cwv_t = _compress_weights(pe_v, w1_v, w2_v, transpose_w2=True)
    gn_g2, gn_b2 = gn_g.reshape(1, RET_WIDTH), gn_b.reshape(1, RET_WIDTH)
    fin_w = (norm_ple.reshape(1, d), w_gate.astype(BF16), w_ple.astype(BF16), norm_f.reshape(1, d))

    tm = PROMPT_ROW_TILE
    (q_ret, k_ret, v_ret, rg, sk_rm, wk_rm, ck_t, cv_t, sv_t, wv_t, sk_t, wk_t, sv_b, wv_b, nq_t, nqr_t, ng_t,
     ngl_t) = _proj(xp.reshape(b * s, d), np.arange(s), g_mix, w_rm_p, w_t_p, b, tm, True)
    y_ret, ret_p = _retention(q_ret, k_ret, v_ret, rg, jnp.zeros((b, RET_HEADS, RET_DK, RET_DK), F32), gn_g2, gn_b2,
                              b, s // RET_CHUNK, RET_CHUNK)
    kc, vc_t = _compress_prompt(ck_t, cv_t, cwk, cwv_t)
    y_nsa = _attn_prompt(nq_t, nqr_t, ngl_t, ng_t, kc, vc_t, sk_rm, sv_b, wk_rm, wv_b)
    y_prompt = _finish(xp.reshape(b * s, d), y_ret, y_nsa, p_p.reshape(b * s, PLE_DIM), w_out_b, *fin_w,
                       FINISH_ROW_TILE)
    wb_p = min(WINDOW, s)
    prompt_states = (ret_p, _to_positions_major(ck_t), _to_positions_major(cv_t), _to_positions_major(sk_t),
                     _to_positions_major(sv_t), _to_positions_major(wk_t[:, :, s - wb_p:]),
                     _to_positions_major(wv_t[:, :, s - wb_p:]))

    n = db * t
    pos_s = np.tile(past + np.arange(t), db)
    (q_ret, k_ret, v_ret, rg, nq, nqr, ng, ngl, ck_n, cv_n, sv_n, wv_n, sk_n, wk_n) = _proj(
        xs.reshape(n, d), pos_s, g_mix, w_rm_s, w_t_s, 1, n, False)
    y_ret, ret_s = _retention(q_ret, k_ret, v_ret, rg, ret_state, gn_g2, gn_b2, db, 1, t)
    kc, vc = _compress_sample(page_table, _cache_t(c_ck), _cache_t(c_cv), cwk, cwv)
    per_batch = lambda a: jnp.transpose(a[0].reshape(KV_WIDTH, db, t), (1, 0, 2))
    lane_pad = lambda a: jnp.pad(per_batch(a), ((0, 0), (0, 0), (0, LANES - t)))
    win_t = lambda w: jnp.transpose(w, (0, 2, 3, 1)).reshape(db, KV_WIDTH, w.shape[1])
    rows3 = lambda a: a.astype(F32).reshape(db, t, a.shape[-1])
    y_nsa = _attn_sample(page_table, _cache_t(c_sk), _cache_t(c_sv), rows3(nq), rows3(nqr), rows3(ngl), rows3(ng), kc, vc,
                         lane_pad(sk_n), lane_pad(sv_n), win_t(win_k), win_t(win_v), lane_pad(wk_n), lane_pad(wv_n), t)
    y_nsa = y_nsa.reshape(n, NSA_WIDTH)
    y_sample = _finish(xs.reshape(n, d), y_ret, y_nsa, p_s.reshape(n, PLE_DIM), w_out_slab, *fin_w, n)
    new_rows = lambda a: _to_positions_major(per_batch(a))
    sample_states = (ret_s, new_rows(ck_n), new_rows(cv_n), new_rows(sk_n), new_rows(sv_n),
                     jnp.concatenate([win_k[:, t:], new_rows(wk_n)], axis=1),
                     jnp.concatenate([win_v[:, t:], new_rows(wv_n)], axis=1))
    return y_prompt.reshape(b, s, d), y_sample.reshape(db, t, d), prompt_states, sample_states


def kernel(x_prompt, x_sample, cache_cmp_k, cache_cmp_v, cache_slc_k, cache_slc_v, state_win_k, state_win_v, state_ret, page_table, p_prompt, p_sample, norm_mix, w_in, ret_gn_g, ret_gn_b, cmp_pe_k, cmp_w1_k, cmp_w2_k, cmp_pe_v, cmp_w1_v, cmp_w2_v, w_out, norm_ple, w_ple_gate, w_ple, norm_f):
    depth = w_in.shape[0]
    assert depth == 1, "single trunk layer"
    l = 0
    yp, ys, sp, ss = _layer(x_prompt, x_sample, cache_cmp_k[l], cache_cmp_v[l], cache_slc_k[l], cache_slc_v[l],
                            state_win_k[l], state_win_v[l], state_ret[l], page_table, p_prompt[l], p_sample[l],
                            norm_mix[l], w_in[l], ret_gn_g[l], ret_gn_b[l], cmp_pe_k[l], cmp_w1_k[l], cmp_w2_k[l],
                            cmp_pe_v[l], cmp_w1_v[l], cmp_w2_v[l], w_out[l], norm_ple[l], w_ple_gate[l], w_ple[l], norm_f)
    return (yp, ys) + tuple(a[None] for a in sp) + tuple(a[None] for a in ss)
```

```python
import functools

import numpy as np
import jax
import jax.numpy as jnp
from jax import lax
from jax.experimental import pallas as pl
from jax.experimental.pallas import tpu as pltpu

F32 = jnp.float32
BF16 = jnp.bfloat16

D_MODEL = 1024
PLE_DIM = 256
PAGE_SIZE = 128
RET_HEADS = 4
RET_DK = 128
RET_WIDTH = 512
RET_CHUNK = 128
NSA_HEADS = 8
KV_HEADS = 2
NSA_GROUP = 4
NSA_HD = 64
NSA_WIDTH = 512
KV_WIDTH = 128
CMP_BLOCK = 32
SLC_BLOCK = 64
TOP_N = 16
WINDOW = 512
ROPE_THETA = 10000.0
RMS_EPS = 1e-6
GN_EPS = 1e-5
NEG_INF = -1e9
FORCED_SCORE = 1e4
M_INIT = -1e30
CAP_OPEN = 3e38
ATTN_SCALE = NSA_HD ** -0.5
SPLIT_SIZES = (512, 512, 512, 512, 512, 128, 128, 128, 128, 128, 128, 24, 512)
N_GATES = 3 * NSA_HEADS

LANES = 128
SUBLANES = 8
PROMPT_Q_TILE = 128
SLC_KEY_TILE = 1024
RET_CHUNKS_PER_STEP = 8
PROMPT_ROW_TILE = 512
FINISH_ROW_TILE = 512
SAMPLE_PAGES_PER_STEP = 64
SAMPLE_STREAMS = 1
COMPRESS_PAGES_PER_STEP = 32
PAGE_SLOTS = 3
BLOCK_PITCH = 40
VMEM_LIMIT = 56 * 1024 * 1024


def _dot(a, b):
    return jnp.dot(a, b, preferred_element_type=F32)


def _dot_nt(a, b):
    return lax.dot_general(a, b, (((1,), (1,)), ((), ())), preferred_element_type=F32)


def _sigmoid(x):
    return 1.0 / (1.0 + jnp.exp(-x))


def _params(sem):
    return pltpu.CompilerParams(dimension_semantics=sem, vmem_limit_bytes=VMEM_LIMIT)


def _rope_lanes(a, cos, sin_signed, half):
    if 2 * half == LANES:
        partner = pltpu.roll(a, half, 1)
    else:
        lane = lax.broadcasted_iota(jnp.int32, (1, LANES), 1)
        partner = jnp.where((lane % (2 * half)) < half, pltpu.roll(a, LANES - half, 1), pltpu.roll(a, half, 1))
    return a * cos + partner * sin_signed


def _rope_sublanes(blk, cos_t, sin_t):
    outs = []
    for h in range(blk.shape[0] // NSA_HD):
        x1 = blk[h * NSA_HD:h * NSA_HD + NSA_HD // 2]
        x2 = blk[h * NSA_HD + NSA_HD // 2:(h + 1) * NSA_HD]
        outs += [x1 * cos_t - x2 * sin_t, x2 * cos_t + x1 * sin_t]
    return jnp.concatenate(outs, axis=0)


def _proj_kernel(*refs, prompt, n_w):
    x_ref, g_ref = refs[:2]
    w_parts = refs[2:2 + n_w]
    (wt_ref, cosr_ref, sinr_ref, cosn_ref, sinn_ref, cost_ref, sint_ref,
     qret_ref, kret_ref, vret_ref, rg_ref) = refs[2 + n_w:13 + n_w]
    refs = refs[n_w - 1:]
    x = x_ref[...]
    ms = jnp.mean(x * x, axis=-1, keepdims=True)
    h = (x * lax.rsqrt(ms + RMS_EPS) * g_ref[...]).astype(BF16)

    def proj(lo, hi):
        if prompt:
            return _dot(h, w_parts[0][:, lo:hi])
        for part in w_parts:
            if lo < part.shape[0]:
                return _dot_nt(h, part[lo:hi, :])
            lo, hi = lo - part.shape[0], hi - part.shape[0]

    cr, sr = cosr_ref[...], sinr_ref[...]
    cn, sn = cosn_ref[...], sinn_ref[...]
    ct, st = cost_ref[...], sint_ref[...]
    yq = proj(0, 512)
    yk = proj(512, 1024)
    for hh in range(RET_HEADS):
        sl = slice(hh * LANES, (hh + 1) * LANES)
        qret_ref[:, sl] = _rope_lanes(yq[:, sl], cr, sr, RET_DK // 2)
        kret_ref[:, sl] = _rope_lanes(yk[:, sl], cr, sr, RET_DK // 2) * (RET_DK ** -0.5)
    vret_ref[...] = proj(1024, 1536)
    rg_ref[...] = proj(1536, 2048)

    yt = _dot_nt(wt_ref[...], h)
    if prompt:
        (skr_ref, wkr_ref, ck_ref, cv_ref, sv_ref, wv_ref, sk_ref, wk_ref, svb_ref, wvb_ref,
         nqt_ref, nqrt_ref, ngt_ref, nglt_ref) = refs[14:]
    else:
        nq_ref, nqr_ref, ng_ref, ngl_ref, ck_ref, cv_ref, sv_ref, wv_ref, sk_ref, wk_ref = refs[14:]
    ck_ref[0] = yt[0:128]
    cv_ref[0] = yt[128:256]
    sv_ref[0] = yt[256:384]
    wv_ref[0] = yt[384:512]
    sk_ref[0] = _rope_sublanes(yt[512:640], ct, st)
    wk_ref[0] = _rope_sublanes(yt[640:768], ct, st)
    if prompt:
        svb_ref[0] = yt[256:384].astype(BF16)
        wvb_ref[0] = yt[384:512].astype(BF16)
        nqt = yt[768:1280]
        nqt_ref[0] = nqt.astype(BF16)
        nqrt_ref[0] = _rope_sublanes(nqt, ct, st).astype(BF16)
        ngt_ref[0] = yt[1280:1792]
        nglt_ref[0] = yt[1792:1920]
        skr_ref[...] = _rope_lanes(proj(2048, 2176), cn, sn, NSA_HD // 2).astype(BF16)
        wkr_ref[...] = _rope_lanes(proj(2176, 2304), cn, sn, NSA_HD // 2).astype(BF16)
    else:
        yn = proj(2048, 2560)
        for gg in range(NSA_GROUP):
            sl = slice(gg * LANES, (gg + 1) * LANES)
            nq_ref[:, sl] = yn[:, sl].astype(BF16)
            nqr_ref[:, sl] = _rope_lanes(yn[:, sl], cn, sn, NSA_HD // 2).astype(BF16)
        ng_ref[...] = proj(2560, 3072)
        ngl_ref[...] = proj(3072, 3200)


def _rope_tables(pos, half):
    inv = ROPE_THETA ** (-np.arange(half, dtype=np.float64) / half)
    ang = np.asarray(pos, np.float64)[:, None] * inv[None, :]
    return np.cos(ang).astype(np.float32), np.sin(ang).astype(np.float32)


def _proj(x2d, pos_rows, norm_g, w_rm, w_t, kv_batch, tm, prompt):
    n = x2d.shape[0]
    nt = pos_rows.shape[0] // tm
    skv = n // kv_batch
    nkt = skv // tm
    c64, s64 = _rope_tables(pos_rows, 64)
    c32, s32 = _rope_tables(pos_rows, 32)
    cosr = np.concatenate([c64, c64], axis=1)
    sinr = np.concatenate([-s64, s64], axis=1)
    cosn = np.concatenate([c32, c32, c32, c32], axis=1)
    sinn = np.concatenate([-s32, s32, -s32, s32], axis=1)
    cost, sint = np.ascontiguousarray(c32.T), np.ascontiguousarray(s32.T)
    row = lambda w: pl.BlockSpec((tm, w), lambda i: (i, 0))
    tab = pl.BlockSpec((tm, LANES), lambda i: (i % nt, 0))
    tabt = pl.BlockSpec((32, tm), lambda i: (0, i % nt))
    tspec = lambda r: pl.BlockSpec((1, r, tm), lambda i: (i // nkt, 0, i % nkt))
    f = lambda w, dt: jax.ShapeDtypeStruct((n, w), dt)
    ts = lambda r, dt: jax.ShapeDtypeStruct((kv_batch, r, skv), dt)
    kvo, kvs = tspec(KV_WIDTH), ts(KV_WIDTH, F32)
    out_specs = [row(512)] * 4
    out_shape = [f(512, F32)] * 4
    if prompt:
        out_specs += [row(KV_WIDTH)] * 2 + [kvo] * 8 + [tspec(512)] * 3 + [tspec(LANES)]
        out_shape += [f(KV_WIDTH, BF16)] * 2 + [kvs] * 6 + [ts(KV_WIDTH, BF16)] * 2 + [ts(512, BF16)] * 2 + [
            ts(512, F32), ts(LANES, F32)]
    else:
        out_specs += [row(512)] * 3 + [row(LANES)] + [kvo] * 6
        out_shape += [f(512, BF16)] * 2 + [f(512, F32), f(LANES, F32)] + [kvs] * 6
    w_parts = tuple(w_rm) if isinstance(w_rm, (tuple, list)) else (w_rm,)
    whole = lambda a: pl.BlockSpec(a.shape, lambda i: (0, 0))
    return pl.pallas_call(
        functools.partial(_proj_kernel, prompt=prompt, n_w=len(w_parts)),
        grid=(n // tm,),
        in_specs=[row(D_MODEL), pl.BlockSpec((1, D_MODEL), lambda i: (0, 0))] + [whole(w) for w in w_parts]
        + [whole(w_t), tab, tab, tab, tab, tabt, tabt],
        out_specs=out_specs,
        out_shape=out_shape,
        compiler_params=_params(("arbitrary",)),
        name="proj_prompt" if prompt else "proj_sample",
    )(x2d, norm_g, *w_parts, w_t, cosr, sinr, cosn, sinn, cost, sint)


def _ret_kernel(q_ref, k_ref, v_ref, rg_ref, st_ref, dm_ref, qd_ref, kd_ref, sd_ref, gg_ref, gb_ref,
                y_ref, so_ref, st_scr, *, n_chunks):
    c = pl.program_id(1)

    @pl.when(c == 0)
    def _():
        st_scr[...] = st_ref[0]

    for h in range(RET_HEADS):
        sl = slice(h * LANES, (h + 1) * LANES)
        state = st_scr[h]
        for cc in range(q_ref.shape[0] // RET_CHUNK):
            rows = slice(cc * RET_CHUNK, (cc + 1) * RET_CHUNK)
            q = q_ref[rows, sl]
            k = k_ref[rows, sl]
            v = v_ref[rows, sl].astype(BF16)
            s = _dot_nt(q.astype(BF16), k.astype(BF16)) * dm_ref[h]
            intra = _dot(s.astype(BF16), v)
            cross = _dot((q * qd_ref[:, sl]).astype(BF16), state.astype(BF16))
            o = intra + cross
            kd_t = (k * kd_ref[:, sl]).T.astype(BF16)
            state = sd_ref[h] * state + _dot(kd_t, v)
            mu = jnp.mean(o, axis=-1, keepdims=True)
            var = jnp.mean(jnp.square(o - mu), axis=-1, keepdims=True)
            on = (o - mu) * lax.rsqrt(var + GN_EPS)
            gate = rg_ref[rows, sl]
            y_ref[rows, sl] = (on * gg_ref[:, sl] + gb_ref[:, sl]) * (gate * _sigmoid(gate))
        st_scr[h] = state

    @pl.when(c == n_chunks - 1)
    def _():
        so_ref[0] = st_scr[...]


def _retention(q, k, v, rg, state, gn_g, gn_b, n_batch, n_chunks, chunk_len):
    c = RET_CHUNK
    log_g = np.log(1.0 - 2.0 ** (-5.0 - np.arange(RET_HEADS, dtype=np.float64)))
    i = np.arange(c, dtype=np.float64)
    diff = i[:, None] - i[None, :]
    causal = diff >= 0
    f32 = lambda a: np.ascontiguousarray(a, dtype=np.float32)
    dmask = f32(np.where(causal[None], np.exp(np.where(causal, diff, 0.0)[None] * log_g[:, None, None]), 0.0))
    expand = lambda t: f32(np.repeat(t, LANES, axis=1))
    qdec = expand(np.exp((i[:, None] + 1.0) * log_g[None, :]))
    kdec = expand(np.exp((chunk_len - 1.0 - i)[:, None] * log_g[None, :]))
    sdec = f32(np.broadcast_to(np.exp(chunk_len * log_g)[:, None, None], (RET_HEADS, 1, LANES)))
    stspec1 = pl.BlockSpec((1, RET_HEADS, 128, 128), lambda b: (b, 0, 0, 0))
    const1 = lambda shape: pl.BlockSpec(shape, lambda b: (0,) * len(shape))
    if chunk_len < c:
        assert n_chunks == 1
        row1 = pl.BlockSpec((chunk_len, 512), lambda b: (b, 0))
        return pl.pallas_call(
            _ret_short_kernel,
            grid=(n_batch,),
            in_specs=[row1, row1, row1, row1, stspec1, const1((RET_HEADS, c, c)), const1((c, 512)), const1((c, 512)),
                      const1((RET_HEADS, 1, LANES)), const1((1, 512)), const1((1, 512))],
            out_specs=[row1, stspec1],
            out_shape=[jax.ShapeDtypeStruct(q.shape, F32), jax.ShapeDtypeStruct(state.shape, F32)],
            scratch_shapes=[pltpu.VMEM((c, 512), F32)] * 3,
            compiler_params=_params(("arbitrary",)),
            name="retention_short",
        )(q, k, v, rg, state, dmask, qdec, kdec, sdec, gn_g, gn_b)
    per_step = RET_CHUNKS_PER_STEP if n_chunks % RET_CHUNKS_PER_STEP == 0 else 1
    n_steps = n_chunks // per_step
    row = pl.BlockSpec((per_step * c, 512), lambda b, j: (b * n_steps + j, 0))
    stspec = pl.BlockSpec((1, RET_HEADS, 128, 128), lambda b, j: (b, 0, 0, 0))
    const = lambda shape: pl.BlockSpec(shape, lambda b, j: (0,) * len(shape))
    return pl.pallas_call(
        functools.partial(_ret_kernel, n_chunks=n_steps),
        grid=(n_batch, n_steps),
        in_specs=[row, row, row, row, stspec, const((RET_HEADS, c, c)), const((c, 512)), const((c, 512)),
                  const((RET_HEADS, 1, LANES)), const((1, 512)), const((1, 512))],
        out_specs=[row, stspec],
        out_shape=[jax.ShapeDtypeStruct(q.shape, F32), jax.ShapeDtypeStruct(state.shape, F32)],
        scratch_shapes=[pltpu.VMEM((RET_HEADS, 128, 128), F32)],
        compiler_params=_params(("arbitrary", "arbitrary")),
        name="retention",
    )(q, k, v, rg, state, dmask, qdec, kdec, sdec, gn_g, gn_b)


def _ret_short_kernel(q_ref, k_ref, v_ref, rg_ref, st_ref, dm_ref, qd_ref, kd_ref, sd_ref, gg_ref, gb_ref,
                      y_ref, so_ref, k_pad, kd_pad, v_pad):
    t = q_ref.shape[0]

    @pl.when(pl.program_id(0) == 0)
    def _():
        for pad in (k_pad, kd_pad, v_pad):
            pad[...] = jnp.zeros(pad.shape, F32)

    k_pad[0:t, :] = k_ref[...]
    kd_pad[0:t, :] = k_ref[...] * kd_ref[0:t, :]
    v_pad[0:t, :] = v_ref[...]
    for h in range(RET_HEADS):
        sl = slice(h * LANES, (h + 1) * LANES)
        q = q_ref[:, sl]
        v = v_pad[:, sl].astype(BF16)
        state = st_ref[0, h]
        s = _dot_nt(q.astype(BF16), k_pad[:, sl].astype(BF16)) * dm_ref[h, 0:t, :]
        intra = _dot(s.astype(BF16), v)
        cross = _dot((q * qd_ref[0:t, sl]).astype(BF16), state.astype(BF16))
        o = intra + cross
        so_ref[0, h] = sd_ref[h] * state + _dot(kd_pad[:, sl].T.astype(BF16), v)
        mu = jnp.mean(o, axis=-1, keepdims=True)
        var = jnp.mean(jnp.square(o - mu), axis=-1, keepdims=True)
        on = (o - mu) * lax.rsqrt(var + GN_EPS)
        gate = rg_ref[:, sl]
        y_ref[:, sl] = (on * gg_ref[:, sl] + gb_ref[:, sl]) * (gate * _sigmoid(gate))


def _gelu_tanh(x):
    return 0.5 * x * (1.0 + jnp.tanh(np.sqrt(2.0 / np.pi).astype(np.float32) * (x + 0.044715 * (x * x * x))))


def _rows_to_scratch(tiles, scr):
    per_tile = LANES // CMP_BLOCK
    for t, tile in enumerate(tiles):
        rows_pm = tile.T
        for c in range(per_tile):
            r0 = (t * per_tile + c) * BLOCK_PITCH
            scr[r0:r0 + CMP_BLOCK, :] = rows_pm[c * CMP_BLOCK:(c + 1) * CMP_BLOCK]


def _compress_scratch(scr, n_blk, pe_ref, w1_ref, w2_ref, transpose_out=False):
    flat = [(scr[pl.ds(j, n_blk, stride=BLOCK_PITCH), :] + pe_ref[j:j + 1, :]).astype(BF16) for j in range(CMP_BLOCK)]
    hid = _gelu_tanh(_dot(jnp.concatenate(flat, axis=1), w1_ref[...]))
    if transpose_out:
        return _dot_nt(w2_ref[...], hid.astype(BF16))
    return _dot(hid.astype(BF16), w2_ref[...])


def _compress_one(tiles, pe_ref, w1_ref, w2_ref, scr, transpose_out=False):
    _rows_to_scratch(tiles, scr)
    return _compress_scratch(scr, len(tiles) * (LANES // CMP_BLOCK), pe_ref, w1_ref, w2_ref, transpose_out)


def _compress_prompt_kernel(k_ref, v_ref, pek_ref, w1k_ref, w2k_ref, pev_ref, w1v_ref, w2v_ref,
                            kc_ref, vct_ref, scr_k, scr_v, *, n_tiles):
    for src, pe, w1, w2, dst, scr, tr in ((k_ref, pek_ref, w1k_ref, w2k_ref, kc_ref, scr_k, False),
                                          (v_ref, pev_ref, w1v_ref, w2v_ref, vct_ref, scr_v, True)):
        tiles = [src[0, :, t * LANES:(t + 1) * LANES] for t in range(n_tiles)]
        dst[0] = _compress_one(tiles, pe, w1, w2, scr, tr)


def _compress_sample_kernel(pt_ref, ck_hbm, cv_hbm, pek_ref, w1k_ref, w2k_ref, pev_ref, w1v_ref, w2v_ref,
                            kc_ref, vc_ref, kbuf, vbuf, sems, scr_k, scr_v, *, n_pages, n_steps, n_batch):
    n_blk = n_pages * PAGE_SIZE // CMP_BLOCK

    def to_rows(step, slot):
        _rows_to_scratch([kbuf[slot, j] for j in range(n_pages)], scr_k.at[step % 2])
        _rows_to_scratch([vbuf[slot, j] for j in range(n_pages)], scr_v.at[step % 2])

    def compress(step):
        rows = pl.ds(pl.multiple_of(step * n_blk, n_blk), n_blk)
        kc_ref[0, rows, :] = _compress_scratch(scr_k.at[step % 2], n_blk, pek_ref, w1k_ref, w2k_ref)
        vc_ref[0, rows, :] = _compress_scratch(scr_v.at[step % 2], n_blk, pev_ref, w1v_ref, w2v_ref)

    _stream_pages(pt_ref, (ck_hbm, cv_hbm), (kbuf, vbuf), sems, n_pages, n_steps, n_batch, to_rows, compress)


def _compress_weights(pe, w1, w2, transpose_w2=False):
    pe_rows = pe.reshape(CMP_BLOCK, KV_WIDTH)
    z = jnp.zeros((CMP_BLOCK, NSA_HD, NSA_HD), F32)
    w1bd = jnp.concatenate([jnp.concatenate([w1[0], z], axis=2), jnp.concatenate([z, w1[1]], axis=2)], axis=1)
    z2 = jnp.zeros((NSA_HD, NSA_HD), F32)
    w2bd = jnp.concatenate([jnp.concatenate([w2[0], z2], axis=1), jnp.concatenate([z2, w2[1]], axis=1)], axis=0)
    if transpose_w2:
        w2bd = w2bd.T
    return pe_rows, w1bd.reshape(CMP_BLOCK * KV_WIDTH, KV_WIDTH).astype(BF16), w2bd.astype(BF16)


def _cw_specs():
    zero = lambda *a: (0, 0)
    return [pl.BlockSpec((CMP_BLOCK, KV_WIDTH), zero), pl.BlockSpec((CMP_BLOCK * KV_WIDTH, KV_WIDTH), zero),
            pl.BlockSpec((KV_WIDTH, KV_WIDTH), zero)]


def _compress_prompt(ck_t, cv_t, cwk, cwv_t):
    b, _, s = ck_t.shape
    n_blk = s // CMP_BLOCK
    src = pl.BlockSpec((1, KV_WIDTH, s), lambda i: (i, 0, 0))
    return pl.pallas_call(
        functools.partial(_compress_prompt_kernel, n_tiles=s // LANES),
        grid=(b,),
        in_specs=[src, src] + _cw_specs() + _cw_specs(),
        out_specs=[pl.BlockSpec((1, n_blk, KV_WIDTH), lambda i: (i, 0, 0)),
                   pl.BlockSpec((1, KV_WIDTH, n_blk), lambda i: (i, 0, 0))],
        out_shape=[jax.ShapeDtypeStruct((b, n_blk, KV_WIDTH), F32), jax.ShapeDtypeStruct((b, KV_WIDTH, n_blk), F32)],
        scratch_shapes=[pltpu.VMEM((n_blk * BLOCK_PITCH, KV_WIDTH), F32)] * 2,
        compiler_params=_params(("arbitrary",)),
        name="compress_prompt",
    )(ck_t, cv_t, *cwk, *cwv_t)


def _compress_sample(page_table, cache_k_t, cache_v_t, cwk, cwv):
    db, n_pages = page_table.shape
    g = COMPRESS_PAGES_PER_STEP
    steps = n_pages // g
    blk_per_step = g * PAGE_SIZE // CMP_BLOCK
    n_blk = n_pages * PAGE_SIZE // CMP_BLOCK

    hbm = pl.BlockSpec(memory_space=pl.ANY)
    dst = pl.BlockSpec((1, n_blk, KV_WIDTH), lambda b, pt: (b, 0, 0))
    shp = jax.ShapeDtypeStruct((db, n_blk, KV_WIDTH), F32)
    page_buf = pltpu.VMEM((PAGE_SLOTS, g, KV_WIDTH, PAGE_SIZE), F32)
    grid_spec = pltpu.PrefetchScalarGridSpec(
        num_scalar_prefetch=1,
        grid=(db,),
        in_specs=[hbm, hbm] + _cw_specs() + _cw_specs(),
        out_specs=[dst, dst],
        scratch_shapes=[page_buf, page_buf, pltpu.SemaphoreType.DMA((PAGE_SLOTS, 2))]
        + [pltpu.VMEM((2, blk_per_step * BLOCK_PITCH, KV_WIDTH), F32)] * 2,
    )
    return pl.pallas_call(
        functools.partial(_compress_sample_kernel, n_pages=g, n_steps=steps, n_batch=db),
        grid_spec=grid_spec,
        out_shape=[shp, shp],
        compiler_params=_params(("arbitrary",)),
        name="compress_sample",
    )(page_table, cache_k_t, cache_v_t, *cwk, *cwv)


def _stack_heads_t(qt_ref):
    nq = qt_ref.shape[2]
    zero = jnp.zeros((NSA_HD, nq), BF16)
    cols = []
    for k in range(KV_HEADS):
        for g in range(NSA_GROUP):
            h = k * NSA_GROUP + g
            tile = qt_ref[0, h * NSA_HD:(h + 1) * NSA_HD, :] * ATTN_SCALE
            cols.append(jnp.concatenate([tile, zero] if k == 0 else [zero, tile], axis=0))
    return jnp.concatenate(cols, axis=1)


def _rank_select_rows(score, n_blk):
    parts = [score[v * SUBLANES:(v + 1) * SUBLANES] for v in range(n_blk // SUBLANES)]
    ranks = [jnp.zeros(p.shape, jnp.int32) for p in parts]
    for j in range(n_blk):
        col = score[j:j + 1, :]
        for v, part in enumerate(parts):
            ge, gt = (col >= part).astype(jnp.int32), (col > part).astype(jnp.int32)
            if v * SUBLANES > j:
                beats = ge
            elif (v + 1) * SUBLANES - 1 <= j:
                beats = gt
            else:
                row = v * SUBLANES + lax.broadcasted_iota(jnp.int32, (SUBLANES, 1), 0)
                beats = jnp.where(row > j, ge, gt)
            ranks[v] = ranks[v] + beats
    return jnp.concatenate([(r < TOP_N).astype(F32) for r in ranks], axis=0)


def _softmax_tile_t(s, v_t, m_ref, acc_ref, s_max=None):
    half = s.shape[1] // KV_HEADS
    m_old = m_ref[...]
    m_new = jnp.maximum(m_old, jnp.max(s, axis=0, keepdims=True) if s_max is None else s_max)
    alpha = jnp.exp(m_old - m_new)
    pb = jnp.exp(s - m_new).astype(BF16)
    m_ref[...] = m_new
    ones = jnp.ones((NSA_HD, v_t.shape[1]), BF16)
    v0 = jnp.concatenate([v_t[:NSA_HD], ones], axis=0)
    v1 = jnp.concatenate([ones, v_t[NSA_HD:]], axis=0)
    pv = jnp.concatenate([_dot(v0, pb[:, :half]), _dot(v1, pb[:, half:])], axis=1)
    acc_ref[...] = alpha * acc_ref[...] + pv


def _normalized_head_t(acc, k, cols):
    den = (1 - k) * NSA_HD
    return acc[k * NSA_HD:(k + 1) * NSA_HD, cols] / acc[den:den + 1, cols]


def _attn_prompt_kernel(nqt_ref, nqrt_ref, nglt_ref, ngt_ref, kc_ref, vct_ref, sk_ref, svt_ref, wk_ref, wvt_ref,
                        et_ref, ewin_ref, y_ref, lhs_scr, qslc_scr, qwin_scr, ocmp_scr, t_scr, m_scr, acc_scr,
                        s_scr, smax_scr, sw_scr):
    i = pl.program_id(1)
    qb = y_ref.shape[0]
    q0 = i * qb
    cols8 = 2 * NSA_GROUP * qb
    qpos = q0 + lax.broadcasted_iota(jnp.int32, (1, qb), 1)
    lhs_scr[...] = _stack_heads_t(nqt_ref)
    q_rot = _stack_heads_t(nqrt_ref)
    qslc_scr[0:KV_WIDTH, :] = q_rot
    qwin_scr[0:KV_WIDTH, :] = q_rot
    def causal_cap():
        own_keys = q0 + lax.broadcasted_iota(jnp.int32, (qb, 1), 0)
        return jnp.concatenate([jnp.where(own_keys <= qpos, CAP_OPEN, NEG_INF)] * (2 * NSA_GROUP), axis=1)

    n_cmp = kc_ref.shape[1]
    s_all = _dot(kc_ref[0].astype(BF16), lhs_scr[...])
    cend = (lax.broadcasted_iota(jnp.int32, (n_cmp, 1), 0) + 1) * CMP_BLOCK - 1
    cmask = cend <= qpos
    vct = vct_ref[0].astype(BF16)
    imps = []
    for k in range(KV_HEADS):
        imp = None
        for g in range(NSA_GROUP):
            c0 = (k * NSA_GROUP + g) * qb
            s = jnp.where(cmask, s_all[:, c0:c0 + qb], NEG_INF)
            p = jnp.exp(s - jnp.max(s, axis=0, keepdims=True))
            p = p / jnp.sum(p, axis=0, keepdims=True) * cmask.astype(F32)
            imp = p if imp is None else imp + p
            ocmp_scr[:, c0:c0 + qb] = _dot(vct, p.astype(BF16))
        imps.append(imp)

    n_slc = n_cmp // 2
    blk = lax.broadcasted_iota(jnp.int32, (n_slc, 1), 0)
    valid = blk * SLC_BLOCK <= qpos

    def store_bias(k, sel):
        bias = jnp.concatenate([(sel - 1.0) * (-NEG_INF), jnp.zeros((LANES - n_slc, qb), F32)], axis=0).astype(BF16)
        c0 = k * NSA_GROUP * qb
        qslc_scr[KV_WIDTH:, c0:c0 + NSA_GROUP * qb] = jnp.concatenate([bias] * NSA_GROUP, axis=1)

    @pl.when(q0 + qb <= TOP_N * SLC_BLOCK)
    def _():
        for k in range(KV_HEADS):
            store_bias(k, valid.astype(F32))

    @pl.when(q0 + qb > TOP_N * SLC_BLOCK)
    def _():
        forced = (blk == 0) | (blk == qpos // SLC_BLOCK)
        for k in range(KV_HEADS):
            pairs = []
            for u in range(qb // LANES):
                t_scr[u] = imps[k][:, u * LANES:(u + 1) * LANES]
                pairs.append(t_scr[u, pl.ds(0, n_slc, stride=2), :] + t_scr[u, pl.ds(1, n_slc, stride=2), :])
            pair = jnp.concatenate(pairs, axis=1)
            score = jnp.where(forced, FORCED_SCORE, jnp.where(valid, pair, NEG_INF))
            store_bias(k, _rank_select_rows(score, n_slc))

    tk = SLC_KEY_TILE
    m_scr[...] = jnp.full(m_scr.shape, M_INIT, F32)
    acc_scr[...] = jnp.zeros(acc_scr.shape, F32)

    def scores(t):
        k0 = pl.multiple_of(t * tk, tk)
        keys = jnp.concatenate([sk_ref[pl.ds(k0, tk), :], et_ref[pl.ds(k0, tk), :]], axis=1)
        s = _dot(keys, qslc_scr[...])
        s_scr[t % 2] = s
        smax_scr[t % 2] = jnp.max(s, axis=0, keepdims=True)

    def update(t):
        k0 = pl.multiple_of(t * tk, tk)
        _softmax_tile_t(s_scr[t % 2], svt_ref[0, :, pl.ds(k0, tk)], m_scr.at[0], acc_scr.at[0], smax_scr[t % 2])

    def slc_step(t, carry):
        scores(t + 1)
        update(t)
        return carry

    n_full = q0 // tk

    @pl.when(n_full >= 1)
    def _():
        scores(0)
        lax.fori_loop(0, n_full - 1, slc_step, 0)
        update(n_full - 1)

    k_rem = pl.multiple_of(n_full * tk, tk)
    for v in range(tk // qb):
        @pl.when((q0 - k_rem) // qb == v)
        def _():
            r = (v + 1) * qb
            keys = jnp.concatenate([sk_ref[pl.ds(k_rem, r), :], et_ref[pl.ds(k_rem, r), :]], axis=1)
            s = _dot(keys, qslc_scr[...])
            own = jnp.minimum(s[r - qb:], causal_cap())
            s = own if v == 0 else jnp.concatenate([s[:r - qb], own], axis=0)
            _softmax_tile_t(s, svt_ref[0, :, pl.ds(k_rem, r)], m_scr.at[0], acc_scr.at[0])

    wlen = WINDOW + qb
    w0 = pl.multiple_of(jnp.maximum(i - WINDOW // qb, 0) * qb, qb)
    later = (w0 + lax.broadcasted_iota(jnp.int32, (LANES, 1), 0) * qb) > q0
    qwin_scr[KV_WIDTH:, :] = jnp.broadcast_to(jnp.where(later, NEG_INF, 0.0), (LANES, cols8)).astype(BF16)
    keys = jnp.concatenate([wk_ref[pl.ds(w0, wlen), :], ewin_ref[...]], axis=1)
    sw_scr[...] = _dot(keys, qwin_scr[...])
    edge_keys = w0 + lax.broadcasted_iota(jnp.int32, (qb, 1), 0)
    edge_cap = jnp.concatenate([jnp.where(qpos - edge_keys <= WINDOW, CAP_OPEN, NEG_INF)] * (2 * NSA_GROUP), axis=1)
    sw_scr[0:qb, :] = jnp.minimum(sw_scr[0:qb, :], edge_cap)
    own = pl.ds(pl.multiple_of(q0 - w0, qb), qb)
    sw_scr[own, :] = jnp.minimum(sw_scr[own, :], causal_cap())
    _softmax_tile_t(sw_scr[...], wvt_ref[0, :, pl.ds(w0, wlen)], m_scr.at[1], acc_scr.at[1])

    acc_slc, acc_win = acc_scr[0], acc_scr[1]
    sig = _sigmoid(nglt_ref[0])
    for pair in range(NSA_HEADS // 2):
        tiles = []
        for h in (2 * pair, 2 * pair + 1):
            k = h // NSA_GROUP
            rows = slice(k * NSA_HD, (k + 1) * NSA_HD)
            cols = slice(h * qb, (h + 1) * qb)
            gate = lambda r: sig[r * NSA_HEADS + h:r * NSA_HEADS + h + 1, :]
            o = (gate(0) * ocmp_scr[rows, cols] + gate(1) * _normalized_head_t(acc_slc, k, cols)
                 + gate(2) * _normalized_head_t(acc_win, k, cols))
            ng = ngt_ref[0, h * NSA_HD:(h + 1) * NSA_HD, :]
            tiles.append(o * (ng * _sigmoid(ng)))
        y_ref[:, pair * LANES:(pair + 1) * LANES] = jnp.concatenate(tiles, axis=0).T.astype(y_ref.dtype)


def _attn_prompt(nq_t, nqr_t, ngl_t, ng_t, kc, vc_t, sk_rm, sv_t, wk_rm, wv_t):
    b, _, s = sv_t.shape
    assert s // CMP_BLOCK == LANES, "one lane per compressed block"
    qb = PROMPT_Q_TILE
    nqb = s // qb
    cols8 = 2 * NSA_GROUP * qb
    expand_t = (np.arange(s)[:, None] // SLC_BLOCK == np.arange(LANES)[None, :]).astype(BF16)
    wlen = WINDOW + qb
    win_blocks = (np.arange(wlen)[:, None] // qb == np.arange(LANES)[None, :]).astype(BF16)
    qcol = lambda r: pl.BlockSpec((1, r, qb), lambda bb, i: (bb, 0, i))
    per_b = lambda shape: pl.BlockSpec((1,) + shape, lambda bb, i: (bb, 0, 0))
    rows_b = pl.BlockSpec((s, KV_WIDTH), lambda bb, i: (bb, 0))
    return pl.pallas_call(
        _attn_prompt_kernel,
        grid=(b, nqb),
        in_specs=[qcol(512), qcol(512), qcol(LANES), qcol(512), per_b(kc.shape[1:]), per_b(vc_t.shape[1:]),
                  rows_b, per_b((KV_WIDTH, s)), rows_b, per_b((KV_WIDTH, s)),
                  pl.BlockSpec(expand_t.shape, lambda bb, i: (0, 0)),
                  pl.BlockSpec(win_blocks.shape, lambda bb, i: (0, 0))],
        out_specs=pl.BlockSpec((qb, 512), lambda bb, i: (bb * nqb + i, 0)),
        out_shape=jax.ShapeDtypeStruct((b * s, 512), BF16),
        scratch_shapes=[pltpu.VMEM((KV_WIDTH, cols8), BF16), pltpu.VMEM((2 * KV_WIDTH, cols8), BF16),
                        pltpu.VMEM((2 * KV_WIDTH, cols8), BF16),
                        pltpu.VMEM((KV_WIDTH, cols8), F32), pltpu.VMEM((qb // LANES, LANES, LANES), F32),
                        pltpu.VMEM((2, 1, cols8), F32), pltpu.VMEM((2, KV_WIDTH, cols8), F32),
                        pltpu.VMEM((2, SLC_KEY_TILE, cols8), F32), pltpu.VMEM((2, 1, cols8), F32),
                        pltpu.VMEM((wlen, cols8), F32)],
        compiler_params=_params(("arbitrary", "arbitrary")),
        name="attn_prompt",
    )(nq_t, nqr_t, ngl_t, ng_t, kc, vc_t, sk_rm, sv_t, wk_rm, wv_t, expand_t, win_blocks)


def _stack_heads(q_ref):
    lane = lax.broadcasted_iota(jnp.int32, (1, LANES), 1)
    lo = lane < NSA_HD
    slabs = [q_ref[:, g * LANES:(g + 1) * LANES] for g in range(NSA_GROUP)]
    zero = jnp.zeros_like(slabs[0])
    stacked = jnp.concatenate([jnp.where(lo, s, zero) for s in slabs] + [jnp.where(lo, zero, s) for s in slabs], axis=0)
    return stacked.astype(BF16)


def _mask_rows(sc, masks, rows):
    n = sc.shape[-1]
    s4 = sc.reshape(2 * NSA_GROUP, rows, n)
    out = [jnp.where(masks[k][None], s4[k * NSA_GROUP:(k + 1) * NSA_GROUP], NEG_INF) for k in range(KV_HEADS)]
    return jnp.concatenate(out, axis=0).reshape(2 * NSA_GROUP * rows, n)


def _pair_scores(imp, qpos):
    n_slc = imp.shape[1] // 2
    blk = lax.broadcasted_iota(jnp.int32, (1, n_slc), 1)
    valid = blk * SLC_BLOCK <= qpos
    forced = (blk == 0) | (blk == qpos // SLC_BLOCK)
    return jnp.where(forced, FORCED_SCORE, jnp.where(valid, imp[:, :n_slc] + imp[:, n_slc:], NEG_INF))


def _rank_select(score, extra_forced):
    n_blk = score.shape[1]
    blk = lax.broadcasted_iota(jnp.int32, (1, n_blk), 1)
    rank = jnp.zeros(score.shape, jnp.int32)
    for j in range(n_blk):
        col = score[:, j:j + 1]
        beats = (col > score) | ((col == score) & (j < blk))
        rank = rank + beats.astype(jnp.int32)
    if extra_forced:
        rank = rank + (score < FORCED_SCORE).astype(jnp.int32)
    return rank < TOP_N


def _merge_kv_heads(acc, rows):
    lane = lax.broadcasted_iota(jnp.int32, (1, LANES), 1)
    lo = lane < NSA_HD
    half = NSA_GROUP * rows
    return [jnp.where(lo, acc[g * rows:(g + 1) * rows], acc[half + g * rows:half + (g + 1) * rows])
            for g in range(NSA_GROUP)]


def _gate_and_store(o_cmp, o_slc, o_win, ngl_ref, ng_ref, y_ref):
    lane = lax.broadcasted_iota(jnp.int32, (1, LANES), 1)
    lo = lane < NSA_HD
    sig = _sigmoid(ngl_ref[...])
    for g in range(NSA_GROUP):
        gates = [jnp.where(lo, sig[:, r * 8 + g:r * 8 + g + 1], sig[:, r * 8 + 4 + g:r * 8 + 4 + g + 1]) for r in range(3)]
        o = gates[0] * o_cmp[g] + gates[1] * o_slc[g] + gates[2] * o_win[g]
        gate = ng_ref[:, g * LANES:(g + 1) * LANES]
        y_ref[:, g * LANES:(g + 1) * LANES] = (o * (gate * _sigmoid(gate))).astype(y_ref.dtype)


def _page_copies(pt_ref, b, chunk, slot, caches, bufs, sems, g):
    out = []
    for ci, (cache, buf) in enumerate(zip(caches, bufs)):
        for j in range(g):
            page = pt_ref[b, chunk * g + j]
            out.append(pltpu.make_async_copy(cache.at[page], buf.at[slot, j], sems.at[slot, ci]))
    return out


def _stream_pages(pt_ref, caches, bufs, sems, g, n_chunks, n_batch, compute, finish=None):
    b = pl.program_id(0)
    total = n_batch * n_chunks
    n_slots = bufs[0].shape[0]
    depth = n_slots - 1
    assert n_chunks >= depth
    copies = functools.partial(_page_copies, pt_ref, caches=caches, bufs=bufs, sems=sems, g=g)

    @pl.when(b == 0)
    def _():
        for d in range(depth):
            for cp in copies(0, d, d):
                cp.start()

    def visit(c):
        flat = b * n_chunks + c
        ahead = jnp.minimum(flat + depth, total - 1)
        for cp in copies(ahead // n_chunks, ahead % n_chunks, (flat + depth) % n_slots):
            cp.start()
        for cp in copies(b, c, flat % n_slots):
            cp.wait()
        compute(c, flat % n_slots)

    def chunk(c, carry):
        visit(c)
        if finish is not None:
            finish(c - 1)
        return carry

    if finish is None:
        lax.fori_loop(0, n_chunks, chunk, 0)
    else:
        visit(0)
        for c in range(1, n_chunks):
            chunk(c, 0)
        finish(n_chunks - 1)

    @pl.when(b == n_batch - 1)
    def _():
        for d in range(depth):
            for cp in copies(n_batch - 1, n_chunks - 1, (total + d) % n_slots):
                cp.wait()


def _attn_sample_kernel(pt_ref, sk_hbm, sv_hbm, nq_ref, nqr_ref, ngl_ref, ng_ref, kc_ref, vc_ref, skn_ref, svn_ref,
                        wko_ref, wvo_ref, wkn_ref, wvn_ref, e_ref, y_ref, kbuf, vbuf, sems, m_scr, l_scr, acc_scr,
                        ocmp_scr, owin_scr, mask_scr, *, n_pages, n_steps, n_batch, t_len, past_len):
    t = t_len
    nq_ref, nqr_ref, ngl_ref, ng_ref, y_ref = (r.at[0] for r in (nq_ref, nqr_ref, ngl_ref, ng_ref, y_ref))
    rows8 = 2 * NSA_GROUP * t
    half = NSA_GROUP * t
    tpos = lax.broadcasted_iota(jnp.int32, (t, 1), 0)
    qpos = past_len + tpos
    n_cmp = kc_ref.shape[1]
    wb = wko_ref.shape[2]
    step_keys = n_pages * PAGE_SIZE

    def before_pages():
        n_slc = n_cmp // 2
        halves = lambda ref: jnp.concatenate([ref[0, pl.ds(0, n_slc, stride=2), :],
                                              ref[0, pl.ds(1, n_slc, stride=2), :]], axis=0).astype(BF16)
        lhs = _stack_heads(nq_ref)
        s = _dot_nt(lhs, halves(kc_ref)) * ATTN_SCALE
        m = jnp.max(s, axis=-1, keepdims=True)
        p = jnp.exp(s - m)
        p = p / jnp.sum(p, axis=-1, keepdims=True)
        oc = _dot(p.astype(BF16), halves(vc_ref))
        for g, slab in enumerate(_merge_kv_heads(oc, t)):
            ocmp_scr[g] = slab
        p4 = p.reshape(8, t, n_cmp)
        sels = []
        for k in range(KV_HEADS):
            imp = p4[k * NSA_GROUP]
            for g in range(1, NSA_GROUP):
                imp = imp + p4[k * NSA_GROUP + g]
            sels.append(_rank_select(_pair_scores(imp, qpos), True).astype(F32))
        sel = jnp.concatenate(sels, axis=0).astype(BF16)
        span = LANES * SLC_BLOCK
        for c in range(n_slc // LANES):
            km = _dot(sel[:, c * LANES:(c + 1) * LANES], e_ref[...])
            for k in range(KV_HEADS):
                kmk = km[k * t:(k + 1) * t]
                if step_keys >= span:
                    off = (c * span) % step_keys
                    mask_scr[(c * span) // step_keys, k, :, off:off + span] = kmk
                else:
                    per = span // step_keys
                    for u in range(per):
                        mask_scr[c * per + u, k] = kmk[:, u * step_keys:(u + 1) * step_keys]

        lhs_r = _stack_heads(nqr_ref)
        so = _dot(lhs_r, wko_ref[0].astype(BF16)) * ATTN_SCALE
        sn = _dot(lhs_r, wkn_ref[0].astype(BF16)) * ATTN_SCALE
        jo = lax.broadcasted_iota(jnp.int32, (1, wb), 1)
        jn = lax.broadcasted_iota(jnp.int32, (1, LANES), 1)
        old_vis = jo >= tpos
        so = _mask_rows(so, [old_vis, old_vis], t)
        new_vis = jn <= tpos
        sn_w = _mask_rows(sn, [new_vis, new_vis], t)
        m = jnp.maximum(jnp.max(so, axis=-1, keepdims=True), jnp.max(sn_w, axis=-1, keepdims=True))
        po = jnp.exp(so - m)
        pn = jnp.exp(sn_w - m)
        den = jnp.sum(po, axis=-1, keepdims=True) + jnp.sum(pn, axis=-1, keepdims=True)
        pob, pnb = (po / den).astype(BF16), (pn / den).astype(BF16)
        wvo, wvn = wvo_ref[0].astype(BF16), wvn_ref[0].astype(BF16)
        ow = jnp.concatenate([_dot_nt(pob[:half], wvo) + _dot_nt(pnb[:half], wvn),
                              _dot_nt(pob[half:], wvo) + _dot_nt(pnb[half:], wvn)], axis=0)
        for g, slab in enumerate(_merge_kv_heads(ow, t)):
            owin_scr[g] = slab

        s2 = _dot(lhs_r, skn_ref[0].astype(BF16)) * ATTN_SCALE
        s2 = _mask_rows(s2, [new_vis, new_vis], t)
        m2 = jnp.max(s2, axis=-1, keepdims=True)
        p2 = jnp.exp(s2 - m2)
        m_scr[0] = m2
        l_scr[0] = jnp.sum(p2, axis=-1, keepdims=True)
        p2b = p2.astype(BF16)
        svn = svn_ref[0].astype(BF16)
        acc_scr[0] = jnp.concatenate([_dot_nt(p2b[:half], svn), _dot_nt(p2b[half:], svn)], axis=0)
        for st in range(1, SAMPLE_STREAMS):
            m_scr[st] = jnp.full((rows8, 1), M_INIT, F32)
            l_scr[st] = jnp.zeros((rows8, 1), F32)
            acc_scr[st] = jnp.zeros((rows8, LANES), F32)

    def on_pages(step, slot):
        lhs_r = _stack_heads(nqr_ref)
        per = n_pages // SAMPLE_STREAMS
        msk = mask_scr[step]
        for st in range(SAMPLE_STREAMS):
            pages = range(st * per, (st + 1) * per)
            keys = slice(st * per * PAGE_SIZE, (st + 1) * per * PAGE_SIZE)
            kt = jnp.concatenate([kbuf[slot, j].astype(BF16) for j in pages], axis=1)
            vt = jnp.concatenate([vbuf[slot, j].astype(BF16) for j in pages], axis=1)
            sc = _dot(lhs_r, kt) * ATTN_SCALE
            sc = _mask_rows(sc, [msk[k][:, keys] > 0.5 for k in range(KV_HEADS)], t)
            m_i = m_scr[st]
            m_new = jnp.maximum(m_i, jnp.max(sc, axis=-1, keepdims=True))
            alpha = jnp.exp(m_i - m_new)
            pr = jnp.exp(sc - m_new)
            l_scr[st] = alpha * l_scr[st] + jnp.sum(pr, axis=-1, keepdims=True)
            m_scr[st] = m_new
            pb = pr.astype(BF16)
            pv = jnp.concatenate([_dot_nt(pb[:half], vt), _dot_nt(pb[half:], vt)], axis=0)
            acc_scr[st] = alpha * acc_scr[st] + pv

    before_pages()
    _stream_pages(pt_ref, (sk_hbm, sv_hbm), (kbuf, vbuf), sems, n_pages, n_steps, n_batch, on_pages)
    m_all = m_scr[0]
    for st in range(1, SAMPLE_STREAMS):
        m_all = jnp.maximum(m_all, m_scr[st])
    l_all = jnp.zeros((rows8, 1), F32)
    acc_all = jnp.zeros((rows8, LANES), F32)
    for st in range(SAMPLE_STREAMS):
        w = jnp.exp(m_scr[st] - m_all)
        l_all = l_all + w * l_scr[st]
        acc_all = acc_all + w * acc_scr[st]
    o_slc = _merge_kv_heads(acc_all / l_all, t)
    o_cmp = [ocmp_scr[g] for g in range(NSA_GROUP)]
    o_win = [owin_scr[g] for g in range(NSA_GROUP)]
    _gate_and_store(o_cmp, o_slc, o_win, ngl_ref, ng_ref, y_ref)


def _attn_sample(page_table, cache_sk_t, cache_sv_t, nq, nqr, ngl, ng, kc, vc, skn, svn, wko, wvo, wkn, wvn, t_len):
    db, n_pages = page_table.shape
    g = SAMPLE_PAGES_PER_STEP
    steps = n_pages // g
    past_len = n_pages * PAGE_SIZE
    step_keys = g * PAGE_SIZE
    span = LANES * SLC_BLOCK
    expand = (np.arange(span)[None, :] // SLC_BLOCK == np.arange(LANES)[:, None]).astype(BF16)

    hbm = pl.BlockSpec(memory_space=pl.ANY)
    row = lambda w: pl.BlockSpec((1, t_len, w), lambda b, pt: (b, 0, 0))
    per_b = lambda shape: pl.BlockSpec((1,) + shape, lambda b, pt: (b, 0, 0))
    rows8 = 2 * NSA_GROUP * t_len
    page_buf = pltpu.VMEM((PAGE_SLOTS, g, KV_WIDTH, PAGE_SIZE), F32)
    grid_spec = pltpu.PrefetchScalarGridSpec(
        num_scalar_prefetch=1,
        grid=(db,),
        in_specs=[hbm, hbm, row(512), row(512), row(LANES), row(512), per_b(kc.shape[1:]), per_b(vc.shape[1:]),
                  per_b(skn.shape[1:]), per_b(svn.shape[1:]), per_b(wko.shape[1:]), per_b(wvo.shape[1:]),
                  per_b(wkn.shape[1:]), per_b(wvn.shape[1:]), pl.BlockSpec(expand.shape, lambda b, pt: (0, 0))],
        out_specs=row(512),
        scratch_shapes=[page_buf, page_buf, pltpu.SemaphoreType.DMA((PAGE_SLOTS, 2)),
                        pltpu.VMEM((SAMPLE_STREAMS, rows8, 1), F32), pltpu.VMEM((SAMPLE_STREAMS, rows8, 1), F32),
                        pltpu.VMEM((SAMPLE_STREAMS, rows8, LANES), F32),
                        pltpu.VMEM((NSA_GROUP, t_len, LANES), F32), pltpu.VMEM((NSA_GROUP, t_len, LANES), F32),
                        pltpu.VMEM((steps, KV_HEADS, t_len, step_keys), F32)],
    )
    return pl.pallas_call(
        functools.partial(_attn_sample_kernel, n_pages=g, n_steps=steps, n_batch=db, t_len=t_len, past_len=past_len),
        grid_spec=grid_spec,
        out_shape=jax.ShapeDtypeStruct((db, t_len, 512), F32),
        compiler_params=_params(("arbitrary",)),
        name="attn_sample",
    )(page_table, cache_sk_t, cache_sv_t, nq, nqr, ngl, ng, kc, vc, skn, svn, wko, wvo, wkn, wvn, expand)


def _finish_kernel(x_ref, yr_ref, yn_ref, p_ref, wo_ref, gple_ref, wg_ref, wp_ref, gf_ref, o_ref):
    x = x_ref[...]
    x = x + _dot(yr_ref[...].astype(BF16), wo_ref[0:RET_WIDTH, :]) + _dot(yn_ref[...].astype(BF16), wo_ref[RET_WIDTH:, :])
    ms = jnp.mean(x * x, axis=-1, keepdims=True)
    hn = (x * lax.rsqrt(ms + RMS_EPS) * gple_ref[...]).astype(BF16)
    gate = _sigmoid(_dot(hn, wg_ref[...]))
    x = x + gate * _dot(p_ref[...].astype(BF16), wp_ref[...])
    ms = jnp.mean(x * x, axis=-1, keepdims=True)
    o_ref[...] = x * lax.rsqrt(ms + RMS_EPS) * gf_ref[...]


def _finish(x2d, y_ret, y_nsa, p2d, w_out, norm_ple, w_gate, w_ple, norm_f, tm):
    n = x2d.shape[0]
    row = lambda w: pl.BlockSpec((tm, w), lambda i: (i, 0))
    const = lambda shape: pl.BlockSpec(shape, lambda i: (0, 0))
    return pl.pallas_call(
        _finish_kernel,
        grid=(n // tm,),
        in_specs=[row(D_MODEL), row(RET_WIDTH), row(NSA_WIDTH), row(PLE_DIM), const(w_out.shape), const((1, D_MODEL)),
                  const(w_gate.shape), const(w_ple.shape), const((1, D_MODEL))],
        out_specs=row(D_MODEL),
        out_shape=jax.ShapeDtypeStruct((n, D_MODEL), F32),
        compiler_params=_params(("arbitrary",)),
        name="finish",
    )(x2d, y_ret, y_nsa, p2d, w_out, norm_ple, w_gate, w_ple, norm_f)


def _slab_perm():
    return np.array([(k * NSA_GROUP + g) * NSA_HD + d for g in range(NSA_GROUP) for k in range(KV_HEADS)
                     for d in range(NSA_HD)], np.int32)


def _to_positions_major(x_t):
    lead = x_t.shape[:-2]
    n = len(lead)
    x4 = x_t.reshape(lead + (KV_HEADS, NSA_HD, x_t.shape[-1]))
    return jnp.transpose(x4, tuple(range(n)) + (n + 2, n, n + 1))


def _cache_t(cache):
    n_pool, page = cache.shape[:2]
    return jnp.transpose(cache, (0, 2, 3, 1)).reshape(n_pool, KV_WIDTH, page)


def _layer(xp, xs, c_ck, c_cv, c_sk, c_sv, win_k, win_v, ret_state, page_table, p_p, p_s, norm_mix, w_in, gn_g, gn_b,
           pe_k, w1_k, w2_k, pe_v, w1_v, w2_v, w_out, norm_ple, w_gate, w_ple, norm_f):
    b, s, d = xp.shape
    db, t, _ = xs.shape
    n_pages = page_table.shape[1]
    past = n_pages * PAGE_SIZE

    off = np.cumsum((0,) + SPLIT_SIZES)
    w_in_t = w_in.T
    col = lambda i: w_in[:, off[i]:off[i + 1]]
    rows = lambda i: w_in_t[off[i]:off[i + 1]]
    perm = _slab_perm()
    ngl_rows = jnp.pad(rows(11), ((0, LANES - N_GATES), (0, 0)))
    kv_rows = [rows(5), rows(6), rows(8), rows(10), rows(7), rows(9)]
    w_rm_p = jnp.concatenate([col(0), col(1), col(2), col(3), col(7), col(9)], axis=1).astype(BF16)
    w_t_p = jnp.concatenate(kv_rows + [rows(4), rows(12), ngl_rows], axis=0).astype(BF16)
    w_rm_s = tuple(w.astype(BF16) for w in (w_in_t[:off[4]], rows(4)[perm], rows(12)[perm], ngl_rows))
    w_t_s = jnp.concatenate(kv_rows, axis=0).astype(BF16)
    w_out_b = w_out.astype(BF16)
    w_out_slab = jnp.concatenate([w_out[:RET_WIDTH], w_out[RET_WIDTH:][perm]], axis=0).astype(BF16)
    g_mix = norm_mix.reshape(1, d)
    cwk = _compress_weights(pe_k, w1_k, w2_k)
    cwv = _compress_weights(pe_v, w1_v, w2_v)
    cwv_t = _compress_weights(pe_v, w1_v, w2_v, transpose_w2=True)
    gn_g2, gn_b2 = gn_g.reshape(1, RET_WIDTH), gn_b.reshape(1, RET_WIDTH)
    fin_w = (norm_ple.reshape(1, d), w_gate.astype(BF16), w_ple.astype(BF16), norm_f.reshape(1, d))

    tm = PROMPT_ROW_TILE
    (q_ret, k_ret, v_ret, rg, sk_rm, wk_rm, ck_t, cv_t, sv_t, wv_t, sk_t, wk_t, sv_b, wv_b, nq_t, nqr_t, ng_t,
     ngl_t) = _proj(xp.reshape(b * s, d), np.arange(s), g_mix, w_rm_p, w_t_p, b, tm, True)
    y_ret, ret_p = _retention(q_ret, k_ret, v_ret, rg, jnp.zeros((b, RET_HEADS, RET_DK, RET_DK), F32), gn_g2, gn_b2,
                              b, s // RET_CHUNK, RET_CHUNK)
    kc, vc_t = _compress_prompt(ck_t, cv_t, cwk, cwv_t)
    y_nsa = _attn_prompt(nq_t, nqr_t, ngl_t, ng_t, kc, vc_t, sk_rm, sv_b, wk_rm, wv_b)
    y_prompt = _finish(xp.reshape(b * s, d), y_ret, y_nsa, p_p.reshape(b * s, PLE_DIM), w_out_b, *fin_w,
                       FINISH_ROW_TILE)
    wb_p = min(WINDOW, s)
    prompt_states = (ret_p, _to_positions_major(ck_t), _to_positions_major(cv_t), _to_positions_major(sk_t),
                     _to_positions_major(sv_t), _to_positions_major(wk_t[:, :, s - wb_p:]),
                     _to_positions_major(wv_t[:, :, s - wb_p:]))

    n = db * t
    pos_s = np.tile(past + np.arange(t), db)
    (q_ret, k_ret, v_ret, rg, nq, nqr, ng, ngl, ck_n, cv_n, sv_n, wv_n, sk_n, wk_n) = _proj(
        xs.reshape(n, d), pos_s, g_mix, w_rm_s, w_t_s, 1, n, False)
    y_ret, ret_s = _retention(q_ret, k_ret, v_ret, rg, ret_state, gn_g2, gn_b2, db, 1, t)
    kc, vc = _compress_sample(page_table, _cache_t(c_ck), _cache_t(c_cv), cwk, cwv)
    per_batch = lambda a: jnp.transpose(a[0].reshape(KV_WIDTH, db, t), (1, 0, 2))
    lane_pad = lambda a: jnp.pad(per_batch(a), ((0, 0), (0, 0), (0, LANES - t)))
    win_t = lambda w: jnp.transpose(w, (0, 2, 3, 1)).reshape(db, KV_WIDTH, w.shape[1])
    rows3 = lambda a: a.astype(F32).reshape(db, t, a.shape[-1])
    y_nsa = _attn_sample(page_table, _cache_t(c_sk), _cache_t(c_sv), rows3(nq), rows3(nqr), rows3(ngl), rows3(ng), kc, vc,
                         lane_pad(sk_n), lane_pad(sv_n), win_t(win_k), win_t(win_v), lane_pad(wk_n), lane_pad(wv_n), t)
    y_nsa = y_nsa.reshape(n, NSA_WIDTH)
    y_sample = _finish(xs.reshape(n, d), y_ret, y_nsa, p_s.reshape(n, PLE_DIM), w_out_slab, *fin_w, n)
    new_rows = lambda a: _to_positions_major(per_batch(a))
    sample_states = (ret_s, new_rows(ck_n), new_rows(cv_n), new_rows(sk_n), new_rows(sv_n),
                     jnp.concatenate([win_k[:, t:], new_rows(wk_n)], axis=1),
                     jnp.concatenate([win_v[:, t:], new_rows(wv_n)], axis=1))
    return y_prompt.reshape(b, s, d), y_sample.reshape(db, t, d), prompt_states, sample_states


def kernel(x_prompt, x_sample, cache_cmp_k, cache_cmp_v, cache_slc_k, cache_slc_v, state_win_k, state_win_v, state_ret, page_table, p_prompt, p_sample, norm_mix, w_in, ret_gn_g, ret_gn_b, cmp_pe_k, cmp_w1_k, cmp_w2_k, cmp_pe_v, cmp_w1_v, cmp_w2_v, w_out, norm_ple, w_ple_gate, w_ple, norm_f):
    depth = w_in.shape[0]
    assert depth == 1, "single trunk layer"
    l = 0
    yp, ys, sp, ss = _layer(x_prompt, x_sample, cache_cmp_k[l], cache_cmp_v[l], cache_slc_k[l], cache_slc_v[l],
                            state_win_k[l], state_win_v[l], state_ret[l], page_table, p_prompt[l], p_sample[l],
                            norm_mix[l], w_in[l], ret_gn_g[l], ret_gn_b[l], cmp_pe_k[l], cmp_w1_k[l], cmp_w2_k[l],
                            cmp_pe_v[l], cmp_w1_v[l], cmp_w2_v[l], w_out[l], norm_ple[l], w_ple_gate[l], w_ple[l], norm_f)
    return (yp, ys) + tuple(a[None] for a in sp) + tuple(a[None] for a in ss)
```

```python
import functools

import numpy as np
import jax
import jax.numpy as jnp
from jax import lax
from jax.experimental import pallas as pl
from jax.experimental.pallas import tpu as pltpu

F32 = jnp.float32
BF16 = jnp.bfloat16

D_MODEL = 1024
PLE_DIM = 256
PAGE_SIZE = 128
RET_HEADS = 4
RET_DK = 128
RET_WIDTH = 512
RET_CHUNK = 128
NSA_HEADS = 8
KV_HEADS = 2
NSA_GROUP = 4
NSA_HD = 64
NSA_WIDTH = 512
KV_WIDTH = 128
CMP_BLOCK = 32
SLC_BLOCK = 64
TOP_N = 16
WINDOW = 512
ROPE_THETA = 10000.0
RMS_EPS = 1e-6
GN_EPS = 1e-5
NEG_INF = -1e9
FORCED_SCORE = 1e4
M_INIT = -1e30
CAP_OPEN = 3e38
ATTN_SCALE = NSA_HD ** -0.5
SPLIT_SIZES = (512, 512, 512, 512, 512, 128, 128, 128, 128, 128, 128, 24, 512)
N_GATES = 3 * NSA_HEADS

LANES = 128
SUBLANES = 8
PROMPT_Q_TILE = 128
SLC_KEY_TILE = 1024
RET_CHUNKS_PER_STEP = 8
PROMPT_ROW_TILE = 512
FINISH_ROW_TILE = 512
SAMPLE_PAGES_PER_STEP = 64
SAMPLE_STREAMS = 1
COMPRESS_PAGES_PER_STEP = 32
PAGE_SLOTS = 3
BLOCK_PITCH = 40
VMEM_LIMIT = 56 * 1024 * 1024


def _dot(a, b):
    return jnp.dot(a, b, preferred_element_type=F32)


def _dot_nt(a, b):
    return lax.dot_general(a, b, (((1,), (1,)), ((), ())), preferred_element_type=F32)


def _sigmoid(x):
    return 1.0 / (1.0 + jnp.exp(-x))


def _params(sem):
    return pltpu.CompilerParams(dimension_semantics=sem, vmem_limit_bytes=VMEM_LIMIT)


def _rope_lanes(a, cos, sin_signed, half):
    if 2 * half == LANES:
        partner = pltpu.roll(a, half, 1)
    else:
        lane = lax.broadcasted_iota(jnp.int32, (1, LANES), 1)
        partner = jnp.where((lane % (2 * half)) < half, pltpu.roll(a, LANES - half, 1), pltpu.roll(a, half, 1))
    return a * cos + partner * sin_signed


def _rope_sublanes(blk, cos_t, sin_t):
    outs = []
    for h in range(blk.shape[0] // NSA_HD):
        x1 = blk[h * NSA_HD:h * NSA_HD + NSA_HD // 2]
        x2 = blk[h * NSA_HD + NSA_HD // 2:(h + 1) * NSA_HD]
        outs += [x1 * cos_t - x2 * sin_t, x2 * cos_t + x1 * sin_t]
    return jnp.concatenate(outs, axis=0)


def _proj_kernel(*refs, prompt, n_w):
    x_ref, g_ref = refs[:2]
    w_parts = refs[2:2 + n_w]
    (wt_ref, cosr_ref, sinr_ref, cosn_ref, sinn_ref, cost_ref, sint_ref,
     qret_ref, kret_ref, vret_ref, rg_ref) = refs[2 + n_w:13 + n_w]
    refs = refs[n_w - 1:]
    x = x_ref[...]
    ms = jnp.mean(x * x, axis=-1, keepdims=True)
    h = (x * lax.rsqrt(ms + RMS_EPS) * g_ref[...]).astype(BF16)

    def proj(lo, hi):
        for part in w_parts:
            if lo < part.shape[0]:
                return _dot_nt(h, part[lo:hi, :])
            lo, hi = lo - part.shape[0], hi - part.shape[0]

    cr, sr = cosr_ref[...], sinr_ref[...]
    cn, sn = cosn_ref[...], sinn_ref[...]
    ct, st = cost_ref[...], sint_ref[...]
    yq = proj(0, 512)
    yk = proj(512, 1024)
    for hh in range(RET_HEADS):
        sl = slice(hh * LANES, (hh + 1) * LANES)
        qret_ref[:, sl] = _rope_lanes(yq[:, sl], cr, sr, RET_DK // 2)
        kret_ref[:, sl] = _rope_lanes(yk[:, sl], cr, sr, RET_DK // 2) * (RET_DK ** -0.5)
    vret_ref[...] = proj(1024, 1536)
    rg_ref[...] = proj(1536, 2048)

    yt = _dot_nt(wt_ref[...], h)
    if prompt:
        (skr_ref, wkr_ref, ck_ref, cv_ref, sv_ref, wv_ref, sk_ref, wk_ref, svb_ref, wvb_ref,
         nqt_ref, nqrt_ref, ngt_ref, nglt_ref) = refs[14:]
    else:
        nq_ref, nqr_ref, ng_ref, ngl_ref, ck_ref, cv_ref, sv_ref, wv_ref, sk_ref, wk_ref = refs[14:]
    ck_ref[0] = yt[0:128]
    cv_ref[0] = yt[128:256]
    sv_ref[0] = yt[256:384]
    wv_ref[0] = yt[384:512]
    sk_ref[0] = _rope_sublanes(yt[512:640], ct, st)
    wk_ref[0] = _rope_sublanes(yt[640:768], ct, st)
    if prompt:
        svb_ref[0] = yt[256:384].astype(BF16)
        wvb_ref[0] = yt[384:512].astype(BF16)
        nqt = yt[768:1280]
        nqt_ref[0] = nqt.astype(BF16)
        nqrt_ref[0] = _rope_sublanes(nqt, ct, st).astype(BF16)
        ngt_ref[0] = yt[1280:1792]
        nglt_ref[0] = yt[1792:1920]
        skr_ref[...] = _rope_lanes(proj(2048, 2176), cn, sn, NSA_HD // 2).astype(BF16)
        wkr_ref[...] = _rope_lanes(proj(2176, 2304), cn, sn, NSA_HD // 2).astype(BF16)
    else:
        yn = proj(2048, 2560)
        for gg in range(NSA_GROUP):
            sl = slice(gg * LANES, (gg + 1) * LANES)
            nq_ref[:, sl] = yn[:, sl].astype(BF16)
            nqr_ref[:, sl] = _rope_lanes(yn[:, sl], cn, sn, NSA_HD // 2).astype(BF16)
        ng_ref[...] = proj(2560, 3072)
        ngl_ref[...] = proj(3072, 3200)


def _rope_tables(pos, half):
    inv = ROPE_THETA ** (-np.arange(half, dtype=np.float64) / half)
    ang = np.asarray(pos, np.float64)[:, None] * inv[None, :]
    return np.cos(ang).astype(np.float32), np.sin(ang).astype(np.float32)


def _proj(x2d, pos_rows, norm_g, w_rm, w_t, kv_batch, tm, prompt):
    n = x2d.shape[0]
    nt = pos_rows.shape[0] // tm
    skv = n // kv_batch
    nkt = skv // tm
    c64, s64 = _rope_tables(pos_rows, 64)
    c32, s32 = _rope_tables(pos_rows, 32)
    cosr = np.concatenate([c64, c64], axis=1)
    sinr = np.concatenate([-s64, s64], axis=1)
    cosn = np.concatenate([c32, c32, c32, c32], axis=1)
    sinn = np.concatenate([-s32, s32, -s32, s32], axis=1)
    cost, sint = np.ascontiguousarray(c32.T), np.ascontiguousarray(s32.T)
    row = lambda w: pl.BlockSpec((tm, w), lambda i: (i, 0))
    tab = pl.BlockSpec((tm, LANES), lambda i: (i % nt, 0))
    tabt = pl.BlockSpec((32, tm), lambda i: (0, i % nt))
    tspec = lambda r: pl.BlockSpec((1, r, tm), lambda i: (i // nkt, 0, i % nkt))
    f = lambda w, dt: jax.ShapeDtypeStruct((n, w), dt)
    ts = lambda r, dt: jax.ShapeDtypeStruct((kv_batch, r, skv), dt)
    kvo, kvs = tspec(KV_WIDTH), ts(KV_WIDTH, F32)
    out_specs = [row(512)] * 4
    out_shape = [f(512, F32)] * 4
    if prompt:
        out_specs += [row(KV_WIDTH)] * 2 + [kvo] * 8 + [tspec(512)] * 3 + [tspec(LANES)]
        out_shape += [f(KV_WIDTH, BF16)] * 2 + [kvs] * 6 + [ts(KV_WIDTH, BF16)] * 2 + [ts(512, BF16)] * 2 + [
            ts(512, F32), ts(LANES, F32)]
    else:
        out_specs += [row(512)] * 3 + [row(LANES)] + [kvo] * 6
        out_shape += [f(512, BF16)] * 2 + [f(512, F32), f(LANES, F32)] + [kvs] * 6
    w_parts = tuple(w_rm) if isinstance(w_rm, (tuple, list)) else (w_rm,)
    whole = lambda a: pl.BlockSpec(a.shape, lambda i: (0, 0))
    return pl.pallas_call(
        functools.partial(_proj_kernel, prompt=prompt, n_w=len(w_parts)),
        grid=(n // tm,),
        in_specs=[row(D_MODEL), pl.BlockSpec((1, D_MODEL), lambda i: (0, 0))] + [whole(w) for w in w_parts]
        + [whole(w_t), tab, tab, tab, tab, tabt, tabt],
        out_specs=out_specs,
        out_shape=out_shape,
        compiler_params=_params(("arbitrary",)),
        name="proj_prompt" if prompt else "proj_sample",
    )(x2d, norm_g, *w_parts, w_t, cosr, sinr, cosn, sinn, cost, sint)


def _ret_kernel(q_ref, k_ref, v_ref, rg_ref, st_ref, dm_ref, qd_ref, kd_ref, sd_ref, gg_ref, gb_ref,
                y_ref, so_ref, st_scr, *, n_chunks):
    c = pl.program_id(1)

    @pl.when(c == 0)
    def _():
        st_scr[...] = st_ref[0]

    for h in range(RET_HEADS):
        sl = slice(h * LANES, (h + 1) * LANES)
        state = st_scr[h]
        for cc in range(q_ref.shape[0] // RET_CHUNK):
            rows = slice(cc * RET_CHUNK, (cc + 1) * RET_CHUNK)
            q = q_ref[rows, sl]
            k = k_ref[rows, sl]
            v = v_ref[rows, sl].astype(BF16)
            s = _dot_nt(q.astype(BF16), k.astype(BF16)) * dm_ref[h]
            intra = _dot(s.astype(BF16), v)
            cross = _dot((q * qd_ref[:, sl]).astype(BF16), state.astype(BF16))
            o = intra + cross
            kd_t = (k * kd_ref[:, sl]).T.astype(BF16)
            state = sd_ref[h] * state + _dot(kd_t, v)
            mu = jnp.mean(o, axis=-1, keepdims=True)
            var = jnp.mean(jnp.square(o - mu), axis=-1, keepdims=True)
            on = (o - mu) * lax.rsqrt(var + GN_EPS)
            gate = rg_ref[rows, sl]
            y_ref[rows, sl] = (on * gg_ref[:, sl] + gb_ref[:, sl]) * (gate * _sigmoid(gate))
        st_scr[h] = state

    @pl.when(c == n_chunks - 1)
    def _():
        so_ref[0] = st_scr[...]


def _retention(q, k, v, rg, state, gn_g, gn_b, n_batch, n_chunks, chunk_len):
    c = RET_CHUNK
    log_g = np.log(1.0 - 2.0 ** (-5.0 - np.arange(RET_HEADS, dtype=np.float64)))
    i = np.arange(c, dtype=np.float64)
    diff = i[:, None] - i[None, :]
    causal = diff >= 0
    f32 = lambda a: np.ascontiguousarray(a, dtype=np.float32)
    dmask = f32(np.where(causal[None], np.exp(np.where(causal, diff, 0.0)[None] * log_g[:, None, None]), 0.0))
    expand = lambda t: f32(np.repeat(t, LANES, axis=1))
    qdec = expand(np.exp((i[:, None] + 1.0) * log_g[None, :]))
    kdec = expand(np.exp((chunk_len - 1.0 - i)[:, None] * log_g[None, :]))
    sdec = f32(np.broadcast_to(np.exp(chunk_len * log_g)[:, None, None], (RET_HEADS, 1, LANES)))
    stspec1 = pl.BlockSpec((1, RET_HEADS, 128, 128), lambda b: (b, 0, 0, 0))
    const1 = lambda shape: pl.BlockSpec(shape, lambda b: (0,) * len(shape))
    if chunk_len < c:
        assert n_chunks == 1
        row1 = pl.BlockSpec((chunk_len, 512), lambda b: (b, 0))
        return pl.pallas_call(
            _ret_short_kernel,
            grid=(n_batch,),
            in_specs=[row1, row1, row1, row1, stspec1, const1((RET_HEADS, c, c)), const1((c, 512)), const1((c, 512)),
                      const1((RET_HEADS, 1, LANES)), const1((1, 512)), const1((1, 512))],
            out_specs=[row1, stspec1],
            out_shape=[jax.ShapeDtypeStruct(q.shape, F32), jax.ShapeDtypeStruct(state.shape, F32)],
            scratch_shapes=[pltpu.VMEM((c, 512), F32)] * 3,
            compiler_params=_params(("arbitrary",)),
            name="retention_short",
        )(q, k, v, rg, state, dmask, qdec, kdec, sdec, gn_g, gn_b)
    per_step = RET_CHUNKS_PER_STEP if n_chunks % RET_CHUNKS_PER_STEP == 0 else 1
    n_steps = n_chunks // per_step
    row = pl.BlockSpec((per_step * c, 512), lambda b, j: (b * n_steps + j, 0))
    stspec = pl.BlockSpec((1, RET_HEADS, 128, 128), lambda b, j: (b, 0, 0, 0))
    const = lambda shape: pl.BlockSpec(shape, lambda b, j: (0,) * len(shape))
    return pl.pallas_call(
        functools.partial(_ret_kernel, n_chunks=n_steps),
        grid=(n_batch, n_steps),
        in_specs=[row, row, row, row, stspec, const((RET_HEADS, c, c)), const((c, 512)), const((c, 512)),
                  const((RET_HEADS, 1, LANES)), const((1, 512)), const((1, 512))],
        out_specs=[row, stspec],
        out_shape=[jax.ShapeDtypeStruct(q.shape, F32), jax.ShapeDtypeStruct(state.shape, F32)],
        scratch_shapes=[pltpu.VMEM((RET_HEADS, 128, 128), F32)],
        compiler_params=_params(("arbitrary", "arbitrary")),
        name="retention",
    )(q, k, v, rg, state, dmask, qdec, kdec, sdec, gn_g, gn_b)


def _ret_short_kernel(q_ref, k_ref, v_ref, rg_ref, st_ref, dm_ref, qd_ref, kd_ref, sd_ref, gg_ref, gb_ref,
                      y_ref, so_ref, k_pad, kd_pad, v_pad):
    t = q_ref.shape[0]

    @pl.when(pl.program_id(0) == 0)
    def _():
        for pad in (k_pad, kd_pad, v_pad):
            pad[...] = jnp.zeros(pad.shape, F32)

    k_pad[0:t, :] = k_ref[...]
    kd_pad[0:t, :] = k_ref[...] * kd_ref[0:t, :]
    v_pad[0:t, :] = v_ref[...]
    for h in range(RET_HEADS):
        sl = slice(h * LANES, (h + 1) * LANES)
        q = q_ref[:, sl]
        v = v_pad[:, sl].astype(BF16)
        state = st_ref[0, h]
        s = _dot_nt(q.astype(BF16), k_pad[:, sl].astype(BF16)) * dm_ref[h, 0:t, :]
        intra = _dot(s.astype(BF16), v)
        cross = _dot((q * qd_ref[0:t, sl]).astype(BF16), state.astype(BF16))
        o = intra + cross
        so_ref[0, h] = sd_ref[h] * state + _dot(kd_pad[:, sl].T.astype(BF16), v)
        mu = jnp.mean(o, axis=-1, keepdims=True)
        var = jnp.mean(jnp.square(o - mu), axis=-1, keepdims=True)
        on = (o - mu) * lax.rsqrt(var + GN_EPS)
        gate = rg_ref[:, sl]
        y_ref[:, sl] = (on * gg_ref[:, sl] + gb_ref[:, sl]) * (gate * _sigmoid(gate))


def _gelu_tanh(x):
    return 0.5 * x * (1.0 + jnp.tanh(np.sqrt(2.0 / np.pi).astype(np.float32) * (x + 0.044715 * (x * x * x))))


def _rows_to_scratch(tiles, scr):
    per_tile = LANES // CMP_BLOCK
    for t, tile in enumerate(tiles):
        rows_pm = tile.T
        for c in range(per_tile):
            r0 = (t * per_tile + c) * BLOCK_PITCH
            scr[r0:r0 + CMP_BLOCK, :] = rows_pm[c * CMP_BLOCK:(c + 1) * CMP_BLOCK]


def _compress_scratch(scr, n_blk, pe_ref, w1_ref, w2_ref, transpose_out=False):
    flat = [(scr[pl.ds(j, n_blk, stride=BLOCK_PITCH), :] + pe_ref[j:j + 1, :]).astype(BF16) for j in range(CMP_BLOCK)]
    hid = _gelu_tanh(_dot(jnp.concatenate(flat, axis=1), w1_ref[...]))
    if transpose_out:
        return _dot_nt(w2_ref[...], hid.astype(BF16))
    return _dot(hid.astype(BF16), w2_ref[...])


def _compress_one(tiles, pe_ref, w1_ref, w2_ref, scr, transpose_out=False):
    _rows_to_scratch(tiles, scr)
    return _compress_scratch(scr, len(tiles) * (LANES // CMP_BLOCK), pe_ref, w1_ref, w2_ref, transpose_out)


def _compress_prompt_kernel(k_ref, v_ref, pek_ref, w1k_ref, w2k_ref, pev_ref, w1v_ref, w2v_ref,
                            kc_ref, vct_ref, scr_k, scr_v, *, n_tiles):
    for src, pe, w1, w2, dst, scr, tr in ((k_ref, pek_ref, w1k_ref, w2k_ref, kc_ref, scr_k, False),
                                          (v_ref, pev_ref, w1v_ref, w2v_ref, vct_ref, scr_v, True)):
        tiles = [src[0, :, t * LANES:(t + 1) * LANES] for t in range(n_tiles)]
        dst[0] = _compress_one(tiles, pe, w1, w2, scr, tr)


def _compress_sample_kernel(pt_ref, ck_hbm, cv_hbm, pek_ref, w1k_ref, w2k_ref, pev_ref, w1v_ref, w2v_ref,
                            kc_ref, vc_ref, kbuf, vbuf, sems, scr_k, scr_v, *, n_pages, n_steps, n_batch):
    n_blk = n_pages * PAGE_SIZE // CMP_BLOCK

    def to_rows(step, slot):
        _rows_to_scratch([kbuf[slot, j] for j in range(n_pages)], scr_k.at[step % 2])
        _rows_to_scratch([vbuf[slot, j] for j in range(n_pages)], scr_v.at[step % 2])

    def compress(step):
        rows = pl.ds(pl.multiple_of(step * n_blk, n_blk), n_blk)
        kc_ref[0, rows, :] = _compress_scratch(scr_k.at[step % 2], n_blk, pek_ref, w1k_ref, w2k_ref)
        vc_ref[0, rows, :] = _compress_scratch(scr_v.at[step % 2], n_blk, pev_ref, w1v_ref, w2v_ref)

    _stream_pages(pt_ref, (ck_hbm, cv_hbm), (kbuf, vbuf), sems, n_pages, n_steps, n_batch, to_rows, compress)


def _compress_weights(pe, w1, w2, transpose_w2=False):
    pe_rows = pe.reshape(CMP_BLOCK, KV_WIDTH)
    z = jnp.zeros((CMP_BLOCK, NSA_HD, NSA_HD), F32)
    w1bd = jnp.concatenate([jnp.concatenate([w1[0], z], axis=2), jnp.concatenate([z, w1[1]], axis=2)], axis=1)
    z2 = jnp.zeros((NSA_HD, NSA_HD), F32)
    w2bd = jnp.concatenate([jnp.concatenate([w2[0], z2], axis=1), jnp.concatenate([z2, w2[1]], axis=1)], axis=0)
    if transpose_w2:
        w2bd = w2bd.T
    return pe_rows, w1bd.reshape(CMP_BLOCK * KV_WIDTH, KV_WIDTH).astype(BF16), w2bd.astype(BF16)


def _cw_specs():
    zero = lambda *a: (0, 0)
    return [pl.BlockSpec((CMP_BLOCK, KV_WIDTH), zero), pl.BlockSpec((CMP_BLOCK * KV_WIDTH, KV_WIDTH), zero),
            pl.BlockSpec((KV_WIDTH, KV_WIDTH), zero)]


def _compress_prompt(ck_t, cv_t, cwk, cwv_t):
    b, _, s = ck_t.shape
    n_blk = s // CMP_BLOCK
    src = pl.BlockSpec((1, KV_WIDTH, s), lambda i: (i, 0, 0))
    return pl.pallas_call(
        functools.partial(_compress_prompt_kernel, n_tiles=s // LANES),
        grid=(b,),
        in_specs=[src, src] + _cw_specs() + _cw_specs(),
        out_specs=[pl.BlockSpec((1, n_blk, KV_WIDTH), lambda i: (i, 0, 0)),
                   pl.BlockSpec((1, KV_WIDTH, n_blk), lambda i: (i, 0, 0))],
        out_shape=[jax.ShapeDtypeStruct((b, n_blk, KV_WIDTH), F32), jax.ShapeDtypeStruct((b, KV_WIDTH, n_blk), F32)],
        scratch_shapes=[pltpu.VMEM((n_blk * BLOCK_PITCH, KV_WIDTH), F32)] * 2,
        compiler_params=_params(("arbitrary",)),
        name="compress_prompt",
    )(ck_t, cv_t, *cwk, *cwv_t)


def _compress_sample(page_table, cache_k_t, cache_v_t, cwk, cwv):
    db, n_pages = page_table.shape
    g = COMPRESS_PAGES_PER_STEP
    steps = n_pages // g
    blk_per_step = g * PAGE_SIZE // CMP_BLOCK
    n_blk = n_pages * PAGE_SIZE // CMP_BLOCK

    hbm = pl.BlockSpec(memory_space=pl.ANY)
    dst = pl.BlockSpec((1, n_blk, KV_WIDTH), lambda b, pt: (b, 0, 0))
    shp = jax.ShapeDtypeStruct((db, n_blk, KV_WIDTH), F32)
    page_buf = pltpu.VMEM((PAGE_SLOTS, g, KV_WIDTH, PAGE_SIZE), F32)
    grid_spec = pltpu.PrefetchScalarGridSpec(
        num_scalar_prefetch=1,
        grid=(db,),
        in_specs=[hbm, hbm] + _cw_specs() + _cw_specs(),
        out_specs=[dst, dst],
        scratch_shapes=[page_buf, page_buf, pltpu.SemaphoreType.DMA((PAGE_SLOTS, 2))]
        + [pltpu.VMEM((2, blk_per_step * BLOCK_PITCH, KV_WIDTH), F32)] * 2,
    )
    return pl.pallas_call(
        functools.partial(_compress_sample_kernel, n_pages=g, n_steps=steps, n_batch=db),
        grid_spec=grid_spec,
        out_shape=[shp, shp],
        compiler_params=_params(("arbitrary",)),
        name="compress_sample",
    )(page_table, cache_k_t, cache_v_t, *cwk, *cwv)


def _stack_heads_t(qt_ref):
    nq = qt_ref.shape[2]
    zero = jnp.zeros((NSA_HD, nq), BF16)
    cols = []
    for k in range(KV_HEADS):
        for g in range(NSA_GROUP):
            h = k * NSA_GROUP + g
            tile = qt_ref[0, h * NSA_HD:(h + 1) * NSA_HD, :] * ATTN_SCALE
            cols.append(jnp.concatenate([tile, zero] if k == 0 else [zero, tile], axis=0))
    return jnp.concatenate(cols, axis=1)


def _rank_select_rows(score, n_blk):
    parts = [score[v * SUBLANES:(v + 1) * SUBLANES] for v in range(n_blk // SUBLANES)]
    ranks = [jnp.zeros(p.shape, jnp.int32) for p in parts]
    for j in range(n_blk):
        col = score[j:j + 1, :]
        for v, part in enumerate(parts):
            ge, gt = (col >= part).astype(jnp.int32), (col > part).astype(jnp.int32)
            if v * SUBLANES > j:
                beats = ge
            elif (v + 1) * SUBLANES - 1 <= j:
                beats = gt
            else:
                row = v * SUBLANES + lax.broadcasted_iota(jnp.int32, (SUBLANES, 1), 0)
                beats = jnp.where(row > j, ge, gt)
            ranks[v] = ranks[v] + beats
    return jnp.concatenate([(r < TOP_N).astype(F32) for r in ranks], axis=0)


def _softmax_tile_t(s, v_t, m_ref, acc_ref, s_max=None):
    half = s.shape[1] // KV_HEADS
    m_old = m_ref[...]
    m_new = jnp.maximum(m_old, jnp.max(s, axis=0, keepdims=True) if s_max is None else s_max)
    alpha = jnp.exp(m_old - m_new)
    pb = jnp.exp(s - m_new).astype(BF16)
    m_ref[...] = m_new
    ones = jnp.ones((NSA_HD, v_t.shape[1]), BF16)
    v0 = jnp.concatenate([v_t[:NSA_HD], ones], axis=0)
    v1 = jnp.concatenate([ones, v_t[NSA_HD:]], axis=0)
    pv = jnp.concatenate([_dot(v0, pb[:, :half]), _dot(v1, pb[:, half:])], axis=1)
    acc_ref[...] = alpha * acc_ref[...] + pv


def _normalized_head_t(acc, k, cols):
    den = (1 - k) * NSA_HD
    return acc[k * NSA_HD:(k + 1) * NSA_HD, cols] / acc[den:den + 1, cols]


def _attn_prompt_kernel(nqt_ref, nqrt_ref, nglt_ref, ngt_ref, kc_ref, vct_ref, sk_ref, svt_ref, wk_ref, wvt_ref,
                        et_ref, ewin_ref, y_ref, lhs_scr, qslc_scr, qwin_scr, ocmp_scr, t_scr, m_scr, acc_scr,
                        s_scr, smax_scr, sw_scr):
    i = pl.program_id(1)
    qb = y_ref.shape[0]
    q0 = i * qb
    cols8 = 2 * NSA_GROUP * qb
    qpos = q0 + lax.broadcasted_iota(jnp.int32, (1, qb), 1)
    lhs_scr[...] = _stack_heads_t(nqt_ref)
    q_rot = _stack_heads_t(nqrt_ref)
    qslc_scr[0:KV_WIDTH, :] = q_rot
    qwin_scr[0:KV_WIDTH, :] = q_rot
    def causal_cap():
        own_keys = q0 + lax.broadcasted_iota(jnp.int32, (qb, 1), 0)
        return jnp.concatenate([jnp.where(own_keys <= qpos, CAP_OPEN, NEG_INF)] * (2 * NSA_GROUP), axis=1)

    n_cmp = kc_ref.shape[1]
    s_all = _dot(kc_ref[0].astype(BF16), lhs_scr[...])
    cend = (lax.broadcasted_iota(jnp.int32, (n_cmp, 1), 0) + 1) * CMP_BLOCK - 1
    cmask = cend <= qpos
    vct = vct_ref[0].astype(BF16)
    imps = []
    for k in range(KV_HEADS):
        imp = None
        for g in range(NSA_GROUP):
            c0 = (k * NSA_GROUP + g) * qb
            s = jnp.where(cmask, s_all[:, c0:c0 + qb], NEG_INF)
            p = jnp.exp(s - jnp.max(s, axis=0, keepdims=True))
            p = p / jnp.sum(p, axis=0, keepdims=True) * cmask.astype(F32)
            imp = p if imp is None else imp + p
            ocmp_scr[:, c0:c0 + qb] = _dot(vct, p.astype(BF16))
        imps.append(imp)

    n_slc = n_cmp // 2
    blk = lax.broadcasted_iota(jnp.int32, (n_slc, 1), 0)
    valid = blk * SLC_BLOCK <= qpos

    def store_bias(k, sel):
        bias = jnp.concatenate([(sel - 1.0) * (-NEG_INF), jnp.zeros((LANES - n_slc, qb), F32)], axis=0).astype(BF16)
        c0 = k * NSA_GROUP * qb
        qslc_scr[KV_WIDTH:, c0:c0 + NSA_GROUP * qb] = jnp.concatenate([bias] * NSA_GROUP, axis=1)

    @pl.when(q0 + qb <= TOP_N * SLC_BLOCK)
    def _():
        for k in range(KV_HEADS):
            store_bias(k, valid.astype(F32))

    @pl.when(q0 + qb > TOP_N * SLC_BLOCK)
    def _():
        forced = (blk == 0) | (blk == qpos // SLC_BLOCK)
        for k in range(KV_HEADS):
            pairs = []
            for u in range(qb // LANES):
                t_scr[u] = imps[k][:, u * LANES:(u + 1) * LANES]
                pairs.append(t_scr[u, pl.ds(0, n_slc, stride=2), :] + t_scr[u, pl.ds(1, n_slc, stride=2), :])
            pair = jnp.concatenate(pairs, axis=1)
            score = jnp.where(forced, FORCED_SCORE, jnp.where(valid, pair, NEG_INF))
            store_bias(k, _rank_select_rows(score, n_slc))

    tk = SLC_KEY_TILE
    m_scr[...] = jnp.full(m_scr.shape, M_INIT, F32)
    acc_scr[...] = jnp.zeros(acc_scr.shape, F32)

    def scores(t):
        k0 = pl.multiple_of(t * tk, tk)
        keys = jnp.concatenate([sk_ref[pl.ds(k0, tk), :], et_ref[pl.ds(k0, tk), :]], axis=1)
        s = _dot(keys, qslc_scr[...])
        s_scr[t % 2] = s
        smax_scr[t % 2] = jnp.max(s, axis=0, keepdims=True)

    def update(t):
        k0 = pl.multiple_of(t * tk, tk)
        _softmax_tile_t(s_scr[t % 2], svt_ref[0, :, pl.ds(k0, tk)], m_scr.at[0], acc_scr.at[0], smax_scr[t % 2])

    def slc_step(t, carry):
        scores(t + 1)
        update(t)
        return carry

    n_full = q0 // tk

    @pl.when(n_full >= 1)
    def _():
        scores(0)
        lax.fori_loop(0, n_full - 1, slc_step, 0)
        update(n_full - 1)

    k_rem = pl.multiple_of(n_full * tk, tk)
    for v in range(tk // qb):
        @pl.when((q0 - k_rem) // qb == v)
        def _():
            r = (v + 1) * qb
            keys = jnp.concatenate([sk_ref[pl.ds(k_rem, r), :], et_ref[pl.ds(k_rem, r), :]], axis=1)
            s = _dot(keys, qslc_scr[...])
            own = jnp.minimum(s[r - qb:], causal_cap())
            s = own if v == 0 else jnp.concatenate([s[:r - qb], own], axis=0)
            _softmax_tile_t(s, svt_ref[0, :, pl.ds(k_rem, r)], m_scr.at[0], acc_scr.at[0])

    wlen = WINDOW + qb
    w0 = pl.multiple_of(jnp.maximum(i - WINDOW // qb, 0) * qb, qb)
    later = (w0 + lax.broadcasted_iota(jnp.int32, (LANES, 1), 0) * qb) > q0
    qwin_scr[KV_WIDTH:, :] = jnp.broadcast_to(jnp.where(later, NEG_INF, 0.0), (LANES, cols8)).astype(BF16)
    keys = jnp.concatenate([wk_ref[pl.ds(w0, wlen), :], ewin_ref[...]], axis=1)
    sw_scr[...] = _dot(keys, qwin_scr[...])
    edge_keys = w0 + lax.broadcasted_iota(jnp.int32, (qb, 1), 0)
    edge_cap = jnp.concatenate([jnp.where(qpos - edge_keys <= WINDOW, CAP_OPEN, NEG_INF)] * (2 * NSA_GROUP), axis=1)
    sw_scr[0:qb, :] = jnp.minimum(sw_scr[0:qb, :], edge_cap)
    own = pl.ds(pl.multiple_of(q0 - w0, qb), qb)
    sw_scr[own, :] = jnp.minimum(sw_scr[own, :], causal_cap())
    _softmax_tile_t(sw_scr[...], wvt_ref[0, :, pl.ds(w0, wlen)], m_scr.at[1], acc_scr.at[1])

    acc_slc, acc_win = acc_scr[0], acc_scr[1]
    sig = _sigmoid(nglt_ref[0])
    for pair in range(NSA_HEADS // 2):
        tiles = []
        for h in (2 * pair, 2 * pair + 1):
            k = h // NSA_GROUP
            rows = slice(k * NSA_HD, (k + 1) * NSA_HD)
            cols = slice(h * qb, (h + 1) * qb)
            gate = lambda r: sig[r * NSA_HEADS + h:r * NSA_HEADS + h + 1, :]
            o = (gate(0) * ocmp_scr[rows, cols] + gate(1) * _normalized_head_t(acc_slc, k, cols)
                 + gate(2) * _normalized_head_t(acc_win, k, cols))
            ng = ngt_ref[0, h * NSA_HD:(h + 1) * NSA_HD, :]
            tiles.append(o * (ng * _sigmoid(ng)))
        y_ref[:, pair * LANES:(pair + 1) * LANES] = jnp.concatenate(tiles, axis=0).T.astype(y_ref.dtype)


def _attn_prompt(nq_t, nqr_t, ngl_t, ng_t, kc, vc_t, sk_rm, sv_t, wk_rm, wv_t):
    b, _, s = sv_t.shape
    assert s // CMP_BLOCK == LANES, "one lane per compressed block"
    qb = PROMPT_Q_TILE
    nqb = s // qb
    cols8 = 2 * NSA_GROUP * qb
    expand_t = (np.arange(s)[:, None] // SLC_BLOCK == np.arange(LANES)[None, :]).astype(BF16)
    wlen = WINDOW + qb
    win_blocks = (np.arange(wlen)[:, None] // qb == np.arange(LANES)[None, :]).astype(BF16)
    qcol = lambda r: pl.BlockSpec((1, r, qb), lambda bb, i: (bb, 0, i))
    per_b = lambda shape: pl.BlockSpec((1,) + shape, lambda bb, i: (bb, 0, 0))
    rows_b = pl.BlockSpec((s, KV_WIDTH), lambda bb, i: (bb, 0))
    return pl.pallas_call(
        _attn_prompt_kernel,
        grid=(b, nqb),
        in_specs=[qcol(512), qcol(512), qcol(LANES), qcol(512), per_b(kc.shape[1:]), per_b(vc_t.shape[1:]),
                  rows_b, per_b((KV_WIDTH, s)), rows_b, per_b((KV_WIDTH, s)),
                  pl.BlockSpec(expand_t.shape, lambda bb, i: (0, 0)),
                  pl.BlockSpec(win_blocks.shape, lambda bb, i: (0, 0))],
        out_specs=pl.BlockSpec((qb, 512), lambda bb, i: (bb * nqb + i, 0)),
        out_shape=jax.ShapeDtypeStruct((b * s, 512), BF16),
        scratch_shapes=[pltpu.VMEM((KV_WIDTH, cols8), BF16), pltpu.VMEM((2 * KV_WIDTH, cols8), BF16),
                        pltpu.VMEM((2 * KV_WIDTH, cols8), BF16),
                        pltpu.VMEM((KV_WIDTH, cols8), F32), pltpu.VMEM((qb // LANES, LANES, LANES), F32),
                        pltpu.VMEM((2, 1, cols8), F32), pltpu.VMEM((2, KV_WIDTH, cols8), F32),
                        pltpu.VMEM((2, SLC_KEY_TILE, cols8), F32), pltpu.VMEM((2, 1, cols8), F32),
                        pltpu.VMEM((wlen, cols8), F32)],
        compiler_params=_params(("arbitrary", "arbitrary")),
        name="attn_prompt",
    )(nq_t, nqr_t, ngl_t, ng_t, kc, vc_t, sk_rm, sv_t, wk_rm, wv_t, expand_t, win_blocks)


def _stack_heads(q_ref):
    lane = lax.broadcasted_iota(jnp.int32, (1, LANES), 1)
    lo = lane < NSA_HD
    slabs = [q_ref[:, g * LANES:(g + 1) * LANES] for g in range(NSA_GROUP)]
    zero = jnp.zeros_like(slabs[0])
    stacked = jnp.concatenate([jnp.where(lo, s, zero) for s in slabs] + [jnp.where(lo, zero, s) for s in slabs], axis=0)
    return stacked.astype(BF16)


def _mask_rows(sc, masks, rows):
    n = sc.shape[-1]
    s4 = sc.reshape(2 * NSA_GROUP, rows, n)
    out = [jnp.where(masks[k][None], s4[k * NSA_GROUP:(k + 1) * NSA_GROUP], NEG_INF) for k in range(KV_HEADS)]
    return jnp.concatenate(out, axis=0).reshape(2 * NSA_GROUP * rows, n)


def _pair_scores(imp, qpos):
    n_slc = imp.shape[1] // 2
    blk = lax.broadcasted_iota(jnp.int32, (1, n_slc), 1)
    valid = blk * SLC_BLOCK <= qpos
    forced = (blk == 0) | (blk == qpos // SLC_BLOCK)
    return jnp.where(forced, FORCED_SCORE, jnp.where(valid, imp[:, :n_slc] + imp[:, n_slc:], NEG_INF))


def _rank_select(score, extra_forced):
    n_blk = score.shape[1]
    blk = lax.broadcasted_iota(jnp.int32, (1, n_blk), 1)
    rank = jnp.zeros(score.shape, jnp.int32)
    for j in range(n_blk):
        col = score[:, j:j + 1]
        beats = (col > score) | ((col == score) & (j < blk))
        rank = rank + beats.astype(jnp.int32)
    if extra_forced:
        rank = rank + (score < FORCED_SCORE).astype(jnp.int32)
    return rank < TOP_N


def _merge_kv_heads(acc, rows):
    lane = lax.broadcasted_iota(jnp.int32, (1, LANES), 1)
    lo = lane < NSA_HD
    half = NSA_GROUP * rows
    return [jnp.where(lo, acc[g * rows:(g + 1) * rows], acc[half + g * rows:half + (g + 1) * rows])
            for g in range(NSA_GROUP)]


def _gate_and_store(o_cmp, o_slc, o_win, ngl_ref, ng_ref, y_ref):
    lane = lax.broadcasted_iota(jnp.int32, (1, LANES), 1)
    lo = lane < NSA_HD
    sig = _sigmoid(ngl_ref[...])
    for g in range(NSA_GROUP):
        gates = [jnp.where(lo, sig[:, r * 8 + g:r * 8 + g + 1], sig[:, r * 8 + 4 + g:r * 8 + 4 + g + 1]) for r in range(3)]
        o = gates[0] * o_cmp[g] + gates[1] * o_slc[g] + gates[2] * o_win[g]
        gate = ng_ref[:, g * LANES:(g + 1) * LANES]
        y_ref[:, g * LANES:(g + 1) * LANES] = (o * (gate * _sigmoid(gate))).astype(y_ref.dtype)


def _page_copies(pt_ref, b, chunk, slot, caches, bufs, sems, g):
    out = []
    for ci, (cache, buf) in enumerate(zip(caches, bufs)):
        for j in range(g):
            page = pt_ref[b, chunk * g + j]
            out.append(pltpu.make_async_copy(cache.at[page], buf.at[slot, j], sems.at[slot, ci]))
    return out


def _stream_pages(pt_ref, caches, bufs, sems, g, n_chunks, n_batch, compute, finish=None):
    b = pl.program_id(0)
    total = n_batch * n_chunks
    n_slots = bufs[0].shape[0]
    depth = n_slots - 1
    assert n_chunks >= depth
    copies = functools.partial(_page_copies, pt_ref, caches=caches, bufs=bufs, sems=sems, g=g)

    @pl.when(b == 0)
    def _():
        for d in range(depth):
            for cp in copies(0, d, d):
                cp.start()

    def visit(c):
        flat = b * n_chunks + c
        ahead = jnp.minimum(flat + depth, total - 1)
        for cp in copies(ahead // n_chunks, ahead % n_chunks, (flat + depth) % n_slots):
            cp.start()
        for cp in copies(b, c, flat % n_slots):
            cp.wait()
        compute(c, flat % n_slots)

    def chunk(c, carry):
        visit(c)
        if finish is not None:
            finish(c - 1)
        return carry

    if finish is None:
        lax.fori_loop(0, n_chunks, chunk, 0)
    else:
        visit(0)
        for c in range(1, n_chunks):
            chunk(c, 0)
        finish(n_chunks - 1)

    @pl.when(b == n_batch - 1)
    def _():
        for d in range(depth):
            for cp in copies(n_batch - 1, n_chunks - 1, (total + d) % n_slots):
                cp.wait()


def _attn_sample_kernel(pt_ref, sk_hbm, sv_hbm, nq_ref, nqr_ref, ngl_ref, ng_ref, kc_ref, vc_ref, skn_ref, svn_ref,
                        wko_ref, wvo_ref, wkn_ref, wvn_ref, e_ref, y_ref, kbuf, vbuf, sems, m_scr, l_scr, acc_scr,
                        ocmp_scr, owin_scr, mask_scr, *, n_pages, n_steps, n_batch, t_len, past_len):
    t = t_len
    nq_ref, nqr_ref, ngl_ref, ng_ref, y_ref = (r.at[0] for r in (nq_ref, nqr_ref, ngl_ref, ng_ref, y_ref))
    rows8 = 2 * NSA_GROUP * t
    half = NSA_GROUP * t
    tpos = lax.broadcasted_iota(jnp.int32, (t, 1), 0)
    qpos = past_len + tpos
    n_cmp = kc_ref.shape[1]
    wb = wko_ref.shape[2]
    step_keys = n_pages * PAGE_SIZE

    def before_pages():
        n_slc = n_cmp // 2
        halves = lambda ref: jnp.concatenate([ref[0, pl.ds(0, n_slc, stride=2), :],
                                              ref[0, pl.ds(1, n_slc, stride=2), :]], axis=0).astype(BF16)
        lhs = _stack_heads(nq_ref)
        s = _dot_nt(lhs, halves(kc_ref)) * ATTN_SCALE
        m = jnp.max(s, axis=-1, keepdims=True)
        p = jnp.exp(s - m)
        p = p / jnp.sum(p, axis=-1, keepdims=True)
        oc = _dot(p.astype(BF16), halves(vc_ref))
        for g, slab in enumerate(_merge_kv_heads(oc, t)):
            ocmp_scr[g] = slab
        p4 = p.reshape(8, t, n_cmp)
        sels = []
        for k in range(KV_HEADS):
            imp = p4[k * NSA_GROUP]
            for g in range(1, NSA_GROUP):
                imp = imp + p4[k * NSA_GROUP + g]
            sels.append(_rank_select(_pair_scores(imp, qpos), True).astype(F32))
        sel = jnp.concatenate(sels, axis=0).astype(BF16)
        span = LANES * SLC_BLOCK
        for c in range(n_slc // LANES):
            km = _dot(sel[:, c * LANES:(c + 1) * LANES], e_ref[...])
            for k in range(KV_HEADS):
                kmk = km[k * t:(k + 1) * t]
                if step_keys >= span:
                    off = (c * span) % step_keys
                    mask_scr[(c * span) // step_keys, k, :, off:off + span] = kmk
                else:
                    per = span // step_keys
                    for u in range(per):
                        mask_scr[c * per + u, k] = kmk[:, u * step_keys:(u + 1) * step_keys]

        lhs_r = _stack_heads(nqr_ref)
        so = _dot(lhs_r, wko_ref[0].astype(BF16)) * ATTN_SCALE
        sn = _dot(lhs_r, wkn_ref[0].astype(BF16)) * ATTN_SCALE
        jo = lax.broadcasted_iota(jnp.int32, (1, wb), 1)
        jn = lax.broadcasted_iota(jnp.int32, (1, LANES), 1)
        old_vis = jo >= tpos
        so = _mask_rows(so, [old_vis, old_vis], t)
        new_vis = jn <= tpos
        sn_w = _mask_rows(sn, [new_vis, new_vis], t)
        m = jnp.maximum(jnp.max(so, axis=-1, keepdims=True), jnp.max(sn_w, axis=-1, keepdims=True))
        po = jnp.exp(so - m)
        pn = jnp.exp(sn_w - m)
        den = jnp.sum(po, axis=-1, keepdims=True) + jnp.sum(pn, axis=-1, keepdims=True)
        pob, pnb = (po / den).astype(BF16), (pn / den).astype(BF16)
        wvo, wvn = wvo_ref[0].astype(BF16), wvn_ref[0].astype(BF16)
        ow = jnp.concatenate([_dot_nt(pob[:half], wvo) + _dot_nt(pnb[:half], wvn),
                              _dot_nt(pob[half:], wvo) + _dot_nt(pnb[half:], wvn)], axis=0)
        for g, slab in enumerate(_merge_kv_heads(ow, t)):
            owin_scr[g] = slab

        s2 = _dot(lhs_r, skn_ref[0].astype(BF16)) * ATTN_SCALE
        s2 = _mask_rows(s2, [new_vis, new_vis], t)
        m2 = jnp.max(s2, axis=-1, keepdims=True)
        p2 = jnp.exp(s2 - m2)
        m_scr[0] = m2
        l_scr[0] = jnp.sum(p2, axis=-1, keepdims=True)
        p2b = p2.astype(BF16)
        svn = svn_ref[0].astype(BF16)
        acc_scr[0] = jnp.concatenate([_dot_nt(p2b[:half], svn), _dot_nt(p2b[half:], svn)], axis=0)
        for st in range(1, SAMPLE_STREAMS):
            m_scr[st] = jnp.full((rows8, 1), M_INIT, F32)
            l_scr[st] = jnp.zeros((rows8, 1), F32)
            acc_scr[st] = jnp.zeros((rows8, LANES), F32)

    def on_pages(step, slot):
        lhs_r = _stack_heads(nqr_ref)
        per = n_pages // SAMPLE_STREAMS
        msk = mask_scr[step]
        for st in range(SAMPLE_STREAMS):
            pages = range(st * per, (st + 1) * per)
            keys = slice(st * per * PAGE_SIZE, (st + 1) * per * PAGE_SIZE)
            kt = jnp.concatenate([kbuf[slot, j].astype(BF16) for j in pages], axis=1)
            vt = jnp.concatenate([vbuf[slot, j].astype(BF16) for j in pages], axis=1)
            sc = _dot(lhs_r, kt) * ATTN_SCALE
            sc = _mask_rows(sc, [msk[k][:, keys] > 0.5 for k in range(KV_HEADS)], t)
            m_i = m_scr[st]
            m_new = jnp.maximum(m_i, jnp.max(sc, axis=-1, keepdims=True))
            alpha = jnp.exp(m_i - m_new)
            pr = jnp.exp(sc - m_new)
            l_scr[st] = alpha * l_scr[st] + jnp.sum(pr, axis=-1, keepdims=True)
            m_scr[st] = m_new
            pb = pr.astype(BF16)
            pv = jnp.concatenate([_dot_nt(pb[:half], vt), _dot_nt(pb[half:], vt)], axis=0)
            acc_scr[st] = alpha * acc_scr[st] + pv

    before_pages()
    _stream_pages(pt_ref, (sk_hbm, sv_hbm), (kbuf, vbuf), sems, n_pages, n_steps, n_batch, on_pages)
    m_all = m_scr[0]
    for st in range(1, SAMPLE_STREAMS):
        m_all = jnp.maximum(m_all, m_scr[st])
    l_all = jnp.zeros((rows8, 1), F32)
    acc_all = jnp.zeros((rows8, LANES), F32)
    for st in range(SAMPLE_STREAMS):
        w = jnp.exp(m_scr[st] - m_all)
        l_all = l_all + w * l_scr[st]
        acc_all = acc_all + w * acc_scr[st]
    o_slc = _merge_kv_heads(acc_all / l_all, t)
    o_cmp = [ocmp_scr[g] for g in range(NSA_GROUP)]
    o_win = [owin_scr[g] for g in range(NSA_GROUP)]
    _gate_and_store(o_cmp, o_slc, o_win, ngl_ref, ng_ref, y_ref)


def _attn_sample(page_table, cache_sk_t, cache_sv_t, nq, nqr, ngl, ng, kc, vc, skn, svn, wko, wvo, wkn, wvn, t_len):
    db, n_pages = page_table.shape
    g = SAMPLE_PAGES_PER_STEP
    steps = n_pages // g
    past_len = n_pages * PAGE_SIZE
    step_keys = g * PAGE_SIZE
    span = LANES * SLC_BLOCK
    expand = (np.arange(span)[None, :] // SLC_BLOCK == np.arange(LANES)[:, None]).astype(BF16)

    hbm = pl.BlockSpec(memory_space=pl.ANY)
    row = lambda w: pl.BlockSpec((1, t_len, w), lambda b, pt: (b, 0, 0))
    per_b = lambda shape: pl.BlockSpec((1,) + shape, lambda b, pt: (b, 0, 0))
    rows8 = 2 * NSA_GROUP * t_len
    page_buf = pltpu.VMEM((PAGE_SLOTS, g, KV_WIDTH, PAGE_SIZE), F32)
    grid_spec = pltpu.PrefetchScalarGridSpec(
        num_scalar_prefetch=1,
        grid=(db,),
        in_specs=[hbm, hbm, row(512), row(512), row(LANES), row(512), per_b(kc.shape[1:]), per_b(vc.shape[1:]),
                  per_b(skn.shape[1:]), per_b(svn.shape[1:]), per_b(wko.shape[1:]), per_b(wvo.shape[1:]),
                  per_b(wkn.shape[1:]), per_b(wvn.shape[1:]), pl.BlockSpec(expand.shape, lambda b, pt: (0, 0))],
        out_specs=row(512),
        scratch_shapes=[page_buf, page_buf, pltpu.SemaphoreType.DMA((PAGE_SLOTS, 2)),
                        pltpu.VMEM((SAMPLE_STREAMS, rows8, 1), F32), pltpu.VMEM((SAMPLE_STREAMS, rows8, 1), F32),
                        pltpu.VMEM((SAMPLE_STREAMS, rows8, LANES), F32),
                        pltpu.VMEM((NSA_GROUP, t_len, LANES), F32), pltpu.VMEM((NSA_GROUP, t_len, LANES), F32),
                        pltpu.VMEM((steps, KV_HEADS, t_len, step_keys), F32)],
    )
    return pl.pallas_call(
        functools.partial(_attn_sample_kernel, n_pages=g, n_steps=steps, n_batch=db, t_len=t_len, past_len=past_len),
        grid_spec=grid_spec,
        out_shape=jax.ShapeDtypeStruct((db, t_len, 512), F32),
        compiler_params=_params(("arbitrary",)),
        name="attn_sample",
    )(page_table, cache_sk_t, cache_sv_t, nq, nqr, ngl, ng, kc, vc, skn, svn, wko, wvo, wkn, wvn, expand)


def _finish_kernel(x_ref, yr_ref, yn_ref, p_ref, wo_ref, gple_ref, wg_ref, wp_ref, gf_ref, o_ref):
    x = x_ref[...]
    x = x + _dot(yr_ref[...].astype(BF16), wo_ref[0:RET_WIDTH, :]) + _dot(yn_ref[...].astype(BF16), wo_ref[RET_WIDTH:, :])
    ms = jnp.mean(x * x, axis=-1, keepdims=True)
    hn = (x * lax.rsqrt(ms + RMS_EPS) * gple_ref[...]).astype(BF16)
    gate = _sigmoid(_dot(hn, wg_ref[...]))
    x = x + gate * _dot(p_ref[...].astype(BF16), wp_ref[...])
    ms = jnp.mean(x * x, axis=-1, keepdims=True)
    o_ref[...] = x * lax.rsqrt(ms + RMS_EPS) * gf_ref[...]


def _finish(x2d, y_ret, y_nsa, p2d, w_out, norm_ple, w_gate, w_ple, norm_f, tm):
    n = x2d.shape[0]
    row = lambda w: pl.BlockSpec((tm, w), lambda i: (i, 0))
    const = lambda shape: pl.BlockSpec(shape, lambda i: (0, 0))
    return pl.pallas_call(
        _finish_kernel,
        grid=(n // tm,),
        in_specs=[row(D_MODEL), row(RET_WIDTH), row(NSA_WIDTH), row(PLE_DIM), const(w_out.shape), const((1, D_MODEL)),
                  const(w_gate.shape), const(w_ple.shape), const((1, D_MODEL))],
        out_specs=row(D_MODEL),
        out_shape=jax.ShapeDtypeStruct((n, D_MODEL), F32),
        compiler_params=_params(("arbitrary",)),
        name="finish",
    )(x2d, y_ret, y_nsa, p2d, w_out, norm_ple, w_gate, w_ple, norm_f)


def _slab_perm():
    return np.array([(k * NSA_GROUP + g) * NSA_HD + d for g in range(NSA_GROUP) for k in range(KV_HEADS)
                     for d in range(NSA_HD)], np.int32)


def _to_positions_major(x_t):
    lead = x_t.shape[:-2]
    n = len(lead)
    x4 = x_t.reshape(lead + (KV_HEADS, NSA_HD, x_t.shape[-1]))
    return jnp.transpose(x4, tuple(range(n)) + (n + 2, n, n + 1))


def _cache_t(cache):
    n_pool, page = cache.shape[:2]
    return jnp.transpose(cache, (0, 2, 3, 1)).reshape(n_pool, KV_WIDTH, page)


def _layer(xp, xs, c_ck, c_cv, c_sk, c_sv, win_k, win_v, ret_state, page_table, p_p, p_s, norm_mix, w_in, gn_g, gn_b,
           pe_k, w1_k, w2_k, pe_v, w1_v, w2_v, w_out, norm_ple, w_gate, w_ple, norm_f):
    b, s, d = xp.shape
    db, t, _ = xs.shape
    n_pages = page_table.shape[1]
    past = n_pages * PAGE_SIZE

    off = np.cumsum((0,) + SPLIT_SIZES)
    w_in_t = w_in.T
    rows = lambda i: w_in_t[off[i]:off[i + 1]]
    perm = _slab_perm()
    ngl_rows = jnp.pad(rows(11), ((0, LANES - N_GATES), (0, 0)))
    kv_rows = [rows(5), rows(6), rows(8), rows(10), rows(7), rows(9)]
    w_rm_p = tuple(w.astype(BF16) for w in (w_in_t[:off[4]], rows(7), rows(9)))
    w_t_p = jnp.concatenate(kv_rows + [rows(4), rows(12), ngl_rows], axis=0).astype(BF16)
    w_rm_s = tuple(w.astype(BF16) for w in (w_in_t[:off[4]], rows(4)[perm], rows(12)[perm], ngl_rows))
    w_t_s = jnp.concatenate(kv_rows, axis=0).astype(BF16)
    w_out_b = w_out.astype(BF16)
    w_out_slab = jnp.concatenate([w_out[:RET_WIDTH], w_out[RET_WIDTH:][perm]], axis=0).astype(BF16)
    g_mix = norm_mix.reshape(1, d)
    cwk = _compress_weights(pe_k, w1_k, w2_k)
    cwv = _compress_weights(pe_v, w1_v, w2_v)
    cwv_t = _compress_weights(pe_v, w1_v, w2_v, transpose_w2=True)
    gn_g2, gn_b2 = gn_g.reshape(1, RET_WIDTH), gn_b.reshape(1, RET_WIDTH)
    fin_w = (norm_ple.reshape(1, d), w_gate.astype(BF16), w_ple.astype(BF16), norm_f.reshape(1, d))

    tm = PROMPT_ROW_TILE
    (q_ret, k_ret, v_ret, rg, sk_rm, wk_rm, ck_t, cv_t, sv_t, wv_t, sk_t, wk_t, sv_b, wv_b, nq_t, nqr_t, ng_t,
     ngl_t) = _proj(xp.reshape(b * s, d), np.arange(s), g_mix, w_rm_p, w_t_p, b, tm, True)
    y_ret, ret_p = _retention(q_ret, k_ret, v_ret, rg, jnp.zeros((b, RET_HEADS, RET_DK, RET_DK), F32), gn_g2, gn_b2,
                              b, s // RET_CHUNK, RET_CHUNK)
    kc, vc_t = _compress_prompt(ck_t, cv_t, cwk, cwv_t)
    y_nsa = _attn_prompt(nq_t, nqr_t, ngl_t, ng_t, kc, vc_t, sk_rm, sv_b, wk_rm, wv_b)
    y_prompt = _finish(xp.reshape(b * s, d), y_ret, y_nsa, p_p.reshape(b * s, PLE_DIM), w_out_b, *fin_w,
                       FINISH_ROW_TILE)
    wb_p = min(WINDOW, s)
    prompt_states = (ret_p, _to_positions_major(ck_t), _to_positions_major(cv_t), _to_positions_major(sk_t),
                     _to_positions_major(sv_t), _to_positions_major(wk_t[:, :, s - wb_p:]),
                     _to_positions_major(wv_t[:, :, s - wb_p:]))

    n = db * t
    pos_s = np.tile(past + np.arange(t), db)
    (q_ret, k_ret, v_ret, rg, nq, nqr, ng, ngl, ck_n, cv_n, sv_n, wv_n, sk_n, wk_n) = _proj(
        xs.reshape(n, d), pos_s, g_mix, w_rm_s, w_t_s, 1, n, False)
    y_ret, ret_s = _retention(q_ret, k_ret, v_ret, rg, ret_state, gn_g2, gn_b2, db, 1, t)
    kc, vc = _compress_sample(page_table, _cache_t(c_ck), _cache_t(c_cv), cwk, cwv)
    per_batch = lambda a: jnp.transpose(a[0].reshape(KV_WIDTH, db, t), (1, 0, 2))
    lane_pad = lambda a: jnp.pad(per_batch(a), ((0, 0), (0, 0), (0, LANES - t)))
    win_t = lambda w: jnp.transpose(w, (0, 2, 3, 1)).reshape(db, KV_WIDTH, w.shape[1])
    rows3 = lambda a: a.astype(F32).reshape(db, t, a.shape[-1])
    y_nsa = _attn_sample(page_table, _cache_t(c_sk), _cache_t(c_sv), rows3(nq), rows3(nqr), rows3(ngl), rows3(ng), kc, vc,
                         lane_pad(sk_n), lane_pad(sv_n), win_t(win_k), win_t(win_v), lane_pad(wk_n), lane_pad(wv_n), t)
    y_nsa = y_nsa.reshape(n, NSA_WIDTH)
    y_sample = _finish(xs.reshape(n, d), y_ret, y_nsa, p_s.reshape(n, PLE_DIM), w_out_slab, *fin_w, n)
    new_rows = lambda a: _to_positions_major(per_batch(a))
    sample_states = (ret_s, new_rows(ck_n), new_rows(cv_n), new_rows(sk_n), new_rows(sv_n),
                     jnp.concatenate([win_k[:, t:], new_rows(wk_n)], axis=1),
                     jnp.concatenate([win_v[:, t:], new_rows(wv_n)], axis=1))
    return y_prompt.reshape(b, s, d), y_sample.reshape(db, t, d), prompt_states, sample_states


def kernel(x_prompt, x_sample, cache_cmp_k, cache_cmp_v, cache_slc_k, cache_slc_v, state_win_k, state_win_v, state_ret, page_table, p_prompt, p_sample, norm_mix, w_in, ret_gn_g, ret_gn_b, cmp_pe_k, cmp_w1_k, cmp_w2_k, cmp_pe_v, cmp_w1_v, cmp_w2_v, w_out, norm_ple, w_ple_gate, w_ple, norm_f):
    depth = w_in.shape[0]
    assert depth == 1, "single trunk layer"
    l = 0
    yp, ys, sp, ss = _layer(x_prompt, x_sample, cache_cmp_k[l], cache_cmp_v[l], cache_slc_k[l], cache_slc_v[l],
                            state_win_k[l], state_win_v[l], state_ret[l], page_table, p_prompt[l], p_sample[l],
                            norm_mix[l], w_in[l], ret_gn_g[l], ret_gn_b[l], cmp_pe_k[l], cmp_w1_k[l], cmp_w2_k[l],
                            cmp_pe_v[l], cmp_w1_v[l], cmp_w2_v[l], w_out[l], norm_ple[l], w_ple_gate[l], w_ple[l], norm_f)
    return (yp, ys) + tuple(a[None] for a in sp) + tuple(a[None] for a in ss)
```

```python
import functools

import numpy as np
import jax
import jax.numpy as jnp
from jax import lax
from jax.experimental import pallas as pl
from jax.experimental.pallas import tpu as pltpu

F32 = jnp.float32
BF16 = jnp.bfloat16

D_MODEL = 1024
PLE_DIM = 256
PAGE_SIZE = 128
RET_HEADS = 4
RET_DK = 128
RET_WIDTH = 512
RET_CHUNK = 128
NSA_HEADS = 8
KV_HEADS = 2
NSA_GROUP = 4
NSA_HD = 64
NSA_WIDTH = 512
KV_WIDTH = 128
CMP_BLOCK = 32
SLC_BLOCK = 64
TOP_N = 16
WINDOW = 512
ROPE_THETA = 10000.0
RMS_EPS = 1e-6
GN_EPS = 1e-5
NEG_INF = -1e9
FORCED_SCORE = 1e4
M_INIT = -1e30
CAP_OPEN = 3e38
ATTN_SCALE = NSA_HD ** -0.5
SPLIT_SIZES = (512, 512, 512, 512, 512, 128, 128, 128, 128, 128, 128, 24, 512)
N_GATES = 3 * NSA_HEADS

LANES = 128
SUBLANES = 8
PROMPT_Q_TILE = 128
SLC_KEY_TILE = 1024
RET_CHUNKS_PER_STEP = 8
PROMPT_ROW_TILE = 512
FINISH_ROW_TILE = 512
SAMPLE_PAGES_PER_STEP = 64
SAMPLE_STREAMS = 1
COMPRESS_PAGES_PER_STEP = 32
PAGE_SLOTS = 3
DMA_THREADS = 2
BLOCK_PITCH = 40
VMEM_LIMIT = 56 * 1024 * 1024


def _dot(a, b):
    return jnp.dot(a, b, preferred_element_type=F32)


def _dot_nt(a, b):
    return lax.dot_general(a, b, (((1,), (1,)), ((), ())), preferred_element_type=F32)


def _sigmoid(x):
    return 1.0 / (1.0 + jnp.exp(-x))


def _params(sem):
    return pltpu.CompilerParams(dimension_semantics=sem, vmem_limit_bytes=VMEM_LIMIT)


def _rope_lanes(a, cos, sin_signed, half):
    if 2 * half == LANES:
        partner = pltpu.roll(a, half, 1)
    else:
        lane = lax.broadcasted_iota(jnp.int32, (1, LANES), 1)
        partner = jnp.where((lane % (2 * half)) < half, pltpu.roll(a, LANES - half, 1), pltpu.roll(a, half, 1))
    return a * cos + partner * sin_signed


def _rope_sublanes(blk, cos_t, sin_t):
    outs = []
    for h in range(blk.shape[0] // NSA_HD):
        x1 = blk[h * NSA_HD:h * NSA_HD + NSA_HD // 2]
        x2 = blk[h * NSA_HD + NSA_HD // 2:(h + 1) * NSA_HD]
        outs += [x1 * cos_t - x2 * sin_t, x2 * cos_t + x1 * sin_t]
    return jnp.concatenate(outs, axis=0)


def _proj_kernel(*refs, prompt, n_w):
    x_ref, g_ref = refs[:2]
    w_parts = refs[2:2 + n_w]
    (wt_ref, cosr_ref, sinr_ref, cosn_ref, sinn_ref, cost_ref, sint_ref,
     qret_ref, kret_ref, vret_ref, rg_ref) = refs[2 + n_w:13 + n_w]
    refs = refs[n_w - 1:]
    x = x_ref[...]
    ms = jnp.mean(x * x, axis=-1, keepdims=True)
    h = (x * lax.rsqrt(ms + RMS_EPS) * g_ref[...]).astype(BF16)

    def proj(lo, hi):
        for part in w_parts:
            if lo < part.shape[0]:
                return _dot_nt(h, part[lo:hi, :])
            lo, hi = lo - part.shape[0], hi - part.shape[0]

    cr, sr = cosr_ref[...], sinr_ref[...]
    cn, sn = cosn_ref[...], sinn_ref[...]
    ct, st = cost_ref[...], sint_ref[...]
    yq = proj(0, 512)
    yk = proj(512, 1024)
    for hh in range(RET_HEADS):
        sl = slice(hh * LANES, (hh + 1) * LANES)
        qret_ref[:, sl] = _rope_lanes(yq[:, sl], cr, sr, RET_DK // 2)
        kret_ref[:, sl] = _rope_lanes(yk[:, sl], cr, sr, RET_DK // 2) * (RET_DK ** -0.5)
    vret_ref[...] = proj(1024, 1536)
    rg_ref[...] = proj(1536, 2048)

    yt = _dot_nt(wt_ref[...], h)
    if prompt:
        (skr_ref, wkr_ref, ck_ref, cv_ref, sv_ref, wv_ref, sk_ref, wk_ref, svb_ref, wvb_ref,
         nqt_ref, nqrt_ref, ngt_ref, nglt_ref) = refs[14:]
    else:
        nq_ref, nqr_ref, ng_ref, ngl_ref, ck_ref, cv_ref, sv_ref, wv_ref, sk_ref, wk_ref = refs[14:]
    ck_ref[0] = yt[0:128]
    cv_ref[0] = yt[128:256]
    sv_ref[0] = yt[256:384]
    wv_ref[0] = yt[384:512]
    sk_ref[0] = _rope_sublanes(yt[512:640], ct, st)
    wk_ref[0] = _rope_sublanes(yt[640:768], ct, st)
    if prompt:
        svb_ref[0] = yt[256:384].astype(BF16)
        wvb_ref[0] = yt[384:512].astype(BF16)
        nqt = yt[768:1280]
        nqt_ref[0] = nqt.astype(BF16)
        nqrt_ref[0] = _rope_sublanes(nqt, ct, st).astype(BF16)
        ngt_ref[0] = yt[1280:1792]
        nglt_ref[0] = yt[1792:1920]
        skr_ref[...] = _rope_lanes(proj(2048, 2176), cn, sn, NSA_HD // 2).astype(BF16)
        wkr_ref[...] = _rope_lanes(proj(2176, 2304), cn, sn, NSA_HD // 2).astype(BF16)
    else:
        yn = proj(2048, 2560)
        for gg in range(NSA_GROUP):
            sl = slice(gg * LANES, (gg + 1) * LANES)
            nq_ref[:, sl] = yn[:, sl].astype(BF16)
            nqr_ref[:, sl] = _rope_lanes(yn[:, sl], cn, sn, NSA_HD // 2).astype(BF16)
        ng_ref[...] = proj(2560, 3072)
        ngl_ref[...] = proj(3072, 3200)


def _rope_tables(pos, half):
    inv = ROPE_THETA ** (-np.arange(half, dtype=np.float64) / half)
    ang = np.asarray(pos, np.float64)[:, None] * inv[None, :]
    return np.cos(ang).astype(np.float32), np.sin(ang).astype(np.float32)


def _proj(x2d, pos_rows, norm_g, w_rm, w_t, kv_batch, tm, prompt):
    n = x2d.shape[0]
    nt = pos_rows.shape[0] // tm
    skv = n // kv_batch
    nkt = skv // tm
    c64, s64 = _rope_tables(pos_rows, 64)
    c32, s32 = _rope_tables(pos_rows, 32)
    cosr = np.concatenate([c64, c64], axis=1)
    sinr = np.concatenate([-s64, s64], axis=1)
    cosn = np.concatenate([c32, c32, c32, c32], axis=1)
    sinn = np.concatenate([-s32, s32, -s32, s32], axis=1)
    cost, sint = np.ascontiguousarray(c32.T), np.ascontiguousarray(s32.T)
    row = lambda w: pl.BlockSpec((tm, w), lambda i: (i, 0))
    tab = pl.BlockSpec((tm, LANES), lambda i: (i % nt, 0))
    tabt = pl.BlockSpec((32, tm), lambda i: (0, i % nt))
    tspec = lambda r: pl.BlockSpec((1, r, tm), lambda i: (i // nkt, 0, i % nkt))
    f = lambda w, dt: jax.ShapeDtypeStruct((n, w), dt)
    ts = lambda r, dt: jax.ShapeDtypeStruct((kv_batch, r, skv), dt)
    kvo, kvs = tspec(KV_WIDTH), ts(KV_WIDTH, F32)
    out_specs = [row(512)] * 4
    out_shape = [f(512, F32)] * 4
    if prompt:
        out_specs += [row(KV_WIDTH)] * 2 + [kvo] * 8 + [tspec(512)] * 3 + [tspec(LANES)]
        out_shape += [f(KV_WIDTH, BF16)] * 2 + [kvs] * 6 + [ts(KV_WIDTH, BF16)] * 2 + [ts(512, BF16)] * 2 + [
            ts(512, F32), ts(LANES, F32)]
    else:
        out_specs += [row(512)] * 3 + [row(LANES)] + [kvo] * 6
        out_shape += [f(512, BF16)] * 2 + [f(512, F32), f(LANES, F32)] + [kvs] * 6
    w_parts = tuple(w_rm) if isinstance(w_rm, (tuple, list)) else (w_rm,)
    whole = lambda a: pl.BlockSpec(a.shape, lambda i: (0, 0))
    return pl.pallas_call(
        functools.partial(_proj_kernel, prompt=prompt, n_w=len(w_parts)),
        grid=(n // tm,),
        in_specs=[row(D_MODEL), pl.BlockSpec((1, D_MODEL), lambda i: (0, 0))] + [whole(w) for w in w_parts]
        + [whole(w_t), tab, tab, tab, tab, tabt, tabt],
        out_specs=out_specs,
        out_shape=out_shape,
        compiler_params=_params(("arbitrary",)),
        name="proj_prompt" if prompt else "proj_sample",
    )(x2d, norm_g, *w_parts, w_t, cosr, sinr, cosn, sinn, cost, sint)


def _ret_kernel(q_ref, k_ref, v_ref, rg_ref, st_ref, dm_ref, qd_ref, kd_ref, sd_ref, gg_ref, gb_ref,
                y_ref, so_ref, st_scr, *, n_chunks):
    c = pl.program_id(1)

    @pl.when(c == 0)
    def _():
        st_scr[...] = st_ref[0]

    for h in range(RET_HEADS):
        sl = slice(h * LANES, (h + 1) * LANES)
        state = st_scr[h]
        for cc in range(q_ref.shape[0] // RET_CHUNK):
            rows = slice(cc * RET_CHUNK, (cc + 1) * RET_CHUNK)
            q = q_ref[rows, sl]
            k = k_ref[rows, sl]
            v = v_ref[rows, sl].astype(BF16)
            s = _dot_nt(q.astype(BF16), k.astype(BF16)) * dm_ref[h]
            intra = _dot(s.astype(BF16), v)
            cross = _dot((q * qd_ref[:, sl]).astype(BF16), state.astype(BF16))
            o = intra + cross
            kd_t = (k * kd_ref[:, sl]).T.astype(BF16)
            state = sd_ref[h] * state + _dot(kd_t, v)
            mu = jnp.mean(o, axis=-1, keepdims=True)
            var = jnp.mean(jnp.square(o - mu), axis=-1, keepdims=True)
            on = (o - mu) * lax.rsqrt(var + GN_EPS)
            gate = rg_ref[rows, sl]
            y_ref[rows, sl] = (on * gg_ref[:, sl] + gb_ref[:, sl]) * (gate * _sigmoid(gate))
        st_scr[h] = state

    @pl.when(c == n_chunks - 1)
    def _():
        so_ref[0] = st_scr[...]


def _retention(q, k, v, rg, state, gn_g, gn_b, n_batch, n_chunks, chunk_len):
    c = RET_CHUNK
    log_g = np.log(1.0 - 2.0 ** (-5.0 - np.arange(RET_HEADS, dtype=np.float64)))
    i = np.arange(c, dtype=np.float64)
    diff = i[:, None] - i[None, :]
    causal = diff >= 0
    f32 = lambda a: np.ascontiguousarray(a, dtype=np.float32)
    dmask = f32(np.where(causal[None], np.exp(np.where(causal, diff, 0.0)[None] * log_g[:, None, None]), 0.0))
    expand = lambda t: f32(np.repeat(t, LANES, axis=1))
    qdec = expand(np.exp((i[:, None] + 1.0) * log_g[None, :]))
    kdec = expand(np.exp((chunk_len - 1.0 - i)[:, None] * log_g[None, :]))
    sdec = f32(np.broadcast_to(np.exp(chunk_len * log_g)[:, None, None], (RET_HEADS, 1, LANES)))
    stspec1 = pl.BlockSpec((1, RET_HEADS, 128, 128), lambda b: (b, 0, 0, 0))
    const1 = lambda shape: pl.BlockSpec(shape, lambda b: (0,) * len(shape))
    if chunk_len < c:
        assert n_chunks == 1
        row1 = pl.BlockSpec((chunk_len, 512), lambda b: (b, 0))
        return pl.pallas_call(
            _ret_short_kernel,
            grid=(n_batch,),
            in_specs=[row1, row1, row1, row1, stspec1, const1((RET_HEADS, c, c)), const1((c, 512)), const1((c, 512)),
                      const1((RET_HEADS, 1, LANES)), const1((1, 512)), const1((1, 512))],
            out_specs=[row1, stspec1],
            out_shape=[jax.ShapeDtypeStruct(q.shape, F32), jax.ShapeDtypeStruct(state.shape, F32)],
            scratch_shapes=[pltpu.VMEM((c, 512), F32)] * 3,
            compiler_params=_params(("arbitrary",)),
            name="retention_short",
        )(q, k, v, rg, state, dmask, qdec, kdec, sdec, gn_g, gn_b)
    per_step = RET_CHUNKS_PER_STEP if n_chunks % RET_CHUNKS_PER_STEP == 0 else 1
    n_steps = n_chunks // per_step
    row = pl.BlockSpec((per_step * c, 512), lambda b, j: (b * n_steps + j, 0))
    stspec = pl.BlockSpec((1, RET_HEADS, 128, 128), lambda b, j: (b, 0, 0, 0))
    const = lambda shape: pl.BlockSpec(shape, lambda b, j: (0,) * len(shape))
    return pl.pallas_call(
        functools.partial(_ret_kernel, n_chunks=n_steps),
        grid=(n_batch, n_steps),
        in_specs=[row, row, row, row, stspec, const((RET_HEADS, c, c)), const((c, 512)), const((c, 512)),
                  const((RET_HEADS, 1, LANES)), const((1, 512)), const((1, 512))],
        out_specs=[row, stspec],
        out_shape=[jax.ShapeDtypeStruct(q.shape, F32), jax.ShapeDtypeStruct(state.shape, F32)],
        scratch_shapes=[pltpu.VMEM((RET_HEADS, 128, 128), F32)],
        compiler_params=_params(("arbitrary", "arbitrary")),
        name="retention",
    )(q, k, v, rg, state, dmask, qdec, kdec, sdec, gn_g, gn_b)


def _ret_short_kernel(q_ref, k_ref, v_ref, rg_ref, st_ref, dm_ref, qd_ref, kd_ref, sd_ref, gg_ref, gb_ref,
                      y_ref, so_ref, k_pad, kd_pad, v_pad):
    t = q_ref.shape[0]

    @pl.when(pl.program_id(0) == 0)
    def _():
        for pad in (k_pad, kd_pad, v_pad):
            pad[...] = jnp.zeros(pad.shape, F32)

    k_pad[0:t, :] = k_ref[...]
    kd_pad[0:t, :] = k_ref[...] * kd_ref[0:t, :]
    v_pad[0:t, :] = v_ref[...]
    for h in range(RET_HEADS):
        sl = slice(h * LANES, (h + 1) * LANES)
        q = q_ref[:, sl]
        v = v_pad[:, sl].astype(BF16)
        state = st_ref[0, h]
        s = _dot_nt(q.astype(BF16), k_pad[:, sl].astype(BF16)) * dm_ref[h, 0:t, :]
        intra = _dot(s.astype(BF16), v)
        cross = _dot((q * qd_ref[0:t, sl]).astype(BF16), state.astype(BF16))
        o = intra + cross
        so_ref[0, h] = sd_ref[h] * state + _dot(kd_pad[:, sl].T.astype(BF16), v)
        mu = jnp.mean(o, axis=-1, keepdims=True)
        var = jnp.mean(jnp.square(o - mu), axis=-1, keepdims=True)
        on = (o - mu) * lax.rsqrt(var + GN_EPS)
        gate = rg_ref[:, sl]
        y_ref[:, sl] = (on * gg_ref[:, sl] + gb_ref[:, sl]) * (gate * _sigmoid(gate))


def _gelu_tanh(x):
    return 0.5 * x * (1.0 + jnp.tanh(np.sqrt(2.0 / np.pi).astype(np.float32) * (x + 0.044715 * (x * x * x))))


def _rows_to_scratch(tiles, scr):
    per_tile = LANES // CMP_BLOCK
    for t, tile in enumerate(tiles):
        rows_pm = tile.T
        for c in range(per_tile):
            r0 = (t * per_tile + c) * BLOCK_PITCH
            scr[r0:r0 + CMP_BLOCK, :] = rows_pm[c * CMP_BLOCK:(c + 1) * CMP_BLOCK]


def _compress_scratch(scr, n_blk, pe_ref, w1_ref, w2_ref, transpose_out=False):
    flat = [(scr[pl.ds(j, n_blk, stride=BLOCK_PITCH), :] + pe_ref[j:j + 1, :]).astype(BF16) for j in range(CMP_BLOCK)]
    hid = _gelu_tanh(_dot(jnp.concatenate(flat, axis=1), w1_ref[...]))
    if transpose_out:
        return _dot_nt(w2_ref[...], hid.astype(BF16))
    return _dot(hid.astype(BF16), w2_ref[...])


def _compress_one(tiles, pe_ref, w1_ref, w2_ref, scr, transpose_out=False):
    _rows_to_scratch(tiles, scr)
    return _compress_scratch(scr, len(tiles) * (LANES // CMP_BLOCK), pe_ref, w1_ref, w2_ref, transpose_out)


def _compress_prompt_kernel(k_ref, v_ref, pek_ref, w1k_ref, w2k_ref, pev_ref, w1v_ref, w2v_ref,
                            kc_ref, vct_ref, scr_k, scr_v, *, n_tiles):
    for src, pe, w1, w2, dst, scr, tr in ((k_ref, pek_ref, w1k_ref, w2k_ref, kc_ref, scr_k, False),
                                          (v_ref, pev_ref, w1v_ref, w2v_ref, vct_ref, scr_v, True)):
        tiles = [src[0, :, t * LANES:(t + 1) * LANES] for t in range(n_tiles)]
        dst[0] = _compress_one(tiles, pe, w1, w2, scr, tr)


def _compress_sample_kernel(pt_ref, ck_hbm, cv_hbm, pek_ref, w1k_ref, w2k_ref, pev_ref, w1v_ref, w2v_ref,
                            kc_ref, vc_ref, kbuf, vbuf, sems, scr_k, scr_v, *, n_pages, n_steps, n_batch):
    n_blk = n_pages * PAGE_SIZE // CMP_BLOCK

    def to_rows(step, slot):
        _rows_to_scratch([kbuf[slot, j] for j in range(n_pages)], scr_k.at[step % 2])
        _rows_to_scratch([vbuf[slot, j] for j in range(n_pages)], scr_v.at[step % 2])

    def compress(step):
        rows = pl.ds(pl.multiple_of(step * n_blk, n_blk), n_blk)
        kc_ref[0, rows, :] = _compress_scratch(scr_k.at[step % 2], n_blk, pek_ref, w1k_ref, w2k_ref)
        vc_ref[0, rows, :] = _compress_scratch(scr_v.at[step % 2], n_blk, pev_ref, w1v_ref, w2v_ref)

    _stream_pages(pt_ref, (ck_hbm, cv_hbm), (kbuf, vbuf), sems, n_pages, n_steps, n_batch, to_rows, compress)


def _compress_weights(pe, w1, w2, transpose_w2=False):
    pe_rows = pe.reshape(CMP_BLOCK, KV_WIDTH)
    z = jnp.zeros((CMP_BLOCK, NSA_HD, NSA_HD), F32)
    w1bd = jnp.concatenate([jnp.concatenate([w1[0], z], axis=2), jnp.concatenate([z, w1[1]], axis=2)], axis=1)
    z2 = jnp.zeros((NSA_HD, NSA_HD), F32)
    w2bd = jnp.concatenate([jnp.concatenate([w2[0], z2], axis=1), jnp.concatenate([z2, w2[1]], axis=1)], axis=0)
    if transpose_w2:
        w2bd = w2bd.T
    return pe_rows, w1bd.reshape(CMP_BLOCK * KV_WIDTH, KV_WIDTH).astype(BF16), w2bd.astype(BF16)


def _cw_specs():
    zero = lambda *a: (0, 0)
    return [pl.BlockSpec((CMP_BLOCK, KV_WIDTH), zero), pl.BlockSpec((CMP_BLOCK * KV_WIDTH, KV_WIDTH), zero),
            pl.BlockSpec((KV_WIDTH, KV_WIDTH), zero)]


def _compress_prompt(ck_t, cv_t, cwk, cwv_t):
    b, _, s = ck_t.shape
    n_blk = s // CMP_BLOCK
    src = pl.BlockSpec((1, KV_WIDTH, s), lambda i: (i, 0, 0))
    return pl.pallas_call(
        functools.partial(_compress_prompt_kernel, n_tiles=s // LANES),
        grid=(b,),
        in_specs=[src, src] + _cw_specs() + _cw_specs(),
        out_specs=[pl.BlockSpec((1, n_blk, KV_WIDTH), lambda i: (i, 0, 0)),
                   pl.BlockSpec((1, KV_WIDTH, n_blk), lambda i: (i, 0, 0))],
        out_shape=[jax.ShapeDtypeStruct((b, n_blk, KV_WIDTH), F32), jax.ShapeDtypeStruct((b, KV_WIDTH, n_blk), F32)],
        scratch_shapes=[pltpu.VMEM((n_blk * BLOCK_PITCH, KV_WIDTH), F32)] * 2,
        compiler_params=_params(("arbitrary",)),
        name="compress_prompt",
    )(ck_t, cv_t, *cwk, *cwv_t)


def _compress_sample(page_table, cache_k_t, cache_v_t, cwk, cwv):
    db, n_pages = page_table.shape
    g = COMPRESS_PAGES_PER_STEP
    steps = n_pages // g
    blk_per_step = g * PAGE_SIZE // CMP_BLOCK
    n_blk = n_pages * PAGE_SIZE // CMP_BLOCK

    hbm = pl.BlockSpec(memory_space=pl.ANY)
    dst = pl.BlockSpec((1, n_blk, KV_WIDTH), lambda b, pt: (b, 0, 0))
    shp = jax.ShapeDtypeStruct((db, n_blk, KV_WIDTH), F32)
    page_buf = pltpu.VMEM((PAGE_SLOTS, g, KV_WIDTH, PAGE_SIZE), F32)
    grid_spec = pltpu.PrefetchScalarGridSpec(
        num_scalar_prefetch=1,
        grid=(db,),
        in_specs=[hbm, hbm] + _cw_specs() + _cw_specs(),
        out_specs=[dst, dst],
        scratch_shapes=[page_buf, page_buf, pltpu.SemaphoreType.DMA((PAGE_SLOTS, 2))]
        + [pltpu.VMEM((2, blk_per_step * BLOCK_PITCH, KV_WIDTH), F32)] * 2,
    )
    return pl.pallas_call(
        functools.partial(_compress_sample_kernel, n_pages=g, n_steps=steps, n_batch=db),
        grid_spec=grid_spec,
        out_shape=[shp, shp],
        compiler_params=_params(("arbitrary",)),
        name="compress_sample",
    )(page_table, cache_k_t, cache_v_t, *cwk, *cwv)


def _stack_heads_t(qt_ref):
    nq = qt_ref.shape[2]
    zero = jnp.zeros((NSA_HD, nq), BF16)
    cols = []
    for k in range(KV_HEADS):
        for g in range(NSA_GROUP):
            h = k * NSA_GROUP + g
            tile = qt_ref[0, h * NSA_HD:(h + 1) * NSA_HD, :] * ATTN_SCALE
            cols.append(jnp.concatenate([tile, zero] if k == 0 else [zero, tile], axis=0))
    return jnp.concatenate(cols, axis=1)


def _rank_select_rows(score, n_blk):
    parts = [score[v * SUBLANES:(v + 1) * SUBLANES] for v in range(n_blk // SUBLANES)]
    ranks = [jnp.zeros(p.shape, jnp.int32) for p in parts]
    for j in range(n_blk):
        col = score[j:j + 1, :]
        for v, part in enumerate(parts):
            ge, gt = (col >= part).astype(jnp.int32), (col > part).astype(jnp.int32)
            if v * SUBLANES > j:
                beats = ge
            elif (v + 1) * SUBLANES - 1 <= j:
                beats = gt
            else:
                row = v * SUBLANES + lax.broadcasted_iota(jnp.int32, (SUBLANES, 1), 0)
                beats = jnp.where(row > j, ge, gt)
            ranks[v] = ranks[v] + beats
    return jnp.concatenate([(r < TOP_N).astype(F32) for r in ranks], axis=0)


def _softmax_tile_t(s, v_t, m_ref, acc_ref, s_max=None):
    half = s.shape[1] // KV_HEADS
    m_old = m_ref[...]
    m_new = jnp.maximum(m_old, jnp.max(s, axis=0, keepdims=True) if s_max is None else s_max)
    alpha = jnp.exp(m_old - m_new)
    pb = jnp.exp(s - m_new).astype(BF16)
    m_ref[...] = m_new
    ones = jnp.ones((NSA_HD, v_t.shape[1]), BF16)
    v0 = jnp.concatenate([v_t[:NSA_HD], ones], axis=0)
    v1 = jnp.concatenate([ones, v_t[NSA_HD:]], axis=0)
    pv = jnp.concatenate([_dot(v0, pb[:, :half]), _dot(v1, pb[:, half:])], axis=1)
    acc_ref[...] = alpha * acc_ref[...] + pv


def _normalized_head_t(acc, k, cols):
    den = (1 - k) * NSA_HD
    return acc[k * NSA_HD:(k + 1) * NSA_HD, cols] / acc[den:den + 1, cols]


def _attn_prompt_kernel(nqt_ref, nqrt_ref, nglt_ref, ngt_ref, kc_ref, vct_ref, sk_ref, svt_ref, wk_ref, wvt_ref,
                        et_ref, ewin_ref, y_ref, lhs_scr, qslc_scr, qwin_scr, ocmp_scr, t_scr, m_scr, acc_scr,
                        s_scr, smax_scr, sw_scr):
    i = pl.program_id(1)
    qb = y_ref.shape[0]
    q0 = i * qb
    cols8 = 2 * NSA_GROUP * qb
    qpos = q0 + lax.broadcasted_iota(jnp.int32, (1, qb), 1)
    lhs_scr[...] = _stack_heads_t(nqt_ref)
    q_rot = _stack_heads_t(nqrt_ref)
    qslc_scr[0:KV_WIDTH, :] = q_rot
    qwin_scr[0:KV_WIDTH, :] = q_rot
    def causal_cap():
        own_keys = q0 + lax.broadcasted_iota(jnp.int32, (qb, 1), 0)
        return jnp.concatenate([jnp.where(own_keys <= qpos, CAP_OPEN, NEG_INF)] * (2 * NSA_GROUP), axis=1)

    n_cmp = kc_ref.shape[1]
    s_all = _dot(kc_ref[0].astype(BF16), lhs_scr[...])
    cend = (lax.broadcasted_iota(jnp.int32, (n_cmp, 1), 0) + 1) * CMP_BLOCK - 1
    cmask = cend <= qpos
    vct = vct_ref[0].astype(BF16)
    imps = []
    for k in range(KV_HEADS):
        imp = None
        for g in range(NSA_GROUP):
            c0 = (k * NSA_GROUP + g) * qb
            s = jnp.where(cmask, s_all[:, c0:c0 + qb], NEG_INF)
            p = jnp.exp(s - jnp.max(s, axis=0, keepdims=True))
            p = p / jnp.sum(p, axis=0, keepdims=True) * cmask.astype(F32)
            imp = p if imp is None else imp + p
            ocmp_scr[:, c0:c0 + qb] = _dot(vct, p.astype(BF16))
        imps.append(imp)

    n_slc = n_cmp // 2
    blk = lax.broadcasted_iota(jnp.int32, (n_slc, 1), 0)
    valid = blk * SLC_BLOCK <= qpos

    def store_bias(k, sel):
        bias = jnp.concatenate([(sel - 1.0) * (-NEG_INF), jnp.zeros((LANES - n_slc, qb), F32)], axis=0).astype(BF16)
        c0 = k * NSA_GROUP * qb
        qslc_scr[KV_WIDTH:, c0:c0 + NSA_GROUP * qb] = jnp.concatenate([bias] * NSA_GROUP, axis=1)

    @pl.when(q0 + qb <= TOP_N * SLC_BLOCK)
    def _():
        for k in range(KV_HEADS):
            store_bias(k, valid.astype(F32))

    @pl.when(q0 + qb > TOP_N * SLC_BLOCK)
    def _():
        forced = (blk == 0) | (blk == qpos // SLC_BLOCK)
        for k in range(KV_HEADS):
            pairs = []
            for u in range(qb // LANES):
                t_scr[u] = imps[k][:, u * LANES:(u + 1) * LANES]
                pairs.append(t_scr[u, pl.ds(0, n_slc, stride=2), :] + t_scr[u, pl.ds(1, n_slc, stride=2), :])
            pair = jnp.concatenate(pairs, axis=1)
            score = jnp.where(forced, FORCED_SCORE, jnp.where(valid, pair, NEG_INF))
            store_bias(k, _rank_select_rows(score, n_slc))

    tk = SLC_KEY_TILE
    m_scr[...] = jnp.full(m_scr.shape, M_INIT, F32)
    acc_scr[...] = jnp.zeros(acc_scr.shape, F32)

    def scores(t):
        k0 = pl.multiple_of(t * tk, tk)
        keys = jnp.concatenate([sk_ref[pl.ds(k0, tk), :], et_ref[pl.ds(k0, tk), :]], axis=1)
        s = _dot(keys, qslc_scr[...])
        s_scr[t % 2] = s
        smax_scr[t % 2] = jnp.max(s, axis=0, keepdims=True)

    def update(t):
        k0 = pl.multiple_of(t * tk, tk)
        _softmax_tile_t(s_scr[t % 2], svt_ref[0, :, pl.ds(k0, tk)], m_scr.at[0], acc_scr.at[0], smax_scr[t % 2])

    def slc_step(t, carry):
        scores(t + 1)
        update(t)
        return carry

    n_full = q0 // tk

    @pl.when(n_full >= 1)
    def _():
        scores(0)
        lax.fori_loop(0, n_full - 1, slc_step, 0)
        update(n_full - 1)

    k_rem = pl.multiple_of(n_full * tk, tk)
    for v in range(tk // qb):
        @pl.when((q0 - k_rem) // qb == v)
        def _():
            r = (v + 1) * qb
            keys = jnp.concatenate([sk_ref[pl.ds(k_rem, r), :], et_ref[pl.ds(k_rem, r), :]], axis=1)
            s = _dot(keys, qslc_scr[...])
            own = jnp.minimum(s[r - qb:], causal_cap())
            s = own if v == 0 else jnp.concatenate([s[:r - qb], own], axis=0)
            _softmax_tile_t(s, svt_ref[0, :, pl.ds(k_rem, r)], m_scr.at[0], acc_scr.at[0])

    wlen = WINDOW + qb
    w0 = pl.multiple_of(jnp.maximum(i - WINDOW // qb, 0) * qb, qb)
    later = (w0 + lax.broadcasted_iota(jnp.int32, (LANES, 1), 0) * qb) > q0
    qwin_scr[KV_WIDTH:, :] = jnp.broadcast_to(jnp.where(later, NEG_INF, 0.0), (LANES, cols8)).astype(BF16)
    keys = jnp.concatenate([wk_ref[pl.ds(w0, wlen), :], ewin_ref[...]], axis=1)
    sw_scr[...] = _dot(keys, qwin_scr[...])
    edge_keys = w0 + lax.broadcasted_iota(jnp.int32, (qb, 1), 0)
    edge_cap = jnp.concatenate([jnp.where(qpos - edge_keys <= WINDOW, CAP_OPEN, NEG_INF)] * (2 * NSA_GROUP), axis=1)
    sw_scr[0:qb, :] = jnp.minimum(sw_scr[0:qb, :], edge_cap)
    own = pl.ds(pl.multiple_of(q0 - w0, qb), qb)
    sw_scr[own, :] = jnp.minimum(sw_scr[own, :], causal_cap())
    _softmax_tile_t(sw_scr[...], wvt_ref[0, :, pl.ds(w0, wlen)], m_scr.at[1], acc_scr.at[1])

    acc_slc, acc_win = acc_scr[0], acc_scr[1]
    sig = _sigmoid(nglt_ref[0])
    for pair in range(NSA_HEADS // 2):
        tiles = []
        for h in (2 * pair, 2 * pair + 1):
            k = h // NSA_GROUP
            rows = slice(k * NSA_HD, (k + 1) * NSA_HD)
            cols = slice(h * qb, (h + 1) * qb)
            gate = lambda r: sig[r * NSA_HEADS + h:r * NSA_HEADS + h + 1, :]
            o = (gate(0) * ocmp_scr[rows, cols] + gate(1) * _normalized_head_t(acc_slc, k, cols)
                 + gate(2) * _normalized_head_t(acc_win, k, cols))
            ng = ngt_ref[0, h * NSA_HD:(h + 1) * NSA_HD, :]
            tiles.append(o * (ng * _sigmoid(ng)))
        y_ref[:, pair * LANES:(pair + 1) * LANES] = jnp.concatenate(tiles, axis=0).T.astype(y_ref.dtype)


def _attn_prompt(nq_t, nqr_t, ngl_t, ng_t, kc, vc_t, sk_rm, sv_t, wk_rm, wv_t):
    b, _, s = sv_t.shape
    assert s // CMP_BLOCK == LANES, "one lane per compressed block"
    qb = PROMPT_Q_TILE
    nqb = s // qb
    cols8 = 2 * NSA_GROUP * qb
    expand_t = (np.arange(s)[:, None] // SLC_BLOCK == np.arange(LANES)[None, :]).astype(BF16)
    wlen = WINDOW + qb
    win_blocks = (np.arange(wlen)[:, None] // qb == np.arange(LANES)[None, :]).astype(BF16)
    qcol = lambda r: pl.BlockSpec((1, r, qb), lambda bb, i: (bb, 0, i))
    per_b = lambda shape: pl.BlockSpec((1,) + shape, lambda bb, i: (bb, 0, 0))
    rows_b = pl.BlockSpec((s, KV_WIDTH), lambda bb, i: (bb, 0))
    return pl.pallas_call(
        _attn_prompt_kernel,
        grid=(b, nqb),
        in_specs=[qcol(512), qcol(512), qcol(LANES), qcol(512), per_b(kc.shape[1:]), per_b(vc_t.shape[1:]),
                  rows_b, per_b((KV_WIDTH, s)), rows_b, per_b((KV_WIDTH, s)),
                  pl.BlockSpec(expand_t.shape, lambda bb, i: (0, 0)),
                  pl.BlockSpec(win_blocks.shape, lambda bb, i: (0, 0))],
        out_specs=pl.BlockSpec((qb, 512), lambda bb, i: (bb * nqb + i, 0)),
        out_shape=jax.ShapeDtypeStruct((b * s, 512), BF16),
        scratch_shapes=[pltpu.VMEM((KV_WIDTH, cols8), BF16), pltpu.VMEM((2 * KV_WIDTH, cols8), BF16),
                        pltpu.VMEM((2 * KV_WIDTH, cols8), BF16),
                        pltpu.VMEM((KV_WIDTH, cols8), F32), pltpu.VMEM((qb // LANES, LANES, LANES), F32),
                        pltpu.VMEM((2, 1, cols8), F32), pltpu.VMEM((2, KV_WIDTH, cols8), F32),
                        pltpu.VMEM((2, SLC_KEY_TILE, cols8), F32), pltpu.VMEM((2, 1, cols8), F32),
                        pltpu.VMEM((wlen, cols8), F32)],
        compiler_params=_params(("arbitrary", "arbitrary")),
        name="attn_prompt",
    )(nq_t, nqr_t, ngl_t, ng_t, kc, vc_t, sk_rm, sv_t, wk_rm, wv_t, expand_t, win_blocks)


def _stack_heads(q_ref):
    lane = lax.broadcasted_iota(jnp.int32, (1, LANES), 1)
    lo = lane < NSA_HD
    slabs = [q_ref[:, g * LANES:(g + 1) * LANES] for g in range(NSA_GROUP)]
    zero = jnp.zeros_like(slabs[0])
    stacked = jnp.concatenate([jnp.where(lo, s, zero) for s in slabs] + [jnp.where(lo, zero, s) for s in slabs], axis=0)
    return stacked.astype(BF16)


def _mask_rows(sc, masks, rows):
    n = sc.shape[-1]
    s4 = sc.reshape(2 * NSA_GROUP, rows, n)
    out = [jnp.where(masks[k][None], s4[k * NSA_GROUP:(k + 1) * NSA_GROUP], NEG_INF) for k in range(KV_HEADS)]
    return jnp.concatenate(out, axis=0).reshape(2 * NSA_GROUP * rows, n)


def _pair_scores(imp, qpos):
    n_slc = imp.shape[1] // 2
    blk = lax.broadcasted_iota(jnp.int32, (1, n_slc), 1)
    valid = blk * SLC_BLOCK <= qpos
    forced = (blk == 0) | (blk == qpos // SLC_BLOCK)
    return jnp.where(forced, FORCED_SCORE, jnp.where(valid, imp[:, :n_slc] + imp[:, n_slc:], NEG_INF))


def _rank_select(score, extra_forced):
    n_blk = score.shape[1]
    blk = lax.broadcasted_iota(jnp.int32, (1, n_blk), 1)
    rank = jnp.zeros(score.shape, jnp.int32)
    for j in range(n_blk):
        col = score[:, j:j + 1]
        beats = (col > score) | ((col == score) & (j < blk))
        rank = rank + beats.astype(jnp.int32)
    if extra_forced:
        rank = rank + (score < FORCED_SCORE).astype(jnp.int32)
    return rank < TOP_N


def _merge_kv_heads(acc, rows):
    lane = lax.broadcasted_iota(jnp.int32, (1, LANES), 1)
    lo = lane < NSA_HD
    half = NSA_GROUP * rows
    return [jnp.where(lo, acc[g * rows:(g + 1) * rows], acc[half + g * rows:half + (g + 1) * rows])
            for g in range(NSA_GROUP)]


def _gate_and_store(o_cmp, o_slc, o_win, ngl_ref, ng_ref, y_ref):
    lane = lax.broadcasted_iota(jnp.int32, (1, LANES), 1)
    lo = lane < NSA_HD
    sig = _sigmoid(ngl_ref[...])
    for g in range(NSA_GROUP):
        gates = [jnp.where(lo, sig[:, r * 8 + g:r * 8 + g + 1], sig[:, r * 8 + 4 + g:r * 8 + 4 + g + 1]) for r in range(3)]
        o = gates[0] * o_cmp[g] + gates[1] * o_slc[g] + gates[2] * o_win[g]
        gate = ng_ref[:, g * LANES:(g + 1) * LANES]
        y_ref[:, g * LANES:(g + 1) * LANES] = (o * (gate * _sigmoid(gate))).astype(y_ref.dtype)


def _page_copies(pt_ref, b, chunk, slot, caches, bufs, sems, g):
    out = []
    for ci, (cache, buf) in enumerate(zip(caches, bufs)):
        for j in range(g):
            page = pt_ref[b, chunk * g + j]
            out.append(pltpu.make_async_copy(cache.at[page], buf.at[slot, j], sems.at[slot, ci]))
    return out


def _stream_pages(pt_ref, caches, bufs, sems, g, n_chunks, n_batch, compute, finish=None):
    b = pl.program_id(0)
    total = n_batch * n_chunks
    n_slots = bufs[0].shape[0]
    depth = n_slots - 1
    assert n_chunks >= depth
    copies = functools.partial(_page_copies, pt_ref, caches=caches, bufs=bufs, sems=sems, g=g)

    @pl.when(b == 0)
    def _():
        for d in range(depth):
            for n, cp in enumerate(copies(0, d, d)):
                cp.start(priority=n % DMA_THREADS)

    def visit(c):
        flat = b * n_chunks + c
        ahead = jnp.minimum(flat + depth, total - 1)
        for n, cp in enumerate(copies(ahead // n_chunks, ahead % n_chunks, (flat + depth) % n_slots)):
            cp.start(priority=n % DMA_THREADS)
        for cp in copies(b, c, flat % n_slots):
            cp.wait()
        compute(c, flat % n_slots)

    def chunk(c, carry):
        visit(c)
        if finish is not None:
            finish(c - 1)
        return carry

    if finish is None:
        lax.fori_loop(0, n_chunks, chunk, 0)
    else:
        visit(0)
        for c in range(1, n_chunks):
            chunk(c, 0)
        finish(n_chunks - 1)

    @pl.when(b == n_batch - 1)
    def _():
        for d in range(depth):
            for cp in copies(n_batch - 1, n_chunks - 1, (total + d) % n_slots):
                cp.wait()


def _attn_sample_kernel(pt_ref, sk_hbm, sv_hbm, nq_ref, nqr_ref, ngl_ref, ng_ref, kc_ref, vc_ref, skn_ref, svn_ref,
                        wko_ref, wvo_ref, wkn_ref, wvn_ref, e_ref, y_ref, kbuf, vbuf, sems, m_scr, l_scr, acc_scr,
                        ocmp_scr, owin_scr, mask_scr, *, n_pages, n_steps, n_batch, t_len, past_len):
    t = t_len
    nq_ref, nqr_ref, ngl_ref, ng_ref, y_ref = (r.at[0] for r in (nq_ref, nqr_ref, ngl_ref, ng_ref, y_ref))
    rows8 = 2 * NSA_GROUP * t
    half = NSA_GROUP * t
    tpos = lax.broadcasted_iota(jnp.int32, (t, 1), 0)
    qpos = past_len + tpos
    n_cmp = kc_ref.shape[1]
    wb = wko_ref.shape[2]
    step_keys = n_pages * PAGE_SIZE

    def before_pages():
        n_slc = n_cmp // 2
        halves = lambda ref: jnp.concatenate([ref[0, pl.ds(0, n_slc, stride=2), :],
                                              ref[0, pl.ds(1, n_slc, stride=2), :]], axis=0).astype(BF16)
        lhs = _stack_heads(nq_ref)
        s = _dot_nt(lhs, halves(kc_ref)) * ATTN_SCALE
        m = jnp.max(s, axis=-1, keepdims=True)
        p = jnp.exp(s - m)
        p = p / jnp.sum(p, axis=-1, keepdims=True)
        oc = _dot(p.astype(BF16), halves(vc_ref))
        for g, slab in enumerate(_merge_kv_heads(oc, t)):
            ocmp_scr[g] = slab
        p4 = p.reshape(8, t, n_cmp)
        sels = []
        for k in range(KV_HEADS):
            imp = p4[k * NSA_GROUP]
            for g in range(1, NSA_GROUP):
                imp = imp + p4[k * NSA_GROUP + g]
            sels.append(_rank_select(_pair_scores(imp, qpos), True).astype(F32))
        sel = jnp.concatenate(sels, axis=0).astype(BF16)
        span = LANES * SLC_BLOCK
        for c in range(n_slc // LANES):
            km = _dot(sel[:, c * LANES:(c + 1) * LANES], e_ref[...])
            for k in range(KV_HEADS):
                kmk = km[k * t:(k + 1) * t]
                if step_keys >= span:
                    off = (c * span) % step_keys
                    mask_scr[(c * span) // step_keys, k, :, off:off + span] = kmk
                else:
                    per = span // step_keys
                    for u in range(per):
                        mask_scr[c * per + u, k] = kmk[:, u * step_keys:(u + 1) * step_keys]

        lhs_r = _stack_heads(nqr_ref)
        so = _dot(lhs_r, wko_ref[0].astype(BF16)) * ATTN_SCALE
        sn = _dot(lhs_r, wkn_ref[0].astype(BF16)) * ATTN_SCALE
        jo = lax.broadcasted_iota(jnp.int32, (1, wb), 1)
        jn = lax.broadcasted_iota(jnp.int32, (1, LANES), 1)
        old_vis = jo >= tpos
        so = _mask_rows(so, [old_vis, old_vis], t)
        new_vis = jn <= tpos
        sn_w = _mask_rows(sn, [new_vis, new_vis], t)
        m = jnp.maximum(jnp.max(so, axis=-1, keepdims=True), jnp.max(sn_w, axis=-1, keepdims=True))
        po = jnp.exp(so - m)
        pn = jnp.exp(sn_w - m)
        den = jnp.sum(po, axis=-1, keepdims=True) + jnp.sum(pn, axis=-1, keepdims=True)
        pob, pnb = (po / den).astype(BF16), (pn / den).astype(BF16)
        wvo, wvn = wvo_ref[0].astype(BF16), wvn_ref[0].astype(BF16)
        ow = jnp.concatenate([_dot_nt(pob[:half], wvo) + _dot_nt(pnb[:half], wvn),
                              _dot_nt(pob[half:], wvo) + _dot_nt(pnb[half:], wvn)], axis=0)
        for g, slab in enumerate(_merge_kv_heads(ow, t)):
            owin_scr[g] = slab

        s2 = _dot(lhs_r, skn_ref[0].astype(BF16)) * ATTN_SCALE
        s2 = _mask_rows(s2, [new_vis, new_vis], t)
        m2 = jnp.max(s2, axis=-1, keepdims=True)
        p2 = jnp.exp(s2 - m2)
        m_scr[0] = m2
        l_scr[0] = jnp.sum(p2, axis=-1, keepdims=True)
        p2b = p2.astype(BF16)
        svn = svn_ref[0].astype(BF16)
        acc_scr[0] = jnp.concatenate([_dot_nt(p2b[:half], svn), _dot_nt(p2b[half:], svn)], axis=0)
        for st in range(1, SAMPLE_STREAMS):
            m_scr[st] = jnp.full((rows8, 1), M_INIT, F32)
            l_scr[st] = jnp.zeros((rows8, 1), F32)
            acc_scr[st] = jnp.zeros((rows8, LANES), F32)

    def on_pages(step, slot):
        lhs_r = _stack_heads(nqr_ref)
        per = n_pages // SAMPLE_STREAMS
        msk = mask_scr[step]
        for st in range(SAMPLE_STREAMS):
            pages = range(st * per, (st + 1) * per)
            keys = slice(st * per * PAGE_SIZE, (st + 1) * per * PAGE_SIZE)
            kt = jnp.concatenate([kbuf[slot, j].astype(BF16) for j in pages], axis=1)
            vt = jnp.concatenate([vbuf[slot, j].astype(BF16) for j in pages], axis=1)
            sc = _dot(lhs_r, kt) * ATTN_SCALE
            sc = _mask_rows(sc, [msk[k][:, keys] > 0.5 for k in range(KV_HEADS)], t)
            m_i = m_scr[st]
            m_new = jnp.maximum(m_i, jnp.max(sc, axis=-1, keepdims=True))
            alpha = jnp.exp(m_i - m_new)
            pr = jnp.exp(sc - m_new)
            l_scr[st] = alpha * l_scr[st] + jnp.sum(pr, axis=-1, keepdims=True)
            m_scr[st] = m_new
            pb = pr.astype(BF16)
            pv = jnp.concatenate([_dot_nt(pb[:half], vt), _dot_nt(pb[half:], vt)], axis=0)
            acc_scr[st] = alpha * acc_scr[st] + pv

    before_pages()
    _stream_pages(pt_ref, (sk_hbm, sv_hbm), (kbuf, vbuf), sems, n_pages, n_steps, n_batch, on_pages)
    m_all = m_scr[0]
    for st in range(1, SAMPLE_STREAMS):
        m_all = jnp.maximum(m_all, m_scr[st])
    l_all = jnp.zeros((rows8, 1), F32)
    acc_all = jnp.zeros((rows8, LANES), F32)
    for st in range(SAMPLE_STREAMS):
        w = jnp.exp(m_scr[st] - m_all)
        l_all = l_all + w * l_scr[st]
        acc_all = acc_all + w * acc_scr[st]
    o_slc = _merge_kv_heads(acc_all / l_all, t)
    o_cmp = [ocmp_scr[g] for g in range(NSA_GROUP)]
    o_win = [owin_scr[g] for g in range(NSA_GROUP)]
    _gate_and_store(o_cmp, o_slc, o_win, ngl_ref, ng_ref, y_ref)


def _attn_sample(page_table, cache_sk_t, cache_sv_t, nq, nqr, ngl, ng, kc, vc, skn, svn, wko, wvo, wkn, wvn, t_len):
    db, n_pages = page_table.shape
    g = SAMPLE_PAGES_PER_STEP
    steps = n_pages // g
    past_len = n_pages * PAGE_SIZE
    step_keys = g * PAGE_SIZE
    span = LANES * SLC_BLOCK
    expand = (np.arange(span)[None, :] // SLC_BLOCK == np.arange(LANES)[:, None]).astype(BF16)

    hbm = pl.BlockSpec(memory_space=pl.ANY)
    row = lambda w: pl.BlockSpec((1, t_len, w), lambda b, pt: (b, 0, 0))
    per_b = lambda shape: pl.BlockSpec((1,) + shape, lambda b, pt: (b, 0, 0))
    rows8 = 2 * NSA_GROUP * t_len
    page_buf = pltpu.VMEM((PAGE_SLOTS, g, KV_WIDTH, PAGE_SIZE), F32)
    grid_spec = pltpu.PrefetchScalarGridSpec(
        num_scalar_prefetch=1,
        grid=(db,),
        in_specs=[hbm, hbm, row(512), row(512), row(LANES), row(512), per_b(kc.shape[1:]), per_b(vc.shape[1:]),
                  per_b(skn.shape[1:]), per_b(svn.shape[1:]), per_b(wko.shape[1:]), per_b(wvo.shape[1:]),
                  per_b(wkn.shape[1:]), per_b(wvn.shape[1:]), pl.BlockSpec(expand.shape, lambda b, pt: (0, 0))],
        out_specs=row(512),
        scratch_shapes=[page_buf, page_buf, pltpu.SemaphoreType.DMA((PAGE_SLOTS, 2)),
                        pltpu.VMEM((SAMPLE_STREAMS, rows8, 1), F32), pltpu.VMEM((SAMPLE_STREAMS, rows8, 1), F32),
                        pltpu.VMEM((SAMPLE_STREAMS, rows8, LANES), F32),
                        pltpu.VMEM((NSA_GROUP, t_len, LANES), F32), pltpu.VMEM((NSA_GROUP, t_len, LANES), F32),
                        pltpu.VMEM((steps, KV_HEADS, t_len, step_keys), F32)],
    )
    return pl.pallas_call(
        functools.partial(_attn_sample_kernel, n_pages=g, n_steps=steps, n_batch=db, t_len=t_len, past_len=past_len),
        grid_spec=grid_spec,
        out_shape=jax.ShapeDtypeStruct((db, t_len, 512), F32),
        compiler_params=_params(("arbitrary",)),
        name="attn_sample",
    )(page_table, cache_sk_t, cache_sv_t, nq, nqr, ngl, ng, kc, vc, skn, svn, wko, wvo, wkn, wvn, expand)


def _finish_kernel(x_ref, yr_ref, yn_ref, p_ref, wo_ref, gple_ref, wg_ref, wp_ref, gf_ref, o_ref):
    x = x_ref[...]
    x = x + _dot(yr_ref[...].astype(BF16), wo_ref[0:RET_WIDTH, :]) + _dot(yn_ref[...].astype(BF16), wo_ref[RET_WIDTH:, :])
    ms = jnp.mean(x * x, axis=-1, keepdims=True)
    hn = (x * lax.rsqrt(ms + RMS_EPS) * gple_ref[...]).astype(BF16)
    gate = _sigmoid(_dot(hn, wg_ref[...]))
    x = x + gate * _dot(p_ref[...].astype(BF16), wp_ref[...])
    ms = jnp.mean(x * x, axis=-1, keepdims=True)
    o_ref[...] = x * lax.rsqrt(ms + RMS_EPS) * gf_ref[...]


def _finish(x2d, y_ret, y_nsa, p2d, w_out, norm_ple, w_gate, w_ple, norm_f, tm):
    n = x2d.shape[0]
    row = lambda w: pl.BlockSpec((tm, w), lambda i: (i, 0))
    const = lambda shape: pl.BlockSpec(shape, lambda i: (0, 0))
    return pl.pallas_call(
        _finish_kernel,
        grid=(n // tm,),
        in_specs=[row(D_MODEL), row(RET_WIDTH), row(NSA_WIDTH), row(PLE_DIM), const(w_out.shape), const((1, D_MODEL)),
                  const(w_gate.shape), const(w_ple.shape), const((1, D_MODEL))],
        out_specs=row(D_MODEL),
        out_shape=jax.ShapeDtypeStruct((n, D_MODEL), F32),
        compiler_params=_params(("arbitrary",)),
        name="finish",
    )(x2d, y_ret, y_nsa, p2d, w_out, norm_ple, w_gate, w_ple, norm_f)


def _slab_perm():
    return np.array([(k * NSA_GROUP + g) * NSA_HD + d for g in range(NSA_GROUP) for k in range(KV_HEADS)
                     for d in range(NSA_HD)], np.int32)


def _to_positions_major(x_t):
    lead = x_t.shape[:-2]
    n = len(lead)
    x4 = x_t.reshape(lead + (KV_HEADS, NSA_HD, x_t.shape[-1]))
    return jnp.transpose(x4, tuple(range(n)) + (n + 2, n, n + 1))


def _cache_t(cache):
    n_pool, page = cache.shape[:2]
    return jnp.transpose(cache, (0, 2, 3, 1)).reshape(n_pool, KV_WIDTH, page)


def _layer(xp, xs, c_ck, c_cv, c_sk, c_sv, win_k, win_v, ret_state, page_table, p_p, p_s, norm_mix, w_in, gn_g, gn_b,
           pe_k, w1_k, w2_k, pe_v, w1_v, w2_v, w_out, norm_ple, w_gate, w_ple, norm_f):
    b, s, d = xp.shape
    db, t, _ = xs.shape
    n_pages = page_table.shape[1]
    past = n_pages * PAGE_SIZE

    off = np.cumsum((0,) + SPLIT_SIZES)
    w_in_t = w_in.T
    rows = lambda i: w_in_t[off[i]:off[i + 1]]
    perm = _slab_perm()
    ngl_rows = jnp.pad(rows(11), ((0, LANES - N_GATES), (0, 0)))
    kv_rows = [rows(5), rows(6), rows(8), rows(10), rows(7), rows(9)]
    w_rm_p = tuple(w.astype(BF16) for w in (w_in_t[:off[4]], rows(7), rows(9)))
    w_t_p = jnp.concatenate(kv_rows + [rows(4), rows(12), ngl_rows], axis=0).astype(BF16)
    w_rm_s = tuple(w.astype(BF16) for w in (w_in_t[:off[4]], rows(4)[perm], rows(12)[perm], ngl_rows))
    w_t_s = jnp.concatenate(kv_rows, axis=0).astype(BF16)
    w_out_b = w_out.astype(BF16)
    w_out_slab = jnp.concatenate([w_out[:RET_WIDTH], w_out[RET_WIDTH:][perm]], axis=0).astype(BF16)
    g_mix = norm_mix.reshape(1, d)
    cwk = _compress_weights(pe_k, w1_k, w2_k)
    cwv = _compress_weights(pe_v, w1_v, w2_v)
    cwv_t = _compress_weights(pe_v, w1_v, w2_v, transpose_w2=True)
    gn_g2, gn_b2 = gn_g.reshape(1, RET_WIDTH), gn_b.reshape(1, RET_WIDTH)
    fin_w = (norm_ple.reshape(1, d), w_gate.astype(BF16), w_ple.astype(BF16), norm_f.reshape(1, d))

    tm = PROMPT_ROW_TILE
    (q_ret, k_ret, v_ret, rg, sk_rm, wk_rm, ck_t, cv_t, sv_t, wv_t, sk_t, wk_t, sv_b, wv_b, nq_t, nqr_t, ng_t,
     ngl_t) = _proj(xp.reshape(b * s, d), np.arange(s), g_mix, w_rm_p, w_t_p, b, tm, True)
    y_ret, ret_p = _retention(q_ret, k_ret, v_ret, rg, jnp.zeros((b, RET_HEADS, RET_DK, RET_DK), F32), gn_g2, gn_b2,
                              b, s // RET_CHUNK, RET_CHUNK)
    kc, vc_t = _compress_prompt(ck_t, cv_t, cwk, cwv_t)
    y_nsa = _attn_prompt(nq_t, nqr_t, ngl_t, ng_t, kc, vc_t, sk_rm, sv_b, wk_rm, wv_b)
    y_prompt = _finish(xp.reshape(b * s, d), y_ret, y_nsa, p_p.reshape(b * s, PLE_DIM), w_out_b, *fin_w,
                       FINISH_ROW_TILE)
    wb_p = min(WINDOW, s)
    prompt_states = (ret_p, _to_positions_major(ck_t), _to_positions_major(cv_t), _to_positions_major(sk_t),
                     _to_positions_major(sv_t), _to_positions_major(wk_t[:, :, s - wb_p:]),
                     _to_positions_major(wv_t[:, :, s - wb_p:]))

    n = db * t
    pos_s = np.tile(past + np.arange(t), db)
    (q_ret, k_ret, v_ret, rg, nq, nqr, ng, ngl, ck_n, cv_n, sv_n, wv_n, sk_n, wk_n) = _proj(
        xs.reshape(n, d), pos_s, g_mix, w_rm_s, w_t_s, 1, n, False)
    y_ret, ret_s = _retention(q_ret, k_ret, v_ret, rg, ret_state, gn_g2, gn_b2, db, 1, t)
    kc, vc = _compress_sample(page_table, _cache_t(c_ck), _cache_t(c_cv), cwk, cwv)
    per_batch = lambda a: jnp.transpose(a[0].reshape(KV_WIDTH, db, t), (1, 0, 2))
    lane_pad = lambda a: jnp.pad(per_batch(a), ((0, 0), (0, 0), (0, LANES - t)))
    win_t = lambda w: jnp.transpose(w, (0, 2, 3, 1)).reshape(db, KV_WIDTH, w.shape[1])
    rows3 = lambda a: a.astype(F32).reshape(db, t, a.shape[-1])
    y_nsa = _attn_sample(page_table, _cache_t(c_sk), _cache_t(c_sv), rows3(nq), rows3(nqr), rows3(ngl), rows3(ng), kc, vc,
                         lane_pad(sk_n), lane_pad(sv_n), win_t(win_k), win_t(win_v), lane_pad(wk_n), lane_pad(wv_n), t)
    y_nsa = y_nsa.reshape(n, NSA_WIDTH)
    y_sample = _finish(xs.reshape(n, d), y_ret, y_nsa, p_s.reshape(n, PLE_DIM), w_out_slab, *fin_w, n)
    new_rows = lambda a: _to_positions_major(per_batch(a))
    sample_states = (ret_s, new_rows(ck_n), new_rows(cv_n), new_rows(sk_n), new_rows(sv_n),
                     jnp.concatenate([win_k[:, t:], new_rows(wk_n)], axis=1),
                     jnp.concatenate([win_v[:, t:], new_rows(wv_n)], axis=1))
    return y_prompt.reshape(b, s, d), y_sample.reshape(db, t, d), prompt_states, sample_states


def kernel(x_prompt, x_sample, cache_cmp_k, cache_cmp_v, cache_slc_k, cache_slc_v, state_win_k, state_win_v, state_ret, page_table, p_prompt, p_sample, norm_mix, w_in, ret_gn_g, ret_gn_b, cmp_pe_k, cmp_w1_k, cmp_w2_k, cmp_pe_v, cmp_w1_v, cmp_w2_v, w_out, norm_ple, w_ple_gate, w_ple, norm_f):
    depth = w_in.shape[0]
    assert depth == 1, "single trunk layer"
    l = 0
    yp, ys, sp, ss = _layer(x_prompt, x_sample, cache_cmp_k[l], cache_cmp_v[l], cache_slc_k[l], cache_slc_v[l],
                            state_win_k[l], state_win_v[l], state_ret[l], page_table, p_prompt[l], p_sample[l],
                            norm_mix[l], w_in[l], ret_gn_g[l], ret_gn_b[l], cmp_pe_k[l], cmp_w1_k[l], cmp_w2_k[l],
                            cmp_pe_v[l], cmp_w1_v[l], cmp_w2_v[l], w_out[l], norm_ple[l], w_ple_gate[l], w_ple[l], norm_f)
    return (yp, ys) + tuple(a[None] for a in sp) + tuple(a[None] for a in ss)
```
